```python
import math
import jax, jax.numpy as jnp
from jax import lax
import numpy as np

D_MODEL = 1024
BATCH = 8
SEQ = 4096
DEPTH = 2

CHUNK = 64
Q_BLOCK = 128
D_MIX = D_MODEL
N_GROUPS = 4
GROUP_WIDTH = D_MIX // N_GROUPS
HEAD_DIM = 64
GROUP_HEADS = GROUP_WIDTH // HEAD_DIM

MLA_Q_LORA = D_MODEL // 4
MLA_KV_LORA = D_MODEL // 8
MLA_NOPE = HEAD_DIM
MLA_ROPE = HEAD_DIM // 2
MLA_V = HEAD_DIM
ROPE_BASE = 10000.0

DIFF_QK = HEAD_DIM // 2

IDX_HEADS = 8
IDX_DIM = 32
TOPK_MAX = 256

CONV_WIDTH = 3

NUM_BUCKETS = 32
MAX_DISTANCE = 128
N_BIAS_HEADS = 2 * GROUP_HEADS + GROUP_HEADS

NORM_EPS = 1e-6
NEG = -1e30

IN_SPLITS = (
    ("a_cq", MLA_Q_LORA), ("a_ckv", MLA_KV_LORA), ("a_krope", MLA_ROPE), ("a_gate", GROUP_WIDTH),
    ("b_q", GROUP_HEADS * 2 * DIFF_QK), ("b_k", GROUP_HEADS * 2 * DIFF_QK), ("b_v", GROUP_WIDTH), ("b_gate", GROUP_WIDTH),
    ("c_q", GROUP_WIDTH), ("c_k", GROUP_WIDTH), ("c_v", GROUP_WIDTH),
    ("c_qidx", IDX_HEADS * IDX_DIM), ("c_kidx", IDX_DIM), ("c_widx", IDX_HEADS), ("c_gate", GROUP_WIDTH),
    ("d_b", GROUP_WIDTH), ("d_c", GROUP_WIDTH), ("d_h", GROUP_WIDTH), ("d_gate", GROUP_WIDTH),
)
IN_COLS = sum(w for _, w in IN_SPLITS)

kernel_name = "hybrid_parallel_mla_diff_dsa_conv"


def _rmsnorm(x, g):
    xf = x.astype(jnp.float32)
    y = xf * lax.rsqrt(jnp.mean(xf * xf, axis=-1, keepdims=True) + NORM_EPS)
    return (y * g.astype(jnp.float32)).astype(x.dtype)


def _split(p):
    out, off = {}, 0
    for name, width in IN_SPLITS:
        out[name] = p[..., off:off + width]
        off += width
    return out


def _rope(x, cos, sin):
    half = x.shape[-1] // 2
    x1, x2 = x[..., :half], x[..., half:]
    c, s = cos.astype(x.dtype), sin.astype(x.dtype)
    return jnp.concatenate([x1 * c - x2 * s, x2 * c + x1 * s], axis=-1)


def _rel_bucket(rel):
    nb = NUM_BUCKETS // 2
    max_exact = nb // 2
    ret = jnp.where(rel > 0, nb, 0)
    n = jnp.abs(rel)
    nf = jnp.maximum(n, max_exact).astype(jnp.float32)
    large = max_exact + (jnp.log(nf / max_exact) / math.log(MAX_DISTANCE / max_exact)
                         * (nb - max_exact)).astype(jnp.int32)
    large = jnp.minimum(large, nb - 1)
    return ret + jnp.where(n < max_exact, n, large)


def _masked_softmax(logits, mask):
    return jax.nn.softmax(jnp.where(mask, logits.astype(jnp.float32), NEG), axis=-1)


def _chunk_mask(qpos, kpos):
    return (kpos // CHUNK)[None, :] <= (qpos // CHUNK)[:, None]


def _sweep(fn, S):
    out = lax.map(fn, jnp.arange(S // Q_BLOCK, dtype=jnp.int32) * Q_BLOCK)
    out = jnp.moveaxis(out, 0, 1)
    return out.reshape((out.shape[0], S) + out.shape[3:])


def _mla(cq, ckv, krope, qa_g, w_uq, kva_g, w_ukv, cos, sin):
    B, S, _ = cq.shape
    H = GROUP_HEADS
    q = (_rmsnorm(cq, qa_g) @ w_uq).reshape(B, S, H, MLA_NOPE + MLA_ROPE)
    q_nope = q[..., :MLA_NOPE]
    q_rope = _rope(q[..., MLA_NOPE:], cos[:, None], sin[:, None])
    kv = (_rmsnorm(ckv, kva_g) @ w_ukv).reshape(B, S, H, MLA_NOPE + MLA_V)
    k_nope, v = kv[..., :MLA_NOPE], kv[..., MLA_NOPE:]
    k_rope = _rope(krope, cos, sin)
    scale = (MLA_NOPE + MLA_ROPE) ** -0.5
    kpos = jnp.arange(S, dtype=jnp.int32)

    def block(start):
        qpos = start + jnp.arange(Q_BLOCK, dtype=jnp.int32)
        qn = lax.dynamic_slice_in_dim(q_nope, start, Q_BLOCK, axis=1)
        qr = lax.dynamic_slice_in_dim(q_rope, start, Q_BLOCK, axis=1)
        logits = (jnp.einsum('bqhd,bkhd->bhqk', qn, k_nope)
                  + jnp.einsum('bqhr,bkr->bhqk', qr, k_rope)) * scale
        p = _masked_softmax(logits, _chunk_mask(qpos, kpos))
        return jnp.einsum('bhqk,bkhd->bqhd', p.astype(v.dtype), v)

    return _sweep(block, S).reshape(B, S, H * MLA_V)


def _diff_attn(q, k, v, lam_params, subln_g, rel_bias, lambda_init):
    B, S, _ = q.shape
    H = GROUP_HEADS
    q = q.reshape(B, S, H, 2, DIFF_QK)
    k = k.reshape(B, S, H, 2, DIFF_QK)
    v = v.reshape(B, S, H, HEAD_DIM)
    lp = lam_params.astype(jnp.float32)
    lam = jnp.exp(jnp.sum(lp[0] * lp[1])) - jnp.exp(jnp.sum(lp[2] * lp[3])) + lambda_init
    table = rel_bias[:, :2 * H].reshape(NUM_BUCKETS, H, 2)
    scale = DIFF_QK ** -0.5
    kpos = jnp.arange(S, dtype=jnp.int32)

    def block(start):
        qpos = start + jnp.arange(Q_BLOCK, dtype=jnp.int32)
        qb = lax.dynamic_slice_in_dim(q, start, Q_BLOCK, axis=1)
        bias = jnp.transpose(table[_rel_bucket(kpos[None, :] - qpos[:, None])], (2, 3, 0, 1))
        logits = jnp.einsum('bqhmd,bkhmd->bhmqk', qb, k) * scale + bias
        p = _masked_softmax(logits, _chunk_mask(qpos, kpos))
        a = p[:, :, 0] - lam * p[:, :, 1]
        return jnp.einsum('bhqk,bkhd->bqhd', a.astype(v.dtype), v)

    o = _sweep(block, S)
    o = _rmsnorm(o, subln_g) * (1.0 - lambda_init)
    return o.reshape(B, S, H * HEAD_DIM)


def _dsa(q, k, v, qidx, kidx, widx, rel_bias, topk):
    B, S, _ = q.shape
    H = GROUP_HEADS
    q = q.reshape(B, S, H, HEAD_DIM)
    k = k.reshape(B, S, H, HEAD_DIM)
    v = v.reshape(B, S, H, HEAD_DIM)
    qidx = qidx.reshape(B, S, IDX_HEADS, IDX_DIM)
    table = rel_bias[:, 2 * H:]
    scale = HEAD_DIM ** -0.5
    idx_scale = (IDX_HEADS ** -0.5) * (IDX_DIM ** -0.5)
    kpos = jnp.arange(S, dtype=jnp.int32)
    gather = jax.vmap(lambda xb, ib: xb[ib])

    def block(start):
        qpos = start + jnp.arange(Q_BLOCK, dtype=jnp.int32)
        qi = lax.dynamic_slice_in_dim(qidx, start, Q_BLOCK, axis=1)
        wi = lax.dynamic_slice_in_dim(widx, start, Q_BLOCK, axis=1).astype(jnp.float32)
        score = jax.nn.relu(jnp.einsum('bqhd,bkd->bqhk', qi, kidx).astype(jnp.float32))
        index = jnp.einsum('bqh,bqhk->bqk', wi, score) * idx_scale
        index = jnp.where(_chunk_mask(qpos, kpos)[None], index, NEG)
        _, sel = lax.top_k(index, topk)
        k_sel = gather(k, sel)
        v_sel = gather(v, sel)
        valid = (sel // CHUNK) <= (qpos // CHUNK)[None, :, None]
        bias = jnp.transpose(table[_rel_bucket(sel - qpos[None, :, None])], (0, 3, 1, 2))
        qb = lax.dynamic_slice_in_dim(q, start, Q_BLOCK, axis=1)
        logits = jnp.einsum('bqhd,bqjhd->bhqj', qb, k_sel) * scale + bias
        p = _masked_softmax(logits, valid[:, None])
        return jnp.einsum('bhqj,bqjhd->bqhd', p.astype(v.dtype), v_sel)

    return _sweep(block, S).reshape(B, S, H * HEAD_DIM)


def _short_conv(b, c, h, conv_w):
    u = c * h
    S = u.shape[1]
    up = jnp.pad(u, ((0, 0), (CONV_WIDTH - 1, 0), (0, 0)))
    y = conv_w[0] * up[:, 0:S]
    for j in range(1, CONV_WIDTH):
        y = y + conv_w[j] * up[:, j:j + S]
    return b * y


def setup_inputs(seed: int = 0) -> dict:
    key = jax.random.key(seed)
    ks = jax.random.split(key, 14)
    f32 = jnp.float32
    nrm = lambda k, shape: jax.random.normal(k, shape, f32)
    H = GROUP_HEADS
    return {
        "x": nrm(ks[0], (BATCH, SEQ, D_MODEL)),
        "norm_g": 1.0 + 0.05 * nrm(ks[1], (DEPTH, D_MODEL)),
        "w_in": nrm(ks[2], (DEPTH, D_MODEL, IN_COLS)) * D_MODEL ** -0.5,
        "mla_qa_g": 1.0 + 0.05 * nrm(ks[3], (DEPTH, MLA_Q_LORA)),
        "mla_w_uq": nrm(ks[4], (DEPTH, MLA_Q_LORA, H * (MLA_NOPE + MLA_ROPE))) * MLA_Q_LORA ** -0.5,
        "mla_kva_g": 1.0 + 0.05 * nrm(ks[5], (DEPTH, MLA_KV_LORA)),
        "mla_w_ukv": nrm(ks[6], (DEPTH, MLA_KV_LORA, H * (MLA_NOPE + MLA_V))) * MLA_KV_LORA ** -0.5,
        "diff_lambda": 0.1 * nrm(ks[7], (DEPTH, 4, DIFF_QK)),
        "diff_subln_g": 1.0 + 0.05 * nrm(ks[8], (DEPTH, HEAD_DIM)),
        "conv_w": nrm(ks[9], (DEPTH, CONV_WIDTH, GROUP_WIDTH)) * CONV_WIDTH ** -0.5,
        "w_out": nrm(ks[10], (DEPTH, D_MIX, D_MODEL)) * D_MIX ** -0.5,
        "rel_bias": 0.5 * nrm(ks[11], (NUM_BUCKETS, N_BIAS_HEADS)),
        "final_g": 1.0 + 0.05 * nrm(ks[12], (D_MODEL,)),
    }


def reference(x, norm_g, w_in, mla_qa_g, mla_w_uq, mla_kva_g, mla_w_ukv, diff_lambda,
              diff_subln_g, conv_w, w_out, rel_bias, final_g):
    S = x.shape[1]
    topk = min(TOPK_MAX, S // 4)
    half = MLA_ROPE // 2
    inv_freq = ROPE_BASE ** (-jnp.arange(half, dtype=jnp.float32) / half)
    ang = jnp.arange(S, dtype=jnp.float32)[:, None] * inv_freq[None, :]
    cos, sin = jnp.cos(ang), jnp.sin(ang)

    for l in range(DEPTH):
        lambda_init = 0.8 - 0.6 * math.exp(-0.3 * l)
        h = _rmsnorm(x, norm_g[l])
        p = _split(h @ w_in[l])
        o_a = _mla(p["a_cq"], p["a_ckv"], p["a_krope"], mla_qa_g[l], mla_w_uq[l],
                   mla_kva_g[l], mla_w_ukv[l], cos, sin)
        o_b = _diff_attn(p["b_q"], p["b_k"], p["b_v"], diff_lambda[l], diff_subln_g[l],
                         rel_bias, lambda_init)
        o_c = _dsa(p["c_q"], p["c_k"], p["c_v"], p["c_qidx"], p["c_kidx"], p["c_widx"],
                   rel_bias, topk)
        o_d = _short_conv(p["d_b"], p["d_c"], p["d_h"], conv_w[l])
        y = jnp.concatenate([
            o_a * jax.nn.silu(p["a_gate"]),
            o_b * jax.nn.silu(p["b_gate"]),
            o_c * jax.nn.silu(p["c_gate"]),
            o_d * jax.nn.silu(p["d_gate"]),
        ], axis=-1)
        x = x + y @ w_out[l]
    return _rmsnorm(x, final_g)
```

```python
import functools
import math

import numpy as np
import jax
import jax.numpy as jnp
from jax import lax
from jax.experimental import pallas as pl
from jax.experimental.pallas import tpu as pltpu

F32 = jnp.float32
BF16 = jnp.bfloat16

D_MODEL = 1024
CHUNK = 64
N_HEADS = 4
HEAD_DIM = 64
GROUP_WIDTH = N_HEADS * HEAD_DIM
MLA_Q_LORA = 256
MLA_KV_LORA = 128
MLA_NOPE = 64
MLA_ROPE = 32
ROPE_BASE = 10000.0
DIFF_QK = 32
IDX_HEADS = 8
IDX_DIM = 32
TOPK_MAX = 256
CONV_WIDTH = 3
NUM_BUCKETS = 32
MAX_DISTANCE = 128
N_DIFF_MAPS = 2 * N_HEADS
N_BIAS_MAPS = N_DIFF_MAPS + N_HEADS
NORM_EPS = 1e-6
NEG = -1e30
NEG_BITS = int(np.float32(NEG).view(np.int32))
INT_MIN = -2 ** 31

LANES = 128
ATT_TILE = 256
PROJ_ROWS = 512
VMEM_LIMIT = 56 * 1024 * 1024

_IN_SPLITS = (
    ("a_cq", 256), ("a_ckv", 128), ("a_krope", 32), ("a_gate", 256),
    ("b_q", 256), ("b_k", 256), ("b_v", 256), ("b_gate", 256),
    ("c_q", 256), ("c_k", 256), ("c_v", 256),
    ("c_qidx", 256), ("c_kidx", 32), ("c_widx", 8), ("c_gate", 256),
    ("d_b", 256), ("d_c", 256), ("d_h", 256), ("d_gate", 256),
)

_SEGS = (
    ("cq", 256), ("ckv", 128), ("kr1", 128), ("kr2", 128), ("a_gate", 256),
    ("b_q", 256), ("b_k", 256), ("b_v", 256), ("b_gate", 256),
    ("c_q", 256), ("c_k", 256), ("c_v", 256), ("c_qidx", 256), ("c_kidx4", 128), ("c_widx", 128),
    ("c_gate", 256), ("d_b", 256), ("d_c", 256), ("d_h", 256), ("d_gate", 256),
)
_SEG_OFF = {}
_off = 0
for _name, _w in _SEGS:
    _SEG_OFF[_name] = (_off, _w)
    _off += _w
W1_COLS = _off


def _cparams(sem):
    return pltpu.CompilerParams(dimension_semantics=sem, vmem_limit_bytes=VMEM_LIMIT)


def _split_cols(w):
    out, off = {}, 0
    for name, width in _IN_SPLITS:
        out[name] = w[:, off:off + width]
        off += width
    return out


def _rot_half_cols(w):
    half = MLA_ROPE // 2
    return jnp.concatenate([-w[..., half:], w[..., :half]], axis=-1)


def _prep_w_in(w):
    p = _split_cols(w)
    z = lambda n: jnp.zeros((w.shape[0], n), w.dtype)
    kr1 = jnp.concatenate([z(MLA_NOPE), p["a_krope"], z(LANES - MLA_NOPE - MLA_ROPE)], axis=1)
    kr2 = jnp.concatenate([z(MLA_NOPE), _rot_half_cols(p["a_krope"]), z(LANES - MLA_NOPE - MLA_ROPE)], axis=1)
    segs = {
        "cq": p["a_cq"], "ckv": p["a_ckv"], "kr1": kr1, "kr2": kr2, "a_gate": p["a_gate"],
        "b_q": p["b_q"] * DIFF_QK ** -0.5, "b_k": p["b_k"], "b_v": p["b_v"], "b_gate": p["b_gate"],
        "c_q": p["c_q"] * HEAD_DIM ** -0.5, "c_k": p["c_k"], "c_v": p["c_v"], "c_qidx": p["c_qidx"],
        "c_kidx4": jnp.tile(p["c_kidx"], (1, LANES // IDX_DIM)),
        "c_widx": jnp.concatenate([p["c_widx"], z(LANES - IDX_HEADS)], axis=1),
        "c_gate": p["c_gate"], "d_b": p["d_b"], "d_c": p["d_c"], "d_h": p["d_h"], "d_gate": p["d_gate"],
    }
    return jnp.concatenate([segs[name] for name, _ in _SEGS], axis=1).astype(BF16)


def _prep_mla_up(w_uq, w_ukv):
    scale = (MLA_NOPE + MLA_ROPE) ** -0.5
    wq = w_uq.reshape(MLA_Q_LORA, N_HEADS, MLA_NOPE + MLA_ROPE) * scale
    zq = jnp.zeros((MLA_Q_LORA, N_HEADS, LANES - MLA_NOPE - MLA_ROPE), w_uq.dtype)
    wq1 = jnp.concatenate([wq, zq], axis=-1).reshape(MLA_Q_LORA, N_HEADS * LANES)
    wq2 = jnp.concatenate([jnp.zeros_like(wq[..., :MLA_NOPE]), _rot_half_cols(wq[..., MLA_NOPE:]), zq],
                          axis=-1).reshape(MLA_Q_LORA, N_HEADS * LANES)
    wkv = w_ukv.reshape(MLA_KV_LORA, N_HEADS, MLA_NOPE + HEAD_DIM)
    wk = jnp.concatenate([wkv[..., :MLA_NOPE], jnp.zeros((MLA_KV_LORA, N_HEADS, LANES - MLA_NOPE), w_ukv.dtype)],
                         axis=-1).reshape(MLA_KV_LORA, N_HEADS * LANES)
    wv = wkv[..., MLA_NOPE:].reshape(MLA_KV_LORA, N_HEADS * HEAD_DIM)
    return wq1.astype(BF16), wq2.astype(BF16), wk.astype(BF16), wv.astype(BF16)


def _rope_tables(seq):
    half = MLA_ROPE // 2
    inv_freq = ROPE_BASE ** (-np.arange(half, dtype=np.float32) / half)
    ang = np.arange(seq, dtype=np.float32)[:, None] * inv_freq[None, :].astype(np.float32)
    cos, sin = np.cos(ang).astype(np.float32), np.sin(ang).astype(np.float32)
    ct = np.zeros((seq, LANES), np.float32)
    st = np.zeros((seq, LANES), np.float32)
    ct[:, :MLA_NOPE] = 1.0
    ct[:, MLA_NOPE:MLA_NOPE + half] = cos
    ct[:, MLA_NOPE + half:MLA_NOPE + MLA_ROPE] = cos
    st[:, MLA_NOPE:MLA_NOPE + half] = sin
    st[:, MLA_NOPE + half:MLA_NOPE + MLA_ROPE] = sin
    return ct, st


def _rel_bucket_np(rel):
    nb = NUM_BUCKETS // 2
    max_exact = nb // 2
    ret = np.where(rel > 0, nb, 0)
    n = np.abs(rel)
    nf = np.maximum(n, max_exact).astype(np.float32)
    large = max_exact + (np.log(nf / np.float32(max_exact)) / np.float32(math.log(MAX_DISTANCE / max_exact))
                         * np.float32(nb - max_exact)).astype(np.int32)
    large = np.minimum(large, nb - 1)
    return (ret + np.where(n < max_exact, n, large)).astype(np.int32)


def _position_constants(seq, tile):
    r = np.arange(tile)
    rel0 = r[None, :] - r[:, None]
    bidx = np.stack([_rel_bucket_np(rel0 - tile), _rel_bucket_np(rel0)]).astype(np.int32)
    far = _rel_bucket_np(np.arange(-(seq - 1), -tile))
    far_bucket = int(far[0]) if far.size else int(_rel_bucket_np(np.array([-tile - 1]))[0])
    assert far.size == 0 or np.all(far == far_bucket)
    mask_add = np.where((r[None, :] // CHUNK) <= (r[:, None] // CHUNK), 0.0, NEG).astype(np.float32)
    return bidx, far_bucket, mask_add


def _bias_kernel(tab_ref, bidx_ref, madd_ref, o_ref, *, far_bucket):
    j = pl.program_id(0)
    c = tab_ref[j, far_bucket]
    for d in range(2):
        idx = bidx_ref[d]
        acc = jnp.zeros(idx.shape, F32)
        for b in range(NUM_BUCKETS):
            acc = jnp.where(idx == b, tab_ref[j, b] - c, acc)
        if d == 1:
            acc = acc + madd_ref[...]
        o_ref[0, d] = acc


def _bias_tiles(rel_bias, bidx, mask_add, far_bucket, tile):
    return pl.pallas_call(
        functools.partial(_bias_kernel, far_bucket=far_bucket),
        grid=(N_BIAS_MAPS,),
        in_specs=[
            pl.BlockSpec(memory_space=pltpu.SMEM),
            pl.BlockSpec((2, tile, tile), lambda j: (0, 0, 0)),
            pl.BlockSpec((tile, tile), lambda j: (0, 0)),
        ],
        out_specs=pl.BlockSpec((1, 2, tile, tile), lambda j: (j, 0, 0, 0)),
        out_shape=jax.ShapeDtypeStruct((N_BIAS_MAPS, 2, tile, tile), F32),
        compiler_params=_cparams(("arbitrary",)),
        name="bias_tiles",
    )(rel_bias.T.astype(F32), bidx, mask_add)


def _silu(x):
    return x * (1.0 / (1.0 + jnp.exp(-x)))


def _rms(x, g):
    return x * lax.rsqrt(jnp.mean(x * x, axis=-1, keepdims=True) + NORM_EPS) * g


def _inproj_kernel(x_ref, g_ref, w_ref, qag_ref, wq1_ref, wq2_ref, kvg_ref, wk_ref, wv_ref, cos_ref, sin_ref,
                   qa_ref, ka_ref, va_ref, ga_ref, qb_ref, kb_ref, vb_ref, gb_ref,
                   qc_ref, kc_ref, vc_ref, qi_ref, ki_ref, wi_ref, gc_ref, yd_ref, ud_ref):
    h = _rms(x_ref[...], g_ref[...]).astype(BF16)

    def seg(name):
        off, width = _SEG_OFF[name]
        return jnp.dot(h, w_ref[:, off:off + width], preferred_element_type=F32)

    cos = cos_ref[...]
    sin = sin_ref[...]

    cqn = _rms(seg("cq"), qag_ref[...]).astype(BF16)
    q1 = jnp.dot(cqn, wq1_ref[...], preferred_element_type=F32)
    q2 = jnp.dot(cqn, wq2_ref[...], preferred_element_type=F32)
    kr = seg("kr1") * cos + seg("kr2") * sin
    ckvn = _rms(seg("ckv"), kvg_ref[...]).astype(BF16)
    kn = jnp.dot(ckvn, wk_ref[...], preferred_element_type=F32)
    for hd in range(N_HEADS):
        sl = slice(hd * LANES, (hd + 1) * LANES)
        qa_ref[:, sl] = (q1[:, sl] * cos + q2[:, sl] * sin).astype(BF16)
        ka_ref[:, sl] = (kn[:, sl] + kr).astype(BF16)
    va_ref[...] = jnp.dot(ckvn, wv_ref[...], preferred_element_type=F32).astype(BF16)
    ga_ref[...] = _silu(seg("a_gate")).astype(BF16)

    qb_ref[...] = seg("b_q").astype(BF16)
    kb_ref[...] = seg("b_k").astype(BF16)
    vb_ref[...] = seg("b_v").astype(BF16)
    gb_ref[...] = _silu(seg("b_gate")).astype(BF16)

    qc_ref[...] = seg("c_q").astype(BF16)
    kc_ref[...] = seg("c_k").astype(BF16)
    vc_ref[...] = seg("c_v").astype(BF16)
    qi_ref[...] = seg("c_qidx").astype(BF16)
    ki_ref[...] = seg("c_kidx4").astype(BF16)
    wi_ref[...] = seg("c_widx")
    gc_ref[...] = _silu(seg("c_gate")).astype(BF16)

    yd_ref[...] = (seg("d_b") * _silu(seg("d_gate"))).astype(BF16)
    ud_ref[...] = seg("d_c") * seg("d_h")


def _inproj(x2, g, w1, qag, wq1, wq2, kvg, wk, wv, cos_t, sin_t, seq):
    n = x2.shape[0]
    tm = min(PROJ_ROWS, seq)
    tiles_per_seq = seq // tm
    row = lambda w: pl.BlockSpec((tm, w), lambda i: (i, 0))
    full = lambda a: pl.BlockSpec(a.shape, lambda i: (0,) * a.ndim)
    tab = pl.BlockSpec((tm, LANES), lambda i: (i % tiles_per_seq, 0))
    widths = (("qa", 512, BF16), ("ka", 512, BF16), ("va", 256, BF16), ("ga", 256, BF16),
              ("qb", 256, BF16), ("kb", 256, BF16), ("vb", 256, BF16), ("gb", 256, BF16),
              ("qc", 256, BF16), ("kc", 256, BF16), ("vc", 256, BF16), ("qi", 256, BF16),
              ("ki", 128, BF16), ("wi", 128, F32), ("gc", 256, BF16), ("yd", 256, BF16), ("ud", 256, F32))
    outs = pl.pallas_call(
        _inproj_kernel,
        grid=(n // tm,),
        in_specs=[row(D_MODEL), full(g), full(w1), full(qag), full(wq1), full(wq2), full(kvg), full(wk), full(wv),
                  tab, tab],
        out_specs=[row(w) for _, w, _ in widths],
        out_shape=[jax.ShapeDtypeStruct((n, w), dt) for _, w, dt in widths],
        compiler_params=_cparams(("parallel",)),
        name="inproj",
    )(x2, g, w1, qag, wq1, wq2, kvg, wk, wv, cos_t, sin_t)
    return {name: o for (name, _, _), o in zip(widths, outs)}


_NT = (((1,), (1,)), ((), ()))


def _flash_tile(q, k, v, add, carry):
    m, l, acc = carry
    s = lax.dot_general(q, k, _NT, preferred_element_type=F32)
    if add is not None:
        s = s + add
    m_new = jnp.maximum(m, jnp.max(s, axis=-1, keepdims=True))
    alpha = jnp.exp(m - m_new)
    p = jnp.exp(s - m_new)
    l = alpha * l + jnp.sum(p, axis=-1, keepdims=True)
    acc = alpha * acc + jnp.dot(p.astype(BF16), v, preferred_element_type=F32)
    return m_new, l, acc


def _flash_init(t):
    return (jnp.full((t, 1), NEG, F32), jnp.zeros((t, 1), F32), jnp.zeros((t, LANES), F32))


def _rows(kt, t):
    return pl.ds(pl.multiple_of(kt * t, t), t)


def _lane_mask(lo, hi, dtype):
    lane = lax.broadcasted_iota(jnp.int32, (1, LANES), 1)
    return ((lane >= lo) & (lane < hi)).astype(dtype)


def _merge_pairs(outs):
    lane = lax.broadcasted_iota(jnp.int32, (1, LANES), 1)
    lo = lane < HEAD_DIM
    return jnp.concatenate([jnp.where(lo, outs[0], outs[1]), jnp.where(lo, outs[2], outs[3])], axis=-1)


def _mla_kernel(q_ref, k_ref, v_ref, g_ref, madd_ref, o_ref, *, t):
    i = pl.program_id(1)
    outs = []
    for hd in range(N_HEADS):
        qsl = slice(hd * LANES, (hd + 1) * LANES)
        vsl = slice((hd // 2) * LANES, (hd // 2 + 1) * LANES)
        q = q_ref[:, qsl]

        def tile(kt, carry, add=None, q=q, qsl=qsl, vsl=vsl):
            rows = _rows(kt, t)
            return _flash_tile(q, k_ref[rows, qsl], v_ref[rows, vsl], add, carry)

        carry = tile(i, _flash_init(t), madd_ref[...])
        m, l, acc = lax.fori_loop(0, i, tile, carry)
        outs.append(acc / l)
    o_ref[...] = (_merge_pairs(outs) * g_ref[...].astype(F32)).astype(BF16)


def _att_specs(t, seq, widths_q, widths_kv):
    nq = seq // t
    qspec = lambda w: pl.BlockSpec((t, w), lambda b, i: (b * nq + i, 0))
    kvspec = lambda w: pl.BlockSpec((seq, w), lambda b, i: (b, 0))
    return [qspec(w) for w in widths_q], [kvspec(w) for w in widths_kv], qspec(GROUP_WIDTH)


def _mla(p, mask_add, batch, seq, t):
    qs, kvs, ospec = _att_specs(t, seq, (512, 256), (512, 256))
    return pl.pallas_call(
        functools.partial(_mla_kernel, t=t),
        grid=(batch, seq // t),
        in_specs=[qs[0], kvs[0], kvs[1], qs[1], pl.BlockSpec((t, t), lambda b, i: (0, 0))],
        out_specs=ospec,
        out_shape=jax.ShapeDtypeStruct((batch * seq, GROUP_WIDTH), BF16),
        compiler_params=_cparams(("parallel", "arbitrary")),
        name="mla_attention",
    )(p["qa"], p["ka"], p["va"], p["ga"], mask_add)


def _diff_kernel(q_ref, k_ref, v_ref, g_ref, bias_ref, lam_ref, subg_ref, o_ref, *, t, lambda_init):
    i = pl.program_id(1)
    lp = lam_ref[...]
    lam = (jnp.exp(jnp.sum(lp[0:1] * lp[1:2], axis=-1, keepdims=True))
           - jnp.exp(jnp.sum(lp[2:3] * lp[3:4], axis=-1, keepdims=True)) + lambda_init)
    subg = subg_ref[...]
    outs = []
    for hd in range(N_HEADS):
        grp = slice((hd // 2) * LANES, (hd // 2 + 1) * LANES)
        maps = []
        for mp in range(2):
            lo = ((hd % 2) * 2 + mp) * DIFF_QK
            q = q_ref[:, grp] * _lane_mask(lo, lo + DIFF_QK, BF16)
            j = 2 * hd + mp

            def tile(kt, carry, add=None, q=q, grp=grp):
                rows = _rows(kt, t)
                return _flash_tile(q, k_ref[rows, grp], v_ref[rows, grp], add, carry)

            carry = tile(i, _flash_init(t), bias_ref[j, 1])
            carry = lax.cond(i >= 1, lambda c, j=j, tile=tile: tile(i - 1, c, bias_ref[j, 0]), lambda c: c, carry)
            m, l, acc = lax.fori_loop(0, jnp.maximum(i - 1, 0), tile, carry)
            maps.append(acc / l)
        o = maps[0] - lam * maps[1]
        hm = _lane_mask((hd % 2) * HEAD_DIM, (hd % 2 + 1) * HEAD_DIM, F32)
        ms = jnp.sum(o * o * hm, axis=-1, keepdims=True) * (1.0 / HEAD_DIM)
        outs.append(o * lax.rsqrt(ms + NORM_EPS) * subg * (1.0 - lambda_init))
    o_ref[...] = (_merge_pairs(outs) * g_ref[...].astype(F32)).astype(BF16)


def _diff(p, bias, lam_params, subln_g, lambda_init, batch, seq, t):
    qs, kvs, ospec = _att_specs(t, seq, (256, 256), (256, 256))
    subg = jnp.tile(subln_g.astype(F32), LANES // HEAD_DIM)[None, :]
    return pl.pallas_call(
        functools.partial(_diff_kernel, t=t, lambda_init=lambda_init),
        grid=(batch, seq // t),
        in_specs=[qs[0], kvs[0], kvs[1], qs[1],
                  pl.BlockSpec((N_DIFF_MAPS, 2, t, t), lambda b, i: (0, 0, 0, 0)),
                  pl.BlockSpec((4, DIFF_QK), lambda b, i: (0, 0)),
                  pl.BlockSpec((1, LANES), lambda b, i: (0, 0))],
        out_specs=ospec,
        out_shape=jax.ShapeDtypeStruct((batch * seq, GROUP_WIDTH), BF16),
        compiler_params=_cparams(("parallel", "arbitrary")),
        name="diff_attention",
    )(p["qb"], p["kb"], p["vb"], p["gb"], bias, lam_params.astype(F32), subg)


def _sort_key(x):
    b = lax.bitcast_convert_type(x + 0.0, jnp.int32)
    return b ^ ((b >> 31) & 0x7FFFFFFF)


def _dsa_kernel(q_ref, k_ref, v_ref, qi_ref, ki_ref, wi_ref, g_ref, bias_ref, tri_ref, o_ref, key_ref, *, t, topk):
    i = pl.program_id(1)
    n_kt = i + 1
    idx_scale = (IDX_HEADS ** -0.5) * (IDX_DIM ** -0.5)

    wi = wi_ref[...] * idx_scale
    qm, wcol = [], []
    for ih in range(IDX_HEADS):
        grp = slice((ih // 4) * LANES, (ih // 4 + 1) * LANES)
        lo = (ih % 4) * IDX_DIM
        qm.append(qi_ref[:, grp] * _lane_mask(lo, lo + IDX_DIM, BF16))
        wcol.append(wi[:, ih:ih + 1])

    def idx_tile(kt):
        kk = ki_ref[_rows(kt, t), :]
        tot = jnp.zeros((t, t), F32)
        for ih in range(IDX_HEADS):
            y = lax.dot_general(qm[ih], kk, _NT, preferred_element_type=F32)
            tot = tot + wcol[ih] * jnp.maximum(y, 0.0)
        return _sort_key(tot)

    def key_body(kt, c):
        key_ref[kt] = idx_tile(kt)
        return c

    lax.fori_loop(0, i, key_body, 0)
    rr = lax.broadcasted_iota(jnp.int32, (t, t), 0) // CHUNK
    cc = lax.broadcasted_iota(jnp.int32, (t, t), 1) // CHUNK
    key_ref[i] = jnp.where(cc <= rr, idx_tile(i), INT_MIN)

    def count(pred):
        def body(kt, c):
            hit = jnp.where(pred(key_ref[kt]), 1.0, 0.0)
            part = hit[:, 0:LANES]
            for lt in range(1, t // LANES):
                part = part + hit[:, lt * LANES:(lt + 1) * LANES]
            return c + part
        return jnp.sum(lax.fori_loop(0, n_kt, body, jnp.zeros((t, LANES), F32)), axis=-1, keepdims=True)

    def bit_body(b, res):
        cand = res | (jnp.int32(1) << (31 - b))
        cand_s = cand ^ INT_MIN
        cnt = count(lambda key: key >= cand_s)
        return jnp.where(cnt >= topk, cand, res)

    thr = lax.fori_loop(0, 32, bit_body, jnp.zeros((t, 1), jnp.int32)) ^ INT_MIN
    cnt_ge = count(lambda key: key >= thr)
    has_ties = jnp.max(cnt_ge) > topk

    @pl.when(jnp.logical_not(has_ties))
    def _():
        def body(kt, c):
            key_ref[kt] = jnp.where(key_ref[kt] >= thr, 0, NEG_BITS)
            return c
        lax.fori_loop(0, n_kt, body, 0)

    @pl.when(has_ties)
    def _():
        need = topk - count(lambda key: key > thr)

        def body(kt, seen):
            key = key_ref[kt]
            eq = jnp.where(key == thr, 1.0, 0.0)
            rank = seen + jnp.dot(eq.astype(BF16), tri_ref[...], preferred_element_type=F32)
            sel = (key > thr) | ((key == thr) & (rank <= need))
            key_ref[kt] = jnp.where(sel, 0, NEG_BITS)
            return seen + jnp.sum(eq, axis=-1, keepdims=True)
        lax.fori_loop(0, n_kt, body, jnp.zeros((t, 1), F32))

    outs = []
    for hd in range(N_HEADS):
        grp = slice((hd // 2) * LANES, (hd // 2 + 1) * LANES)
        q = q_ref[:, grp] * _lane_mask((hd % 2) * HEAD_DIM, (hd % 2 + 1) * HEAD_DIM, BF16)

        def tile(kt, carry, bias=None, q=q, grp=grp):
            rows = _rows(kt, t)
            add = lax.bitcast_convert_type(key_ref[kt], F32)
            if bias is not None:
                add = add + bias
            return _flash_tile(q, k_ref[rows, grp], v_ref[rows, grp], add, carry)

        carry = tile(i, _flash_init(t), bias_ref[hd, 1])
        carry = lax.cond(i >= 1, lambda c, hd=hd, tile=tile: tile(i - 1, c, bias_ref[hd, 0]), lambda c: c, carry)
        m, l, acc = lax.fori_loop(0, jnp.maximum(i - 1, 0), tile, carry)
        outs.append(acc / l)
    o_ref[...] = (_merge_pairs(outs) * g_ref[...].astype(F32)).astype(BF16)


def _dsa(p, bias, batch, seq, t):
    topk = min(TOPK_MAX, seq // 4)
    nq = seq // t
    qs, kvs, ospec = _att_specs(t, seq, (256, 256, 128, 256), (256, 256, 128))
    tri = jnp.asarray(np.triu(np.ones((t, t), np.float32)), BF16)
    return pl.pallas_call(
        functools.partial(_dsa_kernel, t=t, topk=topk),
        grid=(batch, nq),
        in_specs=[qs[0], kvs[0], kvs[1], qs[1], kvs[2], qs[2], qs[3],
                  pl.BlockSpec((N_HEADS, 2, t, t), lambda b, i: (2, 0, 0, 0)),
                  pl.BlockSpec((t, t), lambda b, i: (0, 0))],
        out_specs=ospec,
        out_shape=jax.ShapeDtypeStruct((batch * seq, GROUP_WIDTH), BF16),
        scratch_shapes=[pltpu.VMEM((nq, t, t), jnp.int32)],
        compiler_params=_cparams(("parallel", "arbitrary")),
        name="dsa_attention",
    )(p["qc"], p["kc"], p["vc"], p["qi"], p["ki"], p["wi"], p["gc"], bias, tri)


def _outproj_kernel(x_ref, ya_ref, yb_ref, yc_ref, yd_ref, ud_ref, halo_ref, cw_ref, w_ref, fg_ref, o_ref,
                    *, tiles_per_seq, final):
    i = pl.program_id(0)
    tm = x_ref.shape[0]
    u = ud_ref[...]
    halo = halo_ref[...] * jnp.where(i % tiles_per_seq == 0, 0.0, 1.0)
    row8 = lax.broadcasted_iota(jnp.int32, (8, 1), 0)

    def shifted(k):
        r = pltpu.roll(u, k, 0)
        head = jnp.where(row8 < k, pltpu.roll(halo, k, 0), r[0:8])
        return jnp.concatenate([head, r[8:tm]], axis=0)

    cw = cw_ref[...]
    conv = cw[0:1] * shifted(2) + cw[1:2] * shifted(1) + cw[2:3] * u
    yd = (conv * yd_ref[...].astype(F32)).astype(BF16)
    acc = x_ref[...]
    for g, y in enumerate((ya_ref[...], yb_ref[...], yc_ref[...], yd)):
        acc = acc + jnp.dot(y, w_ref[g * GROUP_WIDTH:(g + 1) * GROUP_WIDTH, :], preferred_element_type=F32)
    if final:
        acc = _rms(acc, fg_ref[...])
    o_ref[...] = acc


def _outproj(x2, ya, yb, yc, yd, ud, conv_w, w_out, final_g, seq, final):
    n = x2.shape[0]
    tm = min(PROJ_ROWS, seq)
    tiles_per_seq = seq // tm
    row = lambda w: pl.BlockSpec((tm, w), lambda i: (i, 0))
    full = lambda a: pl.BlockSpec(a.shape, lambda i: (0,) * a.ndim)
    halo = pl.BlockSpec((8, GROUP_WIDTH), lambda i: (jnp.maximum(i * (tm // 8) - 1, 0), 0))
    return pl.pallas_call(
        functools.partial(_outproj_kernel, tiles_per_seq=tiles_per_seq, final=final),
        grid=(n // tm,),
        in_specs=[row(D_MODEL), row(256), row(256), row(256), row(256), row(256), halo,
                  full(conv_w), full(w_out), full(final_g)],
        out_specs=row(D_MODEL),
        out_shape=jax.ShapeDtypeStruct((n, D_MODEL), F32),
        compiler_params=_cparams(("parallel",)),
        name="outproj",
    )(x2, ya, yb, yc, yd, ud, ud, conv_w, w_out, final_g)


def kernel(x, norm_g, w_in, mla_qa_g, mla_w_uq, mla_kva_g, mla_w_ukv, diff_lambda, diff_subln_g, conv_w, w_out,
           rel_bias, final_g):
    batch, seq, _ = x.shape
    depth = w_in.shape[0]
    t = min(ATT_TILE, seq)
    assert seq % t == 0 and t % CHUNK == 0 and seq % min(PROJ_ROWS, seq) == 0

    bidx, far_bucket, mask_add = _position_constants(seq, t)
    cos_t, sin_t = _rope_tables(seq)
    bias = _bias_tiles(rel_bias, jnp.asarray(bidx), jnp.asarray(mask_add), far_bucket, t)
    mask_add = jnp.asarray(mask_add)
    cos_t, sin_t = jnp.asarray(cos_t), jnp.asarray(sin_t)

    x2 = x.reshape(batch * seq, D_MODEL)
    for l in range(depth):
        lambda_init = 0.8 - 0.6 * math.exp(-0.3 * l)
        w1 = _prep_w_in(w_in[l])
        wq1, wq2, wk, wv = _prep_mla_up(mla_w_uq[l], mla_w_ukv[l])
        p = _inproj(x2, norm_g[l][None, :], w1, mla_qa_g[l][None, :], wq1, wq2, mla_kva_g[l][None, :], wk, wv,
                    cos_t, sin_t, seq)
        ya = _mla(p, mask_add, batch, seq, t)
        yb = _diff(p, bias, diff_lambda[l], diff_subln_g[l], lambda_init, batch, seq, t)
        yc = _dsa(p, bias, batch, seq, t)
        x2 = _outproj(x2, ya, yb, yc, p["yd"], p["ud"], conv_w[l], w_out[l].astype(BF16), final_g[None, :],
                      seq, final=(l == depth - 1))
    return x2.reshape(batch, seq, D_MODEL)
```

```python
import functools
import math

import numpy as np
import jax
import jax.numpy as jnp
from jax import lax
from jax.experimental import pallas as pl
from jax.experimental.pallas import tpu as pltpu

F32 = jnp.float32
BF16 = jnp.bfloat16

D_MODEL = 1024
CHUNK = 64
N_HEADS = 4
HEAD_DIM = 64
GROUP_WIDTH = N_HEADS * HEAD_DIM
MLA_Q_LORA = 256
MLA_KV_LORA = 128
MLA_NOPE = 64
MLA_ROPE = 32
ROPE_BASE = 10000.0
DIFF_QK = 32
IDX_HEADS = 8
IDX_DIM = 32
TOPK_MAX = 256
CONV_WIDTH = 3
NUM_BUCKETS = 32
MAX_DISTANCE = 128
N_DIFF_MAPS = 2 * N_HEADS
N_BIAS_MAPS = N_DIFF_MAPS + N_HEADS
NORM_EPS = 1e-6
NEG = -1e30
NEG_BITS = int(np.float32(NEG).view(np.int32))
INT_MIN = -2 ** 31

LANES = 128
ATT_TILE = 256
PROJ_ROWS = 512
VMEM_LIMIT = 56 * 1024 * 1024

_IN_SPLITS = (
    ("a_cq", 256), ("a_ckv", 128), ("a_krope", 32), ("a_gate", 256),
    ("b_q", 256), ("b_k", 256), ("b_v", 256), ("b_gate", 256),
    ("c_q", 256), ("c_k", 256), ("c_v", 256),
    ("c_qidx", 256), ("c_kidx", 32), ("c_widx", 8), ("c_gate", 256),
    ("d_b", 256), ("d_c", 256), ("d_h", 256), ("d_gate", 256),
)

_SEGS = (
    ("cq", 256), ("ckv", 128), ("kr1", 128), ("kr2", 128), ("a_gate", 256),
    ("b_q", 256), ("b_k", 256), ("b_v", 256), ("b_gate", 256),
    ("c_q", 256), ("c_k", 256), ("c_v", 256), ("c_qidx", 256), ("c_kidx4", 128), ("c_widx", 128),
    ("c_gate", 256), ("d_b", 256), ("d_c", 256), ("d_h", 256), ("d_gate", 256),
)
_SEG_OFF = {}
_off = 0
for _name, _w in _SEGS:
    _SEG_OFF[_name] = (_off, _w)
    _off += _w
W1_COLS = _off


def _cparams(sem):
    return pltpu.CompilerParams(dimension_semantics=sem, vmem_limit_bytes=VMEM_LIMIT)


def _split_cols(w):
    out, off = {}, 0
    for name, width in _IN_SPLITS:
        out[name] = w[:, off:off + width]
        off += width
    return out


def _rot_half_cols(w):
    half = MLA_ROPE // 2
    return jnp.concatenate([-w[..., half:], w[..., :half]], axis=-1)


def _prep_w_in(w):
    p = _split_cols(w)
    z = lambda n: jnp.zeros((w.shape[0], n), w.dtype)
    kr1 = jnp.concatenate([z(MLA_NOPE), p["a_krope"], z(LANES - MLA_NOPE - MLA_ROPE)], axis=1)
    kr2 = jnp.concatenate([z(MLA_NOPE), _rot_half_cols(p["a_krope"]), z(LANES - MLA_NOPE - MLA_ROPE)], axis=1)
    segs = {
        "cq": p["a_cq"], "ckv": p["a_ckv"], "kr1": kr1, "kr2": kr2, "a_gate": p["a_gate"],
        "b_q": p["b_q"] * DIFF_QK ** -0.5, "b_k": p["b_k"], "b_v": p["b_v"], "b_gate": p["b_gate"],
        "c_q": p["c_q"] * HEAD_DIM ** -0.5, "c_k": p["c_k"], "c_v": p["c_v"], "c_qidx": p["c_qidx"],
        "c_kidx4": jnp.tile(p["c_kidx"], (1, LANES // IDX_DIM)),
        "c_widx": jnp.concatenate([p["c_widx"], z(LANES - IDX_HEADS)], axis=1),
        "c_gate": p["c_gate"], "d_b": p["d_b"], "d_c": p["d_c"], "d_h": p["d_h"], "d_gate": p["d_gate"],
    }
    return jnp.concatenate([segs[name] for name, _ in _SEGS], axis=1).astype(BF16)


def _prep_mla_up(w_uq, w_ukv):
    scale = (MLA_NOPE + MLA_ROPE) ** -0.5
    wq = w_uq.reshape(MLA_Q_LORA, N_HEADS, MLA_NOPE + MLA_ROPE) * scale
    zq = jnp.zeros((MLA_Q_LORA, N_HEADS, LANES - MLA_NOPE - MLA_ROPE), w_uq.dtype)
    wq1 = jnp.concatenate([wq, zq], axis=-1).reshape(MLA_Q_LORA, N_HEADS * LANES)
    wq2 = jnp.concatenate([jnp.zeros_like(wq[..., :MLA_NOPE]), _rot_half_cols(wq[..., MLA_NOPE:]), zq],
                          axis=-1).reshape(MLA_Q_LORA, N_HEADS * LANES)
    wkv = w_ukv.reshape(MLA_KV_LORA, N_HEADS, MLA_NOPE + HEAD_DIM)
    wk = jnp.concatenate([wkv[..., :MLA_NOPE], jnp.zeros((MLA_KV_LORA, N_HEADS, LANES - MLA_NOPE), w_ukv.dtype)],
                         axis=-1).reshape(MLA_KV_LORA, N_HEADS * LANES)
    wv = wkv[..., MLA_NOPE:].reshape(MLA_KV_LORA, N_HEADS * HEAD_DIM)
    return wq1.astype(BF16), wq2.astype(BF16), wk.astype(BF16), wv.astype(BF16)


def _rope_tables(seq):
    half = MLA_ROPE // 2
    inv_freq = ROPE_BASE ** (-np.arange(half, dtype=np.float32) / half)
    ang = np.arange(seq, dtype=np.float32)[:, None] * inv_freq[None, :].astype(np.float32)
    cos, sin = np.cos(ang).astype(np.float32), np.sin(ang).astype(np.float32)
    ct = np.zeros((seq, LANES), np.float32)
    st = np.zeros((seq, LANES), np.float32)
    ct[:, :MLA_NOPE] = 1.0
    ct[:, MLA_NOPE:MLA_NOPE + half] = cos
    ct[:, MLA_NOPE + half:MLA_NOPE + MLA_ROPE] = cos
    st[:, MLA_NOPE:MLA_NOPE + half] = sin
    st[:, MLA_NOPE + half:MLA_NOPE + MLA_ROPE] = sin
    return ct, st


def _rel_bucket_np(rel):
    nb = NUM_BUCKETS // 2
    max_exact = nb // 2
    ret = np.where(rel > 0, nb, 0)
    n = np.abs(rel)
    nf = np.maximum(n, max_exact).astype(np.float32)
    large = max_exact + (np.log(nf / np.float32(max_exact)) / np.float32(math.log(MAX_DISTANCE / max_exact))
                         * np.float32(nb - max_exact)).astype(np.int32)
    large = np.minimum(large, nb - 1)
    return (ret + np.where(n < max_exact, n, large)).astype(np.int32)


def _position_constants(seq, tile):
    r = np.arange(tile)
    rel0 = r[None, :] - r[:, None]
    bidx = np.stack([_rel_bucket_np(rel0 - tile), _rel_bucket_np(rel0)]).astype(np.int32)
    far = _rel_bucket_np(np.arange(-(seq - 1), -tile))
    far_bucket = int(far[0]) if far.size else int(_rel_bucket_np(np.array([-tile - 1]))[0])
    assert far.size == 0 or np.all(far == far_bucket)
    mask_add = np.where((r[None, :] // CHUNK) <= (r[:, None] // CHUNK), 0.0, NEG).astype(np.float32)
    return bidx, far_bucket, mask_add


def _bias_kernel(tab_ref, bidx_ref, madd_ref, o_ref, *, far_bucket):
    j = pl.program_id(0)
    c = tab_ref[j, far_bucket]
    for d in range(2):
        idx = bidx_ref[d]
        acc = jnp.zeros(idx.shape, F32)
        for b in range(NUM_BUCKETS):
            acc = jnp.where(idx == b, tab_ref[j, b] - c, acc)
        if d == 1:
            acc = acc + madd_ref[...]
        o_ref[0, d] = acc


def _bias_tiles(rel_bias, bidx, mask_add, far_bucket, tile):
    return pl.pallas_call(
        functools.partial(_bias_kernel, far_bucket=far_bucket),
        grid=(N_BIAS_MAPS,),
        in_specs=[
            pl.BlockSpec(memory_space=pltpu.SMEM),
            pl.BlockSpec((2, tile, tile), lambda j: (0, 0, 0)),
            pl.BlockSpec((tile, tile), lambda j: (0, 0)),
        ],
        out_specs=pl.BlockSpec((1, 2, tile, tile), lambda j: (j, 0, 0, 0)),
        out_shape=jax.ShapeDtypeStruct((N_BIAS_MAPS, 2, tile, tile), F32),
        compiler_params=_cparams(("arbitrary",)),
        name="bias_tiles",
    )(rel_bias.T.astype(F32), bidx, mask_add)


def _silu(x):
    return x * (1.0 / (1.0 + jnp.exp(-x)))


def _rms(x, g):
    return x * lax.rsqrt(jnp.mean(x * x, axis=-1, keepdims=True) + NORM_EPS) * g


def _inproj_kernel(x_ref, g_ref, w_ref, qag_ref, wq1_ref, wq2_ref, kvg_ref, wk_ref, wv_ref, cos_ref, sin_ref,
                   qa_ref, ka_ref, va_ref, ga_ref, qb_ref, kb_ref, vb_ref, gb_ref,
                   qc_ref, kc_ref, vc_ref, qi_ref, ki_ref, wi_ref, gc_ref, yd_ref, ud_ref):
    h = _rms(x_ref[...], g_ref[...]).astype(BF16)

    def seg(name):
        off, width = _SEG_OFF[name]
        return jnp.dot(h, w_ref[:, off:off + width], preferred_element_type=F32)

    cos = cos_ref[...]
    sin = sin_ref[...]

    cqn = _rms(seg("cq"), qag_ref[...]).astype(BF16)
    q1 = jnp.dot(cqn, wq1_ref[...], preferred_element_type=F32)
    q2 = jnp.dot(cqn, wq2_ref[...], preferred_element_type=F32)
    kr = seg("kr1") * cos + seg("kr2") * sin
    ckvn = _rms(seg("ckv"), kvg_ref[...]).astype(BF16)
    kn = jnp.dot(ckvn, wk_ref[...], preferred_element_type=F32)
    for hd in range(N_HEADS):
        sl = slice(hd * LANES, (hd + 1) * LANES)
        qa_ref[:, sl] = (q1[:, sl] * cos + q2[:, sl] * sin).astype(BF16)
        ka_ref[:, sl] = (kn[:, sl] + kr).astype(BF16)
    va_ref[...] = jnp.dot(ckvn, wv_ref[...], preferred_element_type=F32).astype(BF16)
    ga_ref[...] = _silu(seg("a_gate")).astype(BF16)

    qb_ref[...] = seg("b_q").astype(BF16)
    kb_ref[...] = seg("b_k").astype(BF16)
    vb_ref[...] = seg("b_v").astype(BF16)
    gb_ref[...] = _silu(seg("b_gate")).astype(BF16)

    qc_ref[...] = seg("c_q").astype(BF16)
    kc_ref[...] = seg("c_k").astype(BF16)
    vc_ref[...] = seg("c_v").astype(BF16)
    qi_ref[...] = seg("c_qidx").astype(BF16)
    ki_ref[...] = seg("c_kidx4").astype(BF16)
    wi_ref[...] = seg("c_widx")
    gc_ref[...] = _silu(seg("c_gate")).astype(BF16)

    yd_ref[...] = (seg("d_b") * _silu(seg("d_gate"))).astype(BF16)
    ud_ref[...] = seg("d_c") * seg("d_h")


def _inproj(x2, g, w1, qag, wq1, wq2, kvg, wk, wv, cos_t, sin_t, seq):
    n = x2.shape[0]
    tm = min(PROJ_ROWS, seq)
    tiles_per_seq = seq // tm
    row = lambda w: pl.BlockSpec((tm, w), lambda i: (i, 0))
    full = lambda a: pl.BlockSpec(a.shape, lambda i: (0,) * a.ndim)
    tab = pl.BlockSpec((tm, LANES), lambda i: (i % tiles_per_seq, 0))
    widths = (("qa", 512, BF16), ("ka", 512, BF16), ("va", 256, BF16), ("ga", 256, BF16),
              ("qb", 256, BF16), ("kb", 256, BF16), ("vb", 256, BF16), ("gb", 256, BF16),
              ("qc", 256, BF16), ("kc", 256, BF16), ("vc", 256, BF16), ("qi", 256, BF16),
              ("ki", 128, BF16), ("wi", 128, F32), ("gc", 256, BF16), ("yd", 256, BF16), ("ud", 256, F32))
    outs = pl.pallas_call(
        _inproj_kernel,
        grid=(n // tm,),
        in_specs=[row(D_MODEL), full(g), full(w1), full(qag), full(wq1), full(wq2), full(kvg), full(wk), full(wv),
                  tab, tab],
        out_specs=[row(w) for _, w, _ in widths],
        out_shape=[jax.ShapeDtypeStruct((n, w), dt) for _, w, dt in widths],
        compiler_params=_cparams(("parallel",)),
        name="inproj",
    )(x2, g, w1, qag, wq1, wq2, kvg, wk, wv, cos_t, sin_t)
    return {name: o for (name, _, _), o in zip(widths, outs)}


_NT = (((1,), (1,)), ((), ()))


def _rows(kt, t):
    return pl.ds(pl.multiple_of(kt * t, t), t)


def _flash_scratch(n_maps, t):
    return [pltpu.VMEM((n_maps, t, 1), F32), pltpu.VMEM((n_maps, t, 1), F32), pltpu.VMEM((n_maps, t, LANES), F32)]


def _flash_update(j, s, v, state, first):
    m_ref, l_ref, acc_ref = state
    m_tile = jnp.max(s, axis=-1, keepdims=True)
    if first:
        p = jnp.exp(s - m_tile)
        m_ref[j] = m_tile
        l_ref[j] = jnp.sum(p, axis=-1, keepdims=True)
        acc_ref[j] = jnp.dot(p.astype(BF16), v, preferred_element_type=F32)
    else:
        m_old = m_ref[j]
        m_new = jnp.maximum(m_old, m_tile)
        alpha = jnp.exp(m_old - m_new)
        p = jnp.exp(s - m_new)
        m_ref[j] = m_new
        l_ref[j] = alpha * l_ref[j] + jnp.sum(p, axis=-1, keepdims=True)
        acc_ref[j] = alpha * acc_ref[j] + jnp.dot(p.astype(BF16), v, preferred_element_type=F32)


def _flash_sweep(i, t, n_maps, qs, kv, state, diag_add, near_add=None, mask=None):
    def step(kt, n_tiles, extra, first=False):
        rows = pl.ds(pl.multiple_of(kt * t, t), n_tiles * t)
        shared = None
        if mask is not None:
            shared = mask(kt) if n_tiles == 1 else jnp.concatenate([mask(kt + d) for d in range(n_tiles)], axis=-1)
        for j in range(n_maps):
            k, v = kv(j, rows)
            s = lax.dot_general(qs[j], k, _NT, preferred_element_type=F32)
            add = shared
            if extra is not None:
                add = extra(j) if add is None else add + extra(j)
            if add is not None:
                s = s + add
            _flash_update(j, s, v, state, first)

    step(i, 1, diag_add, first=True)
    n_far = i
    if near_add is not None:
        pl.when(i >= 1)(lambda: step(i - 1, 1, near_add))
        n_far = jnp.maximum(i - 1, 0)

    def pair_body(c, carry):
        step(2 * c, 2, None)
        return carry

    lax.fori_loop(0, n_far // 2, pair_body, 0)
    pl.when(n_far % 2 == 1)(lambda: step(n_far - 1, 1, None))
    _, l_ref, acc_ref = state
    return [acc_ref[j] / l_ref[j] for j in range(n_maps)]


def _lane_mask(lo, hi, dtype):
    lane = lax.broadcasted_iota(jnp.int32, (1, LANES), 1)
    return ((lane >= lo) & (lane < hi)).astype(dtype)


def _merge_pairs(outs):
    lane = lax.broadcasted_iota(jnp.int32, (1, LANES), 1)
    lo = lane < HEAD_DIM
    return jnp.concatenate([jnp.where(lo, outs[0], outs[1]), jnp.where(lo, outs[2], outs[3])], axis=-1)


def _mla_kernel(q_ref, k_ref, v_ref, g_ref, madd_ref, o_ref, *state, t):
    i = pl.program_id(1)
    qs = [q_ref[:, hd * LANES:(hd + 1) * LANES] for hd in range(N_HEADS)]

    def kv(hd, rows):
        pair = slice((hd // 2) * LANES, (hd // 2 + 1) * LANES)
        return k_ref[rows, hd * LANES:(hd + 1) * LANES], v_ref[rows, pair]

    outs = _flash_sweep(i, t, N_HEADS, qs, kv, state, lambda hd: madd_ref[...])
    o_ref[...] = (_merge_pairs(outs) * g_ref[...].astype(F32)).astype(BF16)


def _att_specs(t, seq, widths_q, widths_kv):
    nq = seq // t
    qspec = lambda w: pl.BlockSpec((t, w), lambda b, i: (b * nq + i, 0))
    kvspec = lambda w: pl.BlockSpec((seq, w), lambda b, i: (b, 0))
    return [qspec(w) for w in widths_q], [kvspec(w) for w in widths_kv], qspec(GROUP_WIDTH)


def _mla(p, mask_add, batch, seq, t):
    qs, kvs, ospec = _att_specs(t, seq, (512, 256), (512, 256))
    return pl.pallas_call(
        functools.partial(_mla_kernel, t=t),
        grid=(batch, seq // t),
        in_specs=[qs[0], kvs[0], kvs[1], qs[1], pl.BlockSpec((t, t), lambda b, i: (0, 0))],
        out_specs=ospec,
        out_shape=jax.ShapeDtypeStruct((batch * seq, GROUP_WIDTH), BF16),
        scratch_shapes=_flash_scratch(N_HEADS, t),
        compiler_params=_cparams(("parallel", "arbitrary")),
        name="mla_attention",
    )(p["qa"], p["ka"], p["va"], p["ga"], mask_add)


def _diff_kernel(q_ref, k_ref, v_ref, g_ref, bias_ref, lam_ref, subg_ref, o_ref, *state, t, lambda_init):
    i = pl.program_id(1)
    lp = lam_ref[...]
    lam = (jnp.exp(jnp.sum(lp[0:1] * lp[1:2], axis=-1, keepdims=True))
           - jnp.exp(jnp.sum(lp[2:3] * lp[3:4], axis=-1, keepdims=True)) + lambda_init)
    subg = subg_ref[...]
    qs = [q_ref[:, (j // 4) * LANES:(j // 4 + 1) * LANES] * _lane_mask((j % 4) * DIFF_QK, (j % 4 + 1) * DIFF_QK, BF16)
          for j in range(N_DIFF_MAPS)]

    def kv(j, rows):
        grp = slice((j // 4) * LANES, (j // 4 + 1) * LANES)
        return k_ref[rows, grp], v_ref[rows, grp]

    maps = _flash_sweep(i, t, N_DIFF_MAPS, qs, kv, state, lambda j: bias_ref[j, 1], lambda j: bias_ref[j, 0])
    outs = []
    for hd in range(N_HEADS):
        o = maps[2 * hd] - lam * maps[2 * hd + 1]
        hm = _lane_mask((hd % 2) * HEAD_DIM, (hd % 2 + 1) * HEAD_DIM, F32)
        ms = jnp.sum(o * o * hm, axis=-1, keepdims=True) * (1.0 / HEAD_DIM)
        outs.append(o * lax.rsqrt(ms + NORM_EPS) * subg * (1.0 - lambda_init))
    o_ref[...] = (_merge_pairs(outs) * g_ref[...].astype(F32)).astype(BF16)


def _diff(p, bias, lam_params, subln_g, lambda_init, batch, seq, t):
    qs, kvs, ospec = _att_specs(t, seq, (256, 256), (256, 256))
    subg = jnp.tile(subln_g.astype(F32), LANES // HEAD_DIM)[None, :]
    return pl.pallas_call(
        functools.partial(_diff_kernel, t=t, lambda_init=lambda_init),
        grid=(batch, seq // t),
        in_specs=[qs[0], kvs[0], kvs[1], qs[1],
                  pl.BlockSpec((N_DIFF_MAPS, 2, t, t), lambda b, i: (0, 0, 0, 0)),
                  pl.BlockSpec((4, DIFF_QK), lambda b, i: (0, 0)),
                  pl.BlockSpec((1, LANES), lambda b, i: (0, 0))],
        out_specs=ospec,
        out_shape=jax.ShapeDtypeStruct((batch * seq, GROUP_WIDTH), BF16),
        scratch_shapes=_flash_scratch(N_DIFF_MAPS, t),
        compiler_params=_cparams(("parallel", "arbitrary")),
        name="diff_attention",
    )(p["qb"], p["kb"], p["vb"], p["gb"], bias, lam_params.astype(F32), subg)


def _sort_key(x):
    b = lax.bitcast_convert_type(x + 0.0, jnp.int32)
    return b ^ ((b >> 31) & 0x7FFFFFFF)


def _dsa_kernel(q_ref, k_ref, v_ref, qi_ref, ki_ref, wi_ref, g_ref, bias_ref, tri_ref, o_ref, key_ref, *state,
                t, topk):
    i = pl.program_id(1)
    n_kt = i + 1
    idx_scale = (IDX_HEADS ** -0.5) * (IDX_DIM ** -0.5)

    wi = wi_ref[...] * idx_scale
    qm, wcol = [], []
    for ih in range(IDX_HEADS):
        grp = slice((ih // 4) * LANES, (ih // 4 + 1) * LANES)
        lo = (ih % 4) * IDX_DIM
        qm.append(qi_ref[:, grp] * _lane_mask(lo, lo + IDX_DIM, BF16))
        wcol.append(wi[:, ih:ih + 1])

    def idx_tile(kt):
        kk = ki_ref[_rows(kt, t), :]
        tot = jnp.zeros((t, t), F32)
        for ih in range(IDX_HEADS):
            y = lax.dot_general(qm[ih], kk, _NT, preferred_element_type=F32)
            tot = tot + wcol[ih] * jnp.maximum(y, 0.0)
        return _sort_key(tot)

    def key_body(kt, c):
        key_ref[kt] = idx_tile(kt)
        return c

    lax.fori_loop(0, i, key_body, 0)
    rr = lax.broadcasted_iota(jnp.int32, (t, t), 0) // CHUNK
    cc = lax.broadcasted_iota(jnp.int32, (t, t), 1) // CHUNK
    key_ref[i] = jnp.where(cc <= rr, idx_tile(i), INT_MIN)

    def count(pred):
        def body(kt, c):
            hit = jnp.where(pred(key_ref[kt]), 1.0, 0.0)
            part = hit[:, 0:LANES]
            for lt in range(1, t // LANES):
                part = part + hit[:, lt * LANES:(lt + 1) * LANES]
            return c + part
        return jnp.sum(lax.fori_loop(0, n_kt, body, jnp.zeros((t, LANES), F32)), axis=-1, keepdims=True)

    def bit_body(b, res):
        cand = res | (jnp.int32(1) << (31 - b))
        cand_s = cand ^ INT_MIN
        cnt = count(lambda key: key >= cand_s)
        return jnp.where(cnt >= topk, cand, res)

    thr = lax.fori_loop(0, 32, bit_body, jnp.zeros((t, 1), jnp.int32)) ^ INT_MIN
    cnt_ge = count(lambda key: key >= thr)
    has_ties = jnp.max(cnt_ge) > topk

    @pl.when(jnp.logical_not(has_ties))
    def _():
        def body(kt, c):
            key_ref[kt] = jnp.where(key_ref[kt] >= thr, 0, NEG_BITS)
            return c
        lax.fori_loop(0, n_kt, body, 0)

    @pl.when(has_ties)
    def _():
        need = topk - count(lambda key: key > thr)

        def body(kt, seen):
            key = key_ref[kt]
            eq = jnp.where(key == thr, 1.0, 0.0)
            rank = seen + jnp.dot(eq.astype(BF16), tri_ref[...], preferred_element_type=F32)
            sel = (key > thr) | ((key == thr) & (rank <= need))
            key_ref[kt] = jnp.where(sel, 0, NEG_BITS)
            return seen + jnp.sum(eq, axis=-1, keepdims=True)
        lax.fori_loop(0, n_kt, body, jnp.zeros((t, 1), F32))

    qs = [q_ref[:, (hd // 2) * LANES:(hd // 2 + 1) * LANES]
          * _lane_mask((hd % 2) * HEAD_DIM, (hd % 2 + 1) * HEAD_DIM, BF16) for hd in range(N_HEADS)]

    def kv(hd, rows):
        pair = slice((hd // 2) * LANES, (hd // 2 + 1) * LANES)
        return k_ref[rows, pair], v_ref[rows, pair]

    outs = _flash_sweep(i, t, N_HEADS, qs, kv, state, lambda hd: bias_ref[hd, 1], lambda hd: bias_ref[hd, 0],
                        mask=lambda kt: lax.bitcast_convert_type(key_ref[kt], F32))
    o_ref[...] = (_merge_pairs(outs) * g_ref[...].astype(F32)).astype(BF16)


def _dsa(p, bias, batch, seq, t):
    topk = min(TOPK_MAX, seq // 4)
    nq = seq // t
    qs, kvs, ospec = _att_specs(t, seq, (256, 256, 128, 256), (256, 256, 128))
    tri = jnp.asarray(np.triu(np.ones((t, t), np.float32)), BF16)
    return pl.pallas_call(
        functools.partial(_dsa_kernel, t=t, topk=topk),
        grid=(batch, nq),
        in_specs=[qs[0], kvs[0], kvs[1], qs[1], kvs[2], qs[2], qs[3],
                  pl.BlockSpec((N_HEADS, 2, t, t), lambda b, i: (2, 0, 0, 0)),
                  pl.BlockSpec((t, t), lambda b, i: (0, 0))],
        out_specs=ospec,
        out_shape=jax.ShapeDtypeStruct((batch * seq, GROUP_WIDTH), BF16),
        scratch_shapes=[pltpu.VMEM((nq, t, t), jnp.int32)] + _flash_scratch(N_HEADS, t),
        compiler_params=_cparams(("parallel", "arbitrary")),
        name="dsa_attention",
    )(p["qc"], p["kc"], p["vc"], p["qi"], p["ki"], p["wi"], p["gc"], bias, tri)


def _outproj_kernel(x_ref, ya_ref, yb_ref, yc_ref, yd_ref, ud_ref, halo_ref, cw_ref, w_ref, fg_ref, o_ref,
                    *, tiles_per_seq, final):
    i = pl.program_id(0)
    tm = x_ref.shape[0]
    u = ud_ref[...]
    halo = halo_ref[...] * jnp.where(i % tiles_per_seq == 0, 0.0, 1.0)
    row8 = lax.broadcasted_iota(jnp.int32, (8, 1), 0)

    def shifted(k):
        r = pltpu.roll(u, k, 0)
        head = jnp.where(row8 < k, pltpu.roll(halo, k, 0), r[0:8])
        return jnp.concatenate([head, r[8:tm]], axis=0)

    cw = cw_ref[...]
    conv = cw[0:1] * shifted(2) + cw[1:2] * shifted(1) + cw[2:3] * u
    yd = (conv * yd_ref[...].astype(F32)).astype(BF16)
    acc = x_ref[...]
    for g, y in enumerate((ya_ref[...], yb_ref[...], yc_ref[...], yd)):
        acc = acc + jnp.dot(y, w_ref[g * GROUP_WIDTH:(g + 1) * GROUP_WIDTH, :], preferred_element_type=F32)
    if final:
        acc = _rms(acc, fg_ref[...])
    o_ref[...] = acc


def _outproj(x2, ya, yb, yc, yd, ud, conv_w, w_out, final_g, seq, final):
    n = x2.shape[0]
    tm = min(PROJ_ROWS, seq)
    tiles_per_seq = seq // tm
    row = lambda w: pl.BlockSpec((tm, w), lambda i: (i, 0))
    full = lambda a: pl.BlockSpec(a.shape, lambda i: (0,) * a.ndim)
    halo = pl.BlockSpec((8, GROUP_WIDTH), lambda i: (jnp.maximum(i * (tm // 8) - 1, 0), 0))
    return pl.pallas_call(
        functools.partial(_outproj_kernel, tiles_per_seq=tiles_per_seq, final=final),
        grid=(n // tm,),
        in_specs=[row(D_MODEL), row(256), row(256), row(256), row(256), row(256), halo,
                  full(conv_w), full(w_out), full(final_g)],
        out_specs=row(D_MODEL),
        out_shape=jax.ShapeDtypeStruct((n, D_MODEL), F32),
        compiler_params=_cparams(("parallel",)),
        name="outproj",
    )(x2, ya, yb, yc, yd, ud, ud, conv_w, w_out, final_g)


def kernel(x, norm_g, w_in, mla_qa_g, mla_w_uq, mla_kva_g, mla_w_ukv, diff_lambda, diff_subln_g, conv_w, w_out,
           rel_bias, final_g):
    batch, seq, _ = x.shape
    depth = w_in.shape[0]
    t = min(ATT_TILE, seq)
    assert seq % t == 0 and t % CHUNK == 0 and seq % min(PROJ_ROWS, seq) == 0

    bidx, far_bucket, mask_add = _position_constants(seq, t)
    cos_t, sin_t = _rope_tables(seq)
    bias = _bias_tiles(rel_bias, jnp.asarray(bidx), jnp.asarray(mask_add), far_bucket, t)
    mask_add = jnp.asarray(mask_add)
    cos_t, sin_t = jnp.asarray(cos_t), jnp.asarray(sin_t)

    x2 = x.reshape(batch * seq, D_MODEL)
    for l in range(depth):
        lambda_init = 0.8 - 0.6 * math.exp(-0.3 * l)
        w1 = _prep_w_in(w_in[l])
        wq1, wq2, wk, wv = _prep_mla_up(mla_w_uq[l], mla_w_ukv[l])
        p = _inproj(x2, norm_g[l][None, :], w1, mla_qa_g[l][None, :], wq1, wq2, mla_kva_g[l][None, :], wk, wv,
                    cos_t, sin_t, seq)
        ya = _mla(p, mask_add, batch, seq, t)
        yb = _diff(p, bias, diff_lambda[l], diff_subln_g[l], lambda_init, batch, seq, t)
        yc = _dsa(p, bias, batch, seq, t)
        x2 = _outproj(x2, ya, yb, yc, p["yd"], p["ud"], conv_w[l], w_out[l].astype(BF16), final_g[None, :],
                      seq, final=(l == depth - 1))
    return x2.reshape(batch, seq, D_MODEL)
```

```python
import functools
import math

import numpy as np
import jax
import jax.numpy as jnp
from jax import lax
from jax.experimental import pallas as pl
from jax.experimental.pallas import tpu as pltpu

F32 = jnp.float32
BF16 = jnp.bfloat16

D_MODEL = 1024
CHUNK = 64
N_HEADS = 4
HEAD_DIM = 64
GROUP_WIDTH = N_HEADS * HEAD_DIM
MLA_Q_LORA = 256
MLA_KV_LORA = 128
MLA_NOPE = 64
MLA_ROPE = 32
ROPE_BASE = 10000.0
DIFF_QK = 32
IDX_HEADS = 8
IDX_DIM = 32
TOPK_MAX = 256
CONV_WIDTH = 3
NUM_BUCKETS = 32
MAX_DISTANCE = 128
N_DIFF_MAPS = 2 * N_HEADS
N_BIAS_MAPS = N_DIFF_MAPS + N_HEADS
NORM_EPS = 1e-6
NEG = -1e30
NEG_BITS = int(np.float32(NEG).view(np.int32))
INT_MIN = -2 ** 31

LANES = 128
SUBLANES = 8
ATT_TILE = 256
PROJ_ROWS = 512
FLASH_LOOKAHEAD = 4
VMEM_LIMIT = 56 * 1024 * 1024

_IN_SPLITS = (
    ("a_cq", 256), ("a_ckv", 128), ("a_krope", 32), ("a_gate", 256),
    ("b_q", 256), ("b_k", 256), ("b_v", 256), ("b_gate", 256),
    ("c_q", 256), ("c_k", 256), ("c_v", 256),
    ("c_qidx", 256), ("c_kidx", 32), ("c_widx", 8), ("c_gate", 256),
    ("d_b", 256), ("d_c", 256), ("d_h", 256), ("d_gate", 256),
)

_SEGS = (
    ("cq", 256), ("ckv", 128), ("kr1", 128), ("kr2", 128), ("a_gate", 256),
    ("b_q", 256), ("b_k", 256), ("b_gate", 256),
    ("c_q", 256), ("c_k", 256), ("c_qidx", 256), ("c_kidx4", 128), ("c_widx", 128),
    ("c_gate", 256), ("d_b", 256), ("d_c", 256), ("d_h", 256), ("d_gate", 256),
)
_SEG_OFF = {}
_off = 0
for _name, _w in _SEGS:
    _SEG_OFF[_name] = (_off, _w)
    _off += _w
W1_COLS = _off

_NT = (((1,), (1,)), ((), ()))


def _cparams(sem):
    return pltpu.CompilerParams(dimension_semantics=sem, vmem_limit_bytes=VMEM_LIMIT)


def _split_cols(w):
    out, off = {}, 0
    for name, width in _IN_SPLITS:
        out[name] = w[:, off:off + width]
        off += width
    return out


def _rot_half_cols(w):
    half = MLA_ROPE // 2
    return jnp.concatenate([-w[..., half:], w[..., :half]], axis=-1)


def _prep_w_in(w):
    p = _split_cols(w)
    z = lambda n: jnp.zeros((w.shape[0], n), w.dtype)
    kr1 = jnp.concatenate([z(MLA_NOPE), p["a_krope"], z(LANES - MLA_NOPE - MLA_ROPE)], axis=1)
    kr2 = jnp.concatenate([z(MLA_NOPE), _rot_half_cols(p["a_krope"]), z(LANES - MLA_NOPE - MLA_ROPE)], axis=1)
    segs = {
        "cq": p["a_cq"], "ckv": p["a_ckv"], "kr1": kr1, "kr2": kr2, "a_gate": p["a_gate"],
        "b_q": p["b_q"] * DIFF_QK ** -0.5, "b_k": p["b_k"], "b_gate": p["b_gate"],
        "c_q": p["c_q"] * HEAD_DIM ** -0.5, "c_k": p["c_k"], "c_qidx": p["c_qidx"],
        "c_kidx4": jnp.tile(p["c_kidx"], (1, LANES // IDX_DIM)),
        "c_widx": jnp.concatenate([p["c_widx"], z(LANES - IDX_HEADS)], axis=1),
        "c_gate": p["c_gate"], "d_b": p["d_b"], "d_c": p["d_c"], "d_h": p["d_h"], "d_gate": p["d_gate"],
    }
    w1 = jnp.concatenate([segs[name] for name, _ in _SEGS], axis=1).astype(BF16)
    wvt = jnp.concatenate([p["b_v"], p["c_v"]], axis=1).T.astype(BF16)
    return w1, wvt


def _prep_mla_up(w_uq, w_ukv):
    scale = (MLA_NOPE + MLA_ROPE) ** -0.5
    wq = w_uq.reshape(MLA_Q_LORA, N_HEADS, MLA_NOPE + MLA_ROPE) * scale
    zq = jnp.zeros((MLA_Q_LORA, N_HEADS, LANES - MLA_NOPE - MLA_ROPE), w_uq.dtype)
    wq1 = jnp.concatenate([wq, zq], axis=-1).reshape(MLA_Q_LORA, N_HEADS * LANES)
    wq2 = jnp.concatenate([jnp.zeros_like(wq[..., :MLA_NOPE]), _rot_half_cols(wq[..., MLA_NOPE:]), zq],
                          axis=-1).reshape(MLA_Q_LORA, N_HEADS * LANES)
    wkv = w_ukv.reshape(MLA_KV_LORA, N_HEADS, MLA_NOPE + HEAD_DIM)
    wk = jnp.concatenate([wkv[..., :MLA_NOPE], jnp.zeros((MLA_KV_LORA, N_HEADS, LANES - MLA_NOPE), w_ukv.dtype)],
                         axis=-1).reshape(MLA_KV_LORA, N_HEADS * LANES)
    wvt = wkv[..., MLA_NOPE:].reshape(MLA_KV_LORA, N_HEADS * HEAD_DIM).T
    return wq1.astype(BF16), wq2.astype(BF16), wk.astype(BF16), wvt.astype(BF16)


def _rope_tables(seq):
    half = MLA_ROPE // 2
    inv_freq = ROPE_BASE ** (-np.arange(half, dtype=np.float32) / half)
    ang = np.arange(seq, dtype=np.float32)[:, None] * inv_freq[None, :].astype(np.float32)
    cos, sin = np.cos(ang).astype(np.float32), np.sin(ang).astype(np.float32)
    ct = np.zeros((seq, LANES), np.float32)
    st = np.zeros((seq, LANES), np.float32)
    ct[:, :MLA_NOPE] = 1.0
    ct[:, MLA_NOPE:MLA_NOPE + half] = cos
    ct[:, MLA_NOPE + half:MLA_NOPE + MLA_ROPE] = cos
    st[:, MLA_NOPE:MLA_NOPE + half] = sin
    st[:, MLA_NOPE + half:MLA_NOPE + MLA_ROPE] = sin
    return ct, st


def _rel_bucket_np(rel):
    nb = NUM_BUCKETS // 2
    max_exact = nb // 2
    ret = np.where(rel > 0, nb, 0)
    n = np.abs(rel)
    nf = np.maximum(n, max_exact).astype(np.float32)
    large = max_exact + (np.log(nf / np.float32(max_exact)) / np.float32(math.log(MAX_DISTANCE / max_exact))
                         * np.float32(nb - max_exact)).astype(np.int32)
    large = np.minimum(large, nb - 1)
    return (ret + np.where(n < max_exact, n, large)).astype(np.int32)


def _position_constants(seq, tile):
    r = np.arange(tile)
    rel0 = r[:, None] - r[None, :]
    bidx = np.stack([_rel_bucket_np(rel0 - tile), _rel_bucket_np(rel0)]).astype(np.int32)
    far = _rel_bucket_np(np.arange(-(seq - 1), -tile))
    far_bucket = int(far[0]) if far.size else int(_rel_bucket_np(np.array([-tile - 1]))[0])
    assert far.size == 0 or np.all(far == far_bucket)
    mask_add = np.where((r[:, None] // CHUNK) <= (r[None, :] // CHUNK), 0.0, NEG).astype(np.float32)
    return bidx, far_bucket, mask_add


def _bias_kernel(tab_ref, bidx_ref, madd_ref, o_ref, *, far_bucket):
    j = pl.program_id(0)
    c = tab_ref[j, far_bucket]
    for d in range(2):
        idx = bidx_ref[d]
        acc = jnp.zeros(idx.shape, F32)
        for b in range(NUM_BUCKETS):
            acc = jnp.where(idx == b, tab_ref[j, b] - c, acc)
        if d == 1:
            acc = acc + madd_ref[...]
        o_ref[0, d] = acc


def _bias_tiles(rel_bias, bidx, mask_add, far_bucket, tile):
    return pl.pallas_call(
        functools.partial(_bias_kernel, far_bucket=far_bucket),
        grid=(N_BIAS_MAPS,),
        in_specs=[
            pl.BlockSpec(memory_space=pltpu.SMEM),
            pl.BlockSpec((2, tile, tile), lambda j: (0, 0, 0)),
            pl.BlockSpec((tile, tile), lambda j: (0, 0)),
        ],
        out_specs=pl.BlockSpec((1, 2, tile, tile), lambda j: (j, 0, 0, 0)),
        out_shape=jax.ShapeDtypeStruct((N_BIAS_MAPS, 2, tile, tile), F32),
        compiler_params=_cparams(("arbitrary",)),
        name="bias_tiles",
    )(rel_bias.T.astype(F32), bidx, mask_add)


def _silu(x):
    return x * (1.0 / (1.0 + jnp.exp(-x)))


def _rms(x, g):
    return x * lax.rsqrt(jnp.mean(x * x, axis=-1, keepdims=True) + NORM_EPS) * g


def _inproj_kernel(x_ref, g_ref, w_ref, wvt_ref, qag_ref, wq1_ref, wq2_ref, kvg_ref, wk_ref, wavt_ref, cos_ref, sin_ref,
                   qa_ref, ka_ref, vat_ref, ga_ref, qb_ref, kb_ref, vbt_ref, gb_ref,
                   qc_ref, kc_ref, vct_ref, qi_ref, ki_ref, wi_ref, gc_ref, yd_ref, ud_ref, *, t):
    h = _rms(x_ref[...], g_ref[...]).astype(BF16)

    def seg(name):
        off, width = _SEG_OFF[name]
        return jnp.dot(h, w_ref[:, off:off + width], preferred_element_type=F32)

    cos = cos_ref[...]
    sin = sin_ref[...]

    cqn = _rms(seg("cq"), qag_ref[...]).astype(BF16)
    q1 = jnp.dot(cqn, wq1_ref[...], preferred_element_type=F32)
    q2 = jnp.dot(cqn, wq2_ref[...], preferred_element_type=F32)
    kr = seg("kr1") * cos + seg("kr2") * sin
    ckvn = _rms(seg("ckv"), kvg_ref[...]).astype(BF16)
    kn = jnp.dot(ckvn, wk_ref[...], preferred_element_type=F32)
    for hd in range(N_HEADS):
        sl = slice(hd * LANES, (hd + 1) * LANES)
        qa_ref[:, sl] = (q1[:, sl] * cos + q2[:, sl] * sin).astype(BF16)
        ka_ref[:, sl] = (kn[:, sl] + kr).astype(BF16)
    ga_ref[...] = _silu(seg("a_gate")).astype(BF16)

    for c in range(x_ref.shape[0] // t):
        rows = slice(c * t, (c + 1) * t)
        vat_ref[c] = lax.dot_general(wavt_ref[...], ckvn[rows], _NT, preferred_element_type=F32).astype(BF16)
        vt = lax.dot_general(wvt_ref[...], h[rows], _NT, preferred_element_type=F32).astype(BF16)
        vbt_ref[c] = vt[:GROUP_WIDTH]
        vct_ref[c] = vt[GROUP_WIDTH:]

    qb_ref[...] = seg("b_q").astype(BF16)
    kb_ref[...] = seg("b_k").astype(BF16)
    gb_ref[...] = _silu(seg("b_gate")).astype(BF16)

    qc_ref[...] = seg("c_q").astype(BF16)
    kc_ref[...] = seg("c_k").astype(BF16)
    qi_ref[...] = seg("c_qidx").astype(BF16)
    ki_ref[...] = seg("c_kidx4").astype(BF16)
    wi_ref[...] = seg("c_widx")
    gc_ref[...] = _silu(seg("c_gate")).astype(BF16)

    yd_ref[...] = (seg("d_b") * _silu(seg("d_gate"))).astype(BF16)
    ud_ref[...] = seg("d_c") * seg("d_h")


def _inproj(x2, g, w1, wvt, qag, wq1, wq2, kvg, wk, wavt, cos_t, sin_t, seq, t):
    n = x2.shape[0]
    tm = min(PROJ_ROWS, seq)
    tiles_per_seq = seq // tm
    row = lambda w: pl.BlockSpec((tm, w), lambda i: (i, 0))
    full = lambda a: pl.BlockSpec(a.shape, lambda i: (0,) * a.ndim)
    tab = pl.BlockSpec((tm, LANES), lambda i: (i % tiles_per_seq, 0))
    vt_spec = pl.BlockSpec((tm // t, GROUP_WIDTH, t), lambda i: (i, 0, 0))
    vt_shape = jax.ShapeDtypeStruct((n // t, GROUP_WIDTH, t), BF16)
    outs = (("qa", 512, BF16), ("ka", 512, BF16), ("vat", None, None), ("ga", 256, BF16),
            ("qb", 256, BF16), ("kb", 256, BF16), ("vbt", None, None), ("gb", 256, BF16),
            ("qc", 256, BF16), ("kc", 256, BF16), ("vct", None, None), ("qi", 256, BF16),
            ("ki", 128, BF16), ("wi", 128, F32), ("gc", 256, BF16), ("yd", 256, BF16), ("ud", 256, F32))
    res = pl.pallas_call(
        functools.partial(_inproj_kernel, t=t),
        grid=(n // tm,),
        in_specs=[row(D_MODEL), full(g), full(w1), full(wvt), full(qag), full(wq1), full(wq2), full(kvg), full(wk),
                  full(wavt), tab, tab],
        out_specs=[vt_spec if w is None else row(w) for _, w, _ in outs],
        out_shape=[vt_shape if w is None else jax.ShapeDtypeStruct((n, w), dt) for _, w, dt in outs],
        compiler_params=_cparams(("parallel",)),
        name="inproj",
    )(x2, g, w1, wvt, qag, wq1, wq2, kvg, wk, wavt, cos_t, sin_t)
    return {name: o for (name, _, _), o in zip(outs, res)}


def _flash_scratch(n_maps, t):
    return [pltpu.VMEM((n_maps, 1, t), F32), pltpu.VMEM((n_maps, 1, t), F32), pltpu.VMEM((n_maps, HEAD_DIM, t), F32)]


def _flash_update(s, vts, old):
    t = vts[0].shape[1]
    m_tile = jnp.max(s, axis=0, keepdims=True)
    m_new = m_tile if old is None else jnp.maximum(old[0], m_tile)
    p = jnp.exp(s - m_new)
    l_tile = jnp.sum(p, axis=0, keepdims=True)
    pb = p.astype(BF16)
    pv = jnp.dot(vts[0], pb[0:t], preferred_element_type=F32)
    for d in range(1, len(vts)):
        pv = pv + jnp.dot(vts[d], pb[d * t:(d + 1) * t], preferred_element_type=F32)
    if old is None:
        return m_new, l_tile, pv
    alpha = jnp.exp(old[0] - m_new)
    return m_new, alpha * old[1] + l_tile, alpha * old[2] + pv


def _flash_sweep(i, t, n_maps, qs, keys, values, state, diag_add, near_add=None, mask=None):
    m_ref, l_ref, acc_ref = state

    def run(units, first=False):
        cur = [None if first else (m_ref[j], l_ref[j], acc_ref[j]) for j in range(n_maps)]
        shared = {}

        def logits(u, j):
            kt, n_tiles, extra = units[u]
            rows = pl.ds(pl.multiple_of(kt * t, t), n_tiles * t)
            s = lax.dot_general(keys(j, rows), qs[j], _NT, preferred_element_type=F32)
            add = None
            if mask is not None:
                if u not in shared:
                    tiles = [mask(kt + d) for d in range(n_tiles)]
                    shared[u] = tiles[0] if n_tiles == 1 else jnp.concatenate(tiles, axis=0)
                add = shared[u]
            if extra is not None:
                add = extra(j) if add is None else add + extra(j)
            return s if add is None else s + add

        items = [(u, j) for u in range(len(units)) for j in range(n_maps)]
        queue = [logits(*it) for it in items[:FLASH_LOOKAHEAD]]
        for n, (u, j) in enumerate(items):
            if n + FLASH_LOOKAHEAD < len(items):
                queue.append(logits(*items[n + FLASH_LOOKAHEAD]))
            s = queue.pop(0)
            kt, n_tiles, _ = units[u]
            cur[j] = _flash_update(s, [values(j, kt + d) for d in range(n_tiles)], cur[j])
        for j in range(n_maps):
            m_ref[j], l_ref[j], acc_ref[j] = cur[j]

    if near_add is None:
        run([(i, 1, diag_add)], first=True)
        n_far = i
    else:
        pl.when(i == 0)(lambda: run([(i, 1, diag_add)], first=True))
        pl.when(i >= 1)(lambda: run([(i, 1, diag_add), (i - 1, 1, near_add)], first=True))
        n_far = jnp.maximum(i - 1, 0)

    def quad_body(c, carry):
        run([(4 * c, 2, None), (4 * c + 2, 2, None)])
        return carry

    lax.fori_loop(0, n_far // 4, quad_body, 0)
    rem = n_far % 4
    base = n_far - rem
    pl.when(rem >= 2)(lambda: run([(base, 2, None)]))
    pl.when(rem % 2 == 1)(lambda: run([(n_far - 1, 1, None)]))
    return [acc_ref[j] / l_ref[j] for j in range(n_maps)]


def _lane_mask(lo, hi, dtype):
    lane = lax.broadcasted_iota(jnp.int32, (1, LANES), 1)
    return ((lane >= lo) & (lane < hi)).astype(dtype)


def _store_gated(o_ref, g_ref, heads_t):
    o = jnp.concatenate(heads_t, axis=0).T
    o_ref[...] = (o * g_ref[...].astype(F32)).astype(BF16)


def _att_specs(t, seq, widths_q, widths_k):
    nq = seq // t
    qspec = lambda w: pl.BlockSpec((t, w), lambda b, i: (b * nq + i, 0))
    kspec = lambda w: pl.BlockSpec((seq, w), lambda b, i: (b, 0))
    vspec = pl.BlockSpec((nq, GROUP_WIDTH, t), lambda b, i: (b, 0, 0))
    return [qspec(w) for w in widths_q], [kspec(w) for w in widths_k], vspec, qspec(GROUP_WIDTH)


def _mla_kernel(q_ref, k_ref, vt_ref, g_ref, madd_ref, o_ref, *state, t):
    i = pl.program_id(1)
    qs = [q_ref[:, hd * LANES:(hd + 1) * LANES] for hd in range(N_HEADS)]
    keys = lambda hd, rows: k_ref[rows, hd * LANES:(hd + 1) * LANES]
    values = lambda hd, kt: vt_ref[kt, hd * HEAD_DIM:(hd + 1) * HEAD_DIM, :]
    outs = _flash_sweep(i, t, N_HEADS, qs, keys, values, state, lambda hd: madd_ref[...])
    _store_gated(o_ref, g_ref, outs)


def _mla(p, mask_add, batch, seq, t):
    qs, ks, vspec, ospec = _att_specs(t, seq, (512, 256), (512,))
    return pl.pallas_call(
        functools.partial(_mla_kernel, t=t),
        grid=(batch, seq // t),
        in_specs=[qs[0], ks[0], vspec, qs[1], pl.BlockSpec((t, t), lambda b, i: (0, 0))],
        out_specs=ospec,
        out_shape=jax.ShapeDtypeStruct((batch * seq, GROUP_WIDTH), BF16),
        scratch_shapes=_flash_scratch(N_HEADS, t),
        compiler_params=_cparams(("parallel", "arbitrary")),
        name="mla_attention",
    )(p["qa"], p["ka"], p["vat"], p["ga"], mask_add)


def _diff_kernel(q_ref, k_ref, vt_ref, g_ref, bias_ref, lam_ref, subg_ref, o_ref, *state, t, lambda_init):
    i = pl.program_id(1)
    lp = lam_ref[...]
    lam = (jnp.exp(jnp.sum(lp[0:1] * lp[1:2], axis=-1, keepdims=True))
           - jnp.exp(jnp.sum(lp[2:3] * lp[3:4], axis=-1, keepdims=True)) + lambda_init)
    qs = [q_ref[:, (j // 4) * LANES:(j // 4 + 1) * LANES] * _lane_mask((j % 4) * DIFF_QK, (j % 4 + 1) * DIFF_QK, BF16)
          for j in range(N_DIFF_MAPS)]
    keys = lambda j, rows: k_ref[rows, (j // 4) * LANES:(j // 4 + 1) * LANES]
    values = lambda j, kt: vt_ref[kt, (j // 2) * HEAD_DIM:(j // 2 + 1) * HEAD_DIM, :]
    maps = _flash_sweep(i, t, N_DIFF_MAPS, qs, keys, values, state, lambda j: bias_ref[j, 1], lambda j: bias_ref[j, 0])
    outs = []
    for hd in range(N_HEADS):
        o = maps[2 * hd] - lam * maps[2 * hd + 1]
        ms = jnp.mean(o * o, axis=0, keepdims=True)
        outs.append(o * lax.rsqrt(ms + NORM_EPS) * subg_ref[...] * (1.0 - lambda_init))
    _store_gated(o_ref, g_ref, outs)


def _diff(p, bias, lam_params, subln_g, lambda_init, batch, seq, t):
    qs, ks, vspec, ospec = _att_specs(t, seq, (256, 256), (256,))
    subg = jnp.broadcast_to(subln_g.astype(F32)[:, None], (HEAD_DIM, t))
    return pl.pallas_call(
        functools.partial(_diff_kernel, t=t, lambda_init=lambda_init),
        grid=(batch, seq // t),
        in_specs=[qs[0], ks[0], vspec, qs[1],
                  pl.BlockSpec((N_DIFF_MAPS, 2, t, t), lambda b, i: (0, 0, 0, 0)),
                  pl.BlockSpec((4, DIFF_QK), lambda b, i: (0, 0)),
                  pl.BlockSpec((HEAD_DIM, t), lambda b, i: (0, 0))],
        out_specs=ospec,
        out_shape=jax.ShapeDtypeStruct((batch * seq, GROUP_WIDTH), BF16),
        scratch_shapes=_flash_scratch(N_DIFF_MAPS, t),
        compiler_params=_cparams(("parallel", "arbitrary")),
        name="diff_attention",
    )(p["qb"], p["kb"], p["vbt"], p["gb"], bias, lam_params.astype(F32), subg)


def _sort_key(x):
    b = lax.bitcast_convert_type(x + 0.0, jnp.int32)
    return b ^ ((b >> 31) & 0x7FFFFFFF)


def _dsa_kernel(q_ref, k_ref, vt_ref, qi_ref, ki_ref, wi_ref, g_ref, bias_ref, tri_ref, o_ref, key_ref, *state,
                t, topk):
    i = pl.program_id(1)
    n_kt = i + 1
    idx_scale = (IDX_HEADS ** -0.5) * (IDX_DIM ** -0.5)

    wit = wi_ref[...].T * idx_scale
    qm, wrow = [], []
    for ih in range(IDX_HEADS):
        grp = slice((ih // 4) * LANES, (ih // 4 + 1) * LANES)
        lo = (ih % 4) * IDX_DIM
        qm.append(qi_ref[:, grp] * _lane_mask(lo, lo + IDX_DIM, BF16))
        wrow.append(wit[ih:ih + 1, :])

    def idx_tile(kt):
        kk = ki_ref[pl.ds(pl.multiple_of(kt * t, t), t), :]
        tot = jnp.zeros((t, t), F32)
        for ih in range(IDX_HEADS):
            y = lax.dot_general(kk, qm[ih], _NT, preferred_element_type=F32)
            tot = tot + wrow[ih] * jnp.maximum(y, 0.0)
        return _sort_key(tot)

    def key_body(kt, c):
        key_ref[kt] = idx_tile(kt)
        return c

    lax.fori_loop(0, i, key_body, 0)
    kchunk = lax.broadcasted_iota(jnp.int32, (t, t), 0) // CHUNK
    qchunk = lax.broadcasted_iota(jnp.int32, (t, t), 1) // CHUNK
    key_ref[i] = jnp.where(kchunk <= qchunk, idx_tile(i), INT_MIN)

    def count(pred):
        def body(kt, c):
            hit = jnp.where(pred(key_ref[kt]), 1.0, 0.0)
            return c + jnp.sum(hit.reshape(t // SUBLANES, SUBLANES, t), axis=0)
        part = lax.fori_loop(0, n_kt, body, jnp.zeros((SUBLANES, t), F32))
        return jnp.sum(part, axis=0, keepdims=True)

    def bit_body(b, res):
        cand = res | (jnp.int32(1) << (31 - b))
        cand_s = cand ^ INT_MIN
        cnt = count(lambda key: key >= cand_s)
        return jnp.where(cnt >= topk, cand, res)

    thr = lax.fori_loop(0, 32, bit_body, jnp.zeros((1, t), jnp.int32)) ^ INT_MIN
    cnt_ge = count(lambda key: key >= thr)
    has_ties = jnp.max(cnt_ge) > topk

    @pl.when(jnp.logical_not(has_ties))
    def _():
        def body(kt, c):
            key_ref[kt] = jnp.where(key_ref[kt] >= thr, 0, NEG_BITS)
            return c
        lax.fori_loop(0, n_kt, body, 0)

    @pl.when(has_ties)
    def _():
        need = topk - count(lambda key: key > thr)

        def body(kt, seen):
            key = key_ref[kt]
            eq = jnp.where(key == thr, 1.0, 0.0)
            rank = seen + jnp.dot(tri_ref[...], eq.astype(BF16), preferred_element_type=F32)
            sel = (key > thr) | ((key == thr) & (rank <= need))
            key_ref[kt] = jnp.where(sel, 0, NEG_BITS)
            return seen + jnp.sum(eq, axis=0, keepdims=True)
        lax.fori_loop(0, n_kt, body, jnp.zeros((1, t), F32))

    qs = [q_ref[:, (hd // 2) * LANES:(hd // 2 + 1) * LANES]
          * _lane_mask((hd % 2) * HEAD_DIM, (hd % 2 + 1) * HEAD_DIM, BF16) for hd in range(N_HEADS)]
    keys = lambda hd, rows: k_ref[rows, (hd // 2) * LANES:(hd // 2 + 1) * LANES]
    values = lambda hd, kt: vt_ref[kt, hd * HEAD_DIM:(hd + 1) * HEAD_DIM, :]
    outs = _flash_sweep(i, t, N_HEADS, qs, keys, values, state, lambda hd: bias_ref[hd, 1], lambda hd: bias_ref[hd, 0],
                        mask=lambda kt: lax.bitcast_convert_type(key_ref[kt], F32))
    _store_gated(o_ref, g_ref, outs)


def _dsa(p, bias, batch, seq, t):
    topk = min(TOPK_MAX, seq // 4)
    nq = seq // t
    qs, ks, vspec, ospec = _att_specs(t, seq, (256, 256, 128, 256), (256, 128))
    tri = jnp.asarray(np.tril(np.ones((t, t), np.float32)), BF16)
    return pl.pallas_call(
        functools.partial(_dsa_kernel, t=t, topk=topk),
        grid=(batch, nq),
        in_specs=[qs[0], ks[0], vspec, qs[1], ks[1], qs[2], qs[3],
                  pl.BlockSpec((N_HEADS, 2, t, t), lambda b, i: (2, 0, 0, 0)),
                  pl.BlockSpec((t, t), lambda b, i: (0, 0))],
        out_specs=ospec,
        out_shape=jax.ShapeDtypeStruct((batch * seq, GROUP_WIDTH), BF16),
        scratch_shapes=[pltpu.VMEM((nq, t, t), jnp.int32)] + _flash_scratch(N_HEADS, t),
        compiler_params=_cparams(("parallel", "arbitrary")),
        name="dsa_attention",
    )(p["qc"], p["kc"], p["vct"], p["qi"], p["ki"], p["wi"], p["gc"], bias, tri)


def _outproj_kernel(x_ref, ya_ref, yb_ref, yc_ref, yd_ref, ud_ref, halo_ref, cw_ref, w_ref, fg_ref, o_ref,
                    *, tiles_per_seq, final):
    i = pl.program_id(0)
    tm = x_ref.shape[0]
    u = ud_ref[...]
    halo = jnp.where(i % tiles_per_seq == 0, 0.0, halo_ref[...])
    row8 = lax.broadcasted_iota(jnp.int32, (SUBLANES, 1), 0)

    def shifted(k):
        r = pltpu.roll(u, k, 0)
        head = jnp.where(row8 < k, pltpu.roll(halo, k, 0), r[0:SUBLANES])
        return jnp.concatenate([head, r[SUBLANES:tm]], axis=0)

    cw = cw_ref[...]
    conv = cw[0:1] * shifted(2) + cw[1:2] * shifted(1) + cw[2:3] * u
    yd = (conv * yd_ref[...].astype(F32)).astype(BF16)
    acc = x_ref[...]
    for g, y in enumerate((ya_ref[...], yb_ref[...], yc_ref[...], yd)):
        acc = acc + jnp.dot(y, w_ref[g * GROUP_WIDTH:(g + 1) * GROUP_WIDTH, :], preferred_element_type=F32)
    if final:
        acc = _rms(acc, fg_ref[...])
    o_ref[...] = acc


def _outproj(x2, ya, yb, yc, yd, ud, conv_w, w_out, final_g, seq, final):
    n = x2.shape[0]
    tm = min(PROJ_ROWS, seq)
    tiles_per_seq = seq // tm
    row = lambda w: pl.BlockSpec((tm, w), lambda i: (i, 0))
    full = lambda a: pl.BlockSpec(a.shape, lambda i: (0,) * a.ndim)
    halo = pl.BlockSpec((SUBLANES, GROUP_WIDTH), lambda i: (jnp.maximum(i * (tm // SUBLANES) - 1, 0), 0))
    return pl.pallas_call(
        functools.partial(_outproj_kernel, tiles_per_seq=tiles_per_seq, final=final),
        grid=(n // tm,),
        in_specs=[row(D_MODEL), row(256), row(256), row(256), row(256), row(256), halo,
                  full(conv_w), full(w_out), full(final_g)],
        out_specs=row(D_MODEL),
        out_shape=jax.ShapeDtypeStruct((n, D_MODEL), F32),
        compiler_params=_cparams(("parallel",)),
        name="outproj",
    )(x2, ya, yb, yc, yd, ud, ud, conv_w, w_out, final_g)


def kernel(x, norm_g, w_in, mla_qa_g, mla_w_uq, mla_kva_g, mla_w_ukv, diff_lambda, diff_subln_g, conv_w, w_out,
           rel_bias, final_g):
    batch, seq, _ = x.shape
    depth = w_in.shape[0]
    t = min(ATT_TILE, seq)
    assert seq % t == 0 and t % CHUNK == 0 and seq % min(PROJ_ROWS, seq) == 0 and min(PROJ_ROWS, seq) % t == 0

    bidx, far_bucket, mask_add = _position_constants(seq, t)
    cos_t, sin_t = _rope_tables(seq)
    bias = _bias_tiles(rel_bias, jnp.asarray(bidx), jnp.asarray(mask_add), far_bucket, t)
    mask_add = jnp.asarray(mask_add)
    cos_t, sin_t = jnp.asarray(cos_t), jnp.asarray(sin_t)

    x2 = x.reshape(batch * seq, D_MODEL)
    for l in range(depth):
        lambda_init = 0.8 - 0.6 * math.exp(-0.3 * l)
        w1, wvt = _prep_w_in(w_in[l])
        wq1, wq2, wk, wavt = _prep_mla_up(mla_w_uq[l], mla_w_ukv[l])
        p = _inproj(x2, norm_g[l][None, :], w1, wvt, mla_qa_g[l][None, :], wq1, wq2, mla_kva_g[l][None, :], wk, wavt,
                    cos_t, sin_t, seq, t)
        ya = _mla(p, mask_add, batch, seq, t)
        yb = _diff(p, bias, diff_lambda[l], diff_subln_g[l], lambda_init, batch, seq, t)
        yc = _dsa(p, bias, batch, seq, t)
        x2 = _outproj(x2, ya, yb, yc, p["yd"], p["ud"], conv_w[l], w_out[l].astype(BF16), final_g[None, :],
                      seq, final=(l == depth - 1))
    return x2.reshape(batch, seq, D_MODEL)
```

```python
import functools
import math

import numpy as np
import jax
import jax.numpy as jnp
from jax import lax
from jax.experimental import pallas as pl
from jax.experimental.pallas import tpu as pltpu

F32 = jnp.float32
BF16 = jnp.bfloat16

D_MODEL = 1024
CHUNK = 64
N_HEADS = 4
HEAD_DIM = 64
GROUP_WIDTH = N_HEADS * HEAD_DIM
MLA_Q_LORA = 256
MLA_KV_LORA = 128
MLA_NOPE = 64
MLA_ROPE = 32
ROPE_BASE = 10000.0
DIFF_QK = 32
IDX_HEADS = 8
IDX_DIM = 32
TOPK_MAX = 256
CONV_WIDTH = 3
NUM_BUCKETS = 32
MAX_DISTANCE = 128
N_DIFF_MAPS = 2 * N_HEADS
N_BIAS_MAPS = N_DIFF_MAPS + N_HEADS
NORM_EPS = 1e-6
NEG = -1e30
LOG2E = math.log2(math.e)
NEG_BITS = int(np.float32(NEG).view(np.int32))
INT_MIN = -2 ** 31
HALF_BITS = 16
HALF_MASK = 2 ** HALF_BITS - 1
HALF_BIAS = 2 ** (HALF_BITS - 1)

LANES = 128
SUBLANES = 8
ATT_TILE = 256
PROJ_ROWS = 512
FLASH_LOOKAHEAD = 4
VMEM_LIMIT = 56 * 1024 * 1024

_IN_SPLITS = (
    ("a_cq", 256), ("a_ckv", 128), ("a_krope", 32), ("a_gate", 256),
    ("b_q", 256), ("b_k", 256), ("b_v", 256), ("b_gate", 256),
    ("c_q", 256), ("c_k", 256), ("c_v", 256),
    ("c_qidx", 256), ("c_kidx", 32), ("c_widx", 8), ("c_gate", 256),
    ("d_b", 256), ("d_c", 256), ("d_h", 256), ("d_gate", 256),
)

_SEGS = (
    ("cq", 256), ("ckv", 128), ("kr1", 128), ("kr2", 128), ("a_gate", 256),
    ("b_q", 256), ("b_k", 256), ("b_gate", 256),
    ("c_q", 256), ("c_k", 256), ("c_qidx", 256), ("c_kidx4", 128), ("c_widx", 128),
    ("c_gate", 256), ("d_b", 256), ("d_c", 256), ("d_h", 256), ("d_gate", 256),
)
_SEG_OFF = {}
_off = 0
for _name, _w in _SEGS:
    _SEG_OFF[_name] = (_off, _w)
    _off += _w
W1_COLS = _off

_NT = (((1,), (1,)), ((), ()))


def _cparams(sem):
    return pltpu.CompilerParams(dimension_semantics=sem, vmem_limit_bytes=VMEM_LIMIT)


def _split_cols(w):
    out, off = {}, 0
    for name, width in _IN_SPLITS:
        out[name] = w[:, off:off + width]
        off += width
    return out


def _rot_half_cols(w):
    half = MLA_ROPE // 2
    return jnp.concatenate([-w[..., half:], w[..., :half]], axis=-1)


def _prep_w_in(w):
    p = _split_cols(w)
    z = lambda n: jnp.zeros((w.shape[0], n), w.dtype)
    kr1 = jnp.concatenate([z(MLA_NOPE), p["a_krope"], z(LANES - MLA_NOPE - MLA_ROPE)], axis=1)
    kr2 = jnp.concatenate([z(MLA_NOPE), _rot_half_cols(p["a_krope"]), z(LANES - MLA_NOPE - MLA_ROPE)], axis=1)
    segs = {
        "cq": p["a_cq"], "ckv": p["a_ckv"], "kr1": kr1, "kr2": kr2, "a_gate": p["a_gate"],
        "b_q": p["b_q"] * (DIFF_QK ** -0.5 * LOG2E), "b_k": p["b_k"], "b_gate": p["b_gate"],
        "c_q": p["c_q"] * (HEAD_DIM ** -0.5 * LOG2E), "c_k": p["c_k"], "c_qidx": p["c_qidx"],
        "c_kidx4": jnp.tile(p["c_kidx"], (1, LANES // IDX_DIM)),
        "c_widx": jnp.concatenate([p["c_widx"], z(LANES - IDX_HEADS)], axis=1),
        "c_gate": p["c_gate"], "d_b": p["d_b"], "d_c": p["d_c"], "d_h": p["d_h"], "d_gate": p["d_gate"],
    }
    w1 = jnp.concatenate([segs[name] for name, _ in _SEGS], axis=1).astype(BF16)
    wvt = jnp.concatenate([p["b_v"], p["c_v"]], axis=1).T.astype(BF16)
    return w1, wvt


def _prep_mla_up(w_uq, w_ukv):
    scale = (MLA_NOPE + MLA_ROPE) ** -0.5 * LOG2E
    wq = w_uq.reshape(MLA_Q_LORA, N_HEADS, MLA_NOPE + MLA_ROPE) * scale
    zq = jnp.zeros((MLA_Q_LORA, N_HEADS, LANES - MLA_NOPE - MLA_ROPE), w_uq.dtype)
    wq1 = jnp.concatenate([wq, zq], axis=-1).reshape(MLA_Q_LORA, N_HEADS * LANES)
    wq2 = jnp.concatenate([jnp.zeros_like(wq[..., :MLA_NOPE]), _rot_half_cols(wq[..., MLA_NOPE:]), zq],
                          axis=-1).reshape(MLA_Q_LORA, N_HEADS * LANES)
    wkv = w_ukv.reshape(MLA_KV_LORA, N_HEADS, MLA_NOPE + HEAD_DIM)
    wk = jnp.concatenate([wkv[..., :MLA_NOPE], jnp.zeros((MLA_KV_LORA, N_HEADS, LANES - MLA_NOPE), w_ukv.dtype)],
                         axis=-1).reshape(MLA_KV_LORA, N_HEADS * LANES)
    wvt = wkv[..., MLA_NOPE:].reshape(MLA_KV_LORA, N_HEADS * HEAD_DIM).T
    return wq1.astype(BF16), wq2.astype(BF16), wk.astype(BF16), wvt.astype(BF16)


def _rope_tables(seq):
    half = MLA_ROPE // 2
    inv_freq = ROPE_BASE ** (-np.arange(half, dtype=np.float32) / half)
    ang = np.arange(seq, dtype=np.float32)[:, None] * inv_freq[None, :].astype(np.float32)
    cos, sin = np.cos(ang).astype(np.float32), np.sin(ang).astype(np.float32)
    ct = np.zeros((seq, LANES), np.float32)
    st = np.zeros((seq, LANES), np.float32)
    ct[:, :MLA_NOPE] = 1.0
    ct[:, MLA_NOPE:MLA_NOPE + half] = cos
    ct[:, MLA_NOPE + half:MLA_NOPE + MLA_ROPE] = cos
    st[:, MLA_NOPE:MLA_NOPE + half] = sin
    st[:, MLA_NOPE + half:MLA_NOPE + MLA_ROPE] = sin
    return ct, st


def _rel_bucket_np(rel):
    nb = NUM_BUCKETS // 2
    max_exact = nb // 2
    ret = np.where(rel > 0, nb, 0)
    n = np.abs(rel)
    nf = np.maximum(n, max_exact).astype(np.float32)
    large = max_exact + (np.log(nf / np.float32(max_exact)) / np.float32(math.log(MAX_DISTANCE / max_exact))
                         * np.float32(nb - max_exact)).astype(np.int32)
    large = np.minimum(large, nb - 1)
    return (ret + np.where(n < max_exact, n, large)).astype(np.int32)


def _position_constants(seq, tile):
    r = np.arange(tile)
    rel0 = r[:, None] - r[None, :]
    bidx = np.stack([_rel_bucket_np(rel0 - tile), _rel_bucket_np(rel0)]).astype(np.int32)
    far = _rel_bucket_np(np.arange(-(seq - 1), -tile))
    far_bucket = int(far[0]) if far.size else int(_rel_bucket_np(np.array([-tile - 1]))[0])
    assert far.size == 0 or np.all(far == far_bucket)
    mask_add = np.where((r[:, None] // CHUNK) <= (r[None, :] // CHUNK), 0.0, NEG).astype(np.float32)
    return bidx, far_bucket, mask_add


def _bias_kernel(tab_ref, bidx_ref, madd_ref, o_ref, *, far_bucket):
    j = pl.program_id(0)
    c = tab_ref[j, far_bucket]
    for d in range(2):
        idx = bidx_ref[d]
        acc = jnp.zeros(idx.shape, F32)
        for b in range(NUM_BUCKETS):
            acc = jnp.where(idx == b, (tab_ref[j, b] - c) * LOG2E, acc)
        if d == 1:
            acc = acc + madd_ref[...]
        o_ref[0, d] = acc


def _bias_tiles(rel_bias, bidx, mask_add, far_bucket, tile):
    return pl.pallas_call(
        functools.partial(_bias_kernel, far_bucket=far_bucket),
        grid=(N_BIAS_MAPS,),
        in_specs=[
            pl.BlockSpec(memory_space=pltpu.SMEM),
            pl.BlockSpec((2, tile, tile), lambda j: (0, 0, 0)),
            pl.BlockSpec((tile, tile), lambda j: (0, 0)),
        ],
        out_specs=pl.BlockSpec((1, 2, tile, tile), lambda j: (j, 0, 0, 0)),
        out_shape=jax.ShapeDtypeStruct((N_BIAS_MAPS, 2, tile, tile), F32),
        compiler_params=_cparams(("arbitrary",)),
        name="bias_tiles",
    )(rel_bias.T.astype(F32), bidx, mask_add)


def _silu(x):
    return x * (1.0 / (1.0 + jnp.exp(-x)))


def _rms(x, g):
    return x * lax.rsqrt(jnp.mean(x * x, axis=-1, keepdims=True) + NORM_EPS) * g


def _inproj_kernel(x_ref, g_ref, w_ref, wvt_ref, qag_ref, wq1_ref, wq2_ref, kvg_ref, wk_ref, wavt_ref, cos_ref, sin_ref,
                   qa_ref, ka_ref, vat_ref, ga_ref, qb_ref, kb_ref, vbt_ref, gb_ref,
                   qc_ref, kc_ref, vct_ref, qi_ref, ki_ref, wi_ref, gc_ref, yd_ref, ud_ref, *, t):
    h = _rms(x_ref[...], g_ref[...]).astype(BF16)

    def seg(name):
        off, width = _SEG_OFF[name]
        return jnp.dot(h, w_ref[:, off:off + width], preferred_element_type=F32)

    cos = cos_ref[...]
    sin = sin_ref[...]

    cqn = _rms(seg("cq"), qag_ref[...]).astype(BF16)
    q1 = jnp.dot(cqn, wq1_ref[...], preferred_element_type=F32)
    q2 = jnp.dot(cqn, wq2_ref[...], preferred_element_type=F32)
    kr = seg("kr1") * cos + seg("kr2") * sin
    ckvn = _rms(seg("ckv"), kvg_ref[...]).astype(BF16)
    kn = jnp.dot(ckvn, wk_ref[...], preferred_element_type=F32)
    for hd in range(N_HEADS):
        sl = slice(hd * LANES, (hd + 1) * LANES)
        qa_ref[:, sl] = (q1[:, sl] * cos + q2[:, sl] * sin).astype(BF16)
        ka_ref[:, sl] = (kn[:, sl] + kr).astype(BF16)
    ga_ref[...] = _silu(seg("a_gate")).astype(BF16)

    for c in range(x_ref.shape[0] // t):
        rows = slice(c * t, (c + 1) * t)
        vat_ref[c] = lax.dot_general(wavt_ref[...], ckvn[rows], _NT, preferred_element_type=F32).astype(BF16)
        vt = lax.dot_general(wvt_ref[...], h[rows], _NT, preferred_element_type=F32).astype(BF16)
        vbt_ref[c] = vt[:GROUP_WIDTH]
        vct_ref[c] = vt[GROUP_WIDTH:]

    qb_ref[...] = seg("b_q").astype(BF16)
    kb_ref[...] = seg("b_k").astype(BF16)
    gb_ref[...] = _silu(seg("b_gate")).astype(BF16)

    qc_ref[...] = seg("c_q").astype(BF16)
    kc_ref[...] = seg("c_k").astype(BF16)
    qi_ref[...] = seg("c_qidx").astype(BF16)
    ki_ref[...] = seg("c_kidx4").astype(BF16)
    wi_ref[...] = seg("c_widx")
    gc_ref[...] = _silu(seg("c_gate")).astype(BF16)

    yd_ref[...] = (seg("d_b") * _silu(seg("d_gate"))).astype(BF16)
    ud_ref[...] = seg("d_c") * seg("d_h")


def _inproj(x2, g, w1, wvt, qag, wq1, wq2, kvg, wk, wavt, cos_t, sin_t, seq, t):
    n = x2.shape[0]
    tm = min(PROJ_ROWS, seq)
    tiles_per_seq = seq // tm
    row = lambda w: pl.BlockSpec((tm, w), lambda i: (i, 0))
    full = lambda a: pl.BlockSpec(a.shape, lambda i: (0,) * a.ndim)
    tab = pl.BlockSpec((tm, LANES), lambda i: (i % tiles_per_seq, 0))
    vt_spec = pl.BlockSpec((tm // t, GROUP_WIDTH, t), lambda i: (i, 0, 0))
    vt_shape = jax.ShapeDtypeStruct((n // t, GROUP_WIDTH, t), BF16)
    outs = (("qa", 512, BF16), ("ka", 512, BF16), ("vat", None, None), ("ga", 256, BF16),
            ("qb", 256, BF16), ("kb", 256, BF16), ("vbt", None, None), ("gb", 256, BF16),
            ("qc", 256, BF16), ("kc", 256, BF16), ("vct", None, None), ("qi", 256, BF16),
            ("ki", 128, BF16), ("wi", 128, F32), ("gc", 256, BF16), ("yd", 256, BF16), ("ud", 256, F32))
    res = pl.pallas_call(
        functools.partial(_inproj_kernel, t=t),
        grid=(n // tm,),
        in_specs=[row(D_MODEL), full(g), full(w1), full(wvt), full(qag), full(wq1), full(wq2), full(kvg), full(wk),
                  full(wavt), tab, tab],
        out_specs=[vt_spec if w is None else row(w) for _, w, _ in outs],
        out_shape=[vt_shape if w is None else jax.ShapeDtypeStruct((n, w), dt) for _, w, dt in outs],
        compiler_params=_cparams(("parallel",)),
        name="inproj",
    )(x2, g, w1, wvt, qag, wq1, wq2, kvg, wk, wavt, cos_t, sin_t)
    return {name: o for (name, _, _), o in zip(outs, res)}


def _flash_scratch(n_maps, t):
    return [pltpu.VMEM((n_maps, 1, t), F32), pltpu.VMEM((n_maps, 1, t), F32), pltpu.VMEM((n_maps, HEAD_DIM, t), F32)]


def _flash_update(s, vts, old):
    t = vts[0].shape[1]
    m_tile = jnp.max(s, axis=0, keepdims=True)
    m_new = m_tile if old is None else jnp.maximum(old[0], m_tile)
    p = jnp.exp2(s - m_new)
    l_tile = jnp.sum(p, axis=0, keepdims=True)
    pb = p.astype(BF16)
    pv = jnp.dot(vts[0], pb[0:t], preferred_element_type=F32)
    for d in range(1, len(vts)):
        pv = pv + jnp.dot(vts[d], pb[d * t:(d + 1) * t], preferred_element_type=F32)
    if old is None:
        return m_new, l_tile, pv
    alpha = jnp.exp2(old[0] - m_new)
    return m_new, alpha * old[1] + l_tile, alpha * old[2] + pv


def _flash_sweep(i, t, n_maps, qs, keys, values, state, diag_add, near_add=None, mask=None):
    m_ref, l_ref, acc_ref = state

    def run(units, first=False):
        cur = [None if first else (m_ref[j], l_ref[j], acc_ref[j]) for j in range(n_maps)]
        shared = {}

        def logits(u, j):
            kt, n_tiles, extra = units[u]
            rows = pl.ds(pl.multiple_of(kt * t, t), n_tiles * t)
            s = lax.dot_general(keys(j, rows), qs[j], _NT, preferred_element_type=F32)
            add = None
            if mask is not None:
                if u not in shared:
                    tiles = [mask(kt + d) for d in range(n_tiles)]
                    shared[u] = tiles[0] if n_tiles == 1 else jnp.concatenate(tiles, axis=0)
                add = shared[u]
            if extra is not None:
                add = extra(j) if add is None else add + extra(j)
            return s if add is None else s + add

        items = [(u, j) for u in range(len(units)) for j in range(n_maps)]
        queue = [logits(*it) for it in items[:FLASH_LOOKAHEAD]]
        for n, (u, j) in enumerate(items):
            if n + FLASH_LOOKAHEAD < len(items):
                queue.append(logits(*items[n + FLASH_LOOKAHEAD]))
            s = queue.pop(0)
            kt, n_tiles, _ = units[u]
            cur[j] = _flash_update(s, [values(j, kt + d) for d in range(n_tiles)], cur[j])
        for j in range(n_maps):
            m_ref[j], l_ref[j], acc_ref[j] = cur[j]

    if near_add is None:
        run([(i, 1, diag_add)], first=True)
        n_far = i
    else:
        pl.when(i == 0)(lambda: run([(i, 1, diag_add)], first=True))
        pl.when(i >= 1)(lambda: run([(i, 1, diag_add), (i - 1, 1, near_add)], first=True))
        n_far = jnp.maximum(i - 1, 0)

    def quad_body(c, carry):
        run([(4 * c, 2, None), (4 * c + 2, 2, None)])
        return carry

    lax.fori_loop(0, n_far // 4, quad_body, 0)
    rem = n_far % 4
    base = n_far - rem
    pl.when(rem >= 2)(lambda: run([(base, 2, None)]))
    pl.when(rem % 2 == 1)(lambda: run([(n_far - 1, 1, None)]))
    return [acc_ref[j] / l_ref[j] for j in range(n_maps)]


def _lane_mask(lo, hi, dtype):
    lane = lax.broadcasted_iota(jnp.int32, (1, LANES), 1)
    return ((lane >= lo) & (lane < hi)).astype(dtype)


def _store_gated(o_ref, g_ref, heads_t):
    o = jnp.concatenate(heads_t, axis=0).T
    o_ref[...] = (o * g_ref[...].astype(F32)).astype(BF16)


def _att_specs(t, seq, widths_q, widths_k):
    nq = seq // t
    qspec = lambda w: pl.BlockSpec((t, w), lambda b, i: (b * nq + i, 0))
    kspec = lambda w: pl.BlockSpec((seq, w), lambda b, i: (b, 0))
    vspec = pl.BlockSpec((nq, GROUP_WIDTH, t), lambda b, i: (b, 0, 0))
    return [qspec(w) for w in widths_q], [kspec(w) for w in widths_k], vspec, qspec(GROUP_WIDTH)


def _mla_kernel(q_ref, k_ref, vt_ref, g_ref, madd_ref, o_ref, *state, t):
    i = pl.program_id(1)
    qs = [q_ref[:, hd * LANES:(hd + 1) * LANES] for hd in range(N_HEADS)]
    keys = lambda hd, rows: k_ref[rows, hd * LANES:(hd + 1) * LANES]
    values = lambda hd, kt: vt_ref[kt, hd * HEAD_DIM:(hd + 1) * HEAD_DIM, :]
    outs = _flash_sweep(i, t, N_HEADS, qs, keys, values, state, lambda hd: madd_ref[...])
    _store_gated(o_ref, g_ref, outs)


def _mla(p, mask_add, batch, seq, t):
    qs, ks, vspec, ospec = _att_specs(t, seq, (512, 256), (512,))
    return pl.pallas_call(
        functools.partial(_mla_kernel, t=t),
        grid=(batch, seq // t),
        in_specs=[qs[0], ks[0], vspec, qs[1], pl.BlockSpec((t, t), lambda b, i: (0, 0))],
        out_specs=ospec,
        out_shape=jax.ShapeDtypeStruct((batch * seq, GROUP_WIDTH), BF16),
        scratch_shapes=_flash_scratch(N_HEADS, t),
        compiler_params=_cparams(("parallel", "arbitrary")),
        name="mla_attention",
    )(p["qa"], p["ka"], p["vat"], p["ga"], mask_add)


def _diff_kernel(q_ref, k_ref, vt_ref, g_ref, bias_ref, lam_ref, subg_ref, o_ref, *state, t, lambda_init):
    i = pl.program_id(1)
    lp = lam_ref[...]
    lam = (jnp.exp(jnp.sum(lp[0:1] * lp[1:2], axis=-1, keepdims=True))
           - jnp.exp(jnp.sum(lp[2:3] * lp[3:4], axis=-1, keepdims=True)) + lambda_init)
    qs = [q_ref[:, (j // 4) * LANES:(j // 4 + 1) * LANES] * _lane_mask((j % 4) * DIFF_QK, (j % 4 + 1) * DIFF_QK, BF16)
          for j in range(N_DIFF_MAPS)]
    keys = lambda j, rows: k_ref[rows, (j // 4) * LANES:(j // 4 + 1) * LANES]
    values = lambda j, kt: vt_ref[kt, (j // 2) * HEAD_DIM:(j // 2 + 1) * HEAD_DIM, :]
    maps = _flash_sweep(i, t, N_DIFF_MAPS, qs, keys, values, state, lambda j: bias_ref[j, 1], lambda j: bias_ref[j, 0])
    outs = []
    for hd in range(N_HEADS):
        o = maps[2 * hd] - lam * maps[2 * hd + 1]
        ms = jnp.mean(o * o, axis=0, keepdims=True)
        outs.append(o * lax.rsqrt(ms + NORM_EPS) * subg_ref[...] * (1.0 - lambda_init))
    _store_gated(o_ref, g_ref, outs)


def _diff(p, bias, lam_params, subln_g, lambda_init, batch, seq, t):
    qs, ks, vspec, ospec = _att_specs(t, seq, (256, 256), (256,))
    subg = jnp.broadcast_to(subln_g.astype(F32)[:, None], (HEAD_DIM, t))
    return pl.pallas_call(
        functools.partial(_diff_kernel, t=t, lambda_init=lambda_init),
        grid=(batch, seq // t),
        in_specs=[qs[0], ks[0], vspec, qs[1],
                  pl.BlockSpec((N_DIFF_MAPS, 2, t, t), lambda b, i: (0, 0, 0, 0)),
                  pl.BlockSpec((4, DIFF_QK), lambda b, i: (0, 0)),
                  pl.BlockSpec((HEAD_DIM, t), lambda b, i: (0, 0))],
        out_specs=ospec,
        out_shape=jax.ShapeDtypeStruct((batch * seq, GROUP_WIDTH), BF16),
        scratch_shapes=_flash_scratch(N_DIFF_MAPS, t),
        compiler_params=_cparams(("parallel", "arbitrary")),
        name="diff_attention",
    )(p["qb"], p["kb"], p["vbt"], p["gb"], bias, lam_params.astype(F32), subg)


def _sort_key(x):
    b = lax.bitcast_convert_type(x + 0.0, jnp.int32)
    return b ^ ((b >> 31) & 0x7FFFFFFF)


def _dsa_kernel(q_ref, k_ref, vt_ref, qi_ref, ki_ref, wi_ref, g_ref, bias_ref, tri_ref, o_ref,
                key_ref, hi_ref, lo_ref, *state, t, topk):
    i = pl.program_id(1)
    n_kt = i + 1
    idx_scale = (IDX_HEADS ** -0.5) * (IDX_DIM ** -0.5)

    wit = wi_ref[...].T * idx_scale
    qm, wrow = [], []
    for ih in range(IDX_HEADS):
        grp = slice((ih // 4) * LANES, (ih // 4 + 1) * LANES)
        lo = (ih % 4) * IDX_DIM
        qm.append(qi_ref[:, grp] * _lane_mask(lo, lo + IDX_DIM, BF16))
        wrow.append(wit[ih:ih + 1, :])

    def idx_tile(kt):
        kk = ki_ref[pl.ds(pl.multiple_of(kt * t, t), t), :]
        tot = jnp.zeros((t, t), F32)
        for ih in range(IDX_HEADS):
            y = lax.dot_general(kk, qm[ih], _NT, preferred_element_type=F32)
            tot = tot + wrow[ih] * jnp.maximum(y, 0.0)
        return _sort_key(tot)

    def store_keys(kt, key):
        key_ref[kt] = key
        hi_ref[kt] = (key >> HALF_BITS).astype(jnp.int16)
        lo_ref[kt] = ((key & HALF_MASK) - HALF_BIAS).astype(jnp.int16)

    def key_body(kt, c):
        store_keys(kt, idx_tile(kt))
        return c

    lax.fori_loop(0, i, key_body, 0)
    kchunk = lax.broadcasted_iota(jnp.int32, (t, t), 0) // CHUNK
    qchunk = lax.broadcasted_iota(jnp.int32, (t, t), 1) // CHUNK
    store_keys(i, jnp.where(kchunk <= qchunk, idx_tile(i), INT_MIN))

    def count(pred):
        def body(kt, c):
            hit = jnp.where(pred(key_ref[kt]), 1.0, 0.0)
            return c + jnp.sum(hit.reshape(t // SUBLANES, SUBLANES, t), axis=0)
        part = lax.fori_loop(0, n_kt, body, jnp.zeros((SUBLANES, t), F32))
        return jnp.sum(part, axis=0, keepdims=True)

    rows16 = 2 * SUBLANES
    n_acc = 4

    def count16(ref, pred):
        one, zero = jnp.int16(1), jnp.int16(0)

        def body(kt, accs):
            x = ref[kt].reshape(t // rows16, rows16, t)
            accs = list(accs)
            for r in range(t // rows16):
                accs[r % n_acc] = accs[r % n_acc] + jnp.where(pred(x[r]), one, zero)
            return tuple(accs)
        accs = lax.fori_loop(0, n_kt, body, tuple(jnp.zeros((rows16, t), jnp.int16) for _ in range(n_acc)))
        tot = (accs[0] + accs[1]) + (accs[2] + accs[3])
        return jnp.sum(tot.astype(jnp.int32), axis=0, keepdims=True)

    def select16(ref, need):
        def bit_body(b, res):
            cand = res | (jnp.int32(1) << (HALF_BITS - 1 - b))
            cand16 = (cand - HALF_BIAS).astype(jnp.int16)
            cnt = count16(ref, lambda x: x >= cand16)
            return jnp.where(cnt >= need, cand, res)
        return lax.fori_loop(0, HALF_BITS, bit_body, jnp.zeros((1, t), jnp.int32)) - HALF_BIAS

    hi_thr = select16(hi_ref, topk)
    hi_thr16 = hi_thr.astype(jnp.int16)
    need_lo = topk - count16(hi_ref, lambda x: x > hi_thr16)

    def member_body(kt, c):
        lo_ref[kt] = jnp.where(hi_ref[kt] == hi_thr16, lo_ref[kt], jnp.int16(-HALF_BIAS))
        return c

    lax.fori_loop(0, n_kt, member_body, 0)
    lo_thr = select16(lo_ref, need_lo)
    thr = (hi_thr << HALF_BITS) | (lo_thr + HALF_BIAS)
    cnt_ge = count(lambda key: key >= thr)
    has_ties = jnp.max(cnt_ge) > topk

    @pl.when(jnp.logical_not(has_ties))
    def _():
        def body(kt, c):
            key_ref[kt] = jnp.where(key_ref[kt] >= thr, 0, NEG_BITS)
            return c
        lax.fori_loop(0, n_kt, body, 0)

    @pl.when(has_ties)
    def _():
        need = topk - count(lambda key: key > thr)

        def body(kt, seen):
            key = key_ref[kt]
            eq = jnp.where(key == thr, 1.0, 0.0)
            rank = seen + jnp.dot(tri_ref[...], eq.astype(BF16), preferred_element_type=F32)
            sel = (key > thr) | ((key == thr) & (rank <= need))
            key_ref[kt] = jnp.where(sel, 0, NEG_BITS)
            return seen + jnp.sum(eq, axis=0, keepdims=True)
        lax.fori_loop(0, n_kt, body, jnp.zeros((1, t), F32))

    qs = [q_ref[:, (hd // 2) * LANES:(hd // 2 + 1) * LANES]
          * _lane_mask((hd % 2) * HEAD_DIM, (hd % 2 + 1) * HEAD_DIM, BF16) for hd in range(N_HEADS)]
    keys = lambda hd, rows: k_ref[rows, (hd // 2) * LANES:(hd // 2 + 1) * LANES]
    values = lambda hd, kt: vt_ref[kt, hd * HEAD_DIM:(hd + 1) * HEAD_DIM, :]
    outs = _flash_sweep(i, t, N_HEADS, qs, keys, values, state, lambda hd: bias_ref[hd, 1], lambda hd: bias_ref[hd, 0],
                        mask=lambda kt: lax.bitcast_convert_type(key_ref[kt], F32))
    _store_gated(o_ref, g_ref, outs)


def _dsa(p, bias, batch, seq, t):
    topk = min(TOPK_MAX, seq // 4)
    nq = seq // t
    qs, ks, vspec, ospec = _att_specs(t, seq, (256, 256, 128, 256), (256, 128))
    tri = jnp.asarray(np.tril(np.ones((t, t), np.float32)), BF16)
    return pl.pallas_call(
        functools.partial(_dsa_kernel, t=t, topk=topk),
        grid=(batch, nq),
        in_specs=[qs[0], ks[0], vspec, qs[1], ks[1], qs[2], qs[3],
                  pl.BlockSpec((N_HEADS, 2, t, t), lambda b, i: (2, 0, 0, 0)),
                  pl.BlockSpec((t, t), lambda b, i: (0, 0))],
        out_specs=ospec,
        out_shape=jax.ShapeDtypeStruct((batch * seq, GROUP_WIDTH), BF16),
        scratch_shapes=[pltpu.VMEM((nq, t, t), jnp.int32), pltpu.VMEM((nq, t, t), jnp.int16),
                        pltpu.VMEM((nq, t, t), jnp.int16)] + _flash_scratch(N_HEADS, t),
        compiler_params=_cparams(("parallel", "arbitrary")),
        name="dsa_attention",
    )(p["qc"], p["kc"], p["vct"], p["qi"], p["ki"], p["wi"], p["gc"], bias, tri)


def _outproj_kernel(x_ref, ya_ref, yb_ref, yc_ref, yd_ref, ud_ref, halo_ref, cw_ref, w_ref, fg_ref, o_ref,
                    *, tiles_per_seq, final):
    i = pl.program_id(0)
    tm = x_ref.shape[0]
    u = ud_ref[...]
    halo = jnp.where(i % tiles_per_seq == 0, 0.0, halo_ref[...])
    row8 = lax.broadcasted_iota(jnp.int32, (SUBLANES, 1), 0)

    def shifted(k):
        r = pltpu.roll(u, k, 0)
        head = jnp.where(row8 < k, pltpu.roll(halo, k, 0), r[0:SUBLANES])
        return jnp.concatenate([head, r[SUBLANES:tm]], axis=0)

    cw = cw_ref[...]
    conv = cw[0:1] * shifted(2) + cw[1:2] * shifted(1) + cw[2:3] * u
    yd = (conv * yd_ref[...].astype(F32)).astype(BF16)
    acc = x_ref[...]
    for g, y in enumerate((ya_ref[...], yb_ref[...], yc_ref[...], yd)):
        acc = acc + jnp.dot(y, w_ref[g * GROUP_WIDTH:(g + 1) * GROUP_WIDTH, :], preferred_element_type=F32)
    if final:
        acc = _rms(acc, fg_ref[...])
    o_ref[...] = acc


def _outproj(x2, ya, yb, yc, yd, ud, conv_w, w_out, final_g, seq, final):
    n = x2.shape[0]
    tm = min(PROJ_ROWS, seq)
    tiles_per_seq = seq // tm
    row = lambda w: pl.BlockSpec((tm, w), lambda i: (i, 0))
    full = lambda a: pl.BlockSpec(a.shape, lambda i: (0,) * a.ndim)
    halo = pl.BlockSpec((SUBLANES, GROUP_WIDTH), lambda i: (jnp.maximum(i * (tm // SUBLANES) - 1, 0), 0))
    return pl.pallas_call(
        functools.partial(_outproj_kernel, tiles_per_seq=tiles_per_seq, final=final),
        grid=(n // tm,),
        in_specs=[row(D_MODEL), row(256), row(256), row(256), row(256), row(256), halo,
                  full(conv_w), full(w_out), full(final_g)],
        out_specs=row(D_MODEL),
        out_shape=jax.ShapeDtypeStruct((n, D_MODEL), F32),
        compiler_params=_cparams(("parallel",)),
        name="outproj",
    )(x2, ya, yb, yc, yd, ud, ud, conv_w, w_out, final_g)


def kernel(x, norm_g, w_in, mla_qa_g, mla_w_uq, mla_kva_g, mla_w_ukv, diff_lambda, diff_subln_g, conv_w, w_out,
           rel_bias, final_g):
    batch, seq, _ = x.shape
    depth = w_in.shape[0]
    t = min(ATT_TILE, seq)
    assert seq % t == 0 and t % CHUNK == 0 and seq % min(PROJ_ROWS, seq) == 0 and min(PROJ_ROWS, seq) % t == 0

    bidx, far_bucket, mask_add = _position_constants(seq, t)
    cos_t, sin_t = _rope_tables(seq)
    bias = _bias_tiles(rel_bias, jnp.asarray(bidx), jnp.asarray(mask_add), far_bucket, t)
    mask_add = jnp.asarray(mask_add)
    cos_t, sin_t = jnp.asarray(cos_t), jnp.asarray(sin_t)

    x2 = x.reshape(batch * seq, D_MODEL)
    for l in range(depth):
        lambda_init = 0.8 - 0.6 * math.exp(-0.3 * l)
        w1, wvt = _prep_w_in(w_in[l])
        wq1, wq2, wk, wavt = _prep_mla_up(mla_w_uq[l], mla_w_ukv[l])
        p = _inproj(x2, norm_g[l][None, :], w1, wvt, mla_qa_g[l][None, :], wq1, wq2, mla_kva_g[l][None, :], wk, wavt,
                    cos_t, sin_t, seq, t)
        ya = _mla(p, mask_add, batch, seq, t)
        yb = _diff(p, bias, diff_lambda[l], diff_subln_g[l], lambda_init, batch, seq, t)
        yc = _dsa(p, bias, batch, seq, t)
        x2 = _outproj(x2, ya, yb, yc, p["yd"], p["ud"], conv_w[l], w_out[l].astype(BF16), final_g[None, :],
                      seq, final=(l == depth - 1))
    return x2.reshape(batch, seq, D_MODEL)
```

```python
import functools
import math

import numpy as np
import jax
import jax.numpy as jnp
from jax import lax
from jax.experimental import pallas as pl
from jax.experimental.pallas import tpu as pltpu

F32 = jnp.float32
BF16 = jnp.bfloat16

D_MODEL = 1024
CHUNK = 64
N_HEADS = 4
HEAD_DIM = 64
GROUP_WIDTH = N_HEADS * HEAD_DIM
MLA_Q_LORA = 256
MLA_KV_LORA = 128
MLA_NOPE = 64
MLA_ROPE = 32
ROPE_BASE = 10000.0
DIFF_QK = 32
IDX_HEADS = 8
IDX_DIM = 32
TOPK_MAX = 256
CONV_WIDTH = 3
NUM_BUCKETS = 32
MAX_DISTANCE = 128
N_DIFF_MAPS = 2 * N_HEADS
N_BIAS_MAPS = N_DIFF_MAPS + N_HEADS
NORM_EPS = 1e-6
NEG = -1e30
LOG2E = math.log2(math.e)
NEG_BITS = int(np.float32(NEG).view(np.int32))
INT_MIN = -2 ** 31
HALF_BITS = 16
HALF_MASK = 2 ** HALF_BITS - 1
HALF_BIAS = 2 ** (HALF_BITS - 1)

LANES = 128
SUBLANES = 8
ATT_TILE = 256
PROJ_ROWS = 512
FLASH_LOOKAHEAD = 4
VMEM_LIMIT = 56 * 1024 * 1024

_IN_SPLITS = (
    ("a_cq", 256), ("a_ckv", 128), ("a_krope", 32), ("a_gate", 256),
    ("b_q", 256), ("b_k", 256), ("b_v", 256), ("b_gate", 256),
    ("c_q", 256), ("c_k", 256), ("c_v", 256),
    ("c_qidx", 256), ("c_kidx", 32), ("c_widx", 8), ("c_gate", 256),
    ("d_b", 256), ("d_c", 256), ("d_h", 256), ("d_gate", 256),
)

_SEGS = (
    ("cq", 256), ("ckv", 128), ("kr1", 128), ("kr2", 128), ("a_gate", 256),
    ("b_q", 256), ("b_k", 256), ("b_gate", 256),
    ("c_q", 256), ("c_k", 256), ("c_qidx", 256), ("c_kidx4", 128), ("c_widx", 128),
    ("c_gate", 256), ("d_b", 256), ("d_c", 256), ("d_h", 256), ("d_gate", 256),
)
_SEG_OFF = {}
_off = 0
for _name, _w in _SEGS:
    _SEG_OFF[_name] = (_off, _w)
    _off += _w
W1_COLS = _off

_NT = (((1,), (1,)), ((), ()))


def _cparams(sem):
    return pltpu.CompilerParams(dimension_semantics=sem, vmem_limit_bytes=VMEM_LIMIT)


def _split_cols(w):
    out, off = {}, 0
    for name, width in _IN_SPLITS:
        out[name] = w[:, off:off + width]
        off += width
    return out


def _rot_half_cols(w):
    half = MLA_ROPE // 2
    return jnp.concatenate([-w[..., half:], w[..., :half]], axis=-1)


def _prep_w_in(w):
    p = _split_cols(w)
    z = lambda n: jnp.zeros((w.shape[0], n), w.dtype)
    kr1 = jnp.concatenate([z(MLA_NOPE), p["a_krope"], z(LANES - MLA_NOPE - MLA_ROPE)], axis=1)
    kr2 = jnp.concatenate([z(MLA_NOPE), _rot_half_cols(p["a_krope"]), z(LANES - MLA_NOPE - MLA_ROPE)], axis=1)
    segs = {
        "cq": p["a_cq"], "ckv": p["a_ckv"], "kr1": kr1, "kr2": kr2, "a_gate": p["a_gate"],
        "b_q": p["b_q"] * (DIFF_QK ** -0.5 * LOG2E), "b_k": p["b_k"], "b_gate": p["b_gate"],
        "c_q": p["c_q"] * (HEAD_DIM ** -0.5 * LOG2E), "c_k": p["c_k"], "c_qidx": p["c_qidx"],
        "c_kidx4": jnp.tile(p["c_kidx"], (1, LANES // IDX_DIM)),
        "c_widx": jnp.concatenate([p["c_widx"], z(LANES - IDX_HEADS)], axis=1),
        "c_gate": p["c_gate"], "d_b": p["d_b"], "d_c": p["d_c"], "d_h": p["d_h"], "d_gate": p["d_gate"],
    }
    w1 = jnp.concatenate([segs[name] for name, _ in _SEGS], axis=1).astype(BF16)
    wvt = jnp.concatenate([p["b_v"], p["c_v"]], axis=1).T.astype(BF16)
    return w1, wvt


def _prep_mla_up(w_uq, w_ukv):
    scale = (MLA_NOPE + MLA_ROPE) ** -0.5 * LOG2E
    wq = w_uq.reshape(MLA_Q_LORA, N_HEADS, MLA_NOPE + MLA_ROPE) * scale
    zq = jnp.zeros((MLA_Q_LORA, N_HEADS, LANES - MLA_NOPE - MLA_ROPE), w_uq.dtype)
    wq1 = jnp.concatenate([wq, zq], axis=-1).reshape(MLA_Q_LORA, N_HEADS * LANES)
    wq2 = jnp.concatenate([jnp.zeros_like(wq[..., :MLA_NOPE]), _rot_half_cols(wq[..., MLA_NOPE:]), zq],
                          axis=-1).reshape(MLA_Q_LORA, N_HEADS * LANES)
    wkv = w_ukv.reshape(MLA_KV_LORA, N_HEADS, MLA_NOPE + HEAD_DIM)
    wk = jnp.concatenate([wkv[..., :MLA_NOPE], jnp.zeros((MLA_KV_LORA, N_HEADS, LANES - MLA_NOPE), w_ukv.dtype)],
                         axis=-1).reshape(MLA_KV_LORA, N_HEADS * LANES)
    wvt = wkv[..., MLA_NOPE:].reshape(MLA_KV_LORA, N_HEADS * HEAD_DIM).T
    return wq1.astype(BF16), wq2.astype(BF16), wk.astype(BF16), wvt.astype(BF16)


def _rope_tables(seq):
    half = MLA_ROPE // 2
    inv_freq = ROPE_BASE ** (-np.arange(half, dtype=np.float32) / half)
    ang = np.arange(seq, dtype=np.float32)[:, None] * inv_freq[None, :].astype(np.float32)
    cos, sin = np.cos(ang).astype(np.float32), np.sin(ang).astype(np.float32)
    ct = np.zeros((seq, LANES), np.float32)
    st = np.zeros((seq, LANES), np.float32)
    ct[:, :MLA_NOPE] = 1.0
    ct[:, MLA_NOPE:MLA_NOPE + half] = cos
    ct[:, MLA_NOPE + half:MLA_NOPE + MLA_ROPE] = cos
    st[:, MLA_NOPE:MLA_NOPE + half] = sin
    st[:, MLA_NOPE + half:MLA_NOPE + MLA_ROPE] = sin
    return ct, st


def _rel_bucket_np(rel):
    nb = NUM_BUCKETS // 2
    max_exact = nb // 2
    ret = np.where(rel > 0, nb, 0)
    n = np.abs(rel)
    nf = np.maximum(n, max_exact).astype(np.float32)
    large = max_exact + (np.log(nf / np.float32(max_exact)) / np.float32(math.log(MAX_DISTANCE / max_exact))
                         * np.float32(nb - max_exact)).astype(np.int32)
    large = np.minimum(large, nb - 1)
    return (ret + np.where(n < max_exact, n, large)).astype(np.int32)


def _position_constants(seq, tile):
    r = np.arange(tile)
    rel0 = r[:, None] - r[None, :]
    bidx = np.stack([_rel_bucket_np(rel0 - tile), _rel_bucket_np(rel0)]).astype(np.int32)
    far = _rel_bucket_np(np.arange(-(seq - 1), -tile))
    far_bucket = int(far[0]) if far.size else int(_rel_bucket_np(np.array([-tile - 1]))[0])
    assert far.size == 0 or np.all(far == far_bucket)
    mask_add = np.where((r[:, None] // CHUNK) <= (r[None, :] // CHUNK), 0.0, NEG).astype(np.float32)
    return bidx, far_bucket, mask_add


def _bias_kernel(tab_ref, bidx_ref, madd_ref, o_ref, *, far_bucket):
    j = pl.program_id(0)
    c = tab_ref[j, far_bucket]
    for d in range(2):
        idx = bidx_ref[d]
        acc = jnp.zeros(idx.shape, F32)
        for b in range(NUM_BUCKETS):
            acc = jnp.where(idx == b, (tab_ref[j, b] - c) * LOG2E, acc)
        if d == 1:
            acc = acc + madd_ref[...]
        o_ref[0, d] = acc


def _bias_tiles(rel_bias, bidx, mask_add, far_bucket, tile):
    return pl.pallas_call(
        functools.partial(_bias_kernel, far_bucket=far_bucket),
        grid=(N_BIAS_MAPS,),
        in_specs=[
            pl.BlockSpec(memory_space=pltpu.SMEM),
            pl.BlockSpec((2, tile, tile), lambda j: (0, 0, 0)),
            pl.BlockSpec((tile, tile), lambda j: (0, 0)),
        ],
        out_specs=pl.BlockSpec((1, 2, tile, tile), lambda j: (j, 0, 0, 0)),
        out_shape=jax.ShapeDtypeStruct((N_BIAS_MAPS, 2, tile, tile), F32),
        compiler_params=_cparams(("arbitrary",)),
        name="bias_tiles",
    )(rel_bias.T.astype(F32), bidx, mask_add)


def _silu(x):
    return x * (1.0 / (1.0 + jnp.exp(-x)))


def _rms(x, g):
    return x * lax.rsqrt(jnp.mean(x * x, axis=-1, keepdims=True) + NORM_EPS) * g


def _inproj_kernel(x_ref, g_ref, w_ref, wvt_ref, qag_ref, wq1_ref, wq2_ref, kvg_ref, wk_ref, wavt_ref, cos_ref, sin_ref,
                   qa_ref, ka_ref, vat_ref, ga_ref, qb_ref, kb_ref, vbt_ref, gb_ref,
                   qc_ref, kc_ref, vct_ref, qi_ref, ki_ref, wi_ref, gc_ref, yd_ref, ud_ref, *, t):
    h = _rms(x_ref[...], g_ref[...]).astype(BF16)

    def seg(name):
        off, width = _SEG_OFF[name]
        return jnp.dot(h, w_ref[:, off:off + width], preferred_element_type=F32)

    cos = cos_ref[...]
    sin = sin_ref[...]

    cqn = _rms(seg("cq"), qag_ref[...]).astype(BF16)
    q1 = jnp.dot(cqn, wq1_ref[...], preferred_element_type=F32)
    q2 = jnp.dot(cqn, wq2_ref[...], preferred_element_type=F32)
    kr = seg("kr1") * cos + seg("kr2") * sin
    ckvn = _rms(seg("ckv"), kvg_ref[...]).astype(BF16)
    kn = jnp.dot(ckvn, wk_ref[...], preferred_element_type=F32)
    for hd in range(N_HEADS):
        sl = slice(hd * LANES, (hd + 1) * LANES)
        qa_ref[:, sl] = (q1[:, sl] * cos + q2[:, sl] * sin).astype(BF16)
        ka_ref[:, sl] = (kn[:, sl] + kr).astype(BF16)
    ga_ref[...] = _silu(seg("a_gate")).astype(BF16)

    for c in range(x_ref.shape[0] // t):
        rows = slice(c * t, (c + 1) * t)
        vat_ref[c] = lax.dot_general(wavt_ref[...], ckvn[rows], _NT, preferred_element_type=F32).astype(BF16)
        vt = lax.dot_general(wvt_ref[...], h[rows], _NT, preferred_element_type=F32).astype(BF16)
        vbt_ref[c] = vt[:GROUP_WIDTH]
        vct_ref[c] = vt[GROUP_WIDTH:]

    qb_ref[...] = seg("b_q").astype(BF16)
    kb_ref[...] = seg("b_k").astype(BF16)
    gb_ref[...] = _silu(seg("b_gate")).astype(BF16)

    qc_ref[...] = seg("c_q").astype(BF16)
    kc_ref[...] = seg("c_k").astype(BF16)
    qi_ref[...] = seg("c_qidx").astype(BF16)
    ki_ref[...] = seg("c_kidx4").astype(BF16)
    wi_ref[...] = seg("c_widx")
    gc_ref[...] = _silu(seg("c_gate")).astype(BF16)

    yd_ref[...] = (seg("d_b") * _silu(seg("d_gate"))).astype(BF16)
    ud_ref[...] = seg("d_c") * seg("d_h")


def _inproj(x2, g, w1, wvt, qag, wq1, wq2, kvg, wk, wavt, cos_t, sin_t, seq, t):
    n = x2.shape[0]
    tm = min(PROJ_ROWS, seq)
    tiles_per_seq = seq // tm
    row = lambda w: pl.BlockSpec((tm, w), lambda i: (i, 0))
    full = lambda a: pl.BlockSpec(a.shape, lambda i: (0,) * a.ndim)
    tab = pl.BlockSpec((tm, LANES), lambda i: (i % tiles_per_seq, 0))
    vt_spec = pl.BlockSpec((tm // t, GROUP_WIDTH, t), lambda i: (i, 0, 0))
    vt_shape = jax.ShapeDtypeStruct((n // t, GROUP_WIDTH, t), BF16)
    outs = (("qa", 512, BF16), ("ka", 512, BF16), ("vat", None, None), ("ga", 256, BF16),
            ("qb", 256, BF16), ("kb", 256, BF16), ("vbt", None, None), ("gb", 256, BF16),
            ("qc", 256, BF16), ("kc", 256, BF16), ("vct", None, None), ("qi", 256, BF16),
            ("ki", 128, BF16), ("wi", 128, F32), ("gc", 256, BF16), ("yd", 256, BF16), ("ud", 256, F32))
    res = pl.pallas_call(
        functools.partial(_inproj_kernel, t=t),
        grid=(n // tm,),
        in_specs=[row(D_MODEL), full(g), full(w1), full(wvt), full(qag), full(wq1), full(wq2), full(kvg), full(wk),
                  full(wavt), tab, tab],
        out_specs=[vt_spec if w is None else row(w) for _, w, _ in outs],
        out_shape=[vt_shape if w is None else jax.ShapeDtypeStruct((n, w), dt) for _, w, dt in outs],
        compiler_params=_cparams(("parallel",)),
        name="inproj",
    )(x2, g, w1, wvt, qag, wq1, wq2, kvg, wk, wavt, cos_t, sin_t)
    return {name: o for (name, _, _), o in zip(outs, res)}


def _flash_scratch(n_maps, t):
    return [pltpu.VMEM((n_maps, 1, t), F32), pltpu.VMEM((n_maps, 1, t), F32), pltpu.VMEM((n_maps, HEAD_DIM, t), F32),
            pltpu.VMEM((n_maps, 2 * t, t), F32), pltpu.VMEM((n_maps, 2 * t, t), F32)]


def _flash_update(s, vts, old):
    t = vts[0].shape[1]
    m_tile = jnp.max(s, axis=0, keepdims=True)
    m_new = m_tile if old is None else jnp.maximum(old[0], m_tile)
    p = jnp.exp2(s - m_new)
    l_tile = jnp.sum(p, axis=0, keepdims=True)
    pb = p.astype(BF16)
    pv = jnp.dot(vts[0], pb[0:t], preferred_element_type=F32)
    for d in range(1, len(vts)):
        pv = pv + jnp.dot(vts[d], pb[d * t:(d + 1) * t], preferred_element_type=F32)
    if old is None:
        return m_new, l_tile, pv
    alpha = jnp.exp2(old[0] - m_new)
    return m_new, alpha * old[1] + l_tile, alpha * old[2] + pv


def _flash_sweep(i, t, n_maps, qs, keys, values, state, diag_add, near_add=None, mask=None):
    m_ref, l_ref, acc_ref, buf_a, buf_b = state
    n_far = i if near_add is None else jnp.maximum(i - 1, 0)
    n_units = n_far // 2

    def load_state():
        return [(m_ref[j], l_ref[j], acc_ref[j]) for j in range(n_maps)]

    def store_state(cur):
        for j in range(n_maps):
            m_ref[j], l_ref[j], acc_ref[j] = cur[j]

    def tile_logits(j, kt, n_tiles, shared, extra=None):
        rows = pl.ds(pl.multiple_of(kt * t, t), n_tiles * t)
        s = lax.dot_general(keys(j, rows), qs[j], _NT, preferred_element_type=F32)
        add = shared
        if extra is not None:
            add = extra(j) if add is None else add + extra(j)
        return s if add is None else s + add

    def shared_mask(kt, n_tiles):
        if mask is None:
            return None
        tiles = [mask(kt + d) for d in range(n_tiles)]
        return tiles[0] if n_tiles == 1 else jnp.concatenate(tiles, axis=0)

    def far_unit(u):
        return 2 * jnp.minimum(u, n_units - 1)

    def run(units, first=False, prefetch=False):
        cur = [None] * n_maps if first else load_state()
        shared = [shared_mask(kt, n_tiles) for kt, n_tiles, _ in units]
        items = [(u, j) for u in range(len(units)) for j in range(n_maps)]
        if prefetch:
            items += [(None, j) for j in range(n_maps)]
            shared.append(shared_mask(0, 2))

        def logits(u, j):
            if u is None:
                return tile_logits(j, 0, 2, shared[-1])
            kt, n_tiles, extra = units[u]
            return tile_logits(j, kt, n_tiles, shared[u], extra)

        queue = [logits(*it) for it in items[:FLASH_LOOKAHEAD]]
        for n, (u, j) in enumerate(items):
            if n + FLASH_LOOKAHEAD < len(items):
                queue.append(logits(*items[n + FLASH_LOOKAHEAD]))
            s = queue.pop(0)
            if u is None:
                buf_a[j] = s
            else:
                kt, n_tiles, _ = units[u]
                cur[j] = _flash_update(s, [values(j, kt + d) for d in range(n_tiles)], cur[j])
        store_state(cur)

    def first_run(prefetch):
        if near_add is None:
            return lambda: run([(i, 1, diag_add)], first=True, prefetch=prefetch)
        return lambda: run([(i, 1, diag_add), (i - 1, 1, near_add)], first=True, prefetch=prefetch)

    if near_add is None:
        pl.when(n_units == 0)(first_run(False))
    else:
        pl.when(i == 0)(lambda: run([(i, 1, diag_add)], first=True))
        pl.when((i >= 1) & (n_units == 0))(first_run(False))
    pl.when(n_units >= 1)(first_run(True))

    def consume(buf, kt, cur):
        for j in range(n_maps):
            cur[j] = _flash_update(buf[j], [values(j, kt), values(j, kt + 1)], cur[j])

    def far_body(c, carry):
        cur = load_state()
        for buf_in, buf_out, u in ((buf_a, buf_b, 2 * c), (buf_b, buf_a, 2 * c + 1)):
            kt_next = far_unit(u + 1)
            shared = shared_mask(kt_next, 2)
            for j in range(n_maps):
                buf_out[j] = tile_logits(j, kt_next, 2, shared)
                cur[j] = _flash_update(buf_in[j], [values(j, 2 * u), values(j, 2 * u + 1)], cur[j])
        store_state(cur)
        return carry

    lax.fori_loop(0, n_units // 2, far_body, 0)

    @pl.when(n_units % 2 == 1)
    def _():
        cur = load_state()
        consume(buf_a, 2 * (n_units - 1), cur)
        store_state(cur)

    pl.when(n_far % 2 == 1)(lambda: run([(n_far - 1, 1, None)]))
    return [acc_ref[j] / l_ref[j] for j in range(n_maps)]


def _lane_mask(lo, hi, dtype):
    lane = lax.broadcasted_iota(jnp.int32, (1, LANES), 1)
    return ((lane >= lo) & (lane < hi)).astype(dtype)


def _store_gated(o_ref, g_ref, heads_t):
    o = jnp.concatenate(heads_t, axis=0).T
    o_ref[...] = (o * g_ref[...].astype(F32)).astype(BF16)


def _att_specs(t, seq, widths_q, widths_k):
    nq = seq // t
    qspec = lambda w: pl.BlockSpec((t, w), lambda b, i: (b * nq + i, 0))
    kspec = lambda w: pl.BlockSpec((seq, w), lambda b, i: (b, 0))
    vspec = pl.BlockSpec((nq, GROUP_WIDTH, t), lambda b, i: (b, 0, 0))
    return [qspec(w) for w in widths_q], [kspec(w) for w in widths_k], vspec, qspec(GROUP_WIDTH)


def _mla_kernel(q_ref, k_ref, vt_ref, g_ref, madd_ref, o_ref, *state, t):
    i = pl.program_id(1)
    qs = [q_ref[:, hd * LANES:(hd + 1) * LANES] for hd in range(N_HEADS)]
    keys = lambda hd, rows: k_ref[rows, hd * LANES:(hd + 1) * LANES]
    values = lambda hd, kt: vt_ref[kt, hd * HEAD_DIM:(hd + 1) * HEAD_DIM, :]
    outs = _flash_sweep(i, t, N_HEADS, qs, keys, values, state, lambda hd: madd_ref[...])
    _store_gated(o_ref, g_ref, outs)


def _mla(p, mask_add, batch, seq, t):
    qs, ks, vspec, ospec = _att_specs(t, seq, (512, 256), (512,))
    return pl.pallas_call(
        functools.partial(_mla_kernel, t=t),
        grid=(batch, seq // t),
        in_specs=[qs[0], ks[0], vspec, qs[1], pl.BlockSpec((t, t), lambda b, i: (0, 0))],
        out_specs=ospec,
        out_shape=jax.ShapeDtypeStruct((batch * seq, GROUP_WIDTH), BF16),
        scratch_shapes=_flash_scratch(N_HEADS, t),
        compiler_params=_cparams(("parallel", "arbitrary")),
        name="mla_attention",
    )(p["qa"], p["ka"], p["vat"], p["ga"], mask_add)


def _diff_kernel(q_ref, k_ref, vt_ref, g_ref, bias_ref, lam_ref, subg_ref, o_ref, *state, t, lambda_init):
    i = pl.program_id(1)
    lp = lam_ref[...]
    lam = (jnp.exp(jnp.sum(lp[0:1] * lp[1:2], axis=-1, keepdims=True))
           - jnp.exp(jnp.sum(lp[2:3] * lp[3:4], axis=-1, keepdims=True)) + lambda_init)
    qs = [q_ref[:, (j // 4) * LANES:(j // 4 + 1) * LANES] * _lane_mask((j % 4) * DIFF_QK, (j % 4 + 1) * DIFF_QK, BF16)
          for j in range(N_DIFF_MAPS)]
    keys = lambda j, rows: k_ref[rows, (j // 4) * LANES:(j // 4 + 1) * LANES]
    values = lambda j, kt: vt_ref[kt, (j // 2) * HEAD_DIM:(j // 2 + 1) * HEAD_DIM, :]
    maps = _flash_sweep(i, t, N_DIFF_MAPS, qs, keys, values, state, lambda j: bias_ref[j, 1], lambda j: bias_ref[j, 0])
    outs = []
    for hd in range(N_HEADS):
        o = maps[2 * hd] - lam * maps[2 * hd + 1]
        ms = jnp.mean(o * o, axis=0, keepdims=True)
        outs.append(o * lax.rsqrt(ms + NORM_EPS) * subg_ref[...] * (1.0 - lambda_init))
    _store_gated(o_ref, g_ref, outs)


def _diff(p, bias, lam_params, subln_g, lambda_init, batch, seq, t):
    qs, ks, vspec, ospec = _att_specs(t, seq, (256, 256), (256,))
    subg = jnp.broadcast_to(subln_g.astype(F32)[:, None], (HEAD_DIM, t))
    return pl.pallas_call(
        functools.partial(_diff_kernel, t=t, lambda_init=lambda_init),
        grid=(batch, seq // t),
        in_specs=[qs[0], ks[0], vspec, qs[1],
                  pl.BlockSpec((N_DIFF_MAPS, 2, t, t), lambda b, i: (0, 0, 0, 0)),
                  pl.BlockSpec((4, DIFF_QK), lambda b, i: (0, 0)),
                  pl.BlockSpec((HEAD_DIM, t), lambda b, i: (0, 0))],
        out_specs=ospec,
        out_shape=jax.ShapeDtypeStruct((batch * seq, GROUP_WIDTH), BF16),
        scratch_shapes=_flash_scratch(N_DIFF_MAPS, t),
        compiler_params=_cparams(("parallel", "arbitrary")),
        name="diff_attention",
    )(p["qb"], p["kb"], p["vbt"], p["gb"], bias, lam_params.astype(F32), subg)


def _sort_key(x):
    b = lax.bitcast_convert_type(x + 0.0, jnp.int32)
    return b ^ ((b >> 31) & 0x7FFFFFFF)


def _dsa_kernel(q_ref, k_ref, vt_ref, qi_ref, ki_ref, wi_ref, g_ref, bias_ref, tri_ref, o_ref,
                key_ref, hi_ref, lo_ref, *state, t, topk):
    i = pl.program_id(1)
    n_kt = i + 1
    idx_scale = (IDX_HEADS ** -0.5) * (IDX_DIM ** -0.5)

    wit = wi_ref[...].T * idx_scale
    qm, wrow = [], []
    for ih in range(IDX_HEADS):
        grp = slice((ih // 4) * LANES, (ih // 4 + 1) * LANES)
        lo = (ih % 4) * IDX_DIM
        qm.append(qi_ref[:, grp] * _lane_mask(lo, lo + IDX_DIM, BF16))
        wrow.append(wit[ih:ih + 1, :])

    def idx_tile(kt):
        kk = ki_ref[pl.ds(pl.multiple_of(kt * t, t), t), :]
        tot = jnp.zeros((t, t), F32)
        for ih in range(IDX_HEADS):
            y = lax.dot_general(kk, qm[ih], _NT, preferred_element_type=F32)
            tot = tot + wrow[ih] * jnp.maximum(y, 0.0)
        return _sort_key(tot)

    def store_keys(kt, key):
        key_ref[kt] = key
        hi_ref[kt] = (key >> HALF_BITS).astype(jnp.int16)
        lo_ref[kt] = ((key & HALF_MASK) - HALF_BIAS).astype(jnp.int16)

    def key_body(kt, c):
        store_keys(kt, idx_tile(kt))
        return c

    lax.fori_loop(0, i, key_body, 0)
    kchunk = lax.broadcasted_iota(jnp.int32, (t, t), 0) // CHUNK
    qchunk = lax.broadcasted_iota(jnp.int32, (t, t), 1) // CHUNK
    store_keys(i, jnp.where(kchunk <= qchunk, idx_tile(i), INT_MIN))

    def count(pred):
        def body(kt, c):
            hit = jnp.where(pred(key_ref[kt]), 1.0, 0.0)
            return c + jnp.sum(hit.reshape(t // SUBLANES, SUBLANES, t), axis=0)
        part = lax.fori_loop(0, n_kt, body, jnp.zeros((SUBLANES, t), F32))
        return jnp.sum(part, axis=0, keepdims=True)

    rows16 = 2 * SUBLANES
    n_acc = 4

    def count16(ref, pred):
        one, zero = jnp.int16(1), jnp.int16(0)

        def body(kt, accs):
            x = ref[kt].reshape(t // rows16, rows16, t)
            accs = list(accs)
            for r in range(t // rows16):
                accs[r % n_acc] = accs[r % n_acc] + jnp.where(pred(x[r]), one, zero)
            return tuple(accs)
        accs = lax.fori_loop(0, n_kt, body, tuple(jnp.zeros((rows16, t), jnp.int16) for _ in range(n_acc)))
        tot = (accs[0] + accs[1]) + (accs[2] + accs[3])
        return jnp.sum(tot.astype(jnp.int32), axis=0, keepdims=True)

    def select16(ref, need):
        def bit_body(b, res):
            cand = res | (jnp.int32(1) << (HALF_BITS - 1 - b))
            cand16 = (cand - HALF_BIAS).astype(jnp.int16)
            cnt = count16(ref, lambda x: x >= cand16)
            return jnp.where(cnt >= need, cand, res)
        return lax.fori_loop(0, HALF_BITS, bit_body, jnp.zeros((1, t), jnp.int32)) - HALF_BIAS

    hi_thr = select16(hi_ref, topk)
    hi_thr16 = hi_thr.astype(jnp.int16)
    need_lo = topk - count16(hi_ref, lambda x: x > hi_thr16)

    def member_body(kt, c):
        lo_ref[kt] = jnp.where(hi_ref[kt] == hi_thr16, lo_ref[kt], jnp.int16(-HALF_BIAS))
        return c

    lax.fori_loop(0, n_kt, member_body, 0)
    lo_thr = select16(lo_ref, need_lo)
    thr = (hi_thr << HALF_BITS) | (lo_thr + HALF_BIAS)
    cnt_ge = count(lambda key: key >= thr)
    has_ties = jnp.max(cnt_ge) > topk

    @pl.when(jnp.logical_not(has_ties))
    def _():
        def body(kt, c):
            key_ref[kt] = jnp.where(key_ref[kt] >= thr, 0, NEG_BITS)
            return c
        lax.fori_loop(0, n_kt, body, 0)

    @pl.when(has_ties)
    def _():
        need = topk - count(lambda key: key > thr)

        def body(kt, seen):
            key = key_ref[kt]
            eq = jnp.where(key == thr, 1.0, 0.0)
            rank = seen + jnp.dot(tri_ref[...], eq.astype(BF16), preferred_element_type=F32)
            sel = (key > thr) | ((key == thr) & (rank <= need))
            key_ref[kt] = jnp.where(sel, 0, NEG_BITS)
            return seen + jnp.sum(eq, axis=0, keepdims=True)
        lax.fori_loop(0, n_kt, body, jnp.zeros((1, t), F32))

    qs = [q_ref[:, (hd // 2) * LANES:(hd // 2 + 1) * LANES]
          * _lane_mask((hd % 2) * HEAD_DIM, (hd % 2 + 1) * HEAD_DIM, BF16) for hd in range(N_HEADS)]
    keys = lambda hd, rows: k_ref[rows, (hd // 2) * LANES:(hd // 2 + 1) * LANES]
    values = lambda hd, kt: vt_ref[kt, hd * HEAD_DIM:(hd + 1) * HEAD_DIM, :]
    outs = _flash_sweep(i, t, N_HEADS, qs, keys, values, state, lambda hd: bias_ref[hd, 1], lambda hd: bias_ref[hd, 0],
                        mask=lambda kt: lax.bitcast_convert_type(key_ref[kt], F32))
    _store_gated(o_ref, g_ref, outs)


def _dsa(p, bias, batch, seq, t):
    topk = min(TOPK_MAX, seq // 4)
    nq = seq // t
    qs, ks, vspec, ospec = _att_specs(t, seq, (256, 256, 128, 256), (256, 128))
    tri = jnp.asarray(np.tril(np.ones((t, t), np.float32)), BF16)
    return pl.pallas_call(
        functools.partial(_dsa_kernel, t=t, topk=topk),
        grid=(batch, nq),
        in_specs=[qs[0], ks[0], vspec, qs[1], ks[1], qs[2], qs[3],
                  pl.BlockSpec((N_HEADS, 2, t, t), lambda b, i: (2, 0, 0, 0)),
                  pl.BlockSpec((t, t), lambda b, i: (0, 0))],
        out_specs=ospec,
        out_shape=jax.ShapeDtypeStruct((batch * seq, GROUP_WIDTH), BF16),
        scratch_shapes=[pltpu.VMEM((nq, t, t), jnp.int32), pltpu.VMEM((nq, t, t), jnp.int16),
                        pltpu.VMEM((nq, t, t), jnp.int16)] + _flash_scratch(N_HEADS, t),
        compiler_params=_cparams(("parallel", "arbitrary")),
        name="dsa_attention",
    )(p["qc"], p["kc"], p["vct"], p["qi"], p["ki"], p["wi"], p["gc"], bias, tri)


def _outproj_kernel(x_ref, ya_ref, yb_ref, yc_ref, yd_ref, ud_ref, halo_ref, cw_ref, w_ref, fg_ref, o_ref,
                    *, tiles_per_seq, final):
    i = pl.program_id(0)
    tm = x_ref.shape[0]
    u = ud_ref[...]
    halo = jnp.where(i % tiles_per_seq == 0, 0.0, halo_ref[...])
    row8 = lax.broadcasted_iota(jnp.int32, (SUBLANES, 1), 0)

    def shifted(k):
        r = pltpu.roll(u, k, 0)
        head = jnp.where(row8 < k, pltpu.roll(halo, k, 0), r[0:SUBLANES])
        return jnp.concatenate([head, r[SUBLANES:tm]], axis=0)

    cw = cw_ref[...]
    conv = cw[0:1] * shifted(2) + cw[1:2] * shifted(1) + cw[2:3] * u
    yd = (conv * yd_ref[...].astype(F32)).astype(BF16)
    acc = x_ref[...]
    for g, y in enumerate((ya_ref[...], yb_ref[...], yc_ref[...], yd)):
        acc = acc + jnp.dot(y, w_ref[g * GROUP_WIDTH:(g + 1) * GROUP_WIDTH, :], preferred_element_type=F32)
    if final:
        acc = _rms(acc, fg_ref[...])
    o_ref[...] = acc


def _outproj(x2, ya, yb, yc, yd, ud, conv_w, w_out, final_g, seq, final):
    n = x2.shape[0]
    tm = min(PROJ_ROWS, seq)
    tiles_per_seq = seq // tm
    row = lambda w: pl.BlockSpec((tm, w), lambda i: (i, 0))
    full = lambda a: pl.BlockSpec(a.shape, lambda i: (0,) * a.ndim)
    halo = pl.BlockSpec((SUBLANES, GROUP_WIDTH), lambda i: (jnp.maximum(i * (tm // SUBLANES) - 1, 0), 0))
    return pl.pallas_call(
        functools.partial(_outproj_kernel, tiles_per_seq=tiles_per_seq, final=final),
        grid=(n // tm,),
        in_specs=[row(D_MODEL), row(256), row(256), row(256), row(256), row(256), halo,
                  full(conv_w), full(w_out), full(final_g)],
        out_specs=row(D_MODEL),
        out_shape=jax.ShapeDtypeStruct((n, D_MODEL), F32),
        compiler_params=_cparams(("parallel",)),
        name="outproj",
    )(x2, ya, yb, yc, yd, ud, ud, conv_w, w_out, final_g)


def kernel(x, norm_g, w_in, mla_qa_g, mla_w_uq, mla_kva_g, mla_w_ukv, diff_lambda, diff_subln_g, conv_w, w_out,
           rel_bias, final_g):
    batch, seq, _ = x.shape
    depth = w_in.shape[0]
    t = min(ATT_TILE, seq)
    assert seq % t == 0 and t % CHUNK == 0 and seq % min(PROJ_ROWS, seq) == 0 and min(PROJ_ROWS, seq) % t == 0

    bidx, far_bucket, mask_add = _position_constants(seq, t)
    cos_t, sin_t = _rope_tables(seq)
    bias = _bias_tiles(rel_bias, jnp.asarray(bidx), jnp.asarray(mask_add), far_bucket, t)
    mask_add = jnp.asarray(mask_add)
    cos_t, sin_t = jnp.asarray(cos_t), jnp.asarray(sin_t)

    x2 = x.reshape(batch * seq, D_MODEL)
    for l in range(depth):
        lambda_init = 0.8 - 0.6 * math.exp(-0.3 * l)
        w1, wvt = _prep_w_in(w_in[l])
        wq1, wq2, wk, wavt = _prep_mla_up(mla_w_uq[l], mla_w_ukv[l])
        p = _inproj(x2, norm_g[l][None, :], w1, wvt, mla_qa_g[l][None, :], wq1, wq2, mla_kva_g[l][None, :], wk, wavt,
                    cos_t, sin_t, seq, t)
        ya = _mla(p, mask_add, batch, seq, t)
        yb = _diff(p, bias, diff_lambda[l], diff_subln_g[l], lambda_init, batch, seq, t)
        yc = _dsa(p, bias, batch, seq, t)
        x2 = _outproj(x2, ya, yb, yc, p["yd"], p["ud"], conv_w[l], w_out[l].astype(BF16), final_g[None, :],
                      seq, final=(l == depth - 1))
    return x2.reshape(batch, seq, D_MODEL)
```

```python
import functools
import math

import numpy as np
import jax
import jax.numpy as jnp
from jax import lax
from jax.experimental import pallas as pl
from jax.experimental.pallas import tpu as pltpu

F32 = jnp.float32
BF16 = jnp.bfloat16

D_MODEL = 1024
CHUNK = 64
N_HEADS = 4
HEAD_DIM = 64
GROUP_WIDTH = N_HEADS * HEAD_DIM
MLA_Q_LORA = 256
MLA_KV_LORA = 128
MLA_NOPE = 64
MLA_ROPE = 32
ROPE_BASE = 10000.0
DIFF_QK = 32
IDX_HEADS = 8
IDX_DIM = 32
TOPK_MAX = 256
CONV_WIDTH = 3
NUM_BUCKETS = 32
MAX_DISTANCE = 128
N_DIFF_MAPS = 2 * N_HEADS
N_BIAS_MAPS = N_DIFF_MAPS + N_HEADS
NORM_EPS = 1e-6
NEG = -1e30
LOG2E = math.log2(math.e)
NEG_BITS = int(np.float32(NEG).view(np.int32))
INT_MIN = -2 ** 31
HALF_BITS = 16
HALF_MASK = 2 ** HALF_BITS - 1
HALF_BIAS = 2 ** (HALF_BITS - 1)

LANES = 128
SUBLANES = 8
ATT_TILE = 256
PROJ_ROWS = 512
FLASH_LOOKAHEAD = 4
VMEM_LIMIT = 56 * 1024 * 1024

_IN_SPLITS = (
    ("a_cq", 256), ("a_ckv", 128), ("a_krope", 32), ("a_gate", 256),
    ("b_q", 256), ("b_k", 256), ("b_v", 256), ("b_gate", 256),
    ("c_q", 256), ("c_k", 256), ("c_v", 256),
    ("c_qidx", 256), ("c_kidx", 32), ("c_widx", 8), ("c_gate", 256),
    ("d_b", 256), ("d_c", 256), ("d_h", 256), ("d_gate", 256),
)

_SEGS = (
    ("cq", 256), ("ckv", 128), ("kr1", 128), ("kr2", 128), ("a_gate", 256),
    ("b_q", 256), ("b_k", 256), ("b_gate", 256),
    ("c_q", 256), ("c_k", 256), ("c_qidx", 256), ("c_kidx4", 128), ("c_widx", 128),
    ("c_gate", 256), ("d_b", 256), ("d_c", 256), ("d_h", 256), ("d_gate", 256),
)
_SEG_OFF = {}
_off = 0
for _name, _w in _SEGS:
    _SEG_OFF[_name] = (_off, _w)
    _off += _w
W1_COLS = _off

_NT = (((1,), (1,)), ((), ()))


def _cparams(sem):
    return pltpu.CompilerParams(dimension_semantics=sem, vmem_limit_bytes=VMEM_LIMIT)


def _split_cols(w):
    out, off = {}, 0
    for name, width in _IN_SPLITS:
        out[name] = w[:, off:off + width]
        off += width
    return out


def _rot_half_cols(w):
    half = MLA_ROPE // 2
    return jnp.concatenate([-w[..., half:], w[..., :half]], axis=-1)


def _prep_w_in(w):
    p = _split_cols(w)
    z = lambda n: jnp.zeros((w.shape[0], n), w.dtype)
    kr1 = jnp.concatenate([z(MLA_NOPE), p["a_krope"], z(LANES - MLA_NOPE - MLA_ROPE)], axis=1)
    kr2 = jnp.concatenate([z(MLA_NOPE), _rot_half_cols(p["a_krope"]), z(LANES - MLA_NOPE - MLA_ROPE)], axis=1)
    segs = {
        "cq": p["a_cq"], "ckv": p["a_ckv"], "kr1": kr1, "kr2": kr2, "a_gate": p["a_gate"],
        "b_q": p["b_q"] * (DIFF_QK ** -0.5 * LOG2E), "b_k": p["b_k"], "b_gate": p["b_gate"],
        "c_q": p["c_q"] * (HEAD_DIM ** -0.5 * LOG2E), "c_k": p["c_k"], "c_qidx": p["c_qidx"],
        "c_kidx4": jnp.tile(p["c_kidx"], (1, LANES // IDX_DIM)),
        "c_widx": jnp.concatenate([p["c_widx"], z(LANES - IDX_HEADS)], axis=1),
        "c_gate": p["c_gate"], "d_b": p["d_b"], "d_c": p["d_c"], "d_h": p["d_h"], "d_gate": p["d_gate"],
    }
    w1 = jnp.concatenate([segs[name].astype(BF16) for name, _ in _SEGS], axis=1)
    wvt = jnp.concatenate([p["b_v"].astype(BF16), p["c_v"].astype(BF16)], axis=1).T
    return w1, wvt


def _prep_mla_up(w_uq, w_ukv):
    scale = (MLA_NOPE + MLA_ROPE) ** -0.5 * LOG2E
    wq = w_uq.reshape(MLA_Q_LORA, N_HEADS, MLA_NOPE + MLA_ROPE) * scale
    zq = jnp.zeros((MLA_Q_LORA, N_HEADS, LANES - MLA_NOPE - MLA_ROPE), w_uq.dtype)
    wq1 = jnp.concatenate([wq, zq], axis=-1).reshape(MLA_Q_LORA, N_HEADS * LANES)
    wq2 = jnp.concatenate([jnp.zeros_like(wq[..., :MLA_NOPE]), _rot_half_cols(wq[..., MLA_NOPE:]), zq],
                          axis=-1).reshape(MLA_Q_LORA, N_HEADS * LANES)
    wkv = w_ukv.reshape(MLA_KV_LORA, N_HEADS, MLA_NOPE + HEAD_DIM)
    wk = jnp.concatenate([wkv[..., :MLA_NOPE], jnp.zeros((MLA_KV_LORA, N_HEADS, LANES - MLA_NOPE), w_ukv.dtype)],
                         axis=-1).reshape(MLA_KV_LORA, N_HEADS * LANES)
    wvt = wkv[..., MLA_NOPE:].reshape(MLA_KV_LORA, N_HEADS * HEAD_DIM).T
    return wq1.astype(BF16), wq2.astype(BF16), wk.astype(BF16), wvt.astype(BF16)


def _rope_tables(seq):
    half = MLA_ROPE // 2
    inv_freq = ROPE_BASE ** (-np.arange(half, dtype=np.float32) / half)
    ang = np.arange(seq, dtype=np.float32)[:, None] * inv_freq[None, :].astype(np.float32)
    cos, sin = np.cos(ang).astype(np.float32), np.sin(ang).astype(np.float32)
    ct = np.zeros((seq, LANES), np.float32)
    st = np.zeros((seq, LANES), np.float32)
    ct[:, :MLA_NOPE] = 1.0
    ct[:, MLA_NOPE:MLA_NOPE + half] = cos
    ct[:, MLA_NOPE + half:MLA_NOPE + MLA_ROPE] = cos
    st[:, MLA_NOPE:MLA_NOPE + half] = sin
    st[:, MLA_NOPE + half:MLA_NOPE + MLA_ROPE] = sin
    return ct, st


def _rel_bucket_np(rel):
    nb = NUM_BUCKETS // 2
    max_exact = nb // 2
    ret = np.where(rel > 0, nb, 0)
    n = np.abs(rel)
    nf = np.maximum(n, max_exact).astype(np.float32)
    large = max_exact + (np.log(nf / np.float32(max_exact)) / np.float32(math.log(MAX_DISTANCE / max_exact))
                         * np.float32(nb - max_exact)).astype(np.int32)
    large = np.minimum(large, nb - 1)
    return (ret + np.where(n < max_exact, n, large)).astype(np.int32)


def _position_constants(seq, tile):
    r = np.arange(tile)
    rel0 = r[:, None] - r[None, :]
    bidx = np.stack([_rel_bucket_np(rel0 - tile), _rel_bucket_np(rel0)]).astype(np.int32)
    far = _rel_bucket_np(np.arange(-(seq - 1), -tile))
    far_bucket = int(far[0]) if far.size else int(_rel_bucket_np(np.array([-tile - 1]))[0])
    assert far.size == 0 or np.all(far == far_bucket)
    mask_add = np.where((r[:, None] // CHUNK) <= (r[None, :] // CHUNK), 0.0, NEG).astype(np.float32)
    return bidx, far_bucket, mask_add


def _bias_kernel(tab_ref, bidx_ref, madd_ref, o_ref, *, far_bucket):
    j = pl.program_id(0)
    c = tab_ref[j, far_bucket]
    for d in range(2):
        idx = bidx_ref[d]
        acc = jnp.zeros(idx.shape, F32)
        for b in range(NUM_BUCKETS):
            acc = jnp.where(idx == b, (tab_ref[j, b] - c) * LOG2E, acc)
        if d == 1:
            acc = acc + madd_ref[...]
        o_ref[0, d] = acc


def _bias_tiles(rel_bias, bidx, mask_add, far_bucket, tile):
    return pl.pallas_call(
        functools.partial(_bias_kernel, far_bucket=far_bucket),
        grid=(N_BIAS_MAPS,),
        in_specs=[
            pl.BlockSpec(memory_space=pltpu.SMEM),
            pl.BlockSpec((2, tile, tile), lambda j: (0, 0, 0)),
            pl.BlockSpec((tile, tile), lambda j: (0, 0)),
        ],
        out_specs=pl.BlockSpec((1, 2, tile, tile), lambda j: (j, 0, 0, 0)),
        out_shape=jax.ShapeDtypeStruct((N_BIAS_MAPS, 2, tile, tile), F32),
        compiler_params=_cparams(("arbitrary",)),
        name="bias_tiles",
    )(rel_bias.T.astype(F32), bidx, mask_add)


def _silu(x):
    return x * (1.0 / (1.0 + jnp.exp(-x)))


def _rms(x, g):
    return x * lax.rsqrt(jnp.mean(x * x, axis=-1, keepdims=True) + NORM_EPS) * g


def _inproj_kernel(x_ref, g_ref, w_ref, wvt_ref, qag_ref, wq1_ref, wq2_ref, kvg_ref, wk_ref, wavt_ref, cos_ref, sin_ref,
                   qa_ref, ka_ref, vat_ref, ga_ref, qb_ref, kb_ref, vbt_ref, gb_ref,
                   qc_ref, kc_ref, vct_ref, qi_ref, ki_ref, wi_ref, gc_ref, yd_ref, ud_ref, *, t):
    h = _rms(x_ref[...], g_ref[...]).astype(BF16)

    def seg(name):
        off, width = _SEG_OFF[name]
        return jnp.dot(h, w_ref[:, off:off + width], preferred_element_type=F32)

    cos = cos_ref[...]
    sin = sin_ref[...]

    cqn = _rms(seg("cq"), qag_ref[...]).astype(BF16)
    q1 = jnp.dot(cqn, wq1_ref[...], preferred_element_type=F32)
    q2 = jnp.dot(cqn, wq2_ref[...], preferred_element_type=F32)
    kr = seg("kr1") * cos + seg("kr2") * sin
    ckvn = _rms(seg("ckv"), kvg_ref[...]).astype(BF16)
    kn = jnp.dot(ckvn, wk_ref[...], preferred_element_type=F32)
    for hd in range(N_HEADS):
        sl = slice(hd * LANES, (hd + 1) * LANES)
        qa_ref[:, sl] = (q1[:, sl] * cos + q2[:, sl] * sin).astype(BF16)
        ka_ref[:, sl] = (kn[:, sl] + kr).astype(BF16)
    ga_ref[...] = _silu(seg("a_gate")).astype(BF16)

    for c in range(x_ref.shape[0] // t):
        rows = slice(c * t, (c + 1) * t)
        vat_ref[c] = lax.dot_general(wavt_ref[...], ckvn[rows], _NT, preferred_element_type=F32).astype(BF16)
        vt = lax.dot_general(wvt_ref[...], h[rows], _NT, preferred_element_type=F32).astype(BF16)
        vbt_ref[c] = vt[:GROUP_WIDTH]
        vct_ref[c] = vt[GROUP_WIDTH:]

    qb_ref[...] = seg("b_q").astype(BF16)
    kb_ref[...] = seg("b_k").astype(BF16)
    gb_ref[...] = _silu(seg("b_gate")).astype(BF16)

    qc_ref[...] = seg("c_q").astype(BF16)
    kc_ref[...] = seg("c_k").astype(BF16)
    qi_ref[...] = seg("c_qidx").astype(BF16)
    ki_ref[...] = seg("c_kidx4").astype(BF16)
    wi_ref[...] = seg("c_widx")
    gc_ref[...] = _silu(seg("c_gate")).astype(BF16)

    yd_ref[...] = (seg("d_b") * _silu(seg("d_gate"))).astype(BF16)
    ud_ref[...] = seg("d_c") * seg("d_h")


def _inproj(x2, g, w1, wvt, qag, wq1, wq2, kvg, wk, wavt, cos_t, sin_t, seq, t):
    n = x2.shape[0]
    tm = min(PROJ_ROWS, seq)
    tiles_per_seq = seq // tm
    row = lambda w: pl.BlockSpec((tm, w), lambda i: (i, 0))
    full = lambda a: pl.BlockSpec(a.shape, lambda i: (0,) * a.ndim)
    tab = pl.BlockSpec((tm, LANES), lambda i: (i % tiles_per_seq, 0))
    vt_spec = pl.BlockSpec((tm // t, GROUP_WIDTH, t), lambda i: (i, 0, 0))
    vt_shape = jax.ShapeDtypeStruct((n // t, GROUP_WIDTH, t), BF16)
    outs = (("qa", 512, BF16), ("ka", 512, BF16), ("vat", None, None), ("ga", 256, BF16),
            ("qb", 256, BF16), ("kb", 256, BF16), ("vbt", None, None), ("gb", 256, BF16),
            ("qc", 256, BF16), ("kc", 256, BF16), ("vct", None, None), ("qi", 256, BF16),
            ("ki", 128, BF16), ("wi", 128, F32), ("gc", 256, BF16), ("yd", 256, BF16), ("ud", 256, F32))
    res = pl.pallas_call(
        functools.partial(_inproj_kernel, t=t),
        grid=(n // tm,),
        in_specs=[row(D_MODEL), full(g), full(w1), full(wvt), full(qag), full(wq1), full(wq2), full(kvg), full(wk),
                  full(wavt), tab, tab],
        out_specs=[vt_spec if w is None else row(w) for _, w, _ in outs],
        out_shape=[vt_shape if w is None else jax.ShapeDtypeStruct((n, w), dt) for _, w, dt in outs],
        compiler_params=_cparams(("parallel",)),
        name="inproj",
    )(x2, g, w1, wvt, qag, wq1, wq2, kvg, wk, wavt, cos_t, sin_t)
    return {name: o for (name, _, _), o in zip(outs, res)}


def _flash_scratch(n_maps, t):
    return [pltpu.VMEM((n_maps, 1, t), F32), pltpu.VMEM((n_maps, 1, t), F32), pltpu.VMEM((n_maps, HEAD_DIM, t), F32),
            pltpu.VMEM((n_maps, 2 * t, t), F32), pltpu.VMEM((n_maps, 2 * t, t), F32)]


def _flash_update(s, vts, old):
    t = vts[0].shape[1]
    m_tile = jnp.max(s, axis=0, keepdims=True)
    m_new = m_tile if old is None else jnp.maximum(old[0], m_tile)
    p = jnp.exp2(s - m_new)
    l_tile = jnp.sum(p, axis=0, keepdims=True)
    pb = p.astype(BF16)
    pv = jnp.dot(vts[0], pb[0:t], preferred_element_type=F32)
    for d in range(1, len(vts)):
        pv = pv + jnp.dot(vts[d], pb[d * t:(d + 1) * t], preferred_element_type=F32)
    if old is None:
        return m_new, l_tile, pv
    alpha = jnp.exp2(old[0] - m_new)
    return m_new, alpha * old[1] + l_tile, alpha * old[2] + pv


def _flash_sweep(i, t, n_maps, qs, keys, values, state, diag_add, near_add=None, mask=None):
    m_ref, l_ref, acc_ref, buf_a, buf_b = state
    n_far = i if near_add is None else jnp.maximum(i - 1, 0)
    n_units = n_far // 2

    def load_state():
        return [(m_ref[j], l_ref[j], acc_ref[j]) for j in range(n_maps)]

    def store_state(cur):
        for j in range(n_maps):
            m_ref[j], l_ref[j], acc_ref[j] = cur[j]

    def tile_logits(j, kt, n_tiles, shared, extra=None):
        rows = pl.ds(pl.multiple_of(kt * t, t), n_tiles * t)
        s = lax.dot_general(keys(j, rows), qs[j], _NT, preferred_element_type=F32)
        add = shared
        if extra is not None:
            add = extra(j) if add is None else add + extra(j)
        return s if add is None else s + add

    def shared_mask(kt, n_tiles):
        if mask is None:
            return None
        tiles = [mask(kt + d) for d in range(n_tiles)]
        return tiles[0] if n_tiles == 1 else jnp.concatenate(tiles, axis=0)

    def far_unit(u):
        return 2 * jnp.minimum(u, n_units - 1)

    def run(units, first=False, prefetch=False):
        cur = [None] * n_maps if first else load_state()
        shared = [shared_mask(kt, n_tiles) for kt, n_tiles, _ in units]
        items = [(u, j) for u in range(len(units)) for j in range(n_maps)]
        if prefetch:
            items += [(None, j) for j in range(n_maps)]
            shared.append(shared_mask(0, 2))

        def logits(u, j):
            if u is None:
                return tile_logits(j, 0, 2, shared[-1])
            kt, n_tiles, extra = units[u]
            return tile_logits(j, kt, n_tiles, shared[u], extra)

        queue = [logits(*it) for it in items[:FLASH_LOOKAHEAD]]
        for n, (u, j) in enumerate(items):
            if n + FLASH_LOOKAHEAD < len(items):
                queue.append(logits(*items[n + FLASH_LOOKAHEAD]))
            s = queue.pop(0)
            if u is None:
                buf_a[j] = s
            else:
                kt, n_tiles, _ = units[u]
                cur[j] = _flash_update(s, [values(j, kt + d) for d in range(n_tiles)], cur[j])
        store_state(cur)

    def first_run(prefetch):
        if near_add is None:
            return lambda: run([(i, 1, diag_add)], first=True, prefetch=prefetch)
        return lambda: run([(i, 1, diag_add), (i - 1, 1, near_add)], first=True, prefetch=prefetch)

    if near_add is None:
        pl.when(n_units == 0)(first_run(False))
    else:
        pl.when(i == 0)(lambda: run([(i, 1, diag_add)], first=True))
        pl.when((i >= 1) & (n_units == 0))(first_run(False))
    pl.when(n_units >= 1)(first_run(True))

    def consume(buf, kt, cur):
        for j in range(n_maps):
            cur[j] = _flash_update(buf[j], [values(j, kt), values(j, kt + 1)], cur[j])

    def far_body(c, carry):
        cur = load_state()
        for buf_in, buf_out, u in ((buf_a, buf_b, 2 * c), (buf_b, buf_a, 2 * c + 1)):
            kt_next = far_unit(u + 1)
            shared = shared_mask(kt_next, 2)
            for j in range(n_maps):
                buf_out[j] = tile_logits(j, kt_next, 2, shared)
                cur[j] = _flash_update(buf_in[j], [values(j, 2 * u), values(j, 2 * u + 1)], cur[j])
        store_state(cur)
        return carry

    lax.fori_loop(0, n_units // 2, far_body, 0)

    @pl.when(n_units % 2 == 1)
    def _():
        cur = load_state()
        consume(buf_a, 2 * (n_units - 1), cur)
        store_state(cur)

    pl.when(n_far % 2 == 1)(lambda: run([(n_far - 1, 1, None)]))
    return [acc_ref[j] / l_ref[j] for j in range(n_maps)]


def _lane_mask(lo, hi, dtype):
    lane = lax.broadcasted_iota(jnp.int32, (1, LANES), 1)
    return ((lane >= lo) & (lane < hi)).astype(dtype)


def _store_gated(o_ref, g_ref, heads_t):
    o = jnp.concatenate(heads_t, axis=0).T
    o_ref[...] = (o * g_ref[...].astype(F32)).astype(BF16)


def _att_specs(t, seq, widths_q, widths_k):
    nq = seq // t
    qspec = lambda w: pl.BlockSpec((t, w), lambda b, i: (b * nq + i, 0))
    kspec = lambda w: pl.BlockSpec((seq, w), lambda b, i: (b, 0))
    vspec = pl.BlockSpec((nq, GROUP_WIDTH, t), lambda b, i: (b, 0, 0))
    return [qspec(w) for w in widths_q], [kspec(w) for w in widths_k], vspec, qspec(GROUP_WIDTH)


def _mla_kernel(q_ref, k_ref, vt_ref, g_ref, madd_ref, o_ref, *state, t):
    i = pl.program_id(1)
    qs = [q_ref[:, hd * LANES:(hd + 1) * LANES] for hd in range(N_HEADS)]
    keys = lambda hd, rows: k_ref[rows, hd * LANES:(hd + 1) * LANES]
    values = lambda hd, kt: vt_ref[kt, hd * HEAD_DIM:(hd + 1) * HEAD_DIM, :]
    outs = _flash_sweep(i, t, N_HEADS, qs, keys, values, state, lambda hd: madd_ref[...])
    _store_gated(o_ref, g_ref, outs)


def _mla(p, mask_add, batch, seq, t):
    qs, ks, vspec, ospec = _att_specs(t, seq, (512, 256), (512,))
    return pl.pallas_call(
        functools.partial(_mla_kernel, t=t),
        grid=(batch, seq // t),
        in_specs=[qs[0], ks[0], vspec, qs[1], pl.BlockSpec((t, t), lambda b, i: (0, 0))],
        out_specs=ospec,
        out_shape=jax.ShapeDtypeStruct((batch * seq, GROUP_WIDTH), BF16),
        scratch_shapes=_flash_scratch(N_HEADS, t),
        compiler_params=_cparams(("parallel", "arbitrary")),
        name="mla_attention",
    )(p["qa"], p["ka"], p["vat"], p["ga"], mask_add)


def _diff_kernel(q_ref, k_ref, vt_ref, g_ref, bias_ref, lam_ref, subg_ref, o_ref, *state, t, lambda_init):
    i = pl.program_id(1)
    lp = lam_ref[...]
    lam = (jnp.exp(jnp.sum(lp[0:1] * lp[1:2], axis=-1, keepdims=True))
           - jnp.exp(jnp.sum(lp[2:3] * lp[3:4], axis=-1, keepdims=True)) + lambda_init)
    qs = [q_ref[:, (j // 4) * LANES:(j // 4 + 1) * LANES] * _lane_mask((j % 4) * DIFF_QK, (j % 4 + 1) * DIFF_QK, BF16)
          for j in range(N_DIFF_MAPS)]
    keys = lambda j, rows: k_ref[rows, (j // 4) * LANES:(j // 4 + 1) * LANES]
    values = lambda j, kt: vt_ref[kt, (j // 2) * HEAD_DIM:(j // 2 + 1) * HEAD_DIM, :]
    maps = _flash_sweep(i, t, N_DIFF_MAPS, qs, keys, values, state, lambda j: bias_ref[j, 1], lambda j: bias_ref[j, 0])
    outs = []
    for hd in range(N_HEADS):
        o = maps[2 * hd] - lam * maps[2 * hd + 1]
        ms = jnp.mean(o * o, axis=0, keepdims=True)
        outs.append(o * lax.rsqrt(ms + NORM_EPS) * subg_ref[...] * (1.0 - lambda_init))
    _store_gated(o_ref, g_ref, outs)


def _diff(p, bias, lam_params, subln_g, lambda_init, batch, seq, t):
    qs, ks, vspec, ospec = _att_specs(t, seq, (256, 256), (256,))
    subg = jnp.broadcast_to(subln_g.astype(F32)[:, None], (HEAD_DIM, t))
    return pl.pallas_call(
        functools.partial(_diff_kernel, t=t, lambda_init=lambda_init),
        grid=(batch, seq // t),
        in_specs=[qs[0], ks[0], vspec, qs[1],
                  pl.BlockSpec((N_DIFF_MAPS, 2, t, t), lambda b, i: (0, 0, 0, 0)),
                  pl.BlockSpec((4, DIFF_QK), lambda b, i: (0, 0)),
                  pl.BlockSpec((HEAD_DIM, t), lambda b, i: (0, 0))],
        out_specs=ospec,
        out_shape=jax.ShapeDtypeStruct((batch * seq, GROUP_WIDTH), BF16),
        scratch_shapes=_flash_scratch(N_DIFF_MAPS, t),
        compiler_params=_cparams(("parallel", "arbitrary")),
        name="diff_attention",
    )(p["qb"], p["kb"], p["vbt"], p["gb"], bias, lam_params.astype(F32), subg)


def _sort_key(x):
    b = lax.bitcast_convert_type(x + 0.0, jnp.int32)
    return b ^ ((b >> 31) & 0x7FFFFFFF)


def _dsa_kernel(q_ref, k_ref, vt_ref, qi_ref, ki_ref, wi_ref, g_ref, bias_ref, tri_ref, o_ref,
                key_ref, hi_ref, lo_ref, *state, t, topk):
    i = pl.program_id(1)
    n_kt = i + 1
    idx_scale = (IDX_HEADS ** -0.5) * (IDX_DIM ** -0.5)

    wit = wi_ref[...].T * idx_scale
    qm, wrow = [], []
    for ih in range(IDX_HEADS):
        grp = slice((ih // 4) * LANES, (ih // 4 + 1) * LANES)
        lo = (ih % 4) * IDX_DIM
        qm.append(qi_ref[:, grp] * _lane_mask(lo, lo + IDX_DIM, BF16))
        wrow.append(wit[ih:ih + 1, :])

    def idx_tile(kt):
        kk = ki_ref[pl.ds(pl.multiple_of(kt * t, t), t), :]
        dot = lambda ih: lax.dot_general(kk, qm[ih], _NT, preferred_element_type=F32)
        queue = [dot(ih) for ih in range(FLASH_LOOKAHEAD)]
        tot = None
        for ih in range(IDX_HEADS):
            if ih + FLASH_LOOKAHEAD < IDX_HEADS:
                queue.append(dot(ih + FLASH_LOOKAHEAD))
            term = wrow[ih] * jnp.maximum(queue.pop(0), 0.0)
            tot = term if tot is None else tot + term
        return _sort_key(tot)

    def store_keys(kt, key):
        key_ref[kt] = key
        hi_ref[kt] = (key >> HALF_BITS).astype(jnp.int16)
        lo_ref[kt] = ((key & HALF_MASK) - HALF_BIAS).astype(jnp.int16)

    def key_tiles(first, n):
        for d in range(n):
            store_keys(first + d, idx_tile(first + d))

    def key_quad_body(c, carry):
        key_tiles(4 * c, 4)
        return carry

    lax.fori_loop(0, i // 4, key_quad_body, 0)
    pl.when(i % 4 >= 2)(lambda: key_tiles(i - i % 4, 2))
    pl.when(i % 2 == 1)(lambda: key_tiles(i - 1, 1))
    kchunk = lax.broadcasted_iota(jnp.int32, (t, t), 0) // CHUNK
    qchunk = lax.broadcasted_iota(jnp.int32, (t, t), 1) // CHUNK
    store_keys(i, jnp.where(kchunk <= qchunk, idx_tile(i), INT_MIN))

    def count(pred):
        def body(kt, c):
            hit = jnp.where(pred(key_ref[kt]), 1.0, 0.0)
            return c + jnp.sum(hit.reshape(t // SUBLANES, SUBLANES, t), axis=0)
        part = lax.fori_loop(0, n_kt, body, jnp.zeros((SUBLANES, t), F32))
        return jnp.sum(part, axis=0, keepdims=True)

    rows16 = 2 * SUBLANES
    n_acc = 4

    def count16(ref, pred):
        one, zero = jnp.int16(1), jnp.int16(0)

        def body(kt, accs):
            x = ref[kt].reshape(t // rows16, rows16, t)
            accs = list(accs)
            for r in range(t // rows16):
                accs[r % n_acc] = accs[r % n_acc] + jnp.where(pred(x[r]), one, zero)
            return tuple(accs)
        accs = lax.fori_loop(0, n_kt, body, tuple(jnp.zeros((rows16, t), jnp.int16) for _ in range(n_acc)))
        tot = (accs[0] + accs[1]) + (accs[2] + accs[3])
        return jnp.sum(tot.astype(jnp.int32), axis=0, keepdims=True)

    def select16(ref, need):
        def bit_body(b, res):
            cand = res | (jnp.int32(1) << (HALF_BITS - 1 - b))
            cand16 = (cand - HALF_BIAS).astype(jnp.int16)
            cnt = count16(ref, lambda x: x >= cand16)
            return jnp.where(cnt >= need, cand, res)
        return lax.fori_loop(0, HALF_BITS, bit_body, jnp.zeros((1, t), jnp.int32)) - HALF_BIAS

    hi_thr = select16(hi_ref, topk)
    hi_thr16 = hi_thr.astype(jnp.int16)
    need_lo = topk - count16(hi_ref, lambda x: x > hi_thr16)

    def member_body(kt, c):
        lo_ref[kt] = jnp.where(hi_ref[kt] == hi_thr16, lo_ref[kt], jnp.int16(-HALF_BIAS))
        return c

    lax.fori_loop(0, n_kt, member_body, 0)
    lo_thr = select16(lo_ref, need_lo)
    thr = (hi_thr << HALF_BITS) | (lo_thr + HALF_BIAS)
    cnt_ge = count(lambda key: key >= thr)
    has_ties = jnp.max(cnt_ge) > topk

    @pl.when(jnp.logical_not(has_ties))
    def _():
        def body(kt, c):
            key_ref[kt] = jnp.where(key_ref[kt] >= thr, 0, NEG_BITS)
            return c
        lax.fori_loop(0, n_kt, body, 0)

    @pl.when(has_ties)
    def _():
        need = topk - count(lambda key: key > thr)

        def body(kt, seen):
            key = key_ref[kt]
            eq = jnp.where(key == thr, 1.0, 0.0)
            rank = seen + jnp.dot(tri_ref[...], eq.astype(BF16), preferred_element_type=F32)
            sel = (key > thr) | ((key == thr) & (rank <= need))
            key_ref[kt] = jnp.where(sel, 0, NEG_BITS)
            return seen + jnp.sum(eq, axis=0, keepdims=True)
        lax.fori_loop(0, n_kt, body, jnp.zeros((1, t), F32))

    qs = [q_ref[:, (hd // 2) * LANES:(hd // 2 + 1) * LANES]
          * _lane_mask((hd % 2) * HEAD_DIM, (hd % 2 + 1) * HEAD_DIM, BF16) for hd in range(N_HEADS)]
    keys = lambda hd, rows: k_ref[rows, (hd // 2) * LANES:(hd // 2 + 1) * LANES]
    values = lambda hd, kt: vt_ref[kt, hd * HEAD_DIM:(hd + 1) * HEAD_DIM, :]
    outs = _flash_sweep(i, t, N_HEADS, qs, keys, values, state, lambda hd: bias_ref[hd, 1], lambda hd: bias_ref[hd, 0],
                        mask=lambda kt: lax.bitcast_convert_type(key_ref[kt], F32))
    _store_gated(o_ref, g_ref, outs)


def _dsa(p, bias, batch, seq, t):
    topk = min(TOPK_MAX, seq // 4)
    nq = seq // t
    qs, ks, vspec, ospec = _att_specs(t, seq, (256, 256, 128, 256), (256, 128))
    tri = jnp.asarray(np.tril(np.ones((t, t), np.float32)), BF16)
    return pl.pallas_call(
        functools.partial(_dsa_kernel, t=t, topk=topk),
        grid=(batch, nq),
        in_specs=[qs[0], ks[0], vspec, qs[1], ks[1], qs[2], qs[3],
                  pl.BlockSpec((N_HEADS, 2, t, t), lambda b, i: (2, 0, 0, 0)),
                  pl.BlockSpec((t, t), lambda b, i: (0, 0))],
        out_specs=ospec,
        out_shape=jax.ShapeDtypeStruct((batch * seq, GROUP_WIDTH), BF16),
        scratch_shapes=[pltpu.VMEM((nq, t, t), jnp.int32), pltpu.VMEM((nq, t, t), jnp.int16),
                        pltpu.VMEM((nq, t, t), jnp.int16)] + _flash_scratch(N_HEADS, t),
        compiler_params=_cparams(("parallel", "arbitrary")),
        name="dsa_attention",
    )(p["qc"], p["kc"], p["vct"], p["qi"], p["ki"], p["wi"], p["gc"], bias, tri)


def _outproj_kernel(x_ref, ya_ref, yb_ref, yc_ref, yd_ref, ud_ref, halo_ref, cw_ref, w_ref, fg_ref, o_ref,
                    *, tiles_per_seq, final):
    i = pl.program_id(0)
    tm = x_ref.shape[0]
    u = ud_ref[...]
    halo = jnp.where(i % tiles_per_seq == 0, 0.0, halo_ref[...])
    row8 = lax.broadcasted_iota(jnp.int32, (SUBLANES, 1), 0)

    def shifted(k):
        r = pltpu.roll(u, k, 0)
        head = jnp.where(row8 < k, pltpu.roll(halo, k, 0), r[0:SUBLANES])
        return jnp.concatenate([head, r[SUBLANES:tm]], axis=0)

    cw = cw_ref[...]
    conv = cw[0:1] * shifted(2) + cw[1:2] * shifted(1) + cw[2:3] * u
    yd = (conv * yd_ref[...].astype(F32)).astype(BF16)
    acc = x_ref[...]
    for g, y in enumerate((ya_ref[...], yb_ref[...], yc_ref[...], yd)):
        acc = acc + jnp.dot(y, w_ref[g * GROUP_WIDTH:(g + 1) * GROUP_WIDTH, :], preferred_element_type=F32)
    if final:
        acc = _rms(acc, fg_ref[...])
    o_ref[...] = acc


def _outproj(x2, ya, yb, yc, yd, ud, conv_w, w_out, final_g, seq, final):
    n = x2.shape[0]
    tm = min(PROJ_ROWS, seq)
    tiles_per_seq = seq // tm
    row = lambda w: pl.BlockSpec((tm, w), lambda i: (i, 0))
    full = lambda a: pl.BlockSpec(a.shape, lambda i: (0,) * a.ndim)
    halo = pl.BlockSpec((SUBLANES, GROUP_WIDTH), lambda i: (jnp.maximum(i * (tm // SUBLANES) - 1, 0), 0))
    return pl.pallas_call(
        functools.partial(_outproj_kernel, tiles_per_seq=tiles_per_seq, final=final),
        grid=(n // tm,),
        in_specs=[row(D_MODEL), row(256), row(256), row(256), row(256), row(256), halo,
                  full(conv_w), full(w_out), full(final_g)],
        out_specs=row(D_MODEL),
        out_shape=jax.ShapeDtypeStruct((n, D_MODEL), F32),
        compiler_params=_cparams(("parallel",)),
        name="outproj",
    )(x2, ya, yb, yc, yd, ud, ud, conv_w, w_out, final_g)


def kernel(x, norm_g, w_in, mla_qa_g, mla_w_uq, mla_kva_g, mla_w_ukv, diff_lambda, diff_subln_g, conv_w, w_out,
           rel_bias, final_g):
    batch, seq, _ = x.shape
    depth = w_in.shape[0]
    t = min(ATT_TILE, seq)
    assert seq % t == 0 and t % CHUNK == 0 and seq % min(PROJ_ROWS, seq) == 0 and min(PROJ_ROWS, seq) % t == 0

    bidx, far_bucket, mask_add = _position_constants(seq, t)
    cos_t, sin_t = _rope_tables(seq)
    bias = _bias_tiles(rel_bias, jnp.asarray(bidx), jnp.asarray(mask_add), far_bucket, t)
    mask_add = jnp.asarray(mask_add)
    cos_t, sin_t = jnp.asarray(cos_t), jnp.asarray(sin_t)

    x2 = x.reshape(batch * seq, D_MODEL)
    for l in range(depth):
        lambda_init = 0.8 - 0.6 * math.exp(-0.3 * l)
        w1, wvt = _prep_w_in(w_in[l])
        wq1, wq2, wk, wavt = _prep_mla_up(mla_w_uq[l], mla_w_ukv[l])
        p = _inproj(x2, norm_g[l][None, :], w1, wvt, mla_qa_g[l][None, :], wq1, wq2, mla_kva_g[l][None, :], wk, wavt,
                    cos_t, sin_t, seq, t)
        ya = _mla(p, mask_add, batch, seq, t)
        yb = _diff(p, bias, diff_lambda[l], diff_subln_g[l], lambda_init, batch, seq, t)
        yc = _dsa(p, bias, batch, seq, t)
        x2 = _outproj(x2, ya, yb, yc, p["yd"], p["ud"], conv_w[l], w_out[l].astype(BF16), final_g[None, :],
                      seq, final=(l == depth - 1))
    return x2.reshape(batch, seq, D_MODEL)
```

```python
import functools
import math

import numpy as np
import jax
import jax.numpy as jnp
from jax import lax
from jax.experimental import pallas as pl
from jax.experimental.pallas import tpu as pltpu

F32 = jnp.float32
BF16 = jnp.bfloat16

D_MODEL = 1024
CHUNK = 64
N_HEADS = 4
HEAD_DIM = 64
GROUP_WIDTH = N_HEADS * HEAD_DIM
MLA_Q_LORA = 256
MLA_KV_LORA = 128
MLA_NOPE = 64
MLA_ROPE = 32
ROPE_BASE = 10000.0
DIFF_QK = 32
IDX_HEADS = 8
IDX_DIM = 32
TOPK_MAX = 256
CONV_WIDTH = 3
NUM_BUCKETS = 32
MAX_DISTANCE = 128
N_DIFF_MAPS = 2 * N_HEADS
N_BIAS_MAPS = N_DIFF_MAPS + N_HEADS
NORM_EPS = 1e-6
NEG = -1e30
LOG2E = math.log2(math.e)
NEG_BITS = int(np.float32(NEG).view(np.int32))
INT_MIN = -2 ** 31
HALF_BITS = 16
HALF_MASK = 2 ** HALF_BITS - 1
HALF_BIAS = 2 ** (HALF_BITS - 1)

LANES = 128
SUBLANES = 8
ATT_TILE = 512
FAR_UNIT_KEYS = 512
PROJ_ROWS = 512
FLASH_LOOKAHEAD = 4
VMEM_LIMIT = 56 * 1024 * 1024

_IN_SPLITS = (
    ("a_cq", 256), ("a_ckv", 128), ("a_krope", 32), ("a_gate", 256),
    ("b_q", 256), ("b_k", 256), ("b_v", 256), ("b_gate", 256),
    ("c_q", 256), ("c_k", 256), ("c_v", 256),
    ("c_qidx", 256), ("c_kidx", 32), ("c_widx", 8), ("c_gate", 256),
    ("d_b", 256), ("d_c", 256), ("d_h", 256), ("d_gate", 256),
)

_SEGS = (
    ("cq", 256), ("ckv", 128), ("kr1", 128), ("kr2", 128), ("c_kidx4", 128), ("a_gate", 256),
    ("b_q", 256), ("b_k", 256), ("b_gate", 256),
    ("c_q", 256), ("c_k", 256), ("c_qidx", 256),
    ("c_gate", 256), ("d_b", 256), ("d_c", 256), ("d_h", 256), ("d_gate", 256), ("c_widx", 128),
)
_SEG_OFF = {}
_off = 0
for _name, _w in _SEGS:
    _SEG_OFF[_name] = (_off, _w)
    _off += _w
W1_COLS = _off

_NT = (((1,), (1,)), ((), ()))
_ONE_BUFFER = pl.Buffered(1)


def _cparams(sem):
    return pltpu.CompilerParams(dimension_semantics=sem, vmem_limit_bytes=VMEM_LIMIT)


def _split_cols(w):
    out, off = {}, 0
    for name, width in _IN_SPLITS:
        out[name] = w[:, off:off + width]
        off += width
    return out


def _rot_half_cols(w):
    half = MLA_ROPE // 2
    return jnp.concatenate([-w[..., half:], w[..., :half]], axis=-1)


def _prep_w_in(w):
    p = _split_cols(w)
    z = lambda n: jnp.zeros((w.shape[0], n), w.dtype)
    kr1 = jnp.concatenate([z(MLA_NOPE), p["a_krope"], z(LANES - MLA_NOPE - MLA_ROPE)], axis=1)
    kr2 = jnp.concatenate([z(MLA_NOPE), _rot_half_cols(p["a_krope"]), z(LANES - MLA_NOPE - MLA_ROPE)], axis=1)
    segs = {
        "cq": p["a_cq"], "ckv": p["a_ckv"], "kr1": kr1, "kr2": kr2, "a_gate": p["a_gate"],
        "b_q": p["b_q"] * (DIFF_QK ** -0.5 * LOG2E), "b_k": p["b_k"], "b_gate": p["b_gate"],
        "c_q": p["c_q"] * (HEAD_DIM ** -0.5 * LOG2E), "c_k": p["c_k"], "c_qidx": p["c_qidx"],
        "c_kidx4": jnp.tile(p["c_kidx"], (1, LANES // IDX_DIM)),
        "c_widx": jnp.concatenate([p["c_widx"], z(LANES - IDX_HEADS)], axis=1),
        "c_gate": p["c_gate"], "d_b": p["d_b"], "d_c": p["d_c"], "d_h": p["d_h"], "d_gate": p["d_gate"],
    }
    w1 = jnp.concatenate([segs[name].astype(BF16) for name, _ in _SEGS], axis=1)
    wvt = jnp.concatenate([p["b_v"].astype(BF16), p["c_v"].astype(BF16)], axis=1).T
    return w1, wvt


def _prep_mla_up(w_uq, w_ukv):
    scale = (MLA_NOPE + MLA_ROPE) ** -0.5 * LOG2E
    wq = w_uq.reshape(MLA_Q_LORA, N_HEADS, MLA_NOPE + MLA_ROPE) * scale
    zq = jnp.zeros((MLA_Q_LORA, N_HEADS, LANES - MLA_NOPE - MLA_ROPE), w_uq.dtype)
    wq1 = jnp.concatenate([wq, zq], axis=-1).reshape(MLA_Q_LORA, N_HEADS * LANES)
    wq2 = jnp.concatenate([jnp.zeros_like(wq[..., :MLA_NOPE]), _rot_half_cols(wq[..., MLA_NOPE:]), zq],
                          axis=-1).reshape(MLA_Q_LORA, N_HEADS * LANES)
    wkv = w_ukv.reshape(MLA_KV_LORA, N_HEADS, MLA_NOPE + HEAD_DIM)
    wk = jnp.concatenate([wkv[..., :MLA_NOPE], jnp.zeros((MLA_KV_LORA, N_HEADS, LANES - MLA_NOPE), w_ukv.dtype)],
                         axis=-1).reshape(MLA_KV_LORA, N_HEADS * LANES)
    wvt = wkv[..., MLA_NOPE:].reshape(MLA_KV_LORA, N_HEADS * HEAD_DIM).T
    return wq1.astype(BF16), wq2.astype(BF16), wk.astype(BF16), wvt.astype(BF16)


def _rope_tables(seq):
    half = MLA_ROPE // 2
    inv_freq = ROPE_BASE ** (-np.arange(half, dtype=np.float32) / half)
    ang = np.arange(seq, dtype=np.float32)[:, None] * inv_freq[None, :].astype(np.float32)
    cos, sin = np.cos(ang).astype(np.float32), np.sin(ang).astype(np.float32)
    ct = np.zeros((seq, LANES), np.float32)
    st = np.zeros((seq, LANES), np.float32)
    ct[:, :MLA_NOPE] = 1.0
    ct[:, MLA_NOPE:MLA_NOPE + half] = cos
    ct[:, MLA_NOPE + half:MLA_NOPE + MLA_ROPE] = cos
    st[:, MLA_NOPE:MLA_NOPE + half] = sin
    st[:, MLA_NOPE + half:MLA_NOPE + MLA_ROPE] = sin
    return ct, st


def _rel_bucket_np(rel):
    nb = NUM_BUCKETS // 2
    max_exact = nb // 2
    ret = np.where(rel > 0, nb, 0)
    n = np.abs(rel)
    nf = np.maximum(n, max_exact).astype(np.float32)
    large = max_exact + (np.log(nf / np.float32(max_exact)) / np.float32(math.log(MAX_DISTANCE / max_exact))
                         * np.float32(nb - max_exact)).astype(np.int32)
    large = np.minimum(large, nb - 1)
    return (ret + np.where(n < max_exact, n, large)).astype(np.int32)


def _position_constants(seq, tile):
    r = np.arange(tile)
    rel0 = r[:, None] - r[None, :]
    bidx = np.stack([_rel_bucket_np(rel0 - tile), _rel_bucket_np(rel0)]).astype(np.int32)
    far = _rel_bucket_np(np.arange(-(seq - 1), -tile))
    far_bucket = int(far[0]) if far.size else int(_rel_bucket_np(np.array([-tile - 1]))[0])
    assert far.size == 0 or np.all(far == far_bucket)
    mask_add = np.where((r[:, None] // CHUNK) <= (r[None, :] // CHUNK), 0.0, NEG).astype(np.float32)
    return bidx, far_bucket, mask_add


def _bias_kernel(tab_ref, bidx_ref, madd_ref, o_ref, *, far_bucket, buckets):
    j = pl.program_id(0)
    c = tab_ref[j, far_bucket]
    for d in range(2):
        idx = bidx_ref[d]
        acc = jnp.zeros(idx.shape, F32)
        for b in buckets[d]:
            acc = jnp.where(idx == b, (tab_ref[j, b] - c) * LOG2E, acc)
        if d == 1:
            acc = acc + madd_ref[...]
        o_ref[0, d] = acc


def _bias_tiles(rel_bias, bidx, mask_add, far_bucket, tile):
    buckets = tuple(tuple(int(b) for b in np.unique(bidx[d])) for d in range(2))
    return pl.pallas_call(
        functools.partial(_bias_kernel, far_bucket=far_bucket, buckets=buckets),
        grid=(N_BIAS_MAPS,),
        in_specs=[
            pl.BlockSpec(memory_space=pltpu.SMEM),
            pl.BlockSpec((2, tile, tile), lambda j: (0, 0, 0)),
            pl.BlockSpec((tile, tile), lambda j: (0, 0)),
        ],
        out_specs=pl.BlockSpec((1, 2, tile, tile), lambda j: (j, 0, 0, 0)),
        out_shape=jax.ShapeDtypeStruct((N_BIAS_MAPS, 2, tile, tile), F32),
        compiler_params=_cparams(("arbitrary",)),
        name="bias_tiles",
    )(rel_bias.T.astype(F32), jnp.asarray(bidx), jnp.asarray(mask_add))


def _silu(x):
    return x * (1.0 / (1.0 + jnp.exp(-x)))


def _rms(x, g):
    return x * lax.rsqrt(jnp.mean(x * x, axis=-1, keepdims=True) + NORM_EPS) * g


def _inproj_kernel(x_ref, g_ref, w_ref, wvt_ref, qag_ref, wq1_ref, wq2_ref, kvg_ref, wk_ref, wavt_ref, cos_ref, sin_ref,
                   qa_ref, ka_ref, vat_ref, ga_ref, qb_ref, kb_ref, vbt_ref, gb_ref,
                   qc_ref, kc_ref, vct_ref, qi_ref, ki_ref, wi_ref, gc_ref, yd_ref, ud_ref, *, t):
    h = _rms(x_ref[...], g_ref[...]).astype(BF16)

    def seg(name):
        off, width = _SEG_OFF[name]
        return jnp.dot(h, w_ref[:, off:off + width], preferred_element_type=F32)

    def seg_pair(first, second):
        off, width = _SEG_OFF[first]
        assert _SEG_OFF[second] == (off + width, width) and width == LANES
        y = jnp.dot(h, w_ref[:, off:off + 2 * width], preferred_element_type=F32)
        return y[:, :width], y[:, width:]

    cos = cos_ref[...]
    sin = sin_ref[...]

    cqn = _rms(seg("cq"), qag_ref[...]).astype(BF16)
    q1 = jnp.dot(cqn, wq1_ref[...], preferred_element_type=F32)
    q2 = jnp.dot(cqn, wq2_ref[...], preferred_element_type=F32)
    ckv, kr1 = seg_pair("ckv", "kr1")
    kr2, kidx4 = seg_pair("kr2", "c_kidx4")
    kr = kr1 * cos + kr2 * sin
    ckvn = _rms(ckv, kvg_ref[...]).astype(BF16)
    kn = jnp.dot(ckvn, wk_ref[...], preferred_element_type=F32)
    for hd in range(N_HEADS):
        sl = slice(hd * LANES, (hd + 1) * LANES)
        qa_ref[:, sl] = (q1[:, sl] * cos + q2[:, sl] * sin).astype(BF16)
        ka_ref[:, sl] = (kn[:, sl] + kr).astype(BF16)
    ga_ref[...] = _silu(seg("a_gate")).astype(BF16)

    for c in range(x_ref.shape[0] // t):
        rows = slice(c * t, (c + 1) * t)
        vat_ref[c] = lax.dot_general(wavt_ref[...], ckvn[rows], _NT, preferred_element_type=F32).astype(BF16)
        vt = lax.dot_general(wvt_ref[...], h[rows], _NT, preferred_element_type=F32).astype(BF16)
        vbt_ref[c] = vt[:GROUP_WIDTH]
        vct_ref[c] = vt[GROUP_WIDTH:]

    qb_ref[...] = seg("b_q").astype(BF16)
    kb_ref[...] = seg("b_k").astype(BF16)
    gb_ref[...] = _silu(seg("b_gate")).astype(BF16)

    qc_ref[...] = seg("c_q").astype(BF16)
    kc_ref[...] = seg("c_k").astype(BF16)
    qi_ref[...] = seg("c_qidx").astype(BF16)
    ki_ref[...] = kidx4.astype(BF16)
    wi_ref[...] = seg("c_widx")
    gc_ref[...] = _silu(seg("c_gate")).astype(BF16)

    yd_ref[...] = (seg("d_b") * _silu(seg("d_gate"))).astype(BF16)
    ud_ref[...] = seg("d_c") * seg("d_h")


def _inproj(x2, g, w1, wvt, qag, wq1, wq2, kvg, wk, wavt, cos_t, sin_t, seq, t):
    n = x2.shape[0]
    tm = min(PROJ_ROWS, seq)
    tiles_per_seq = seq // tm
    row = lambda w: pl.BlockSpec((tm, w), lambda i: (i, 0))
    full = lambda a: pl.BlockSpec(a.shape, lambda i: (0,) * a.ndim)
    tab = pl.BlockSpec((tm, LANES), lambda i: (i % tiles_per_seq, 0))
    vt_spec = pl.BlockSpec((tm // t, GROUP_WIDTH, t), lambda i: (i, 0, 0))
    vt_shape = jax.ShapeDtypeStruct((n // t, GROUP_WIDTH, t), BF16)
    outs = (("qa", 512, BF16), ("ka", 512, BF16), ("vat", None, None), ("ga", 256, BF16),
            ("qb", 256, BF16), ("kb", 256, BF16), ("vbt", None, None), ("gb", 256, BF16),
            ("qc", 256, BF16), ("kc", 256, BF16), ("vct", None, None), ("qi", 256, BF16),
            ("ki", 128, BF16), ("wi", 128, F32), ("gc", 256, BF16), ("yd", 256, BF16), ("ud", 256, F32))
    res = pl.pallas_call(
        functools.partial(_inproj_kernel, t=t),
        grid=(n // tm,),
        in_specs=[row(D_MODEL), full(g), full(w1), full(wvt), full(qag), full(wq1), full(wq2), full(kvg), full(wk),
                  full(wavt), tab, tab],
        out_specs=[vt_spec if w is None else row(w) for _, w, _ in outs],
        out_shape=[vt_shape if w is None else jax.ShapeDtypeStruct((n, w), dt) for _, w, dt in outs],
        compiler_params=_cparams(("parallel",)),
        name="inproj",
    )(x2, g, w1, wvt, qag, wq1, wq2, kvg, wk, wavt, cos_t, sin_t)
    return {name: o for (name, _, _), o in zip(outs, res)}


def _flash_scratch(n_maps, t):
    unit = _far_unit_tiles(t) * t
    return [pltpu.VMEM((n_maps, 1, t), F32), pltpu.VMEM((n_maps, 1, t), F32), pltpu.VMEM((n_maps, HEAD_DIM, t), F32),
            pltpu.VMEM((n_maps, unit, t), F32), pltpu.VMEM((n_maps, unit, t), F32)]


def _far_unit_tiles(t):
    return max(1, FAR_UNIT_KEYS // t)


def _flash_update(s, vts, old):
    t = vts[0].shape[1]
    m_tile = jnp.max(s, axis=0, keepdims=True)
    m_new = m_tile if old is None else jnp.maximum(old[0], m_tile)
    p = jnp.exp2(s - m_new)
    l_tile = jnp.sum(p, axis=0, keepdims=True)
    pb = p.astype(BF16)
    pv = jnp.dot(vts[0], pb[0:t], preferred_element_type=F32)
    for d in range(1, len(vts)):
        pv = pv + jnp.dot(vts[d], pb[d * t:(d + 1) * t], preferred_element_type=F32)
    if old is None:
        return m_new, l_tile, pv
    alpha = jnp.exp2(old[0] - m_new)
    return m_new, alpha * old[1] + l_tile, alpha * old[2] + pv


def _flash_sweep(i, t, n_maps, qs, keys, values, state, diag_add, near_add=None, mask=None):
    m_ref, l_ref, acc_ref, buf_a, buf_b = state
    n_far = i if near_add is None else jnp.maximum(i - 1, 0)
    ut = _far_unit_tiles(t)
    n_units = n_far // ut

    def load_state():
        return [(m_ref[j], l_ref[j], acc_ref[j]) for j in range(n_maps)]

    def store_state(cur):
        for j in range(n_maps):
            m_ref[j], l_ref[j], acc_ref[j] = cur[j]

    def tile_logits(j, kt, n_tiles, shared, extra=None):
        rows = pl.ds(pl.multiple_of(kt * t, t), n_tiles * t)
        s = lax.dot_general(keys(j, rows), qs[j], _NT, preferred_element_type=F32)
        add = shared
        if extra is not None:
            add = extra(j) if add is None else add + extra(j)
        return s if add is None else s + add

    def shared_mask(kt, n_tiles):
        if mask is None:
            return None
        tiles = [mask(kt + d) for d in range(n_tiles)]
        return tiles[0] if n_tiles == 1 else jnp.concatenate(tiles, axis=0)

    def far_unit(u):
        return ut * jnp.minimum(u, n_units - 1)

    def run(units, first=False, prefetch=False):
        cur = [None] * n_maps if first else load_state()
        shared = [shared_mask(kt, n_tiles) for kt, n_tiles, _ in units]
        items = [(u, j) for u in range(len(units)) for j in range(n_maps)]
        if prefetch:
            items += [(None, j) for j in range(n_maps)]
            shared.append(shared_mask(0, ut))

        def logits(u, j):
            if u is None:
                return tile_logits(j, 0, ut, shared[-1])
            kt, n_tiles, extra = units[u]
            return tile_logits(j, kt, n_tiles, shared[u], extra)

        queue = [logits(*it) for it in items[:FLASH_LOOKAHEAD]]
        for n, (u, j) in enumerate(items):
            if n + FLASH_LOOKAHEAD < len(items):
                queue.append(logits(*items[n + FLASH_LOOKAHEAD]))
            s = queue.pop(0)
            if u is None:
                buf_a[j] = s
            else:
                kt, n_tiles, _ = units[u]
                cur[j] = _flash_update(s, [values(j, kt + d) for d in range(n_tiles)], cur[j])
        store_state(cur)

    def first_run(prefetch):
        if near_add is None:
            return lambda: run([(i, 1, diag_add)], first=True, prefetch=prefetch)
        return lambda: run([(i, 1, diag_add), (i - 1, 1, near_add)], first=True, prefetch=prefetch)

    if near_add is None:
        pl.when(n_units == 0)(first_run(False))
    else:
        pl.when(i == 0)(lambda: run([(i, 1, diag_add)], first=True))
        pl.when((i >= 1) & (n_units == 0))(first_run(False))
    pl.when(n_units >= 1)(first_run(True))

    def consume(buf, kt, j, old):
        return _flash_update(buf[j], [values(j, kt + d) for d in range(ut)], old)

    def far_body(c, carry):
        cur = load_state()
        for buf_in, buf_out, u in ((buf_a, buf_b, 2 * c), (buf_b, buf_a, 2 * c + 1)):
            kt_next = far_unit(u + 1)
            shared = shared_mask(kt_next, ut)
            for j in range(n_maps):
                buf_out[j] = tile_logits(j, kt_next, ut, shared)
                cur[j] = consume(buf_in, ut * u, j, cur[j])
        store_state(cur)
        return carry

    lax.fori_loop(0, n_units // 2, far_body, 0)

    @pl.when(n_units % 2 == 1)
    def _():
        cur = load_state()
        for j in range(n_maps):
            cur[j] = consume(buf_a, ut * (n_units - 1), j, cur[j])
        store_state(cur)

    for r in range(1, ut):
        pl.when(n_far % ut >= r)(lambda r=r: run([(n_far - r, 1, None)]))
    return [acc_ref[j] / l_ref[j] for j in range(n_maps)]


def _lane_mask(lo, hi, dtype):
    lane = lax.broadcasted_iota(jnp.int32, (1, LANES), 1)
    return ((lane >= lo) & (lane < hi)).astype(dtype)


def _store_gated(o_ref, g_ref, heads_t):
    o = jnp.concatenate(heads_t, axis=0).T
    o_ref[...] = (o * g_ref[...].astype(F32)).astype(BF16)


def _att_specs(t, seq, widths_q, widths_k):
    nq = seq // t
    qspec = lambda w: pl.BlockSpec((t, w), lambda b, i: (b * nq + i, 0))
    kspec = lambda w: pl.BlockSpec((seq, w), lambda b, i: (b, 0), pipeline_mode=_ONE_BUFFER)
    vspec = pl.BlockSpec((nq, GROUP_WIDTH, t), lambda b, i: (b, 0, 0), pipeline_mode=_ONE_BUFFER)
    return [qspec(w) for w in widths_q], [kspec(w) for w in widths_k], vspec, qspec(GROUP_WIDTH)


def _mla_kernel(q_ref, k_ref, vt_ref, g_ref, madd_ref, o_ref, *state, t):
    i = pl.program_id(1)
    qs = [q_ref[:, hd * LANES:(hd + 1) * LANES] for hd in range(N_HEADS)]
    keys = lambda hd, rows: k_ref[rows, hd * LANES:(hd + 1) * LANES]
    values = lambda hd, kt: vt_ref[kt, hd * HEAD_DIM:(hd + 1) * HEAD_DIM, :]
    outs = _flash_sweep(i, t, N_HEADS, qs, keys, values, state, lambda hd: madd_ref[...])
    _store_gated(o_ref, g_ref, outs)


def _mla(p, mask_add, batch, seq, t):
    qs, ks, vspec, ospec = _att_specs(t, seq, (512, 256), (512,))
    return pl.pallas_call(
        functools.partial(_mla_kernel, t=t),
        grid=(batch, seq // t),
        in_specs=[qs[0], ks[0], vspec, qs[1], pl.BlockSpec((t, t), lambda b, i: (0, 0), pipeline_mode=_ONE_BUFFER)],
        out_specs=ospec,
        out_shape=jax.ShapeDtypeStruct((batch * seq, GROUP_WIDTH), BF16),
        scratch_shapes=_flash_scratch(N_HEADS, t),
        compiler_params=_cparams(("parallel", "arbitrary")),
        name="mla_attention",
    )(p["qa"], p["ka"], p["vat"], p["ga"], mask_add)


def _diff_kernel(q_ref, k_ref, vt_ref, g_ref, bias_ref, lam_ref, subg_ref, o_ref, *state, t, lambda_init):
    i = pl.program_id(1)
    lp = lam_ref[...]
    lam = (jnp.exp(jnp.sum(lp[0:1] * lp[1:2], axis=-1, keepdims=True))
           - jnp.exp(jnp.sum(lp[2:3] * lp[3:4], axis=-1, keepdims=True)) + lambda_init)
    qs = [q_ref[:, (j // 4) * LANES:(j // 4 + 1) * LANES] * _lane_mask((j % 4) * DIFF_QK, (j % 4 + 1) * DIFF_QK, BF16)
          for j in range(N_DIFF_MAPS)]
    keys = lambda j, rows: k_ref[rows, (j // 4) * LANES:(j // 4 + 1) * LANES]
    values = lambda j, kt: vt_ref[kt, (j // 2) * HEAD_DIM:(j // 2 + 1) * HEAD_DIM, :]
    maps = _flash_sweep(i, t, N_DIFF_MAPS, qs, keys, values, state, lambda j: bias_ref[j, 1], lambda j: bias_ref[j, 0])
    outs = []
    for hd in range(N_HEADS):
        o = maps[2 * hd] - lam * maps[2 * hd + 1]
        ms = jnp.mean(o * o, axis=0, keepdims=True)
        outs.append(o * lax.rsqrt(ms + NORM_EPS) * subg_ref[...] * (1.0 - lambda_init))
    _store_gated(o_ref, g_ref, outs)


def _diff(p, bias, lam_params, subln_g, lambda_init, batch, seq, t):
    qs, ks, vspec, ospec = _att_specs(t, seq, (256, 256), (256,))
    subg = jnp.broadcast_to(subln_g.astype(F32)[:, None], (HEAD_DIM, t))
    return pl.pallas_call(
        functools.partial(_diff_kernel, t=t, lambda_init=lambda_init),
        grid=(batch, seq // t),
        in_specs=[qs[0], ks[0], vspec, qs[1],
                  pl.BlockSpec((N_DIFF_MAPS, 2, t, t), lambda b, i: (0, 0, 0, 0), pipeline_mode=_ONE_BUFFER),
                  pl.BlockSpec((4, DIFF_QK), lambda b, i: (0, 0)),
                  pl.BlockSpec((HEAD_DIM, t), lambda b, i: (0, 0))],
        out_specs=ospec,
        out_shape=jax.ShapeDtypeStruct((batch * seq, GROUP_WIDTH), BF16),
        scratch_shapes=_flash_scratch(N_DIFF_MAPS, t),
        compiler_params=_cparams(("parallel", "arbitrary")),
        name="diff_attention",
    )(p["qb"], p["kb"], p["vbt"], p["gb"], bias, lam_params.astype(F32), subg)


def _sort_key(x):
    b = lax.bitcast_convert_type(x + 0.0, jnp.int32)
    return b ^ ((b >> 31) & 0x7FFFFFFF)


def _dsa_kernel(q_ref, k_ref, vt_ref, qi_ref, ki_ref, wi_ref, g_ref, bias_ref, tri_ref, o_ref,
                key_ref, hi_ref, lo_ref, *state, t, topk):
    i = pl.program_id(1)
    n_kt = i + 1
    idx_scale = (IDX_HEADS ** -0.5) * (IDX_DIM ** -0.5)

    wit = wi_ref[...].T * idx_scale
    qm, wrow = [], []
    for ih in range(IDX_HEADS):
        grp = slice((ih // 4) * LANES, (ih // 4 + 1) * LANES)
        lo = (ih % 4) * IDX_DIM
        qm.append(qi_ref[:, grp] * _lane_mask(lo, lo + IDX_DIM, BF16))
        wrow.append(wit[ih:ih + 1, :])

    def idx_tile(kt):
        kk = ki_ref[pl.ds(pl.multiple_of(kt * t, t), t), :]
        dot = lambda ih: lax.dot_general(kk, qm[ih], _NT, preferred_element_type=F32)
        queue = [dot(ih) for ih in range(FLASH_LOOKAHEAD)]
        tot = None
        for ih in range(IDX_HEADS):
            if ih + FLASH_LOOKAHEAD < IDX_HEADS:
                queue.append(dot(ih + FLASH_LOOKAHEAD))
            term = wrow[ih] * jnp.maximum(queue.pop(0), 0.0)
            tot = term if tot is None else tot + term
        return _sort_key(tot)

    def store_keys(kt, key):
        key_ref[kt] = key
        hi_ref[kt] = (key >> HALF_BITS).astype(jnp.int16)
        lo_ref[kt] = ((key & HALF_MASK) - HALF_BIAS).astype(jnp.int16)

    def key_tiles(first, n):
        for d in range(n):
            store_keys(first + d, idx_tile(first + d))

    def key_quad_body(c, carry):
        key_tiles(4 * c, 4)
        return carry

    lax.fori_loop(0, i // 4, key_quad_body, 0)
    pl.when(i % 4 >= 2)(lambda: key_tiles(i - i % 4, 2))
    pl.when(i % 2 == 1)(lambda: key_tiles(i - 1, 1))
    kchunk = lax.broadcasted_iota(jnp.int32, (t, t), 0) // CHUNK
    qchunk = lax.broadcasted_iota(jnp.int32, (t, t), 1) // CHUNK
    store_keys(i, jnp.where(kchunk <= qchunk, idx_tile(i), INT_MIN))

    def count(pred):
        def body(kt, c):
            hit = jnp.where(pred(key_ref[kt]), 1.0, 0.0)
            return c + jnp.sum(hit.reshape(t // SUBLANES, SUBLANES, t), axis=0)
        part = lax.fori_loop(0, n_kt, body, jnp.zeros((SUBLANES, t), F32))
        return jnp.sum(part, axis=0, keepdims=True)

    rows16 = 2 * SUBLANES
    n_acc = 4

    def count16(ref, pred):
        one, zero = jnp.int16(1), jnp.int16(0)

        def body(kt, accs):
            x = ref[kt].reshape(t // rows16, rows16, t)
            accs = list(accs)
            for r in range(t // rows16):
                accs[r % n_acc] = accs[r % n_acc] + jnp.where(pred(x[r]), one, zero)
            return tuple(accs)
        accs = lax.fori_loop(0, n_kt, body, tuple(jnp.zeros((rows16, t), jnp.int16) for _ in range(n_acc)))
        tot = (accs[0] + accs[1]) + (accs[2] + accs[3])
        return jnp.sum(tot.astype(jnp.int32), axis=0, keepdims=True)

    def select16(ref, need):
        def bit_body(b, res):
            cand = res | (jnp.int32(1) << (HALF_BITS - 1 - b))
            cand16 = (cand - HALF_BIAS).astype(jnp.int16)
            cnt = count16(ref, lambda x: x >= cand16)
            return jnp.where(cnt >= need, cand, res)
        return lax.fori_loop(0, HALF_BITS, bit_body, jnp.zeros((1, t), jnp.int32)) - HALF_BIAS

    hi_thr = select16(hi_ref, topk)
    hi_thr16 = hi_thr.astype(jnp.int16)
    need_lo = topk - count16(hi_ref, lambda x: x > hi_thr16)

    def member_body(kt, c):
        lo_ref[kt] = jnp.where(hi_ref[kt] == hi_thr16, lo_ref[kt], jnp.int16(-HALF_BIAS))
        return c

    lax.fori_loop(0, n_kt, member_body, 0)
    lo_thr = select16(lo_ref, need_lo)
    thr = (hi_thr << HALF_BITS) | (lo_thr + HALF_BIAS)
    cnt_ge = count(lambda key: key >= thr)
    has_ties = jnp.max(cnt_ge) > topk

    @pl.when(jnp.logical_not(has_ties))
    def _():
        def body(kt, c):
            key_ref[kt] = jnp.where(key_ref[kt] >= thr, 0, NEG_BITS)
            return c
        lax.fori_loop(0, n_kt, body, 0)

    @pl.when(has_ties)
    def _():
        need = topk - count(lambda key: key > thr)

        def body(kt, seen):
            key = key_ref[kt]
            eq = jnp.where(key == thr, 1.0, 0.0)
            rank = seen + jnp.dot(tri_ref[...], eq.astype(BF16), preferred_element_type=F32)
            sel = (key > thr) | ((key == thr) & (rank <= need))
            key_ref[kt] = jnp.where(sel, 0, NEG_BITS)
            return seen + jnp.sum(eq, axis=0, keepdims=True)
        lax.fori_loop(0, n_kt, body, jnp.zeros((1, t), F32))

    qs = [q_ref[:, (hd // 2) * LANES:(hd // 2 + 1) * LANES]
          * _lane_mask((hd % 2) * HEAD_DIM, (hd % 2 + 1) * HEAD_DIM, BF16) for hd in range(N_HEADS)]
    keys = lambda hd, rows: k_ref[rows, (hd // 2) * LANES:(hd // 2 + 1) * LANES]
    values = lambda hd, kt: vt_ref[kt, hd * HEAD_DIM:(hd + 1) * HEAD_DIM, :]
    outs = _flash_sweep(i, t, N_HEADS, qs, keys, values, state, lambda hd: bias_ref[hd, 1], lambda hd: bias_ref[hd, 0],
                        mask=lambda kt: lax.bitcast_convert_type(key_ref[kt], F32))
    _store_gated(o_ref, g_ref, outs)


def _dsa(p, bias, batch, seq, t):
    topk = min(TOPK_MAX, seq // 4)
    nq = seq // t
    qs, ks, vspec, ospec = _att_specs(t, seq, (256, 256, 128, 256), (256, 128))
    tri = jnp.asarray(np.tril(np.ones((t, t), np.float32)), BF16)
    return pl.pallas_call(
        functools.partial(_dsa_kernel, t=t, topk=topk),
        grid=(batch, nq),
        in_specs=[qs[0], ks[0], vspec, qs[1], ks[1], qs[2], qs[3],
                  pl.BlockSpec((N_HEADS, 2, t, t), lambda b, i: (2, 0, 0, 0), pipeline_mode=_ONE_BUFFER),
                  pl.BlockSpec((t, t), lambda b, i: (0, 0), pipeline_mode=_ONE_BUFFER)],
        out_specs=ospec,
        out_shape=jax.ShapeDtypeStruct((batch * seq, GROUP_WIDTH), BF16),
        scratch_shapes=[pltpu.VMEM((nq, t, t), jnp.int32), pltpu.VMEM((nq, t, t), jnp.int16),
                        pltpu.VMEM((nq, t, t), jnp.int16)] + _flash_scratch(N_HEADS, t),
        compiler_params=_cparams(("parallel", "arbitrary")),
        name="dsa_attention",
    )(p["qc"], p["kc"], p["vct"], p["qi"], p["ki"], p["wi"], p["gc"], bias, tri)


def _outproj_kernel(x_ref, ya_ref, yb_ref, yc_ref, yd_ref, ud_ref, halo_ref, cw_ref, w_ref, fg_ref, o_ref,
                    *, tiles_per_seq, final):
    i = pl.program_id(0)
    tm = x_ref.shape[0]
    u = ud_ref[...]
    halo = jnp.where(i % tiles_per_seq == 0, 0.0, halo_ref[...])
    row8 = lax.broadcasted_iota(jnp.int32, (SUBLANES, 1), 0)

    def shifted(k):
        r = pltpu.roll(u, k, 0)
        head = jnp.where(row8 < k, pltpu.roll(halo, k, 0), r[0:SUBLANES])
        return jnp.concatenate([head, r[SUBLANES:tm]], axis=0)

    cw = cw_ref[...]
    conv = cw[0:1] * shifted(2) + cw[1:2] * shifted(1) + cw[2:3] * u
    yd = (conv * yd_ref[...].astype(F32)).astype(BF16)
    acc = x_ref[...]
    for g, y in enumerate((ya_ref[...], yb_ref[...], yc_ref[...], yd)):
        acc = acc + jnp.dot(y, w_ref[g * GROUP_WIDTH:(g + 1) * GROUP_WIDTH, :], preferred_element_type=F32)
    if final:
        acc = _rms(acc, fg_ref[...])
    o_ref[...] = acc


def _outproj(x2, ya, yb, yc, yd, ud, conv_w, w_out, final_g, seq, final):
    n = x2.shape[0]
    tm = min(PROJ_ROWS, seq)
    tiles_per_seq = seq // tm
    row = lambda w: pl.BlockSpec((tm, w), lambda i: (i, 0))
    full = lambda a: pl.BlockSpec(a.shape, lambda i: (0,) * a.ndim)
    halo = pl.BlockSpec((SUBLANES, GROUP_WIDTH), lambda i: (jnp.maximum(i * (tm // SUBLANES) - 1, 0), 0))
    return pl.pallas_call(
        functools.partial(_outproj_kernel, tiles_per_seq=tiles_per_seq, final=final),
        grid=(n // tm,),
        in_specs=[row(D_MODEL), row(256), row(256), row(256), row(256), row(256), halo,
                  full(conv_w), full(w_out), full(final_g)],
        out_specs=row(D_MODEL),
        out_shape=jax.ShapeDtypeStruct((n, D_MODEL), F32),
        compiler_params=_cparams(("parallel",)),
        name="outproj",
    )(x2, ya, yb, yc, yd, ud, ud, conv_w, w_out, final_g)


def kernel(x, norm_g, w_in, mla_qa_g, mla_w_uq, mla_kva_g, mla_w_ukv, diff_lambda, diff_subln_g, conv_w, w_out,
           rel_bias, final_g):
    batch, seq, _ = x.shape
    depth = w_in.shape[0]
    t = min(ATT_TILE, seq)
    assert seq % t == 0 and t % CHUNK == 0 and seq % min(PROJ_ROWS, seq) == 0 and min(PROJ_ROWS, seq) % t == 0

    bidx, far_bucket, mask_add = _position_constants(seq, t)
    cos_t, sin_t = _rope_tables(seq)
    bias = _bias_tiles(rel_bias, bidx, mask_add, far_bucket, t)
    mask_add = jnp.asarray(mask_add)
    cos_t, sin_t = jnp.asarray(cos_t), jnp.asarray(sin_t)

    x2 = x.reshape(batch * seq, D_MODEL)
    for l in range(depth):
        lambda_init = 0.8 - 0.6 * math.exp(-0.3 * l)
        w1, wvt = _prep_w_in(w_in[l])
        wq1, wq2, wk, wavt = _prep_mla_up(mla_w_uq[l], mla_w_ukv[l])
        p = _inproj(x2, norm_g[l][None, :], w1, wvt, mla_qa_g[l][None, :], wq1, wq2, mla_kva_g[l][None, :], wk, wavt,
                    cos_t, sin_t, seq, t)
        ya = _mla(p, mask_add, batch, seq, t)
        yb = _diff(p, bias, diff_lambda[l], diff_subln_g[l], lambda_init, batch, seq, t)
        yc = _dsa(p, bias, batch, seq, t)
        x2 = _outproj(x2, ya, yb, yc, p["yd"], p["ud"], conv_w[l], w_out[l].astype(BF16), final_g[None, :],
                      seq, final=(l == depth - 1))
    return x2.reshape(batch, seq, D_MODEL)
```

```python
import functools
import math

import numpy as np
import jax
import jax.numpy as jnp
from jax import lax
from jax.experimental import pallas as pl
from jax.experimental.pallas import tpu as pltpu

F32 = jnp.float32
BF16 = jnp.bfloat16

D_MODEL = 1024
CHUNK = 64
N_HEADS = 4
HEAD_DIM = 64
GROUP_WIDTH = N_HEADS * HEAD_DIM
MLA_Q_LORA = 256
MLA_KV_LORA = 128
MLA_NOPE = 64
MLA_ROPE = 32
ROPE_BASE = 10000.0
DIFF_QK = 32
IDX_HEADS = 8
IDX_DIM = 32
TOPK_MAX = 256
CONV_WIDTH = 3
NUM_BUCKETS = 32
MAX_DISTANCE = 128
N_DIFF_MAPS = 2 * N_HEADS
N_BIAS_MAPS = N_DIFF_MAPS + N_HEADS
NORM_EPS = 1e-6
NEG = -1e30
LOG2E = math.log2(math.e)
NEG_BITS = int(np.float32(NEG).view(np.int32))
INT_MIN = -2 ** 31
HALF_BITS = 16
HALF_MASK = 2 ** HALF_BITS - 1
HALF_BIAS = 2 ** (HALF_BITS - 1)

LANES = 128
SUBLANES = 8
MLA_TILE = 256
DIFF_TILE = 256
DSA_TILE = 512
VT_TILE = 256
FAR_UNIT_KEYS = 512
PROJ_ROWS = 512
FLASH_LOOKAHEAD = 4
VMEM_LIMIT = 56 * 1024 * 1024

_IN_SPLITS = (
    ("a_cq", 256), ("a_ckv", 128), ("a_krope", 32), ("a_gate", 256),
    ("b_q", 256), ("b_k", 256), ("b_v", 256), ("b_gate", 256),
    ("c_q", 256), ("c_k", 256), ("c_v", 256),
    ("c_qidx", 256), ("c_kidx", 32), ("c_widx", 8), ("c_gate", 256),
    ("d_b", 256), ("d_c", 256), ("d_h", 256), ("d_gate", 256),
)

_SEGS = (
    ("cq", 256), ("ckv", 128), ("kr1", 128), ("kr2", 128), ("c_kidx4", 128), ("a_gate", 256),
    ("b_q", 256), ("b_k", 256), ("b_gate", 256),
    ("c_q", 256), ("c_k", 256), ("c_qidx", 256),
    ("c_gate", 256), ("d_b", 256), ("d_c", 256), ("d_h", 256), ("d_gate", 256), ("c_widx", 128),
)
_SEG_OFF = {}
_off = 0
for _name, _w in _SEGS:
    _SEG_OFF[_name] = (_off, _w)
    _off += _w
W1_COLS = _off

_NT = (((1,), (1,)), ((), ()))
_ONE_BUFFER = pl.Buffered(1)


def _cparams(sem):
    return pltpu.CompilerParams(dimension_semantics=sem, vmem_limit_bytes=VMEM_LIMIT)


def _split_cols(w):
    out, off = {}, 0
    for name, width in _IN_SPLITS:
        out[name] = w[:, off:off + width]
        off += width
    return out


def _rot_half_cols(w):
    half = MLA_ROPE // 2
    return jnp.concatenate([-w[..., half:], w[..., :half]], axis=-1)


def _prep_w_in(w):
    p = _split_cols(w)
    z = lambda n: jnp.zeros((w.shape[0], n), w.dtype)
    kr1 = jnp.concatenate([z(MLA_NOPE), p["a_krope"], z(LANES - MLA_NOPE - MLA_ROPE)], axis=1)
    kr2 = jnp.concatenate([z(MLA_NOPE), _rot_half_cols(p["a_krope"]), z(LANES - MLA_NOPE - MLA_ROPE)], axis=1)
    segs = {
        "cq": p["a_cq"], "ckv": p["a_ckv"], "kr1": kr1, "kr2": kr2, "a_gate": p["a_gate"],
        "b_q": p["b_q"] * (DIFF_QK ** -0.5 * LOG2E), "b_k": p["b_k"], "b_gate": p["b_gate"],
        "c_q": p["c_q"] * (HEAD_DIM ** -0.5 * LOG2E), "c_k": p["c_k"], "c_qidx": p["c_qidx"],
        "c_kidx4": jnp.tile(p["c_kidx"], (1, LANES // IDX_DIM)),
        "c_widx": jnp.concatenate([p["c_widx"], z(LANES - IDX_HEADS)], axis=1),
        "c_gate": p["c_gate"], "d_b": p["d_b"], "d_c": p["d_c"], "d_h": p["d_h"], "d_gate": p["d_gate"],
    }
    w1 = jnp.concatenate([segs[name].astype(BF16) for name, _ in _SEGS], axis=1)
    wvt = jnp.concatenate([p["b_v"].astype(BF16), p["c_v"].astype(BF16)], axis=1).T
    return w1, wvt


def _prep_mla_up(w_uq, w_ukv):
    scale = (MLA_NOPE + MLA_ROPE) ** -0.5 * LOG2E
    wq = w_uq.reshape(MLA_Q_LORA, N_HEADS, MLA_NOPE + MLA_ROPE) * scale
    zq = jnp.zeros((MLA_Q_LORA, N_HEADS, LANES - MLA_NOPE - MLA_ROPE), w_uq.dtype)
    wq1 = jnp.concatenate([wq, zq], axis=-1).reshape(MLA_Q_LORA, N_HEADS * LANES)
    wq2 = jnp.concatenate([jnp.zeros_like(wq[..., :MLA_NOPE]), _rot_half_cols(wq[..., MLA_NOPE:]), zq],
                          axis=-1).reshape(MLA_Q_LORA, N_HEADS * LANES)
    wkv = w_ukv.reshape(MLA_KV_LORA, N_HEADS, MLA_NOPE + HEAD_DIM)
    wk = jnp.concatenate([wkv[..., :MLA_NOPE], jnp.zeros((MLA_KV_LORA, N_HEADS, LANES - MLA_NOPE), w_ukv.dtype)],
                         axis=-1).reshape(MLA_KV_LORA, N_HEADS * LANES)
    wvt = wkv[..., MLA_NOPE:].reshape(MLA_KV_LORA, N_HEADS * HEAD_DIM).T
    return wq1.astype(BF16), wq2.astype(BF16), wk.astype(BF16), wvt.astype(BF16)


def _rope_tables(seq):
    half = MLA_ROPE // 2
    inv_freq = ROPE_BASE ** (-np.arange(half, dtype=np.float32) / half)
    ang = np.arange(seq, dtype=np.float32)[:, None] * inv_freq[None, :].astype(np.float32)
    cos, sin = np.cos(ang).astype(np.float32), np.sin(ang).astype(np.float32)
    ct = np.zeros((seq, LANES), np.float32)
    st = np.zeros((seq, LANES), np.float32)
    ct[:, :MLA_NOPE] = 1.0
    ct[:, MLA_NOPE:MLA_NOPE + half] = cos
    ct[:, MLA_NOPE + half:MLA_NOPE + MLA_ROPE] = cos
    st[:, MLA_NOPE:MLA_NOPE + half] = sin
    st[:, MLA_NOPE + half:MLA_NOPE + MLA_ROPE] = sin
    return ct, st


def _rel_bucket_np(rel):
    nb = NUM_BUCKETS // 2
    max_exact = nb // 2
    ret = np.where(rel > 0, nb, 0)
    n = np.abs(rel)
    nf = np.maximum(n, max_exact).astype(np.float32)
    large = max_exact + (np.log(nf / np.float32(max_exact)) / np.float32(math.log(MAX_DISTANCE / max_exact))
                         * np.float32(nb - max_exact)).astype(np.int32)
    large = np.minimum(large, nb - 1)
    return (ret + np.where(n < max_exact, n, large)).astype(np.int32)


def _position_constants(seq, tile):
    r = np.arange(tile)
    rel0 = r[:, None] - r[None, :]
    bidx = np.stack([_rel_bucket_np(rel0 - tile), _rel_bucket_np(rel0)]).astype(np.int32)
    far = _rel_bucket_np(np.arange(-(seq - 1), -tile))
    far_bucket = int(far[0]) if far.size else int(_rel_bucket_np(np.array([-tile - 1]))[0])
    assert far.size == 0 or np.all(far == far_bucket)
    mask_add = np.where((r[:, None] // CHUNK) <= (r[None, :] // CHUNK), 0.0, NEG).astype(np.float32)
    return bidx, far_bucket, mask_add


def _bias_kernel(tab_ref, bidx_ref, madd_ref, o_ref, *, far_bucket, buckets):
    j = pl.program_id(0)
    c = tab_ref[j, far_bucket]
    for d in range(2):
        idx = bidx_ref[d]
        acc = jnp.zeros(idx.shape, F32)
        for b in buckets[d]:
            acc = jnp.where(idx == b, (tab_ref[j, b] - c) * LOG2E, acc)
        if d == 1:
            acc = acc + madd_ref[...]
        o_ref[0, d] = acc


def _bias_tiles(table, seq, tile):
    bidx, far_bucket, mask_add = _position_constants(seq, tile)
    n_maps = table.shape[1]
    buckets = tuple(tuple(int(b) for b in np.unique(bidx[d])) for d in range(2))
    return pl.pallas_call(
        functools.partial(_bias_kernel, far_bucket=far_bucket, buckets=buckets),
        grid=(n_maps,),
        in_specs=[
            pl.BlockSpec(memory_space=pltpu.SMEM),
            pl.BlockSpec((2, tile, tile), lambda j: (0, 0, 0)),
            pl.BlockSpec((tile, tile), lambda j: (0, 0)),
        ],
        out_specs=pl.BlockSpec((1, 2, tile, tile), lambda j: (j, 0, 0, 0)),
        out_shape=jax.ShapeDtypeStruct((n_maps, 2, tile, tile), F32),
        compiler_params=_cparams(("arbitrary",)),
        name="bias_tiles",
    )(table.T.astype(F32), jnp.asarray(bidx), jnp.asarray(mask_add))


def _silu(x):
    return x * (1.0 / (1.0 + jnp.exp(-x)))


def _rms(x, g):
    return x * lax.rsqrt(jnp.mean(x * x, axis=-1, keepdims=True) + NORM_EPS) * g


def _inproj_kernel(x_ref, g_ref, w_ref, wvt_ref, qag_ref, wq1_ref, wq2_ref, kvg_ref, wk_ref, wavt_ref, cos_ref, sin_ref,
                   qa_ref, ka_ref, vat_ref, ga_ref, qb_ref, kb_ref, vbt_ref, gb_ref,
                   qc_ref, kc_ref, vct_ref, qi_ref, ki_ref, wi_ref, gc_ref, yd_ref, ud_ref):
    h = _rms(x_ref[...], g_ref[...]).astype(BF16)

    def seg(name):
        off, width = _SEG_OFF[name]
        return jnp.dot(h, w_ref[:, off:off + width], preferred_element_type=F32)

    def seg_pair(first, second):
        off, width = _SEG_OFF[first]
        assert _SEG_OFF[second] == (off + width, width) and width == LANES
        y = jnp.dot(h, w_ref[:, off:off + 2 * width], preferred_element_type=F32)
        return y[:, :width], y[:, width:]

    cos = cos_ref[...]
    sin = sin_ref[...]

    cqn = _rms(seg("cq"), qag_ref[...]).astype(BF16)
    q1 = jnp.dot(cqn, wq1_ref[...], preferred_element_type=F32)
    q2 = jnp.dot(cqn, wq2_ref[...], preferred_element_type=F32)
    ckv, kr1 = seg_pair("ckv", "kr1")
    kr2, kidx4 = seg_pair("kr2", "c_kidx4")
    kr = kr1 * cos + kr2 * sin
    ckvn = _rms(ckv, kvg_ref[...]).astype(BF16)
    kn = jnp.dot(ckvn, wk_ref[...], preferred_element_type=F32)
    for hd in range(N_HEADS):
        sl = slice(hd * LANES, (hd + 1) * LANES)
        qa_ref[:, sl] = (q1[:, sl] * cos + q2[:, sl] * sin).astype(BF16)
        ka_ref[:, sl] = (kn[:, sl] + kr).astype(BF16)
    ga_ref[...] = _silu(seg("a_gate")).astype(BF16)

    def put(ref, first_token, vt):
        tile = ref.shape[2]
        ref[first_token // tile, :, first_token % tile:first_token % tile + VT_TILE] = vt

    for c in range(x_ref.shape[0] // VT_TILE):
        rows = slice(c * VT_TILE, (c + 1) * VT_TILE)
        put(vat_ref, c * VT_TILE,
            lax.dot_general(wavt_ref[...], ckvn[rows], _NT, preferred_element_type=F32).astype(BF16))
        vt = lax.dot_general(wvt_ref[...], h[rows], _NT, preferred_element_type=F32).astype(BF16)
        put(vbt_ref, c * VT_TILE, vt[:GROUP_WIDTH])
        put(vct_ref, c * VT_TILE, vt[GROUP_WIDTH:])

    qb_ref[...] = seg("b_q").astype(BF16)
    kb_ref[...] = seg("b_k").astype(BF16)
    gb_ref[...] = _silu(seg("b_gate")).astype(BF16)

    qc_ref[...] = seg("c_q").astype(BF16)
    kc_ref[...] = seg("c_k").astype(BF16)
    qi_ref[...] = seg("c_qidx").astype(BF16)
    ki_ref[...] = kidx4.astype(BF16)
    wi_ref[...] = seg("c_widx")
    gc_ref[...] = _silu(seg("c_gate")).astype(BF16)

    yd_ref[...] = (seg("d_b") * _silu(seg("d_gate"))).astype(BF16)
    ud_ref[...] = seg("d_c") * seg("d_h")


def _inproj(x2, g, w1, wvt, qag, wq1, wq2, kvg, wk, wavt, cos_t, sin_t, seq, tiles):
    n = x2.shape[0]
    tm = min(PROJ_ROWS, seq)
    tiles_per_seq = seq // tm
    row = lambda w: pl.BlockSpec((tm, w), lambda i: (i, 0))
    full = lambda a: pl.BlockSpec(a.shape, lambda i: (0,) * a.ndim)
    tab = pl.BlockSpec((tm, LANES), lambda i: (i % tiles_per_seq, 0))
    vt_spec = lambda t: pl.BlockSpec((tm // t, GROUP_WIDTH, t), lambda i: (i, 0, 0))
    vt_shape = lambda t: jax.ShapeDtypeStruct((n // t, GROUP_WIDTH, t), BF16)
    outs = (("qa", 512, BF16), ("ka", 512, BF16), ("vat", None, tiles[0]), ("ga", 256, BF16),
            ("qb", 256, BF16), ("kb", 256, BF16), ("vbt", None, tiles[1]), ("gb", 256, BF16),
            ("qc", 256, BF16), ("kc", 256, BF16), ("vct", None, tiles[2]), ("qi", 256, BF16),
            ("ki", 128, BF16), ("wi", 128, F32), ("gc", 256, BF16), ("yd", 256, BF16), ("ud", 256, F32))
    res = pl.pallas_call(
        _inproj_kernel,
        grid=(n // tm,),
        in_specs=[row(D_MODEL), full(g), full(w1), full(wvt), full(qag), full(wq1), full(wq2), full(kvg), full(wk),
                  full(wavt), tab, tab],
        out_specs=[vt_spec(dt) if w is None else row(w) for _, w, dt in outs],
        out_shape=[vt_shape(dt) if w is None else jax.ShapeDtypeStruct((n, w), dt) for _, w, dt in outs],
        compiler_params=_cparams(("parallel",)),
        name="inproj",
    )(x2, g, w1, wvt, qag, wq1, wq2, kvg, wk, wavt, cos_t, sin_t)
    return {name: o for (name, _, _), o in zip(outs, res)}


def _flash_scratch(n_maps, t):
    unit = _far_unit_tiles(t) * t
    return [pltpu.VMEM((n_maps, 1, t), F32), pltpu.VMEM((n_maps, 1, t), F32), pltpu.VMEM((n_maps, HEAD_DIM, t), F32),
            pltpu.VMEM((n_maps, unit, t), F32), pltpu.VMEM((n_maps, unit, t), F32)]


def _far_unit_tiles(t):
    return max(1, FAR_UNIT_KEYS // t)


def _flash_update(s, vts, old):
    t = vts[0].shape[1]
    m_tile = jnp.max(s, axis=0, keepdims=True)
    m_new = m_tile if old is None else jnp.maximum(old[0], m_tile)
    p = jnp.exp2(s - m_new)
    l_tile = jnp.sum(p, axis=0, keepdims=True)
    pb = p.astype(BF16)
    pv = jnp.dot(vts[0], pb[0:t], preferred_element_type=F32)
    for d in range(1, len(vts)):
        pv = pv + jnp.dot(vts[d], pb[d * t:(d + 1) * t], preferred_element_type=F32)
    if old is None:
        return m_new, l_tile, pv
    alpha = jnp.exp2(old[0] - m_new)
    return m_new, alpha * old[1] + l_tile, alpha * old[2] + pv


def _flash_sweep(i, t, n_maps, qs, keys, values, state, diag_add, near_add=None, mask=None):
    m_ref, l_ref, acc_ref, buf_a, buf_b = state
    n_far = i if near_add is None else jnp.maximum(i - 1, 0)
    ut = _far_unit_tiles(t)
    n_units = n_far // ut

    def load_state():
        return [(m_ref[j], l_ref[j], acc_ref[j]) for j in range(n_maps)]

    def store_state(cur):
        for j in range(n_maps):
            m_ref[j], l_ref[j], acc_ref[j] = cur[j]

    def tile_logits(j, kt, n_tiles, shared, extra=None):
        rows = pl.ds(pl.multiple_of(kt * t, t), n_tiles * t)
        s = lax.dot_general(keys(j, rows), qs[j], _NT, preferred_element_type=F32)
        add = shared
        if extra is not None:
            add = extra(j) if add is None else add + extra(j)
        return s if add is None else s + add

    def shared_mask(kt, n_tiles):
        if mask is None:
            return None
        tiles = [mask(kt + d) for d in range(n_tiles)]
        return tiles[0] if n_tiles == 1 else jnp.concatenate(tiles, axis=0)

    def far_unit(u):
        return ut * jnp.minimum(u, n_units - 1)

    def run(units, first=False, prefetch=False):
        cur = [None] * n_maps if first else load_state()
        shared = [shared_mask(kt, n_tiles) for kt, n_tiles, _ in units]
        items = [(u, j) for u in range(len(units)) for j in range(n_maps)]
        if prefetch:
            items += [(None, j) for j in range(n_maps)]
            shared.append(shared_mask(0, ut))

        def logits(u, j):
            if u is None:
                return tile_logits(j, 0, ut, shared[-1])
            kt, n_tiles, extra = units[u]
            return tile_logits(j, kt, n_tiles, shared[u], extra)

        queue = [logits(*it) for it in items[:FLASH_LOOKAHEAD]]
        for n, (u, j) in enumerate(items):
            if n + FLASH_LOOKAHEAD < len(items):
                queue.append(logits(*items[n + FLASH_LOOKAHEAD]))
            s = queue.pop(0)
            if u is None:
                buf_a[j] = s
            else:
                kt, n_tiles, _ = units[u]
                cur[j] = _flash_update(s, [values(j, kt + d) for d in range(n_tiles)], cur[j])
        store_state(cur)

    def first_run(prefetch):
        if near_add is None:
            return lambda: run([(i, 1, diag_add)], first=True, prefetch=prefetch)
        return lambda: run([(i, 1, diag_add), (i - 1, 1, near_add)], first=True, prefetch=prefetch)

    if near_add is None:
        pl.when(n_units == 0)(first_run(False))
    else:
        pl.when(i == 0)(lambda: run([(i, 1, diag_add)], first=True))
        pl.when((i >= 1) & (n_units == 0))(first_run(False))
    pl.when(n_units >= 1)(first_run(True))

    def consume(buf, kt, j, old):
        return _flash_update(buf[j], [values(j, kt + d) for d in range(ut)], old)

    def far_body(c, carry):
        cur = load_state()
        for buf_in, buf_out, u in ((buf_a, buf_b, 2 * c), (buf_b, buf_a, 2 * c + 1)):
            kt_next = far_unit(u + 1)
            shared = shared_mask(kt_next, ut)
            for j in range(n_maps):
                buf_out[j] = tile_logits(j, kt_next, ut, shared)
                cur[j] = consume(buf_in, ut * u, j, cur[j])
        store_state(cur)
        return carry

    lax.fori_loop(0, n_units // 2, far_body, 0)

    @pl.when(n_units % 2 == 1)
    def _():
        cur = load_state()
        for j in range(n_maps):
            cur[j] = consume(buf_a, ut * (n_units - 1), j, cur[j])
        store_state(cur)

    for r in range(1, ut):
        pl.when(n_far % ut >= r)(lambda r=r: run([(n_far - r, 1, None)]))
    return [acc_ref[j] / l_ref[j] for j in range(n_maps)]


def _lane_mask(lo, hi, dtype):
    lane = lax.broadcasted_iota(jnp.int32, (1, LANES), 1)
    return ((lane >= lo) & (lane < hi)).astype(dtype)


def _store_gated(o_ref, g_ref, heads_t):
    o = jnp.concatenate(heads_t, axis=0).T
    o_ref[...] = (o * g_ref[...].astype(F32)).astype(BF16)


def _att_specs(t, seq, widths_q, widths_k):
    nq = seq // t
    qspec = lambda w: pl.BlockSpec((t, w), lambda b, i: (b * nq + i, 0))
    kspec = lambda w: pl.BlockSpec((seq, w), lambda b, i: (b, 0), pipeline_mode=_ONE_BUFFER)
    vspec = pl.BlockSpec((nq, GROUP_WIDTH, t), lambda b, i: (b, 0, 0), pipeline_mode=_ONE_BUFFER)
    return [qspec(w) for w in widths_q], [kspec(w) for w in widths_k], vspec, qspec(GROUP_WIDTH)


def _mla_kernel(q_ref, k_ref, vt_ref, g_ref, madd_ref, o_ref, *state, t):
    i = pl.program_id(1)
    qs = [q_ref[:, hd * LANES:(hd + 1) * LANES] for hd in range(N_HEADS)]
    keys = lambda hd, rows: k_ref[rows, hd * LANES:(hd + 1) * LANES]
    values = lambda hd, kt: vt_ref[kt, hd * HEAD_DIM:(hd + 1) * HEAD_DIM, :]
    outs = _flash_sweep(i, t, N_HEADS, qs, keys, values, state, lambda hd: madd_ref[...])
    _store_gated(o_ref, g_ref, outs)


def _mla(p, mask_add, batch, seq, t):
    qs, ks, vspec, ospec = _att_specs(t, seq, (512, 256), (512,))
    return pl.pallas_call(
        functools.partial(_mla_kernel, t=t),
        grid=(batch, seq // t),
        in_specs=[qs[0], ks[0], vspec, qs[1], pl.BlockSpec((t, t), lambda b, i: (0, 0), pipeline_mode=_ONE_BUFFER)],
        out_specs=ospec,
        out_shape=jax.ShapeDtypeStruct((batch * seq, GROUP_WIDTH), BF16),
        scratch_shapes=_flash_scratch(N_HEADS, t),
        compiler_params=_cparams(("parallel", "arbitrary")),
        name="mla_attention",
    )(p["qa"], p["ka"], p["vat"], p["ga"], mask_add)


def _diff_kernel(q_ref, k_ref, vt_ref, g_ref, bias_ref, lam_ref, subg_ref, o_ref, *state, t, lambda_init):
    i = pl.program_id(1)
    lp = lam_ref[...]
    lam = (jnp.exp(jnp.sum(lp[0:1] * lp[1:2], axis=-1, keepdims=True))
           - jnp.exp(jnp.sum(lp[2:3] * lp[3:4], axis=-1, keepdims=True)) + lambda_init)
    qs = [q_ref[:, (j // 4) * LANES:(j // 4 + 1) * LANES] * _lane_mask((j % 4) * DIFF_QK, (j % 4 + 1) * DIFF_QK, BF16)
          for j in range(N_DIFF_MAPS)]
    keys = lambda j, rows: k_ref[rows, (j // 4) * LANES:(j // 4 + 1) * LANES]
    values = lambda j, kt: vt_ref[kt, (j // 2) * HEAD_DIM:(j // 2 + 1) * HEAD_DIM, :]
    maps = _flash_sweep(i, t, N_DIFF_MAPS, qs, keys, values, state, lambda j: bias_ref[j, 1], lambda j: bias_ref[j, 0])
    outs = []
    for hd in range(N_HEADS):
        o = maps[2 * hd] - lam * maps[2 * hd + 1]
        ms = jnp.mean(o * o, axis=0, keepdims=True)
        outs.append(o * lax.rsqrt(ms + NORM_EPS) * subg_ref[...] * (1.0 - lambda_init))
    _store_gated(o_ref, g_ref, outs)


def _diff(p, bias, lam_params, subln_g, lambda_init, batch, seq, t):
    qs, ks, vspec, ospec = _att_specs(t, seq, (256, 256), (256,))
    subg = jnp.broadcast_to(subln_g.astype(F32)[:, None], (HEAD_DIM, t))
    return pl.pallas_call(
        functools.partial(_diff_kernel, t=t, lambda_init=lambda_init),
        grid=(batch, seq // t),
        in_specs=[qs[0], ks[0], vspec, qs[1],
                  pl.BlockSpec((N_DIFF_MAPS, 2, t, t), lambda b, i: (0, 0, 0, 0), pipeline_mode=_ONE_BUFFER),
                  pl.BlockSpec((4, DIFF_QK), lambda b, i: (0, 0)),
                  pl.BlockSpec((HEAD_DIM, t), lambda b, i: (0, 0))],
        out_specs=ospec,
        out_shape=jax.ShapeDtypeStruct((batch * seq, GROUP_WIDTH), BF16),
        scratch_shapes=_flash_scratch(N_DIFF_MAPS, t),
        compiler_params=_cparams(("parallel", "arbitrary")),
        name="diff_attention",
    )(p["qb"], p["kb"], p["vbt"], p["gb"], bias, lam_params.astype(F32), subg)


def _sort_key(x):
    b = lax.bitcast_convert_type(x + 0.0, jnp.int32)
    return b ^ ((b >> 31) & 0x7FFFFFFF)


def _dsa_kernel(q_ref, k_ref, vt_ref, qi_ref, ki_ref, wi_ref, g_ref, bias_ref, tri_ref, o_ref,
                key_ref, hi_ref, lo_ref, *state, t, topk):
    i = pl.program_id(1)
    n_kt = i + 1
    idx_scale = (IDX_HEADS ** -0.5) * (IDX_DIM ** -0.5)

    wit = wi_ref[...].T * idx_scale
    qm, wrow = [], []
    for ih in range(IDX_HEADS):
        grp = slice((ih // 4) * LANES, (ih // 4 + 1) * LANES)
        lo = (ih % 4) * IDX_DIM
        qm.append(qi_ref[:, grp] * _lane_mask(lo, lo + IDX_DIM, BF16))
        wrow.append(wit[ih:ih + 1, :])

    def idx_tile(kt):
        kk = ki_ref[pl.ds(pl.multiple_of(kt * t, t), t), :]
        dot = lambda ih: lax.dot_general(kk, qm[ih], _NT, preferred_element_type=F32)
        queue = [dot(ih) for ih in range(FLASH_LOOKAHEAD)]
        tot = None
        for ih in range(IDX_HEADS):
            if ih + FLASH_LOOKAHEAD < IDX_HEADS:
                queue.append(dot(ih + FLASH_LOOKAHEAD))
            term = wrow[ih] * jnp.maximum(queue.pop(0), 0.0)
            tot = term if tot is None else tot + term
        return _sort_key(tot)

    def store_keys(kt, key):
        key_ref[kt] = key
        hi_ref[kt] = (key >> HALF_BITS).astype(jnp.int16)
        lo_ref[kt] = ((key & HALF_MASK) - HALF_BIAS).astype(jnp.int16)

    def key_tiles(first, n):
        for d in range(n):
            store_keys(first + d, idx_tile(first + d))

    def key_quad_body(c, carry):
        key_tiles(4 * c, 4)
        return carry

    lax.fori_loop(0, i // 4, key_quad_body, 0)
    pl.when(i % 4 >= 2)(lambda: key_tiles(i - i % 4, 2))
    pl.when(i % 2 == 1)(lambda: key_tiles(i - 1, 1))
    kchunk = lax.broadcasted_iota(jnp.int32, (t, t), 0) // CHUNK
    qchunk = lax.broadcasted_iota(jnp.int32, (t, t), 1) // CHUNK
    store_keys(i, jnp.where(kchunk <= qchunk, idx_tile(i), INT_MIN))

    def count(pred):
        def body(kt, c):
            hit = jnp.where(pred(key_ref[kt]), 1.0, 0.0)
            return c + jnp.sum(hit.reshape(t // SUBLANES, SUBLANES, t), axis=0)
        part = lax.fori_loop(0, n_kt, body, jnp.zeros((SUBLANES, t), F32))
        return jnp.sum(part, axis=0, keepdims=True)

    rows16 = 2 * SUBLANES
    n_acc = 4

    def count16(ref, pred):
        one, zero = jnp.int16(1), jnp.int16(0)

        def body(kt, accs):
            x = ref[kt].reshape(t // rows16, rows16, t)
            accs = list(accs)
            for r in range(t // rows16):
                accs[r % n_acc] = accs[r % n_acc] + jnp.where(pred(x[r]), one, zero)
            return tuple(accs)
        accs = lax.fori_loop(0, n_kt, body, tuple(jnp.zeros((rows16, t), jnp.int16) for _ in range(n_acc)))
        tot = (accs[0] + accs[1]) + (accs[2] + accs[3])
        return jnp.sum(tot.astype(jnp.int32), axis=0, keepdims=True)

    def select16(ref, need):
        def bit_body(b, res):
            cand = res | (jnp.int32(1) << (HALF_BITS - 1 - b))
            cand16 = (cand - HALF_BIAS).astype(jnp.int16)
            cnt = count16(ref, lambda x: x >= cand16)
            return jnp.where(cnt >= need, cand, res)
        return lax.fori_loop(0, HALF_BITS, bit_body, jnp.zeros((1, t), jnp.int32)) - HALF_BIAS

    hi_thr = select16(hi_ref, topk)
    hi_thr16 = hi_thr.astype(jnp.int16)
    need_lo = topk - count16(hi_ref, lambda x: x > hi_thr16)

    def member_body(kt, c):
        lo_ref[kt] = jnp.where(hi_ref[kt] == hi_thr16, lo_ref[kt], jnp.int16(-HALF_BIAS))
        return c

    lax.fori_loop(0, n_kt, member_body, 0)
    lo_thr = select16(lo_ref, need_lo)
    thr = (hi_thr << HALF_BITS) | (lo_thr + HALF_BIAS)
    cnt_ge = count(lambda key: key >= thr)
    has_ties = jnp.max(cnt_ge) > topk

    @pl.when(jnp.logical_not(has_ties))
    def _():
        def body(kt, c):
            key_ref[kt] = jnp.where(key_ref[kt] >= thr, 0, NEG_BITS)
            return c
        lax.fori_loop(0, n_kt, body, 0)

    @pl.when(has_ties)
    def _():
        need = topk - count(lambda key: key > thr)

        def body(kt, seen):
            key = key_ref[kt]
            eq = jnp.where(key == thr, 1.0, 0.0)
            rank = seen + jnp.dot(tri_ref[...], eq.astype(BF16), preferred_element_type=F32)
            sel = (key > thr) | ((key == thr) & (rank <= need))
            key_ref[kt] = jnp.where(sel, 0, NEG_BITS)
            return seen + jnp.sum(eq, axis=0, keepdims=True)
        lax.fori_loop(0, n_kt, body, jnp.zeros((1, t), F32))

    qs = [q_ref[:, (hd // 2) * LANES:(hd // 2 + 1) * LANES]
          * _lane_mask((hd % 2) * HEAD_DIM, (hd % 2 + 1) * HEAD_DIM, BF16) for hd in range(N_HEADS)]
    keys = lambda hd, rows: k_ref[rows, (hd // 2) * LANES:(hd // 2 + 1) * LANES]
    values = lambda hd, kt: vt_ref[kt, hd * HEAD_DIM:(hd + 1) * HEAD_DIM, :]
    outs = _flash_sweep(i, t, N_HEADS, qs, keys, values, state, lambda hd: bias_ref[hd, 1], lambda hd: bias_ref[hd, 0],
                        mask=lambda kt: lax.bitcast_convert_type(key_ref[kt], F32))
    _store_gated(o_ref, g_ref, outs)


def _dsa(p, bias, batch, seq, t):
    topk = min(TOPK_MAX, seq // 4)
    nq = seq // t
    qs, ks, vspec, ospec = _att_specs(t, seq, (256, 256, 128, 256), (256, 128))
    tri = jnp.asarray(np.tril(np.ones((t, t), np.float32)), BF16)
    return pl.pallas_call(
        functools.partial(_dsa_kernel, t=t, topk=topk),
        grid=(batch, nq),
        in_specs=[qs[0], ks[0], vspec, qs[1], ks[1], qs[2], qs[3],
                  pl.BlockSpec((N_HEADS, 2, t, t), lambda b, i: (0, 0, 0, 0), pipeline_mode=_ONE_BUFFER),
                  pl.BlockSpec((t, t), lambda b, i: (0, 0), pipeline_mode=_ONE_BUFFER)],
        out_specs=ospec,
        out_shape=jax.ShapeDtypeStruct((batch * seq, GROUP_WIDTH), BF16),
        scratch_shapes=[pltpu.VMEM((nq, t, t), jnp.int32), pltpu.VMEM((nq, t, t), jnp.int16),
                        pltpu.VMEM((nq, t, t), jnp.int16)] + _flash_scratch(N_HEADS, t),
        compiler_params=_cparams(("parallel", "arbitrary")),
        name="dsa_attention",
    )(p["qc"], p["kc"], p["vct"], p["qi"], p["ki"], p["wi"], p["gc"], bias, tri)


def _outproj_kernel(x_ref, ya_ref, yb_ref, yc_ref, yd_ref, ud_ref, halo_ref, cw_ref, w_ref, fg_ref, o_ref,
                    *, tiles_per_seq, final):
    i = pl.program_id(0)
    tm = x_ref.shape[0]
    u = ud_ref[...]
    halo = jnp.where(i % tiles_per_seq == 0, 0.0, halo_ref[...])
    row8 = lax.broadcasted_iota(jnp.int32, (SUBLANES, 1), 0)

    def shifted(k):
        r = pltpu.roll(u, k, 0)
        head = jnp.where(row8 < k, pltpu.roll(halo, k, 0), r[0:SUBLANES])
        return jnp.concatenate([head, r[SUBLANES:tm]], axis=0)

    cw = cw_ref[...]
    conv = cw[0:1] * shifted(2) + cw[1:2] * shifted(1) + cw[2:3] * u
    yd = (conv * yd_ref[...].astype(F32)).astype(BF16)
    acc = x_ref[...]
    for g, y in enumerate((ya_ref[...], yb_ref[...], yc_ref[...], yd)):
        acc = acc + jnp.dot(y, w_ref[g * GROUP_WIDTH:(g + 1) * GROUP_WIDTH, :], preferred_element_type=F32)
    if final:
        acc = _rms(acc, fg_ref[...])
    o_ref[...] = acc


def _outproj(x2, ya, yb, yc, yd, ud, conv_w, w_out, final_g, seq, final):
    n = x2.shape[0]
    tm = min(PROJ_ROWS, seq)
    tiles_per_seq = seq // tm
    row = lambda w: pl.BlockSpec((tm, w), lambda i: (i, 0))
    full = lambda a: pl.BlockSpec(a.shape, lambda i: (0,) * a.ndim)
    halo = pl.BlockSpec((SUBLANES, GROUP_WIDTH), lambda i: (jnp.maximum(i * (tm // SUBLANES) - 1, 0), 0))
    return pl.pallas_call(
        functools.partial(_outproj_kernel, tiles_per_seq=tiles_per_seq, final=final),
        grid=(n // tm,),
        in_specs=[row(D_MODEL), row(256), row(256), row(256), row(256), row(256), halo,
                  full(conv_w), full(w_out), full(final_g)],
        out_specs=row(D_MODEL),
        out_shape=jax.ShapeDtypeStruct((n, D_MODEL), F32),
        compiler_params=_cparams(("parallel",)),
        name="outproj",
    )(x2, ya, yb, yc, yd, ud, ud, conv_w, w_out, final_g)


def kernel(x, norm_g, w_in, mla_qa_g, mla_w_uq, mla_kva_g, mla_w_ukv, diff_lambda, diff_subln_g, conv_w, w_out,
           rel_bias, final_g):
    batch, seq, _ = x.shape
    depth = w_in.shape[0]
    tm = min(PROJ_ROWS, seq)
    tiles = tuple(min(t, seq) for t in (MLA_TILE, DIFF_TILE, DSA_TILE))
    assert seq % tm == 0 and tm % VT_TILE == 0
    assert all(seq % t == 0 and t % CHUNK == 0 and tm % t == 0 and t % VT_TILE == 0 for t in tiles)

    cos_t, sin_t = (jnp.asarray(a) for a in _rope_tables(seq))
    mask_add = jnp.asarray(_position_constants(seq, tiles[0])[2])
    bias_diff = _bias_tiles(rel_bias[:, :N_DIFF_MAPS], seq, tiles[1])
    bias_dsa = _bias_tiles(rel_bias[:, N_DIFF_MAPS:], seq, tiles[2])

    x2 = x.reshape(batch * seq, D_MODEL)
    for l in range(depth):
        lambda_init = 0.8 - 0.6 * math.exp(-0.3 * l)
        w1, wvt = _prep_w_in(w_in[l])
        wq1, wq2, wk, wavt = _prep_mla_up(mla_w_uq[l], mla_w_ukv[l])
        p = _inproj(x2, norm_g[l][None, :], w1, wvt, mla_qa_g[l][None, :], wq1, wq2, mla_kva_g[l][None, :], wk, wavt,
                    cos_t, sin_t, seq, tiles)
        ya = _mla(p, mask_add, batch, seq, tiles[0])
        yb = _diff(p, bias_diff, diff_lambda[l], diff_subln_g[l], lambda_init, batch, seq, tiles[1])
        yc = _dsa(p, bias_dsa, batch, seq, tiles[2])
        x2 = _outproj(x2, ya, yb, yc, p["yd"], p["ud"], conv_w[l], w_out[l].astype(BF16), final_g[None, :],
                      seq, final=(l == depth - 1))
    return x2.reshape(batch, seq, D_MODEL)
```

```python
import functools
import math

import numpy as np
import jax
import jax.numpy as jnp
from jax import lax
from jax.experimental import pallas as pl
from jax.experimental.pallas import tpu as pltpu

F32 = jnp.float32
BF16 = jnp.bfloat16

D_MODEL = 1024
CHUNK = 64
N_HEADS = 4
HEAD_DIM = 64
GROUP_WIDTH = N_HEADS * HEAD_DIM
MLA_Q_LORA = 256
MLA_KV_LORA = 128
MLA_NOPE = 64
MLA_ROPE = 32
ROPE_BASE = 10000.0
DIFF_QK = 32
IDX_HEADS = 8
IDX_DIM = 32
TOPK_MAX = 256
CONV_WIDTH = 3
NUM_BUCKETS = 32
MAX_DISTANCE = 128
N_DIFF_MAPS = 2 * N_HEADS
N_BIAS_MAPS = N_DIFF_MAPS + N_HEADS
NORM_EPS = 1e-6
NEG = -1e30
LOG2E = math.log2(math.e)
INT_MIN = -2 ** 31
KEY_LOWEST = int(np.float32(-np.finfo(np.float32).max).view(np.int32)) ^ 0x7FFFFFFF
HALF_BITS = 16
HALF_MASK = 2 ** HALF_BITS - 1
HALF_BIAS = 2 ** (HALF_BITS - 1)

LANES = 128
SUBLANES = 8
MLA_TILE = 256
DIFF_TILE = 256
DSA_TILE = 512
VT_TILE = 256
FAR_UNIT_KEYS = 512
PROJ_ROWS = 512
FLASH_LOOKAHEAD = 4
VMEM_LIMIT = 56 * 1024 * 1024

_IN_SPLITS = (
    ("a_cq", 256), ("a_ckv", 128), ("a_krope", 32), ("a_gate", 256),
    ("b_q", 256), ("b_k", 256), ("b_v", 256), ("b_gate", 256),
    ("c_q", 256), ("c_k", 256), ("c_v", 256),
    ("c_qidx", 256), ("c_kidx", 32), ("c_widx", 8), ("c_gate", 256),
    ("d_b", 256), ("d_c", 256), ("d_h", 256), ("d_gate", 256),
)

_SEGS = (
    ("cq", 256), ("ckv", 128), ("kr1", 128), ("kr2", 128), ("c_kidx4", 128), ("a_gate", 256),
    ("b_q", 256), ("b_k", 256), ("b_gate", 256),
    ("c_q", 256), ("c_k", 256), ("c_qidx", 256),
    ("c_gate", 256), ("d_b", 256), ("d_c", 256), ("d_h", 256), ("d_gate", 256), ("c_widx", 128),
)
_SEG_OFF = {}
_off = 0
for _name, _w in _SEGS:
    _SEG_OFF[_name] = (_off, _w)
    _off += _w
W1_COLS = _off

_NT = (((1,), (1,)), ((), ()))
_ONE_BUFFER = pl.Buffered(1)


def _cparams(sem):
    return pltpu.CompilerParams(dimension_semantics=sem, vmem_limit_bytes=VMEM_LIMIT)


def _split_cols(w):
    out, off = {}, 0
    for name, width in _IN_SPLITS:
        out[name] = w[..., off:off + width]
        off += width
    return out


def _rot_half_cols(w):
    half = MLA_ROPE // 2
    return jnp.concatenate([-w[..., half:], w[..., :half]], axis=-1)


def _prep_w_in(w):
    p = _split_cols(w)
    z = lambda n: jnp.zeros(w.shape[:-1] + (n,), w.dtype)
    cat = lambda parts: jnp.concatenate(parts, axis=-1)
    segs = {
        "cq": p["a_cq"], "ckv": p["a_ckv"], "a_gate": p["a_gate"],
        "kr1": cat([z(MLA_NOPE), p["a_krope"], z(LANES - MLA_NOPE - MLA_ROPE)]),
        "kr2": cat([z(MLA_NOPE), _rot_half_cols(p["a_krope"]), z(LANES - MLA_NOPE - MLA_ROPE)]),
        "b_q": p["b_q"] * (DIFF_QK ** -0.5 * LOG2E), "b_k": p["b_k"], "b_gate": p["b_gate"],
        "c_q": p["c_q"] * (HEAD_DIM ** -0.5 * LOG2E), "c_k": p["c_k"], "c_qidx": p["c_qidx"],
        "c_kidx4": cat([p["c_kidx"]] * (LANES // IDX_DIM)),
        "c_widx": cat([p["c_widx"], z(LANES - IDX_HEADS)]),
        "c_gate": p["c_gate"], "d_b": p["d_b"], "d_c": p["d_c"], "d_h": p["d_h"], "d_gate": p["d_gate"],
    }
    w1 = cat([segs[name].astype(BF16) for name, _ in _SEGS])
    wvt = jnp.swapaxes(cat([p["b_v"].astype(BF16), p["c_v"].astype(BF16)]), -1, -2)
    return w1, wvt


def _prep_mla_up(w_uq, w_ukv):
    depth = w_uq.shape[0]
    scale = (MLA_NOPE + MLA_ROPE) ** -0.5 * LOG2E
    wq = w_uq.reshape(depth, MLA_Q_LORA, N_HEADS, MLA_NOPE + MLA_ROPE) * scale
    zq = jnp.zeros((depth, MLA_Q_LORA, N_HEADS, LANES - MLA_NOPE - MLA_ROPE), w_uq.dtype)
    wq1 = jnp.concatenate([wq, zq], axis=-1).reshape(depth, MLA_Q_LORA, N_HEADS * LANES)
    wq2 = jnp.concatenate([jnp.zeros_like(wq[..., :MLA_NOPE]), _rot_half_cols(wq[..., MLA_NOPE:]), zq],
                          axis=-1).reshape(depth, MLA_Q_LORA, N_HEADS * LANES)
    wkv = w_ukv.reshape(depth, MLA_KV_LORA, N_HEADS, MLA_NOPE + HEAD_DIM)
    zk = jnp.zeros((depth, MLA_KV_LORA, N_HEADS, LANES - MLA_NOPE), w_ukv.dtype)
    wk = jnp.concatenate([wkv[..., :MLA_NOPE], zk], axis=-1).reshape(depth, MLA_KV_LORA, N_HEADS * LANES)
    wvt = jnp.swapaxes(wkv[..., MLA_NOPE:].reshape(depth, MLA_KV_LORA, N_HEADS * HEAD_DIM), -1, -2)
    return wq1.astype(BF16), wq2.astype(BF16), wk.astype(BF16), wvt.astype(BF16)


def _rope_tables(seq):
    half = MLA_ROPE // 2
    inv_freq = ROPE_BASE ** (-np.arange(half, dtype=np.float32) / half)
    ang = np.arange(seq, dtype=np.float32)[:, None] * inv_freq[None, :].astype(np.float32)
    cos, sin = np.cos(ang).astype(np.float32), np.sin(ang).astype(np.float32)
    ct = np.zeros((seq, LANES), np.float32)
    st = np.zeros((seq, LANES), np.float32)
    ct[:, :MLA_NOPE] = 1.0
    ct[:, MLA_NOPE:MLA_NOPE + half] = cos
    ct[:, MLA_NOPE + half:MLA_NOPE + MLA_ROPE] = cos
    st[:, MLA_NOPE:MLA_NOPE + half] = sin
    st[:, MLA_NOPE + half:MLA_NOPE + MLA_ROPE] = sin
    return ct, st


def _rel_bucket_np(rel):
    nb = NUM_BUCKETS // 2
    max_exact = nb // 2
    ret = np.where(rel > 0, nb, 0)
    n = np.abs(rel)
    nf = np.maximum(n, max_exact).astype(np.float32)
    large = max_exact + (np.log(nf / np.float32(max_exact)) / np.float32(math.log(MAX_DISTANCE / max_exact))
                         * np.float32(nb - max_exact)).astype(np.int32)
    large = np.minimum(large, nb - 1)
    return (ret + np.where(n < max_exact, n, large)).astype(np.int32)


def _position_constants(seq, tile):
    r = np.arange(tile)
    rel0 = r[:, None] - r[None, :]
    bidx = np.stack([_rel_bucket_np(rel0 - tile), _rel_bucket_np(rel0)]).astype(np.int32)
    far = _rel_bucket_np(np.arange(-(seq - 1), -tile))
    far_bucket = int(far[0]) if far.size else int(_rel_bucket_np(np.array([-tile - 1]))[0])
    assert far.size == 0 or np.all(far == far_bucket)
    mask_add = np.where((r[:, None] // CHUNK) <= (r[None, :] // CHUNK), 0.0, NEG).astype(np.float32)
    return bidx, far_bucket, mask_add


def _bias_kernel(tab_ref, bidx_ref, madd_ref, o_ref, *, far_bucket, buckets):
    j = pl.program_id(0)
    c = tab_ref[j, far_bucket]
    for d in range(2):
        idx = bidx_ref[d]
        acc = jnp.zeros(idx.shape, F32)
        for b in buckets[d]:
            acc = jnp.where(idx == b, (tab_ref[j, b] - c) * LOG2E, acc)
        if d == 1:
            acc = acc + madd_ref[...]
        o_ref[0, d] = acc


def _bias_tiles(table, seq, tile):
    bidx, far_bucket, mask_add = _position_constants(seq, tile)
    n_maps = table.shape[1]
    buckets = tuple(tuple(int(b) for b in np.unique(bidx[d])) for d in range(2))
    return pl.pallas_call(
        functools.partial(_bias_kernel, far_bucket=far_bucket, buckets=buckets),
        grid=(n_maps,),
        in_specs=[
            pl.BlockSpec(memory_space=pltpu.SMEM),
            pl.BlockSpec((2, tile, tile), lambda j: (0, 0, 0)),
            pl.BlockSpec((tile, tile), lambda j: (0, 0)),
        ],
        out_specs=pl.BlockSpec((1, 2, tile, tile), lambda j: (j, 0, 0, 0)),
        out_shape=jax.ShapeDtypeStruct((n_maps, 2, tile, tile), F32),
        compiler_params=_cparams(("arbitrary",)),
        name="bias_tiles",
    )(table.T.astype(F32), jnp.asarray(bidx), jnp.asarray(mask_add))


def _silu(x):
    return x * (1.0 / (1.0 + jnp.exp(-x)))


def _rms(x, g):
    return x * lax.rsqrt(jnp.mean(x * x, axis=-1, keepdims=True) + NORM_EPS) * g


def _inproj_kernel(x_ref, g_ref, w_ref, wvt_ref, qag_ref, wq1_ref, wq2_ref, kvg_ref, wk_ref, wavt_ref, cos_ref, sin_ref,
                   qa_ref, ka_ref, vat_ref, ga_ref, qb_ref, kb_ref, vbt_ref, gb_ref,
                   qc_ref, kc_ref, vct_ref, qi_ref, ki_ref, wi_ref, gc_ref, yd_ref, ud_ref):
    h = _rms(x_ref[...], g_ref[...]).astype(BF16)

    def seg(name):
        off, width = _SEG_OFF[name]
        return jnp.dot(h, w_ref[:, off:off + width], preferred_element_type=F32)

    def seg_pair(first, second):
        off, width = _SEG_OFF[first]
        assert _SEG_OFF[second] == (off + width, width) and width == LANES
        y = jnp.dot(h, w_ref[:, off:off + 2 * width], preferred_element_type=F32)
        return y[:, :width], y[:, width:]

    cos = cos_ref[...]
    sin = sin_ref[...]

    cqn = _rms(seg("cq"), qag_ref[...]).astype(BF16)
    q1 = jnp.dot(cqn, wq1_ref[...], preferred_element_type=F32)
    q2 = jnp.dot(cqn, wq2_ref[...], preferred_element_type=F32)
    ckv, kr1 = seg_pair("ckv", "kr1")
    kr2, kidx4 = seg_pair("kr2", "c_kidx4")
    kr = kr1 * cos + kr2 * sin
    ckvn = _rms(ckv, kvg_ref[...]).astype(BF16)
    kn = jnp.dot(ckvn, wk_ref[...], preferred_element_type=F32)
    for hd in range(N_HEADS):
        sl = slice(hd * LANES, (hd + 1) * LANES)
        qa_ref[:, sl] = (q1[:, sl] * cos + q2[:, sl] * sin).astype(BF16)
        ka_ref[:, sl] = (kn[:, sl] + kr).astype(BF16)
    ga_ref[...] = _silu(seg("a_gate")).astype(BF16)

    def put(ref, first_token, vt):
        tile = ref.shape[2]
        ref[first_token // tile, :, first_token % tile:first_token % tile + VT_TILE] = vt

    for c in range(x_ref.shape[0] // VT_TILE):
        rows = slice(c * VT_TILE, (c + 1) * VT_TILE)
        put(vat_ref, c * VT_TILE,
            lax.dot_general(wavt_ref[...], ckvn[rows], _NT, preferred_element_type=F32).astype(BF16))
        vt = lax.dot_general(wvt_ref[...], h[rows], _NT, preferred_element_type=F32).astype(BF16)
        put(vbt_ref, c * VT_TILE, vt[:GROUP_WIDTH])
        put(vct_ref, c * VT_TILE, vt[GROUP_WIDTH:])

    qb_ref[...] = seg("b_q").astype(BF16)
    kb_ref[...] = seg("b_k").astype(BF16)
    gb_ref[...] = _silu(seg("b_gate")).astype(BF16)

    qc_ref[...] = seg("c_q").astype(BF16)
    kc_ref[...] = seg("c_k").astype(BF16)
    qi_ref[...] = seg("c_qidx").astype(BF16)
    ki_ref[...] = kidx4.astype(BF16)
    wi_ref[...] = seg("c_widx")
    gc_ref[...] = _silu(seg("c_gate")).astype(BF16)

    yd_ref[...] = (seg("d_b") * _silu(seg("d_gate"))).astype(BF16)
    ud_ref[...] = seg("d_c") * seg("d_h")


def _layer_spec(a, l):
    return pl.BlockSpec((None,) + a.shape[1:], lambda i: (l,) + (0,) * (a.ndim - 1))


def _inproj(x2, l, g, w1, wvt, qag, wq1, wq2, kvg, wk, wavt, cos_t, sin_t, seq, tiles):
    n = x2.shape[0]
    tm = min(PROJ_ROWS, seq)
    tiles_per_seq = seq // tm
    row = lambda w: pl.BlockSpec((tm, w), lambda i: (i, 0))
    full = lambda a: _layer_spec(a, l)
    tab = pl.BlockSpec((tm, LANES), lambda i: (i % tiles_per_seq, 0))
    vt_spec = lambda t: pl.BlockSpec((tm // t, GROUP_WIDTH, t), lambda i: (i, 0, 0))
    vt_shape = lambda t: jax.ShapeDtypeStruct((n // t, GROUP_WIDTH, t), BF16)
    outs = (("qa", 512, BF16), ("ka", 512, BF16), ("vat", None, tiles[0]), ("ga", 256, BF16),
            ("qb", 256, BF16), ("kb", 256, BF16), ("vbt", None, tiles[1]), ("gb", 256, BF16),
            ("qc", 256, BF16), ("kc", 256, BF16), ("vct", None, tiles[2]), ("qi", 256, BF16),
            ("ki", 128, BF16), ("wi", 128, F32), ("gc", 256, BF16), ("yd", 256, BF16), ("ud", 256, F32))
    res = pl.pallas_call(
        _inproj_kernel,
        grid=(n // tm,),
        in_specs=[row(D_MODEL), full(g), full(w1), full(wvt), full(qag), full(wq1), full(wq2), full(kvg), full(wk),
                  full(wavt), tab, tab],
        out_specs=[vt_spec(dt) if w is None else row(w) for _, w, dt in outs],
        out_shape=[vt_shape(dt) if w is None else jax.ShapeDtypeStruct((n, w), dt) for _, w, dt in outs],
        compiler_params=_cparams(("parallel",)),
        name="inproj",
    )(x2, g, w1, wvt, qag, wq1, wq2, kvg, wk, wavt, cos_t, sin_t)
    return {name: o for (name, _, _), o in zip(outs, res)}


def _flash_scratch(n_maps, t):
    unit = _far_unit_tiles(t) * t
    return [pltpu.VMEM((n_maps, 1, t), F32), pltpu.VMEM((n_maps, 1, t), F32), pltpu.VMEM((n_maps, HEAD_DIM, t), F32),
            pltpu.VMEM((n_maps, unit, t), F32), pltpu.VMEM((n_maps, unit, t), F32)]


def _far_unit_tiles(t):
    return max(1, FAR_UNIT_KEYS // t)


def _flash_update(s, vts, old):
    t = vts[0].shape[1]
    m_tile = jnp.max(s, axis=0, keepdims=True)
    m_new = m_tile if old is None else jnp.maximum(old[0], m_tile)
    p = jnp.exp2(s - m_new)
    l_tile = jnp.sum(p, axis=0, keepdims=True)
    pb = p.astype(BF16)
    pv = jnp.dot(vts[0], pb[0:t], preferred_element_type=F32)
    for d in range(1, len(vts)):
        pv = pv + jnp.dot(vts[d], pb[d * t:(d + 1) * t], preferred_element_type=F32)
    if old is None:
        return m_new, l_tile, pv
    alpha = jnp.exp2(old[0] - m_new)
    return m_new, alpha * old[1] + l_tile, alpha * old[2] + pv


def _flash_sweep(i, t, n_maps, qs, keys, values, state, diag_add, near_add=None, mask=None):
    m_ref, l_ref, acc_ref, buf_a, buf_b = state
    n_far = i if near_add is None else jnp.maximum(i - 1, 0)
    ut = _far_unit_tiles(t)
    n_units = n_far // ut

    def load_state():
        return [(m_ref[j], l_ref[j], acc_ref[j]) for j in range(n_maps)]

    def store_state(cur):
        for j in range(n_maps):
            m_ref[j], l_ref[j], acc_ref[j] = cur[j]

    def tile_logits(j, kt, n_tiles, shared, extra=None):
        rows = pl.ds(pl.multiple_of(kt * t, t), n_tiles * t)
        s = lax.dot_general(keys(j, rows), qs[j], _NT, preferred_element_type=F32)
        add = shared
        if extra is not None:
            add = extra(j) if add is None else add + extra(j)
        return s if add is None else s + add

    def shared_mask(kt, n_tiles):
        if mask is None:
            return None
        tiles = [mask(kt + d) for d in range(n_tiles)]
        return tiles[0] if n_tiles == 1 else jnp.concatenate(tiles, axis=0)

    def run(units, first=False, prefetch=False):
        cur = [None] * n_maps if first else load_state()
        shared = [shared_mask(kt, n_tiles) for kt, n_tiles, _ in units]
        items = [(u, j) for u in range(len(units)) for j in range(n_maps)]
        if prefetch:
            items += [(None, j) for j in range(n_maps)]
            shared.append(shared_mask(0, ut))

        def logits(u, j):
            if u is None:
                return tile_logits(j, 0, ut, shared[-1])
            kt, n_tiles, extra = units[u]
            return tile_logits(j, kt, n_tiles, shared[u], extra)

        queue = [logits(*it) for it in items[:FLASH_LOOKAHEAD]]
        for n, (u, j) in enumerate(items):
            if n + FLASH_LOOKAHEAD < len(items):
                queue.append(logits(*items[n + FLASH_LOOKAHEAD]))
            s = queue.pop(0)
            if u is None:
                buf_a[j] = s
            else:
                kt, n_tiles, _ = units[u]
                cur[j] = _flash_update(s, [values(j, kt + d) for d in range(n_tiles)], cur[j])
        store_state(cur)

    def first_run(prefetch):
        if near_add is None:
            return lambda: run([(i, 1, diag_add)], first=True, prefetch=prefetch)
        return lambda: run([(i, 1, diag_add), (i - 1, 1, near_add)], first=True, prefetch=prefetch)

    if near_add is None:
        pl.when(n_units == 0)(first_run(False))
    else:
        pl.when(i == 0)(lambda: run([(i, 1, diag_add)], first=True))
        pl.when((i >= 1) & (n_units == 0))(first_run(False))
    pl.when(n_units >= 1)(first_run(True))

    def consume(buf, kt, j, old):
        return _flash_update(buf[j], [values(j, kt + d) for d in range(ut)], old)

    def overlapped(cur, buf_in, buf_out, u):
        kt_next = ut * (u + 1)
        shared = shared_mask(kt_next, ut)
        for j in range(n_maps):
            buf_out[j] = tile_logits(j, kt_next, ut, shared)
            cur[j] = consume(buf_in, ut * u, j, cur[j])

    def far_body(c, carry):
        cur = load_state()
        overlapped(cur, buf_a, buf_b, 2 * c)
        overlapped(cur, buf_b, buf_a, 2 * c + 1)
        store_state(cur)
        return carry

    lax.fori_loop(0, (n_units - 1) // 2, far_body, 0)

    @pl.when(n_units % 2 == 1)
    def _():
        cur = load_state()
        for j in range(n_maps):
            cur[j] = consume(buf_a, ut * (n_units - 1), j, cur[j])
        store_state(cur)

    @pl.when((n_units >= 2) & (n_units % 2 == 0))
    def _():
        cur = load_state()
        overlapped(cur, buf_a, buf_b, n_units - 2)
        for j in range(n_maps):
            cur[j] = consume(buf_b, ut * (n_units - 1), j, cur[j])
        store_state(cur)

    for r in range(1, ut):
        pl.when(n_far % ut >= r)(lambda r=r: run([(n_far - r, 1, None)]))
    return [acc_ref[j] / l_ref[j] for j in range(n_maps)]


def _lane_mask(lo, hi, dtype):
    lane = lax.broadcasted_iota(jnp.int32, (1, LANES), 1)
    return ((lane >= lo) & (lane < hi)).astype(dtype)


def _store_gated(o_ref, g_ref, heads_t):
    o = jnp.concatenate(heads_t, axis=0).T
    o_ref[...] = (o * g_ref[...].astype(F32)).astype(BF16)


def _att_specs(t, seq, widths_q, widths_k):
    nq = seq // t
    qspec = lambda w: pl.BlockSpec((t, w), lambda b, i: (b * nq + i, 0))
    kspec = lambda w: pl.BlockSpec((seq, w), lambda b, i: (b, 0), pipeline_mode=_ONE_BUFFER)
    vspec = pl.BlockSpec((nq, GROUP_WIDTH, t), lambda b, i: (b, 0, 0), pipeline_mode=_ONE_BUFFER)
    return [qspec(w) for w in widths_q], [kspec(w) for w in widths_k], vspec, qspec(GROUP_WIDTH)


def _mla_kernel(q_ref, k_ref, vt_ref, g_ref, madd_ref, o_ref, *state, t):
    i = pl.program_id(1)
    qs = [q_ref[:, hd * LANES:(hd + 1) * LANES] for hd in range(N_HEADS)]
    keys = lambda hd, rows: k_ref[rows, hd * LANES:(hd + 1) * LANES]
    values = lambda hd, kt: vt_ref[kt, hd * HEAD_DIM:(hd + 1) * HEAD_DIM, :]
    outs = _flash_sweep(i, t, N_HEADS, qs, keys, values, state, lambda hd: madd_ref[...])
    _store_gated(o_ref, g_ref, outs)


def _mla(p, mask_add, batch, seq, t):
    qs, ks, vspec, ospec = _att_specs(t, seq, (512, 256), (512,))
    return pl.pallas_call(
        functools.partial(_mla_kernel, t=t),
        grid=(batch, seq // t),
        in_specs=[qs[0], ks[0], vspec, qs[1], pl.BlockSpec((t, t), lambda b, i: (0, 0), pipeline_mode=_ONE_BUFFER)],
        out_specs=ospec,
        out_shape=jax.ShapeDtypeStruct((batch * seq, GROUP_WIDTH), BF16),
        scratch_shapes=_flash_scratch(N_HEADS, t),
        compiler_params=_cparams(("parallel", "arbitrary")),
        name="mla_attention",
    )(p["qa"], p["ka"], p["vat"], p["ga"], mask_add)


def _diff_kernel(q_ref, k_ref, vt_ref, g_ref, bias_ref, lam_ref, subg_ref, o_ref, *state, t, lambda_init):
    i = pl.program_id(1)
    lp = lam_ref[...]
    lam = (jnp.exp(jnp.sum(lp[0:1] * lp[1:2], axis=-1, keepdims=True))
           - jnp.exp(jnp.sum(lp[2:3] * lp[3:4], axis=-1, keepdims=True)) + lambda_init)
    qs = [q_ref[:, (j // 4) * LANES:(j // 4 + 1) * LANES] * _lane_mask((j % 4) * DIFF_QK, (j % 4 + 1) * DIFF_QK, BF16)
          for j in range(N_DIFF_MAPS)]
    keys = lambda j, rows: k_ref[rows, (j // 4) * LANES:(j // 4 + 1) * LANES]
    values = lambda j, kt: vt_ref[kt, (j // 2) * HEAD_DIM:(j // 2 + 1) * HEAD_DIM, :]
    maps = _flash_sweep(i, t, N_DIFF_MAPS, qs, keys, values, state, lambda j: bias_ref[j, 1], lambda j: bias_ref[j, 0])
    outs = []
    for hd in range(N_HEADS):
        o = maps[2 * hd] - lam * maps[2 * hd + 1]
        ms = jnp.mean(o * o, axis=0, keepdims=True)
        outs.append(o * lax.rsqrt(ms + NORM_EPS) * subg_ref[...] * (1.0 - lambda_init))
    _store_gated(o_ref, g_ref, outs)


def _diff(p, bias, lam_params, subln_g, lambda_init, batch, seq, t):
    qs, ks, vspec, ospec = _att_specs(t, seq, (256, 256), (256,))
    subg = jnp.broadcast_to(subln_g.astype(F32)[:, None], (HEAD_DIM, t))
    return pl.pallas_call(
        functools.partial(_diff_kernel, t=t, lambda_init=lambda_init),
        grid=(batch, seq // t),
        in_specs=[qs[0], ks[0], vspec, qs[1],
                  pl.BlockSpec((N_DIFF_MAPS, 2, t, t), lambda b, i: (0, 0, 0, 0), pipeline_mode=_ONE_BUFFER),
                  pl.BlockSpec((4, DIFF_QK), lambda b, i: (0, 0)),
                  pl.BlockSpec((HEAD_DIM, t), lambda b, i: (0, 0))],
        out_specs=ospec,
        out_shape=jax.ShapeDtypeStruct((batch * seq, GROUP_WIDTH), BF16),
        scratch_shapes=_flash_scratch(N_DIFF_MAPS, t),
        compiler_params=_cparams(("parallel", "arbitrary")),
        name="diff_attention",
    )(p["qb"], p["kb"], p["vbt"], p["gb"], bias, lam_params.astype(F32), subg)


def _sort_key(x):
    b = lax.bitcast_convert_type(x + 0.0, jnp.int32)
    return b ^ ((b >> 31) & 0x7FFFFFFF)


def _dsa_kernel(q_ref, k_ref, vt_ref, qi_ref, ki_ref, wi_ref, g_ref, bias_ref, tri_ref, o_ref,
                score_ref, hi_ref, lo_ref, thr_ref, cnt_ref, *state, t, topk):
    i = pl.program_id(1)
    n_kt = i + 1
    idx_scale = (IDX_HEADS ** -0.5) * (IDX_DIM ** -0.5)

    wit = wi_ref[...].T * idx_scale
    qm, wrow = [], []
    for ih in range(IDX_HEADS):
        grp = slice((ih // 4) * LANES, (ih // 4 + 1) * LANES)
        lo = (ih % 4) * IDX_DIM
        qm.append(qi_ref[:, grp] * _lane_mask(lo, lo + IDX_DIM, BF16))
        wrow.append(wit[ih:ih + 1, :])

    def idx_tile(kt):
        kk = ki_ref[pl.ds(pl.multiple_of(kt * t, t), t), :]
        dot = lambda ih: lax.dot_general(kk, qm[ih], _NT, preferred_element_type=F32)
        queue = [dot(ih) for ih in range(FLASH_LOOKAHEAD)]
        tot = None
        for ih in range(IDX_HEADS):
            if ih + FLASH_LOOKAHEAD < IDX_HEADS:
                queue.append(dot(ih + FLASH_LOOKAHEAD))
            term = wrow[ih] * jnp.maximum(queue.pop(0), 0.0)
            tot = term if tot is None else tot + term
        return tot + 0.0

    def store_scores(kt, score):
        score_ref[kt] = score
        key = _sort_key(score)
        hi_ref[kt] = (key >> HALF_BITS).astype(jnp.int16)
        lo_ref[kt] = ((key & HALF_MASK) - HALF_BIAS).astype(jnp.int16)

    def key_tiles(first, n):
        for d in range(n):
            store_scores(first + d, idx_tile(first + d))

    def key_quad_body(c, carry):
        key_tiles(4 * c, 4)
        return carry

    lax.fori_loop(0, i // 4, key_quad_body, 0)
    pl.when(i % 4 >= 2)(lambda: key_tiles(i - i % 4, 2))
    pl.when(i % 2 == 1)(lambda: key_tiles(i - 1, 1))
    kchunk = lax.broadcasted_iota(jnp.int32, (t, t), 0) // CHUNK
    qchunk = lax.broadcasted_iota(jnp.int32, (t, t), 1) // CHUNK
    store_scores(i, jnp.where(kchunk <= qchunk, idx_tile(i), -jnp.inf))

    def count(pred):
        def body(kt, c):
            hit = jnp.where(pred(score_ref[kt]), 1.0, 0.0)
            return c + jnp.sum(hit.reshape(t // SUBLANES, SUBLANES, t), axis=0)
        part = lax.fori_loop(0, n_kt, body, jnp.zeros((SUBLANES, t), F32))
        return jnp.sum(part, axis=0, keepdims=True)

    rows16 = 2 * SUBLANES
    n_acc = 4

    def count16(ref, pred):
        one, zero = jnp.int16(1), jnp.int16(0)

        def body(kt, accs):
            x = ref[kt].reshape(t // rows16, rows16, t)
            accs = list(accs)
            for r in range(t // rows16):
                accs[r % n_acc] = accs[r % n_acc] + jnp.where(pred(x[r]), one, zero)
            return tuple(accs)
        accs = lax.fori_loop(0, n_kt, body, tuple(jnp.zeros((rows16, t), jnp.int16) for _ in range(n_acc)))
        tot = (accs[0] + accs[1]) + (accs[2] + accs[3])
        return jnp.sum(tot.astype(jnp.int32), axis=0, keepdims=True)

    def select16(ref, need):
        def bit_body(b, res):
            cand = res | (jnp.int32(1) << (HALF_BITS - 1 - b))
            cand16 = (cand - HALF_BIAS).astype(jnp.int16)
            cnt = count16(ref, lambda x: x >= cand16)
            return jnp.where(cnt >= need, cand, res)
        return lax.fori_loop(0, HALF_BITS, bit_body, jnp.zeros((1, t), jnp.int32)) - HALF_BIAS

    hi_thr = select16(hi_ref, topk)
    hi_thr16 = hi_thr.astype(jnp.int16)
    need_lo = topk - count16(hi_ref, lambda x: x > hi_thr16)

    def member_body(kt, c):
        lo_ref[kt] = jnp.where(hi_ref[kt] == hi_thr16, lo_ref[kt], jnp.int16(-HALF_BIAS))
        return c

    lax.fori_loop(0, n_kt, member_body, 0)
    lo_thr = select16(lo_ref, need_lo)

    def key_to_score(key):
        key = jnp.maximum(key, KEY_LOWEST)
        return lax.bitcast_convert_type(key ^ ((key >> 31) & 0x7FFFFFFF), F32)

    thr_ref[...] = key_to_score((hi_thr << HALF_BITS) | (lo_thr + HALF_BIAS))
    cnt_ref[...] = count(lambda x: x >= thr_ref[...])
    n_valid = ((i * t + lax.broadcasted_iota(jnp.int32, (1, t), 1)) // CHUNK + 1) * CHUNK
    few = n_valid <= topk

    @pl.when(jnp.max(jnp.where(few, 0.0, jnp.abs(cnt_ref[...] - topk))) > 0.0)
    def _():
        cnt_gt = count(lambda x: x > thr_ref[...])
        wrong = (cnt_ref[...] < topk) | (cnt_gt >= topk)

        @pl.when(jnp.max(jnp.where(few | jnp.logical_not(wrong), 0.0, 1.0)) > 0.0)
        def _():
            def bit_body(b, res):
                cand = res | (jnp.int32(1) << (31 - b))
                cand_score = key_to_score(cand ^ INT_MIN)
                cnt = count(lambda x: x >= cand_score)
                return jnp.where(cnt >= topk, cand, res)
            res = lax.fori_loop(0, 32, bit_body, jnp.zeros((1, t), jnp.int32))
            thr_ref[...] = key_to_score(res ^ INT_MIN)
            cnt_ref[...] = count(lambda x: x >= thr_ref[...])

    thr = thr_ref[...]
    has_ties = jnp.max(jnp.where(few, 0.0, cnt_ref[...])) > topk

    @pl.when(jnp.logical_not(has_ties))
    def _():
        def body(kt, c):
            score_ref[kt] = jnp.where(score_ref[kt] >= thr, 0.0, NEG)
            return c
        lax.fori_loop(0, n_kt, body, 0)

    @pl.when(has_ties)
    def _():
        need = topk - count(lambda x: x > thr)

        def body(kt, seen):
            x = score_ref[kt]
            eq = jnp.where(x == thr, 1.0, 0.0)
            rank = seen + jnp.dot(tri_ref[...], eq.astype(BF16), preferred_element_type=F32)
            sel = (x > thr) | ((x == thr) & (rank <= need))
            score_ref[kt] = jnp.where(sel, 0.0, NEG)
            return seen + jnp.sum(eq, axis=0, keepdims=True)
        lax.fori_loop(0, n_kt, body, jnp.zeros((1, t), F32))

    qs = [q_ref[:, (hd // 2) * LANES:(hd // 2 + 1) * LANES]
          * _lane_mask((hd % 2) * HEAD_DIM, (hd % 2 + 1) * HEAD_DIM, BF16) for hd in range(N_HEADS)]
    keys = lambda hd, rows: k_ref[rows, (hd // 2) * LANES:(hd // 2 + 1) * LANES]
    values = lambda hd, kt: vt_ref[kt, hd * HEAD_DIM:(hd + 1) * HEAD_DIM, :]
    outs = _flash_sweep(i, t, N_HEADS, qs, keys, values, state, lambda hd: bias_ref[hd, 1], lambda hd: bias_ref[hd, 0],
                        mask=lambda kt: score_ref[kt])
    _store_gated(o_ref, g_ref, outs)


def _dsa(p, bias, batch, seq, t):
    topk = min(TOPK_MAX, seq // 4)
    nq = seq // t
    qs, ks, vspec, ospec = _att_specs(t, seq, (256, 256, 128, 256), (256, 128))
    tri = jnp.asarray(np.tril(np.ones((t, t), np.float32)), BF16)
    return pl.pallas_call(
        functools.partial(_dsa_kernel, t=t, topk=topk),
        grid=(batch, nq),
        in_specs=[qs[0], ks[0], vspec, qs[1], ks[1], qs[2], qs[3],
                  pl.BlockSpec((N_HEADS, 2, t, t), lambda b, i: (0, 0, 0, 0), pipeline_mode=_ONE_BUFFER),
                  pl.BlockSpec((t, t), lambda b, i: (0, 0), pipeline_mode=_ONE_BUFFER)],
        out_specs=ospec,
        out_shape=jax.ShapeDtypeStruct((batch * seq, GROUP_WIDTH), BF16),
        scratch_shapes=[pltpu.VMEM((nq, t, t), F32), pltpu.VMEM((nq, t, t), jnp.int16),
                        pltpu.VMEM((nq, t, t), jnp.int16), pltpu.VMEM((1, t), F32), pltpu.VMEM((1, t), F32)]
        + _flash_scratch(N_HEADS, t),
        compiler_params=_cparams(("parallel", "arbitrary")),
        name="dsa_attention",
    )(p["qc"], p["kc"], p["vct"], p["qi"], p["ki"], p["wi"], p["gc"], bias, tri)


def _outproj_kernel(x_ref, ya_ref, yb_ref, yc_ref, yd_ref, ud_ref, halo_ref, cw_ref, w_ref, fg_ref, o_ref,
                    *, tiles_per_seq, final):
    i = pl.program_id(0)
    tm = x_ref.shape[0]
    u = ud_ref[...]
    halo = jnp.where(i % tiles_per_seq == 0, 0.0, halo_ref[...])
    row8 = lax.broadcasted_iota(jnp.int32, (SUBLANES, 1), 0)

    def shifted(k):
        r = pltpu.roll(u, k, 0)
        head = jnp.where(row8 < k, pltpu.roll(halo, k, 0), r[0:SUBLANES])
        return jnp.concatenate([head, r[SUBLANES:tm]], axis=0)

    cw = cw_ref[...]
    conv = cw[0:1] * shifted(2) + cw[1:2] * shifted(1) + cw[2:3] * u
    yd = (conv * yd_ref[...].astype(F32)).astype(BF16)
    acc = x_ref[...]
    for g, y in enumerate((ya_ref[...], yb_ref[...], yc_ref[...], yd)):
        acc = acc + jnp.dot(y, w_ref[g * GROUP_WIDTH:(g + 1) * GROUP_WIDTH, :], preferred_element_type=F32)
    if final:
        acc = _rms(acc, fg_ref[...])
    o_ref[...] = acc


def _outproj(x2, l, ya, yb, yc, yd, ud, conv_w, w_out, final_g, seq, final):
    n = x2.shape[0]
    tm = min(PROJ_ROWS, seq)
    tiles_per_seq = seq // tm
    row = lambda w: pl.BlockSpec((tm, w), lambda i: (i, 0))
    halo = pl.BlockSpec((SUBLANES, GROUP_WIDTH), lambda i: (jnp.maximum(i * (tm // SUBLANES) - 1, 0), 0))
    return pl.pallas_call(
        functools.partial(_outproj_kernel, tiles_per_seq=tiles_per_seq, final=final),
        grid=(n // tm,),
        in_specs=[row(D_MODEL), row(256), row(256), row(256), row(256), row(256), halo,
                  _layer_spec(conv_w, l), _layer_spec(w_out, l), pl.BlockSpec(final_g.shape, lambda i: (0, 0))],
        out_specs=row(D_MODEL),
        out_shape=jax.ShapeDtypeStruct((n, D_MODEL), F32),
        compiler_params=_cparams(("parallel",)),
        name="outproj",
    )(x2, ya, yb, yc, yd, ud, ud, conv_w, w_out, final_g)


def kernel(x, norm_g, w_in, mla_qa_g, mla_w_uq, mla_kva_g, mla_w_ukv, diff_lambda, diff_subln_g, conv_w, w_out,
           rel_bias, final_g):
    batch, seq, _ = x.shape
    depth = w_in.shape[0]
    tm = min(PROJ_ROWS, seq)
    tiles = tuple(min(t, seq) for t in (MLA_TILE, DIFF_TILE, DSA_TILE))
    assert seq % tm == 0 and tm % VT_TILE == 0
    assert all(seq % t == 0 and t % CHUNK == 0 and tm % t == 0 and t % VT_TILE == 0 for t in tiles)

    cos_t, sin_t = (jnp.asarray(a) for a in _rope_tables(seq))
    mask_add = jnp.asarray(_position_constants(seq, tiles[0])[2])
    bias_diff = _bias_tiles(rel_bias[:, :N_DIFF_MAPS], seq, tiles[1])
    bias_dsa = _bias_tiles(rel_bias[:, N_DIFF_MAPS:], seq, tiles[2])

    w1, wvt = _prep_w_in(w_in)
    wq1, wq2, wk, wavt = _prep_mla_up(mla_w_uq, mla_w_ukv)
    row3 = lambda a: a[:, None, :]
    w_out_b = w_out.astype(BF16)

    x2 = x.reshape(batch * seq, D_MODEL)
    for l in range(depth):
        lambda_init = 0.8 - 0.6 * math.exp(-0.3 * l)
        p = _inproj(x2, l, row3(norm_g), w1, wvt, row3(mla_qa_g), wq1, wq2, row3(mla_kva_g), wk, wavt,
                    cos_t, sin_t, seq, tiles)
        ya = _mla(p, mask_add, batch, seq, tiles[0])
        yb = _diff(p, bias_diff, diff_lambda[l], diff_subln_g[l], lambda_init, batch, seq, tiles[1])
        yc = _dsa(p, bias_dsa, batch, seq, tiles[2])
        x2 = _outproj(x2, l, ya, yb, yc, p["yd"], p["ud"], conv_w, w_out_b, final_g[None, :],
                      seq, final=(l == depth - 1))
    return x2.reshape(batch, seq, D_MODEL)
```

```python
import functools
import math

import numpy as np
import jax
import jax.numpy as jnp
from jax import lax
from jax.experimental import pallas as pl
from jax.experimental.pallas import tpu as pltpu

F32 = jnp.float32
BF16 = jnp.bfloat16

D_MODEL = 1024
CHUNK = 64
N_HEADS = 4
HEAD_DIM = 64
GROUP_WIDTH = N_HEADS * HEAD_DIM
MLA_Q_LORA = 256
MLA_KV_LORA = 128
MLA_NOPE = 64
MLA_ROPE = 32
ROPE_BASE = 10000.0
DIFF_QK = 32
IDX_HEADS = 8
IDX_DIM = 32
TOPK_MAX = 256
CONV_WIDTH = 3
NUM_BUCKETS = 32
MAX_DISTANCE = 128
N_DIFF_MAPS = 2 * N_HEADS
NORM_EPS = 1e-6
NEG = -1e30
LOG2E = math.log2(math.e)
INT_MIN = -2 ** 31
KEY_LOWEST = int(np.float32(-np.finfo(np.float32).max).view(np.int32)) ^ 0x7FFFFFFF
HALF_BITS = 16
HALF_MASK = 2 ** HALF_BITS - 1
HALF_BIAS = 2 ** (HALF_BITS - 1)

LANES = 128
SUBLANES = 8
MLA_TILE = 256
DIFF_TILE = 256
DSA_TILE = 512
VT_TILE = 256
FAR_UNIT_KEYS = 512
PROJ_ROWS = 512
FLASH_LOOKAHEAD = 4
V7X_VMEM_BYTES = 64 * 1024 * 1024
VMEM_LIMIT = V7X_VMEM_BYTES * 7 // 8

_IN_SPLITS = (
    ("a_cq", 256), ("a_ckv", 128), ("a_krope", 32), ("a_gate", 256),
    ("b_q", 256), ("b_k", 256), ("b_v", 256), ("b_gate", 256),
    ("c_q", 256), ("c_k", 256), ("c_v", 256),
    ("c_qidx", 256), ("c_kidx", 32), ("c_widx", 8), ("c_gate", 256),
    ("d_b", 256), ("d_c", 256), ("d_h", 256), ("d_gate", 256),
)

_SEGS = (
    ("cq", 256), ("ckv", 128), ("kr1", 128), ("kr2", 128), ("c_kidx4", 128), ("a_gate", 256),
    ("b_q", 256), ("b_k", 256), ("b_gate", 256),
    ("c_q", 256), ("c_k", 256), ("c_qidx", 256),
    ("c_gate", 256), ("d_b", 256), ("d_c", 256), ("d_h", 256), ("d_gate", 256), ("c_widx", 128),
)
_SEG_OFF = {}
_off = 0
for _name, _w in _SEGS:
    _SEG_OFF[_name] = (_off, _w)
    _off += _w
W1_COLS = _off

_NT = (((1,), (1,)), ((), ()))
_ONE_BUFFER = pl.Buffered(1)


def _cparams(sem):
    return pltpu.CompilerParams(dimension_semantics=sem, vmem_limit_bytes=VMEM_LIMIT)


def _split_cols(w):
    out, off = {}, 0
    for name, width in _IN_SPLITS:
        out[name] = w[..., off:off + width]
        off += width
    return out


def _rot_half_cols(w):
    half = MLA_ROPE // 2
    return jnp.concatenate([-w[..., half:], w[..., :half]], axis=-1)


def _prep_w_in(w):
    p = _split_cols(w)
    z = lambda n: jnp.zeros(w.shape[:-1] + (n,), w.dtype)
    cat = lambda parts: jnp.concatenate(parts, axis=-1)
    segs = {
        "cq": p["a_cq"], "ckv": p["a_ckv"], "a_gate": p["a_gate"],
        "kr1": cat([z(MLA_NOPE), p["a_krope"], z(LANES - MLA_NOPE - MLA_ROPE)]),
        "kr2": cat([z(MLA_NOPE), _rot_half_cols(p["a_krope"]), z(LANES - MLA_NOPE - MLA_ROPE)]),
        "b_q": p["b_q"] * (DIFF_QK ** -0.5 * LOG2E), "b_k": p["b_k"], "b_gate": p["b_gate"],
        "c_q": p["c_q"] * (HEAD_DIM ** -0.5 * LOG2E), "c_k": p["c_k"], "c_qidx": p["c_qidx"],
        "c_kidx4": cat([p["c_kidx"]] * (LANES // IDX_DIM)),
        "c_widx": cat([p["c_widx"], z(LANES - IDX_HEADS)]),
        "c_gate": p["c_gate"], "d_b": p["d_b"], "d_c": p["d_c"], "d_h": p["d_h"], "d_gate": p["d_gate"],
    }
    w1 = cat([segs[name].astype(BF16) for name, _ in _SEGS])
    wvt = jnp.swapaxes(cat([p["b_v"].astype(BF16), p["c_v"].astype(BF16)]), -1, -2)
    return w1, wvt


def _prep_mla_up(w_uq, w_ukv):
    depth = w_uq.shape[0]
    scale = (MLA_NOPE + MLA_ROPE) ** -0.5 * LOG2E
    wq = w_uq.reshape(depth, MLA_Q_LORA, N_HEADS, MLA_NOPE + MLA_ROPE) * scale
    zq = jnp.zeros((depth, MLA_Q_LORA, N_HEADS, LANES - MLA_NOPE - MLA_ROPE), w_uq.dtype)
    wq1 = jnp.concatenate([wq, zq], axis=-1).reshape(depth, MLA_Q_LORA, N_HEADS * LANES)
    wq2 = jnp.concatenate([jnp.zeros_like(wq[..., :MLA_NOPE]), _rot_half_cols(wq[..., MLA_NOPE:]), zq],
                          axis=-1).reshape(depth, MLA_Q_LORA, N_HEADS * LANES)
    wkv = w_ukv.reshape(depth, MLA_KV_LORA, N_HEADS, MLA_NOPE + HEAD_DIM)
    zk = jnp.zeros((depth, MLA_KV_LORA, N_HEADS, LANES - MLA_NOPE), w_ukv.dtype)
    wk = jnp.concatenate([wkv[..., :MLA_NOPE], zk], axis=-1).reshape(depth, MLA_KV_LORA, N_HEADS * LANES)
    wvt = jnp.swapaxes(wkv[..., MLA_NOPE:].reshape(depth, MLA_KV_LORA, N_HEADS * HEAD_DIM), -1, -2)
    return wq1.astype(BF16), wq2.astype(BF16), wk.astype(BF16), wvt.astype(BF16)


def _rope_tables(seq):
    half = MLA_ROPE // 2
    inv_freq = ROPE_BASE ** (-np.arange(half, dtype=np.float32) / half)
    ang = np.arange(seq, dtype=np.float32)[:, None] * inv_freq[None, :].astype(np.float32)
    cos, sin = np.cos(ang).astype(np.float32), np.sin(ang).astype(np.float32)
    ct = np.zeros((seq, LANES), np.float32)
    st = np.zeros((seq, LANES), np.float32)
    ct[:, :MLA_NOPE] = 1.0
    ct[:, MLA_NOPE:MLA_NOPE + half] = cos
    ct[:, MLA_NOPE + half:MLA_NOPE + MLA_ROPE] = cos
    st[:, MLA_NOPE:MLA_NOPE + half] = sin
    st[:, MLA_NOPE + half:MLA_NOPE + MLA_ROPE] = sin
    return ct, st


def _rel_bucket_np(rel):
    nb = NUM_BUCKETS // 2
    max_exact = nb // 2
    ret = np.where(rel > 0, nb, 0)
    n = np.abs(rel)
    nf = np.maximum(n, max_exact).astype(np.float32)
    large = max_exact + (np.log(nf / np.float32(max_exact)) / np.float32(math.log(MAX_DISTANCE / max_exact))
                         * np.float32(nb - max_exact)).astype(np.int32)
    large = np.minimum(large, nb - 1)
    return (ret + np.where(n < max_exact, n, large)).astype(np.int32)


def _position_constants(seq, tile):
    r = np.arange(tile)
    rel0 = r[:, None] - r[None, :]
    bidx = np.stack([_rel_bucket_np(rel0 - tile), _rel_bucket_np(rel0)]).astype(np.int32)
    far = _rel_bucket_np(np.arange(-(seq - 1), -tile))
    far_bucket = int(far[0]) if far.size else int(_rel_bucket_np(np.array([-tile - 1]))[0])
    assert far.size == 0 or np.all(far == far_bucket)
    mask_add = np.where((r[:, None] // CHUNK) <= (r[None, :] // CHUNK), 0.0, NEG).astype(np.float32)
    return bidx, far_bucket, mask_add


def _bias_kernel(tab_ref, bidx_ref, madd_ref, o_ref, *, far_bucket, buckets):
    j = pl.program_id(0)
    c = tab_ref[j, far_bucket]
    for d in range(2):
        idx = bidx_ref[d]
        acc = jnp.zeros(idx.shape, F32)
        for b in buckets[d]:
            acc = jnp.where(idx == b, (tab_ref[j, b] - c) * LOG2E, acc)
        if d == 1:
            acc = acc + madd_ref[...]
        o_ref[0, d] = acc


def _bias_tiles(table, seq, tile):
    bidx, far_bucket, mask_add = _position_constants(seq, tile)
    n_maps = table.shape[1]
    buckets = tuple(tuple(int(b) for b in np.unique(bidx[d])) for d in range(2))
    return pl.pallas_call(
        functools.partial(_bias_kernel, far_bucket=far_bucket, buckets=buckets),
        grid=(n_maps,),
        in_specs=[
            pl.BlockSpec(memory_space=pltpu.SMEM),
            pl.BlockSpec((2, tile, tile), lambda j: (0, 0, 0)),
            pl.BlockSpec((tile, tile), lambda j: (0, 0)),
        ],
        out_specs=pl.BlockSpec((1, 2, tile, tile), lambda j: (j, 0, 0, 0)),
        out_shape=jax.ShapeDtypeStruct((n_maps, 2, tile, tile), F32),
        compiler_params=_cparams(("arbitrary",)),
        name="bias_tiles",
    )(table.T.astype(F32), jnp.asarray(bidx), jnp.asarray(mask_add))


def _silu(x):
    return x * (1.0 / (1.0 + jnp.exp(-x)))


def _rms(x, g):
    return x * lax.rsqrt(jnp.mean(x * x, axis=-1, keepdims=True) + NORM_EPS) * g


def _inproj_kernel(x_ref, g_ref, w_ref, wvt_ref, qag_ref, wq1_ref, wq2_ref, kvg_ref, wk_ref, wavt_ref, cos_ref, sin_ref,
                   qa_ref, ka_ref, vat_ref, ga_ref, qb_ref, kb_ref, vbt_ref, gb_ref,
                   qc_ref, kc_ref, vct_ref, qi_ref, ki_ref, wi_ref, gc_ref, yd_ref, ud_ref):
    h = _rms(x_ref[...], g_ref[...]).astype(BF16)

    def seg(name):
        off, width = _SEG_OFF[name]
        return jnp.dot(h, w_ref[:, off:off + width], preferred_element_type=F32)

    def seg_pair(first, second):
        off, width = _SEG_OFF[first]
        assert _SEG_OFF[second] == (off + width, width) and width == LANES
        y = jnp.dot(h, w_ref[:, off:off + 2 * width], preferred_element_type=F32)
        return y[:, :width], y[:, width:]

    cos = cos_ref[...]
    sin = sin_ref[...]

    cqn = _rms(seg("cq"), qag_ref[...]).astype(BF16)
    q1 = jnp.dot(cqn, wq1_ref[...], preferred_element_type=F32)
    q2 = jnp.dot(cqn, wq2_ref[...], preferred_element_type=F32)
    ckv, kr1 = seg_pair("ckv", "kr1")
    kr2, kidx4 = seg_pair("kr2", "c_kidx4")
    kr = kr1 * cos + kr2 * sin
    ckvn = _rms(ckv, kvg_ref[...]).astype(BF16)
    kn = jnp.dot(ckvn, wk_ref[...], preferred_element_type=F32)
    for hd in range(N_HEADS):
        sl = slice(hd * LANES, (hd + 1) * LANES)
        qa_ref[:, sl] = (q1[:, sl] * cos + q2[:, sl] * sin).astype(BF16)
        ka_ref[:, sl] = (kn[:, sl] + kr).astype(BF16)
    ga_ref[...] = _silu(seg("a_gate")).astype(BF16)

    def put(ref, first_token, vt):
        tile = ref.shape[2]
        ref[first_token // tile, :, first_token % tile:first_token % tile + VT_TILE] = vt

    for c in range(x_ref.shape[0] // VT_TILE):
        rows = slice(c * VT_TILE, (c + 1) * VT_TILE)
        put(vat_ref, c * VT_TILE,
            lax.dot_general(wavt_ref[...], ckvn[rows], _NT, preferred_element_type=F32).astype(BF16))
        vt = lax.dot_general(wvt_ref[...], h[rows], _NT, preferred_element_type=F32).astype(BF16)
        put(vbt_ref, c * VT_TILE, vt[:GROUP_WIDTH])
        put(vct_ref, c * VT_TILE, vt[GROUP_WIDTH:])

    qb_ref[...] = seg("b_q").astype(BF16)
    kb_ref[...] = seg("b_k").astype(BF16)
    gb_ref[...] = _silu(seg("b_gate")).astype(BF16)

    qc_ref[...] = seg("c_q").astype(BF16)
    kc_ref[...] = seg("c_k").astype(BF16)
    qi_ref[...] = seg("c_qidx").astype(BF16)
    ki_ref[...] = kidx4.astype(BF16)
    wi_ref[...] = seg("c_widx")
    gc_ref[...] = _silu(seg("c_gate")).astype(BF16)

    yd_ref[...] = (seg("d_b") * _silu(seg("d_gate"))).astype(BF16)
    ud_ref[...] = seg("d_c") * seg("d_h")


def _layer_spec(a, l):
    return pl.BlockSpec((None,) + a.shape[1:], lambda i: (l,) + (0,) * (a.ndim - 1))


def _inproj(x2, l, g, w1, wvt, qag, wq1, wq2, kvg, wk, wavt, cos_t, sin_t, seq, tiles):
    n = x2.shape[0]
    tm = min(PROJ_ROWS, seq)
    tiles_per_seq = seq // tm
    row = lambda w: pl.BlockSpec((tm, w), lambda i: (i, 0))
    full = lambda a: _layer_spec(a, l)
    tab = pl.BlockSpec((tm, LANES), lambda i: (i % tiles_per_seq, 0))
    vt_spec = lambda t: pl.BlockSpec((tm // t, GROUP_WIDTH, t), lambda i: (i, 0, 0))
    vt_shape = lambda t: jax.ShapeDtypeStruct((n // t, GROUP_WIDTH, t), BF16)
    outs = (("qa", 512, BF16), ("ka", 512, BF16), ("vat", None, tiles[0]), ("ga", 256, BF16),
            ("qb", 256, BF16), ("kb", 256, BF16), ("vbt", None, tiles[1]), ("gb", 256, BF16),
            ("qc", 256, BF16), ("kc", 256, BF16), ("vct", None, tiles[2]), ("qi", 256, BF16),
            ("ki", 128, BF16), ("wi", 128, F32), ("gc", 256, BF16), ("yd", 256, BF16), ("ud", 256, F32))
    res = pl.pallas_call(
        _inproj_kernel,
        grid=(n // tm,),
        in_specs=[row(D_MODEL), full(g), full(w1), full(wvt), full(qag), full(wq1), full(wq2), full(kvg), full(wk),
                  full(wavt), tab, tab],
        out_specs=[vt_spec(dt) if w is None else row(w) for _, w, dt in outs],
        out_shape=[vt_shape(dt) if w is None else jax.ShapeDtypeStruct((n, w), dt) for _, w, dt in outs],
        compiler_params=_cparams(("parallel",)),
        name="inproj",
    )(x2, g, w1, wvt, qag, wq1, wq2, kvg, wk, wavt, cos_t, sin_t)
    return {name: o for (name, _, _), o in zip(outs, res)}


def _flash_scratch(n_maps, t):
    unit = _far_unit_tiles(t) * t
    return [pltpu.VMEM((n_maps, 1, t), F32), pltpu.VMEM((n_maps, 1, t), F32), pltpu.VMEM((n_maps, HEAD_DIM, t), F32),
            pltpu.VMEM((n_maps, unit, t), F32), pltpu.VMEM((n_maps, unit, t), F32)]


def _far_unit_tiles(t):
    return max(1, FAR_UNIT_KEYS // t)


def _flash_update(s, vts, old):
    t = vts[0].shape[1]
    m_tile = jnp.max(s, axis=0, keepdims=True)
    m_new = m_tile if old is None else jnp.maximum(old[0], m_tile)
    p = jnp.exp2(s - m_new)
    l_tile = jnp.sum(p, axis=0, keepdims=True)
    pb = p.astype(BF16)
    pv = jnp.dot(vts[0], pb[0:t], preferred_element_type=F32)
    for d in range(1, len(vts)):
        pv = pv + jnp.dot(vts[d], pb[d * t:(d + 1) * t], preferred_element_type=F32)
    if old is None:
        return m_new, l_tile, pv
    alpha = jnp.exp2(old[0] - m_new)
    return m_new, alpha * old[1] + l_tile, alpha * old[2] + pv


def _flash_sweep(i, t, n_maps, qs, keys, values, state, diag_add, near_add=None, mask=None):
    m_ref, l_ref, acc_ref, buf_a, buf_b = state
    n_far = i if near_add is None else jnp.maximum(i - 1, 0)
    ut = _far_unit_tiles(t)
    n_units = n_far // ut

    def load_state():
        return [(m_ref[j], l_ref[j], acc_ref[j]) for j in range(n_maps)]

    def store_state(cur):
        for j in range(n_maps):
            m_ref[j], l_ref[j], acc_ref[j] = cur[j]

    def tile_logits(j, kt, n_tiles, shared, extra=None):
        rows = pl.ds(pl.multiple_of(kt * t, t), n_tiles * t)
        s = lax.dot_general(keys(j, rows), qs[j], _NT, preferred_element_type=F32)
        add = shared
        if extra is not None:
            add = extra(j) if add is None else add + extra(j)
        return s if add is None else s + add

    def shared_mask(kt, n_tiles):
        if mask is None:
            return None
        tiles = [mask(kt + d) for d in range(n_tiles)]
        return tiles[0] if n_tiles == 1 else jnp.concatenate(tiles, axis=0)

    def run(units, first=False, prefetch=False):
        cur = [None] * n_maps if first else load_state()
        shared = [shared_mask(kt, n_tiles) for kt, n_tiles, _ in units]
        items = [(u, j) for u in range(len(units)) for j in range(n_maps)]
        if prefetch:
            items += [(None, j) for j in range(n_maps)]
            shared.append(shared_mask(0, ut))

        def logits(u, j):
            if u is None:
                return tile_logits(j, 0, ut, shared[-1])
            kt, n_tiles, extra = units[u]
            return tile_logits(j, kt, n_tiles, shared[u], extra)

        queue = [logits(*it) for it in items[:FLASH_LOOKAHEAD]]
        for n, (u, j) in enumerate(items):
            if n + FLASH_LOOKAHEAD < len(items):
                queue.append(logits(*items[n + FLASH_LOOKAHEAD]))
            s = queue.pop(0)
            if u is None:
                buf_a[j] = s
            else:
                kt, n_tiles, _ = units[u]
                cur[j] = _flash_update(s, [values(j, kt + d) for d in range(n_tiles)], cur[j])
        store_state(cur)

    def first_run(prefetch):
        if near_add is None:
            return lambda: run([(i, 1, diag_add)], first=True, prefetch=prefetch)
        return lambda: run([(i, 1, diag_add), (i - 1, 1, near_add)], first=True, prefetch=prefetch)

    if near_add is None:
        pl.when(n_units == 0)(first_run(False))
    else:
        pl.when(i == 0)(lambda: run([(i, 1, diag_add)], first=True))
        pl.when((i >= 1) & (n_units == 0))(first_run(False))
    pl.when(n_units >= 1)(first_run(True))

    def consume(buf, kt, j, old):
        return _flash_update(buf[j], [values(j, kt + d) for d in range(ut)], old)

    def overlapped(cur, buf_in, buf_out, u):
        kt_next = ut * (u + 1)
        shared = shared_mask(kt_next, ut)
        for j in range(n_maps):
            buf_out[j] = tile_logits(j, kt_next, ut, shared)
            cur[j] = consume(buf_in, ut * u, j, cur[j])

    def far_body(c, carry):
        cur = load_state()
        overlapped(cur, buf_a, buf_b, 2 * c)
        overlapped(cur, buf_b, buf_a, 2 * c + 1)
        store_state(cur)
        return carry

    lax.fori_loop(0, (n_units - 1) // 2, far_body, 0)

    @pl.when(n_units % 2 == 1)
    def _():
        cur = load_state()
        for j in range(n_maps):
            cur[j] = consume(buf_a, ut * (n_units - 1), j, cur[j])
        store_state(cur)

    @pl.when((n_units >= 2) & (n_units % 2 == 0))
    def _():
        cur = load_state()
        overlapped(cur, buf_a, buf_b, n_units - 2)
        for j in range(n_maps):
            cur[j] = consume(buf_b, ut * (n_units - 1), j, cur[j])
        store_state(cur)

    for r in range(1, ut):
        pl.when(n_far % ut >= r)(lambda r=r: run([(n_far - r, 1, None)]))
    return [acc_ref[j] / l_ref[j] for j in range(n_maps)]


def _lane_mask(lo, hi, dtype):
    lane = lax.broadcasted_iota(jnp.int32, (1, LANES), 1)
    return ((lane >= lo) & (lane < hi)).astype(dtype)


def _store_gated(o_ref, g_ref, heads_t):
    o = jnp.concatenate(heads_t, axis=0).T
    o_ref[...] = (o * g_ref[...].astype(F32)).astype(BF16)


def _att_specs(t, seq, widths_q, widths_k):
    nq = seq // t
    qspec = lambda w: pl.BlockSpec((t, w), lambda b, i: (b * nq + i, 0))
    kspec = lambda w: pl.BlockSpec((seq, w), lambda b, i: (b, 0), pipeline_mode=_ONE_BUFFER)
    vspec = pl.BlockSpec((nq, GROUP_WIDTH, t), lambda b, i: (b, 0, 0), pipeline_mode=_ONE_BUFFER)
    return [qspec(w) for w in widths_q], [kspec(w) for w in widths_k], vspec, qspec(GROUP_WIDTH)


def _mla_kernel(q_ref, k_ref, vt_ref, g_ref, madd_ref, o_ref, *state, t):
    i = pl.program_id(1)
    qs = [q_ref[:, hd * LANES:(hd + 1) * LANES] for hd in range(N_HEADS)]
    keys = lambda hd, rows: k_ref[rows, hd * LANES:(hd + 1) * LANES]
    values = lambda hd, kt: vt_ref[kt, hd * HEAD_DIM:(hd + 1) * HEAD_DIM, :]
    outs = _flash_sweep(i, t, N_HEADS, qs, keys, values, state, lambda hd: madd_ref[...])
    _store_gated(o_ref, g_ref, outs)


def _mla(p, mask_add, batch, seq, t):
    qs, ks, vspec, ospec = _att_specs(t, seq, (512, 256), (512,))
    return pl.pallas_call(
        functools.partial(_mla_kernel, t=t),
        grid=(batch, seq // t),
        in_specs=[qs[0], ks[0], vspec, qs[1], pl.BlockSpec((t, t), lambda b, i: (0, 0), pipeline_mode=_ONE_BUFFER)],
        out_specs=ospec,
        out_shape=jax.ShapeDtypeStruct((batch * seq, GROUP_WIDTH), BF16),
        scratch_shapes=_flash_scratch(N_HEADS, t),
        compiler_params=_cparams(("parallel", "arbitrary")),
        name="mla_attention",
    )(p["qa"], p["ka"], p["vat"], p["ga"], mask_add)


def _diff_kernel(q_ref, k_ref, vt_ref, g_ref, bias_ref, lam_ref, subg_ref, o_ref, *state, t, lambda_init):
    i = pl.program_id(1)
    lp = lam_ref[...]
    lam = (jnp.exp(jnp.sum(lp[0:1] * lp[1:2], axis=-1, keepdims=True))
           - jnp.exp(jnp.sum(lp[2:3] * lp[3:4], axis=-1, keepdims=True)) + lambda_init)
    qs = [q_ref[:, (j // 4) * LANES:(j // 4 + 1) * LANES] * _lane_mask((j % 4) * DIFF_QK, (j % 4 + 1) * DIFF_QK, BF16)
          for j in range(N_DIFF_MAPS)]
    keys = lambda j, rows: k_ref[rows, (j // 4) * LANES:(j // 4 + 1) * LANES]
    values = lambda j, kt: vt_ref[kt, (j // 2) * HEAD_DIM:(j // 2 + 1) * HEAD_DIM, :]
    maps = _flash_sweep(i, t, N_DIFF_MAPS, qs, keys, values, state, lambda j: bias_ref[j, 1], lambda j: bias_ref[j, 0])
    outs = []
    for hd in range(N_HEADS):
        o = maps[2 * hd] - lam * maps[2 * hd + 1]
        ms = jnp.mean(o * o, axis=0, keepdims=True)
        outs.append(o * lax.rsqrt(ms + NORM_EPS) * subg_ref[...] * (1.0 - lambda_init))
    _store_gated(o_ref, g_ref, outs)


def _diff(p, bias, lam_params, subln_g, lambda_init, batch, seq, t):
    qs, ks, vspec, ospec = _att_specs(t, seq, (256, 256), (256,))
    subg = jnp.broadcast_to(subln_g.astype(F32)[:, None], (HEAD_DIM, t))
    return pl.pallas_call(
        functools.partial(_diff_kernel, t=t, lambda_init=lambda_init),
        grid=(batch, seq // t),
        in_specs=[qs[0], ks[0], vspec, qs[1],
                  pl.BlockSpec((N_DIFF_MAPS, 2, t, t), lambda b, i: (0, 0, 0, 0), pipeline_mode=_ONE_BUFFER),
                  pl.BlockSpec((4, DIFF_QK), lambda b, i: (0, 0)),
                  pl.BlockSpec((HEAD_DIM, t), lambda b, i: (0, 0))],
        out_specs=ospec,
        out_shape=jax.ShapeDtypeStruct((batch * seq, GROUP_WIDTH), BF16),
        scratch_shapes=_flash_scratch(N_DIFF_MAPS, t),
        compiler_params=_cparams(("parallel", "arbitrary")),
        name="diff_attention",
    )(p["qb"], p["kb"], p["vbt"], p["gb"], bias, lam_params.astype(F32), subg)


def _sort_key(x):
    b = lax.bitcast_convert_type(x + 0.0, jnp.int32)
    return b ^ ((b >> 31) & 0x7FFFFFFF)


def _dsa_kernel(q_ref, k_ref, vt_ref, qi_ref, ki_ref, wi_ref, g_ref, bias_ref, tri_ref, o_ref,
                score_ref, hi_ref, lo_ref, thr_ref, cnt_ref, *state, t, topk):
    i = pl.program_id(1)
    n_kt = i + 1
    idx_scale = (IDX_HEADS ** -0.5) * (IDX_DIM ** -0.5)

    wit = wi_ref[...].T * idx_scale
    qm, wrow = [], []
    for ih in range(IDX_HEADS):
        grp = slice((ih // 4) * LANES, (ih // 4 + 1) * LANES)
        lo = (ih % 4) * IDX_DIM
        qm.append(qi_ref[:, grp] * _lane_mask(lo, lo + IDX_DIM, BF16))
        wrow.append(wit[ih:ih + 1, :])

    def idx_tile(kt):
        kk = ki_ref[pl.ds(pl.multiple_of(kt * t, t), t), :]
        dot = lambda ih: lax.dot_general(kk, qm[ih], _NT, preferred_element_type=F32)
        queue = [dot(ih) for ih in range(FLASH_LOOKAHEAD)]
        tot = None
        for ih in range(IDX_HEADS):
            if ih + FLASH_LOOKAHEAD < IDX_HEADS:
                queue.append(dot(ih + FLASH_LOOKAHEAD))
            term = wrow[ih] * jnp.maximum(queue.pop(0), 0.0)
            tot = term if tot is None else tot + term
        return tot + 0.0

    def store_scores(kt, score):
        score_ref[kt] = score
        key = _sort_key(score)
        hi_ref[kt] = (key >> HALF_BITS).astype(jnp.int16)
        lo_ref[kt] = ((key & HALF_MASK) - HALF_BIAS).astype(jnp.int16)

    def key_tiles(first, n):
        for d in range(n):
            store_scores(first + d, idx_tile(first + d))

    def key_quad_body(c, carry):
        key_tiles(4 * c, 4)
        return carry

    lax.fori_loop(0, i // 4, key_quad_body, 0)
    pl.when(i % 4 >= 2)(lambda: key_tiles(i - i % 4, 2))
    pl.when(i % 2 == 1)(lambda: key_tiles(i - 1, 1))
    kchunk = lax.broadcasted_iota(jnp.int32, (t, t), 0) // CHUNK
    qchunk = lax.broadcasted_iota(jnp.int32, (t, t), 1) // CHUNK
    store_scores(i, jnp.where(kchunk <= qchunk, idx_tile(i), -jnp.inf))

    def count(pred):
        def body(kt, c):
            hit = jnp.where(pred(score_ref[kt]), 1.0, 0.0)
            return c + jnp.sum(hit.reshape(t // SUBLANES, SUBLANES, t), axis=0)
        part = lax.fori_loop(0, n_kt, body, jnp.zeros((SUBLANES, t), F32))
        return jnp.sum(part, axis=0, keepdims=True)

    rows16 = 2 * SUBLANES
    n_acc = 4

    def count16(ref, pred):
        one, zero = jnp.int16(1), jnp.int16(0)

        def body(kt, accs):
            x = ref[kt].reshape(t // rows16, rows16, t)
            accs = list(accs)
            for r in range(t // rows16):
                accs[r % n_acc] = accs[r % n_acc] + jnp.where(pred(x[r]), one, zero)
            return tuple(accs)
        accs = lax.fori_loop(0, n_kt, body, tuple(jnp.zeros((rows16, t), jnp.int16) for _ in range(n_acc)))
        tot = (accs[0] + accs[1]) + (accs[2] + accs[3])
        return jnp.sum(tot.astype(jnp.int32), axis=0, keepdims=True)

    def select16(ref, need):
        def bit_body(b, res):
            cand = res | (jnp.int32(1) << (HALF_BITS - 1 - b))
            cand16 = (cand - HALF_BIAS).astype(jnp.int16)
            cnt = count16(ref, lambda x: x >= cand16)
            return jnp.where(cnt >= need, cand, res)
        return lax.fori_loop(0, HALF_BITS, bit_body, jnp.zeros((1, t), jnp.int32)) - HALF_BIAS

    hi_thr = select16(hi_ref, topk)
    hi_thr16 = hi_thr.astype(jnp.int16)
    need_lo = topk - count16(hi_ref, lambda x: x > hi_thr16)

    def member_body(kt, c):
        lo_ref[kt] = jnp.where(hi_ref[kt] == hi_thr16, lo_ref[kt], jnp.int16(-HALF_BIAS))
        return c

    lax.fori_loop(0, n_kt, member_body, 0)
    lo_thr = select16(lo_ref, need_lo)

    def key_to_score(key):
        key = jnp.maximum(key, KEY_LOWEST)
        return lax.bitcast_convert_type(key ^ ((key >> 31) & 0x7FFFFFFF), F32)

    thr_ref[...] = key_to_score((hi_thr << HALF_BITS) | (lo_thr + HALF_BIAS))
    cnt_ref[...] = count(lambda x: x >= thr_ref[...])
    n_valid = ((i * t + lax.broadcasted_iota(jnp.int32, (1, t), 1)) // CHUNK + 1) * CHUNK
    few = n_valid <= topk

    @pl.when(jnp.max(jnp.where(few, 0.0, jnp.abs(cnt_ref[...] - topk))) > 0.0)
    def _():
        cnt_gt = count(lambda x: x > thr_ref[...])
        wrong = (cnt_ref[...] < topk) | (cnt_gt >= topk)

        @pl.when(jnp.max(jnp.where(few | jnp.logical_not(wrong), 0.0, 1.0)) > 0.0)
        def _():
            def bit_body(b, res):
                cand = res | (jnp.int32(1) << (31 - b))
                cand_score = key_to_score(cand ^ INT_MIN)
                cnt = count(lambda x: x >= cand_score)
                return jnp.where(cnt >= topk, cand, res)
            res = lax.fori_loop(0, 32, bit_body, jnp.zeros((1, t), jnp.int32))
            thr_ref[...] = key_to_score(res ^ INT_MIN)
            cnt_ref[...] = count(lambda x: x >= thr_ref[...])

    thr = thr_ref[...]
    has_ties = jnp.max(jnp.where(few, 0.0, cnt_ref[...])) > topk

    @pl.when(jnp.logical_not(has_ties))
    def _():
        def body(kt, c):
            score_ref[kt] = jnp.where(score_ref[kt] >= thr, 0.0, NEG)
            return c
        lax.fori_loop(0, n_kt, body, 0)

    @pl.when(has_ties)
    def _():
        need = topk - count(lambda x: x > thr)

        def body(kt, seen):
            x = score_ref[kt]
            eq = jnp.where(x == thr, 1.0, 0.0)
            rank = seen + jnp.dot(tri_ref[...], eq.astype(BF16), preferred_element_type=F32)
            sel = (x > thr) | ((x == thr) & (rank <= need))
            score_ref[kt] = jnp.where(sel, 0.0, NEG)
            return seen + jnp.sum(eq, axis=0, keepdims=True)
        lax.fori_loop(0, n_kt, body, jnp.zeros((1, t), F32))

    qs = [q_ref[:, (hd // 2) * LANES:(hd // 2 + 1) * LANES]
          * _lane_mask((hd % 2) * HEAD_DIM, (hd % 2 + 1) * HEAD_DIM, BF16) for hd in range(N_HEADS)]
    keys = lambda hd, rows: k_ref[rows, (hd // 2) * LANES:(hd // 2 + 1) * LANES]
    values = lambda hd, kt: vt_ref[kt, hd * HEAD_DIM:(hd + 1) * HEAD_DIM, :]
    outs = _flash_sweep(i, t, N_HEADS, qs, keys, values, state, lambda hd: bias_ref[hd, 1], lambda hd: bias_ref[hd, 0],
                        mask=lambda kt: score_ref[kt])
    _store_gated(o_ref, g_ref, outs)


def _dsa(p, bias, batch, seq, t):
    topk = min(TOPK_MAX, seq // 4)
    nq = seq // t
    qs, ks, vspec, ospec = _att_specs(t, seq, (256, 256, 128, 256), (256, 128))
    tri = jnp.asarray(np.tril(np.ones((t, t), np.float32)), BF16)
    return pl.pallas_call(
        functools.partial(_dsa_kernel, t=t, topk=topk),
        grid=(batch, nq),
        in_specs=[qs[0], ks[0], vspec, qs[1], ks[1], qs[2], qs[3],
                  pl.BlockSpec((N_HEADS, 2, t, t), lambda b, i: (0, 0, 0, 0), pipeline_mode=_ONE_BUFFER),
                  pl.BlockSpec((t, t), lambda b, i: (0, 0), pipeline_mode=_ONE_BUFFER)],
        out_specs=ospec,
        out_shape=jax.ShapeDtypeStruct((batch * seq, GROUP_WIDTH), BF16),
        scratch_shapes=[pltpu.VMEM((nq, t, t), F32), pltpu.VMEM((nq, t, t), jnp.int16),
                        pltpu.VMEM((nq, t, t), jnp.int16), pltpu.VMEM((1, t), F32), pltpu.VMEM((1, t), F32)]
        + _flash_scratch(N_HEADS, t),
        compiler_params=_cparams(("parallel", "arbitrary")),
        name="dsa_attention",
    )(p["qc"], p["kc"], p["vct"], p["qi"], p["ki"], p["wi"], p["gc"], bias, tri)


def _outproj_kernel(x_ref, ya_ref, yb_ref, yc_ref, yd_ref, ud_ref, halo_ref, cw_ref, w_ref, fg_ref, o_ref,
                    *, tiles_per_seq, final):
    i = pl.program_id(0)
    tm = x_ref.shape[0]
    u = ud_ref[...]
    halo = jnp.where(i % tiles_per_seq == 0, 0.0, halo_ref[...])
    row8 = lax.broadcasted_iota(jnp.int32, (SUBLANES, 1), 0)

    def shifted(k):
        r = pltpu.roll(u, k, 0)
        head = jnp.where(row8 < k, pltpu.roll(halo, k, 0), r[0:SUBLANES])
        return jnp.concatenate([head, r[SUBLANES:tm]], axis=0)

    cw = cw_ref[...]
    conv = cw[0:1] * shifted(CONV_WIDTH - 1)
    for j in range(1, CONV_WIDTH):
        conv = conv + cw[j:j + 1] * (shifted(CONV_WIDTH - 1 - j) if j < CONV_WIDTH - 1 else u)
    yd = (conv * yd_ref[...].astype(F32)).astype(BF16)
    acc = x_ref[...]
    for g, y in enumerate((ya_ref[...], yb_ref[...], yc_ref[...], yd)):
        acc = acc + jnp.dot(y, w_ref[g * GROUP_WIDTH:(g + 1) * GROUP_WIDTH, :], preferred_element_type=F32)
    if final:
        acc = _rms(acc, fg_ref[...])
    o_ref[...] = acc


def _outproj(x2, l, ya, yb, yc, yd, ud, conv_w, w_out, final_g, seq, final):
    n = x2.shape[0]
    tm = min(PROJ_ROWS, seq)
    tiles_per_seq = seq // tm
    row = lambda w: pl.BlockSpec((tm, w), lambda i: (i, 0))
    halo = pl.BlockSpec((SUBLANES, GROUP_WIDTH), lambda i: (jnp.maximum(i * (tm // SUBLANES) - 1, 0), 0))
    return pl.pallas_call(
        functools.partial(_outproj_kernel, tiles_per_seq=tiles_per_seq, final=final),
        grid=(n // tm,),
        in_specs=[row(D_MODEL), row(256), row(256), row(256), row(256), row(256), halo,
                  _layer_spec(conv_w, l), _layer_spec(w_out, l), pl.BlockSpec(final_g.shape, lambda i: (0, 0))],
        out_specs=row(D_MODEL),
        out_shape=jax.ShapeDtypeStruct((n, D_MODEL), F32),
        compiler_params=_cparams(("parallel",)),
        name="outproj",
    )(x2, ya, yb, yc, yd, ud, ud, conv_w, w_out, final_g)


def kernel(x, norm_g, w_in, mla_qa_g, mla_w_uq, mla_kva_g, mla_w_ukv, diff_lambda, diff_subln_g, conv_w, w_out,
           rel_bias, final_g):
    batch, seq, _ = x.shape
    depth = w_in.shape[0]
    tm = min(PROJ_ROWS, seq)
    tiles = tuple(min(t, seq) for t in (MLA_TILE, DIFF_TILE, DSA_TILE))
    assert seq % tm == 0 and tm % VT_TILE == 0 and conv_w.shape[1] == CONV_WIDTH <= SUBLANES
    assert all(seq % t == 0 and t % CHUNK == 0 and tm % t == 0 and t % VT_TILE == 0 for t in tiles)

    cos_t, sin_t = (jnp.asarray(a) for a in _rope_tables(seq))
    mask_add = jnp.asarray(_position_constants(seq, tiles[0])[2])
    bias_diff = _bias_tiles(rel_bias[:, :N_DIFF_MAPS], seq, tiles[1])
    bias_dsa = _bias_tiles(rel_bias[:, N_DIFF_MAPS:], seq, tiles[2])

    w1, wvt = _prep_w_in(w_in)
    wq1, wq2, wk, wavt = _prep_mla_up(mla_w_uq, mla_w_ukv)
    row3 = lambda a: a[:, None, :]
    w_out_b = w_out.astype(BF16)

    x2 = x.reshape(batch * seq, D_MODEL)
    for l in range(depth):
        lambda_init = 0.8 - 0.6 * math.exp(-0.3 * l)
        p = _inproj(x2, l, row3(norm_g), w1, wvt, row3(mla_qa_g), wq1, wq2, row3(mla_kva_g), wk, wavt,
                    cos_t, sin_t, seq, tiles)
        ya = _mla(p, mask_add, batch, seq, tiles[0])
        yb = _diff(p, bias_diff, diff_lambda[l], diff_subln_g[l], lambda_init, batch, seq, tiles[1])
        yc = _dsa(p, bias_dsa, batch, seq, tiles[2])
        x2 = _outproj(x2, l, ya, yb, yc, p["yd"], p["ud"], conv_w, w_out_b, final_g[None, :],
                      seq, final=(l == depth - 1))
    return x2.reshape(batch, seq, D_MODEL)
```

```python
import functools
import math

import numpy as np
import jax
import jax.numpy as jnp
from jax import lax
from jax.experimental import pallas as pl
from jax.experimental.pallas import tpu as pltpu

F32 = jnp.float32
BF16 = jnp.bfloat16

D_MODEL = 1024
CHUNK = 64
N_HEADS = 4
HEAD_DIM = 64
GROUP_WIDTH = N_HEADS * HEAD_DIM
MLA_Q_LORA = 256
MLA_KV_LORA = 128
MLA_NOPE = 64
MLA_ROPE = 32
ROPE_BASE = 10000.0
DIFF_QK = 32
IDX_HEADS = 8
IDX_DIM = 32
TOPK_MAX = 256
CONV_WIDTH = 3
NUM_BUCKETS = 32
MAX_DISTANCE = 128
N_DIFF_MAPS = 2 * N_HEADS
NORM_EPS = 1e-6
NEG = -1e30
LOG2E = math.log2(math.e)
INT_MIN = -2 ** 31
KEY_LOWEST = int(np.float32(-np.finfo(np.float32).max).view(np.int32)) ^ 0x7FFFFFFF
HALF_BITS = 16
HALF_MASK = 2 ** HALF_BITS - 1
HALF_BIAS = 2 ** (HALF_BITS - 1)

LANES = 128
SUBLANES = 8
MLA_TILE = 256
DIFF_TILE = 256
DSA_TILE = 512
VT_TILE = 256
FAR_UNIT_KEYS = 512
PROJ_ROWS = 512
FLASH_LOOKAHEAD = 4
V7X_VMEM_BYTES = 64 * 1024 * 1024
VMEM_LIMIT = V7X_VMEM_BYTES * 7 // 8

_IN_SPLITS = (
    ("a_cq", 256), ("a_ckv", 128), ("a_krope", 32), ("a_gate", 256),
    ("b_q", 256), ("b_k", 256), ("b_v", 256), ("b_gate", 256),
    ("c_q", 256), ("c_k", 256), ("c_v", 256),
    ("c_qidx", 256), ("c_kidx", 32), ("c_widx", 8), ("c_gate", 256),
    ("d_b", 256), ("d_c", 256), ("d_h", 256), ("d_gate", 256),
)

_SEGS = (
    ("cq", 256), ("ckv", 128), ("kr1", 128), ("kr2", 128), ("c_kidx4", 128), ("a_gate", 256),
    ("b_q", 256), ("b_k", 256), ("b_gate", 256),
    ("c_q", 256), ("c_k", 256), ("c_qidx", 256),
    ("c_gate", 256), ("d_b", 256), ("d_c", 256), ("d_h", 256), ("d_gate", 256), ("c_widx", 128),
)
_SEG_OFF = {}
_off = 0
for _name, _w in _SEGS:
    _SEG_OFF[_name] = (_off, _w)
    _off += _w
W1_COLS = _off

_NT = (((1,), (1,)), ((), ()))
_ONE_BUFFER = pl.Buffered(1)


def _cparams(sem):
    return pltpu.CompilerParams(dimension_semantics=sem, vmem_limit_bytes=VMEM_LIMIT)


def _split_cols(w):
    out, off = {}, 0
    for name, width in _IN_SPLITS:
        out[name] = w[..., off:off + width]
        off += width
    return out


def _rot_half_cols(w):
    half = MLA_ROPE // 2
    return jnp.concatenate([-w[..., half:], w[..., :half]], axis=-1)


def _prep_w_in(w):
    p = _split_cols(w)
    z = lambda n: jnp.zeros(w.shape[:-1] + (n,), w.dtype)
    cat = lambda parts: jnp.concatenate(parts, axis=-1)
    segs = {
        "cq": p["a_cq"], "ckv": p["a_ckv"], "a_gate": p["a_gate"],
        "kr1": cat([z(MLA_NOPE), p["a_krope"], z(LANES - MLA_NOPE - MLA_ROPE)]),
        "kr2": cat([z(MLA_NOPE), _rot_half_cols(p["a_krope"]), z(LANES - MLA_NOPE - MLA_ROPE)]),
        "b_q": p["b_q"] * (DIFF_QK ** -0.5 * LOG2E), "b_k": p["b_k"], "b_gate": p["b_gate"],
        "c_q": p["c_q"] * (HEAD_DIM ** -0.5 * LOG2E), "c_k": p["c_k"], "c_qidx": p["c_qidx"],
        "c_kidx4": cat([p["c_kidx"]] * (LANES // IDX_DIM)),
        "c_widx": cat([p["c_widx"], z(LANES - IDX_HEADS)]),
        "c_gate": p["c_gate"], "d_b": p["d_b"], "d_c": p["d_c"], "d_h": p["d_h"], "d_gate": p["d_gate"],
    }
    w1 = cat([segs[name].astype(BF16) for name, _ in _SEGS])
    wvt = jnp.swapaxes(cat([p["b_v"].astype(BF16), p["c_v"].astype(BF16)]), -1, -2)
    return w1, wvt


def _prep_mla_up(w_uq, w_ukv):
    depth = w_uq.shape[0]
    scale = (MLA_NOPE + MLA_ROPE) ** -0.5 * LOG2E
    wq = w_uq.reshape(depth, MLA_Q_LORA, N_HEADS, MLA_NOPE + MLA_ROPE) * scale
    zq = jnp.zeros((depth, MLA_Q_LORA, N_HEADS, LANES - MLA_NOPE - MLA_ROPE), w_uq.dtype)
    wq1 = jnp.concatenate([wq, zq], axis=-1).reshape(depth, MLA_Q_LORA, N_HEADS * LANES)
    wq2 = jnp.concatenate([jnp.zeros_like(wq[..., :MLA_NOPE]), _rot_half_cols(wq[..., MLA_NOPE:]), zq],
                          axis=-1).reshape(depth, MLA_Q_LORA, N_HEADS * LANES)
    wkv = w_ukv.reshape(depth, MLA_KV_LORA, N_HEADS, MLA_NOPE + HEAD_DIM)
    zk = jnp.zeros((depth, MLA_KV_LORA, N_HEADS, LANES - MLA_NOPE), w_ukv.dtype)
    wk = jnp.concatenate([wkv[..., :MLA_NOPE], zk], axis=-1).reshape(depth, MLA_KV_LORA, N_HEADS * LANES)
    wvt = jnp.swapaxes(wkv[..., MLA_NOPE:].reshape(depth, MLA_KV_LORA, N_HEADS * HEAD_DIM), -1, -2)
    return wq1.astype(BF16), wq2.astype(BF16), wk.astype(BF16), wvt.astype(BF16)


def _rope_tables(seq):
    half = MLA_ROPE // 2
    inv_freq = ROPE_BASE ** (-np.arange(half, dtype=np.float32) / half)
    ang = np.arange(seq, dtype=np.float32)[:, None] * inv_freq[None, :].astype(np.float32)
    cos, sin = np.cos(ang).astype(np.float32), np.sin(ang).astype(np.float32)
    ct = np.zeros((seq, LANES), np.float32)
    st = np.zeros((seq, LANES), np.float32)
    ct[:, :MLA_NOPE] = 1.0
    ct[:, MLA_NOPE:MLA_NOPE + half] = cos
    ct[:, MLA_NOPE + half:MLA_NOPE + MLA_ROPE] = cos
    st[:, MLA_NOPE:MLA_NOPE + half] = sin
    st[:, MLA_NOPE + half:MLA_NOPE + MLA_ROPE] = sin
    return ct, st


def _rel_bucket_np(rel):
    nb = NUM_BUCKETS // 2
    max_exact = nb // 2
    ret = np.where(rel > 0, nb, 0)
    n = np.abs(rel)
    nf = np.maximum(n, max_exact).astype(np.float32)
    large = max_exact + (np.log(nf / np.float32(max_exact)) / np.float32(math.log(MAX_DISTANCE / max_exact))
                         * np.float32(nb - max_exact)).astype(np.int32)
    large = np.minimum(large, nb - 1)
    return (ret + np.where(n < max_exact, n, large)).astype(np.int32)


def _position_constants(seq, tile):
    r = np.arange(tile)
    rel0 = r[:, None] - r[None, :]
    bidx = np.stack([_rel_bucket_np(rel0 - tile), _rel_bucket_np(rel0)]).astype(np.int32)
    far = _rel_bucket_np(np.arange(-(seq - 1), -tile))
    far_bucket = int(far[0]) if far.size else int(_rel_bucket_np(np.array([-tile - 1]))[0])
    assert far.size == 0 or np.all(far == far_bucket)
    mask_add = np.where((r[:, None] // CHUNK) <= (r[None, :] // CHUNK), 0.0, NEG).astype(np.float32)
    return bidx, far_bucket, mask_add


def _bias_kernel(tab_ref, bidx_ref, madd_ref, o_ref, *, far_bucket, buckets):
    j = pl.program_id(0)
    c = tab_ref[j, far_bucket]
    for d in range(2):
        idx = bidx_ref[d]
        acc = jnp.zeros(idx.shape, F32)
        for b in buckets[d]:
            acc = jnp.where(idx == b, (tab_ref[j, b] - c) * LOG2E, acc)
        if d == 1:
            acc = acc + madd_ref[...]
        o_ref[0, d] = acc


def _bias_tiles(table, seq, tile):
    bidx, far_bucket, mask_add = _position_constants(seq, tile)
    n_maps = table.shape[1]
    buckets = tuple(tuple(int(b) for b in np.unique(bidx[d])) for d in range(2))
    return pl.pallas_call(
        functools.partial(_bias_kernel, far_bucket=far_bucket, buckets=buckets),
        grid=(n_maps,),
        in_specs=[
            pl.BlockSpec(memory_space=pltpu.SMEM),
            pl.BlockSpec((2, tile, tile), lambda j: (0, 0, 0)),
            pl.BlockSpec((tile, tile), lambda j: (0, 0)),
        ],
        out_specs=pl.BlockSpec((1, 2, tile, tile), lambda j: (j, 0, 0, 0)),
        out_shape=jax.ShapeDtypeStruct((n_maps, 2, tile, tile), F32),
        compiler_params=_cparams(("arbitrary",)),
        name="bias_tiles",
    )(table.T.astype(F32), jnp.asarray(bidx), jnp.asarray(mask_add))


def _silu(x):
    return x * (1.0 / (1.0 + jnp.exp(-x)))


def _rms(x, g):
    return x * lax.rsqrt(jnp.mean(x * x, axis=-1, keepdims=True) + NORM_EPS) * g


def _inproj_kernel(x_ref, g_ref, w_ref, wvt_ref, qag_ref, wq1_ref, wq2_ref, kvg_ref, wk_ref, wavt_ref, cos_ref, sin_ref,
                   qa_ref, ka_ref, vat_ref, ga_ref, qb_ref, kb_ref, vbt_ref, gb_ref,
                   qc_ref, kc_ref, vct_ref, qi_ref, ki_ref, wi_ref, gc_ref, yd_ref, ud_ref):
    h = _rms(x_ref[...], g_ref[...]).astype(BF16)

    def seg(name):
        off, width = _SEG_OFF[name]
        return jnp.dot(h, w_ref[:, off:off + width], preferred_element_type=F32)

    def seg_pair(first, second):
        off, width = _SEG_OFF[first]
        assert _SEG_OFF[second] == (off + width, width) and width == LANES
        y = jnp.dot(h, w_ref[:, off:off + 2 * width], preferred_element_type=F32)
        return y[:, :width], y[:, width:]

    cos = cos_ref[...]
    sin = sin_ref[...]

    cqn = _rms(seg("cq"), qag_ref[...]).astype(BF16)
    q1 = jnp.dot(cqn, wq1_ref[...], preferred_element_type=F32)
    q2 = jnp.dot(cqn, wq2_ref[...], preferred_element_type=F32)
    ckv, kr1 = seg_pair("ckv", "kr1")
    kr2, kidx4 = seg_pair("kr2", "c_kidx4")
    kr = kr1 * cos + kr2 * sin
    ckvn = _rms(ckv, kvg_ref[...]).astype(BF16)
    kn = jnp.dot(ckvn, wk_ref[...], preferred_element_type=F32)
    for hd in range(N_HEADS):
        sl = slice(hd * LANES, (hd + 1) * LANES)
        qa_ref[:, sl] = (q1[:, sl] * cos + q2[:, sl] * sin).astype(BF16)
        ka_ref[:, sl] = (kn[:, sl] + kr).astype(BF16)
    ga_ref[...] = _silu(seg("a_gate")).astype(BF16)

    def put(ref, first_token, vt):
        tile = ref.shape[2]
        ref[first_token // tile, :, first_token % tile:first_token % tile + VT_TILE] = vt

    for c in range(x_ref.shape[0] // VT_TILE):
        rows = slice(c * VT_TILE, (c + 1) * VT_TILE)
        put(vat_ref, c * VT_TILE,
            lax.dot_general(wavt_ref[...], ckvn[rows], _NT, preferred_element_type=F32).astype(BF16))
        vt = lax.dot_general(wvt_ref[...], h[rows], _NT, preferred_element_type=F32).astype(BF16)
        put(vbt_ref, c * VT_TILE, vt[:GROUP_WIDTH])
        put(vct_ref, c * VT_TILE, vt[GROUP_WIDTH:])

    qb_ref[...] = seg("b_q").astype(BF16)
    kb_ref[...] = seg("b_k").astype(BF16)
    gb_ref[...] = _silu(seg("b_gate")).astype(BF16)

    qc_ref[...] = seg("c_q").astype(BF16)
    kc_ref[...] = seg("c_k").astype(BF16)
    qi_ref[...] = seg("c_qidx").astype(BF16)
    ki_ref[...] = kidx4.astype(BF16)
    wi_ref[...] = seg("c_widx")
    gc_ref[...] = _silu(seg("c_gate")).astype(BF16)

    yd_ref[...] = (seg("d_b") * _silu(seg("d_gate"))).astype(BF16)
    ud_ref[...] = seg("d_c") * seg("d_h")


def _layer_spec(a, l):
    return pl.BlockSpec((None,) + a.shape[1:], lambda i: (l,) + (0,) * (a.ndim - 1))


def _inproj(x2, l, g, w1, wvt, qag, wq1, wq2, kvg, wk, wavt, cos_t, sin_t, seq, tiles):
    n = x2.shape[0]
    tm = min(PROJ_ROWS, seq)
    tiles_per_seq = seq // tm
    row = lambda w: pl.BlockSpec((tm, w), lambda i: (i, 0))
    full = lambda a: _layer_spec(a, l)
    tab = pl.BlockSpec((tm, LANES), lambda i: (i % tiles_per_seq, 0))
    vt_spec = lambda t: pl.BlockSpec((tm // t, GROUP_WIDTH, t), lambda i: (i, 0, 0))
    vt_shape = lambda t: jax.ShapeDtypeStruct((n // t, GROUP_WIDTH, t), BF16)
    outs = (("qa", 512, BF16), ("ka", 512, BF16), ("vat", None, tiles[0]), ("ga", 256, BF16),
            ("qb", 256, BF16), ("kb", 256, BF16), ("vbt", None, tiles[1]), ("gb", 256, BF16),
            ("qc", 256, BF16), ("kc", 256, BF16), ("vct", None, tiles[2]), ("qi", 256, BF16),
            ("ki", 128, BF16), ("wi", 128, F32), ("gc", 256, BF16), ("yd", 256, BF16), ("ud", 256, F32))
    res = pl.pallas_call(
        _inproj_kernel,
        grid=(n // tm,),
        in_specs=[row(D_MODEL), full(g), full(w1), full(wvt), full(qag), full(wq1), full(wq2), full(kvg), full(wk),
                  full(wavt), tab, tab],
        out_specs=[vt_spec(dt) if w is None else row(w) for _, w, dt in outs],
        out_shape=[vt_shape(dt) if w is None else jax.ShapeDtypeStruct((n, w), dt) for _, w, dt in outs],
        compiler_params=_cparams(("parallel",)),
        name="inproj",
    )(x2, g, w1, wvt, qag, wq1, wq2, kvg, wk, wavt, cos_t, sin_t)
    return {name: o for (name, _, _), o in zip(outs, res)}


def _flash_scratch(n_maps, t):
    unit = _far_unit_tiles(t) * t
    return [pltpu.VMEM((n_maps, 1, t), F32), pltpu.VMEM((n_maps, 1, t), F32), pltpu.VMEM((n_maps, HEAD_DIM, t), F32),
            pltpu.VMEM((n_maps, unit, t), F32), pltpu.VMEM((n_maps, unit, t), F32)]


def _far_unit_tiles(t):
    return max(1, FAR_UNIT_KEYS // t)


def _flash_update(s, vts, old):
    t = vts[0].shape[1]
    m_tile = jnp.max(s, axis=0, keepdims=True)
    m_new = m_tile if old is None else jnp.maximum(old[0], m_tile)
    p = jnp.exp2(s - m_new)
    l_tile = jnp.sum(p, axis=0, keepdims=True)
    pb = p.astype(BF16)
    pv = jnp.dot(vts[0], pb[0:t], preferred_element_type=F32)
    for d in range(1, len(vts)):
        pv = pv + jnp.dot(vts[d], pb[d * t:(d + 1) * t], preferred_element_type=F32)
    if old is None:
        return m_new, l_tile, pv
    alpha = jnp.exp2(old[0] - m_new)
    return m_new, alpha * old[1] + l_tile, alpha * old[2] + pv


def _flash_sweep(i, t, n_maps, qs, keys, values, state, diag_add, near_add=None, mask=None):
    m_ref, l_ref, acc_ref, buf_a, buf_b = state
    n_far = jnp.maximum(i - 1, 0)
    ut = _far_unit_tiles(t)
    n_units = n_far // ut

    def load_state():
        return [(m_ref[j], l_ref[j], acc_ref[j]) for j in range(n_maps)]

    def store_state(cur):
        for j in range(n_maps):
            m_ref[j], l_ref[j], acc_ref[j] = cur[j]

    def tile_logits(j, kt, n_tiles, shared, extra=None):
        rows = pl.ds(pl.multiple_of(kt * t, t), n_tiles * t)
        s = lax.dot_general(keys(j, rows), qs[j], _NT, preferred_element_type=F32)
        add = shared
        if extra is not None:
            add = extra(j) if add is None else add + extra(j)
        return s if add is None else s + add

    def shared_mask(kt, n_tiles):
        if mask is None:
            return None
        tiles = [mask(kt + d) for d in range(n_tiles)]
        return tiles[0] if n_tiles == 1 else jnp.concatenate(tiles, axis=0)

    def run(units, first=False, prefetch=False):
        cur = [None] * n_maps if first else load_state()
        shared = [shared_mask(kt, n_tiles) for kt, n_tiles, _ in units]
        items = [(u, j) for u in range(len(units)) for j in range(n_maps)]
        if prefetch:
            items += [(None, j) for j in range(n_maps)]
            shared.append(shared_mask(0, ut))

        def logits(u, j):
            if u is None:
                return tile_logits(j, 0, ut, shared[-1])
            kt, n_tiles, extra = units[u]
            return tile_logits(j, kt, n_tiles, shared[u], extra)

        queue = [logits(*it) for it in items[:FLASH_LOOKAHEAD]]
        for n, (u, j) in enumerate(items):
            if n + FLASH_LOOKAHEAD < len(items):
                queue.append(logits(*items[n + FLASH_LOOKAHEAD]))
            s = queue.pop(0)
            if u is None:
                buf_a[j] = s
            else:
                kt, n_tiles, _ = units[u]
                cur[j] = _flash_update(s, [values(j, kt + d) for d in range(n_tiles)], cur[j])
        store_state(cur)

    def pair_add(j):
        near = jnp.zeros((t, qs[j].shape[0]), F32) if near_add is None else near_add(j)
        return jnp.concatenate([near, diag_add(j)], axis=0)

    def first_run(prefetch):
        if ut >= 2:
            return lambda: run([(i - 1, 2, pair_add)], first=True, prefetch=prefetch)
        return lambda: run([(i, 1, diag_add), (i - 1, 1, near_add)], first=True, prefetch=prefetch)

    pl.when(i == 0)(lambda: run([(i, 1, diag_add)], first=True))
    pl.when((i >= 1) & (n_units == 0))(first_run(False))
    pl.when(n_units >= 1)(first_run(True))

    def consume(buf, kt, j, old):
        return _flash_update(buf[j], [values(j, kt + d) for d in range(ut)], old)

    def overlapped(cur, buf_in, buf_out, u):
        kt_next = ut * (u + 1)
        shared = shared_mask(kt_next, ut)
        for j in range(n_maps):
            buf_out[j] = tile_logits(j, kt_next, ut, shared)
            cur[j] = consume(buf_in, ut * u, j, cur[j])

    def far_body(c, carry):
        cur = load_state()
        overlapped(cur, buf_a, buf_b, 2 * c)
        overlapped(cur, buf_b, buf_a, 2 * c + 1)
        store_state(cur)
        return carry

    lax.fori_loop(0, (n_units - 1) // 2, far_body, 0)

    @pl.when(n_units % 2 == 1)
    def _():
        cur = load_state()
        for j in range(n_maps):
            cur[j] = consume(buf_a, ut * (n_units - 1), j, cur[j])
        store_state(cur)

    @pl.when((n_units >= 2) & (n_units % 2 == 0))
    def _():
        cur = load_state()
        overlapped(cur, buf_a, buf_b, n_units - 2)
        for j in range(n_maps):
            cur[j] = consume(buf_b, ut * (n_units - 1), j, cur[j])
        store_state(cur)

    for r in range(1, ut):
        pl.when(n_far % ut >= r)(lambda r=r: run([(n_far - r, 1, None)]))
    return [acc_ref[j] / l_ref[j] for j in range(n_maps)]


def _lane_mask(lo, hi, dtype):
    lane = lax.broadcasted_iota(jnp.int32, (1, LANES), 1)
    return ((lane >= lo) & (lane < hi)).astype(dtype)


def _store_gated(o_ref, g_ref, heads_t):
    o = jnp.concatenate(heads_t, axis=0).T
    o_ref[...] = (o * g_ref[...].astype(F32)).astype(BF16)


def _att_specs(t, seq, widths_q, widths_k):
    nq = seq // t
    qspec = lambda w: pl.BlockSpec((t, w), lambda b, i: (b * nq + i, 0))
    kspec = lambda w: pl.BlockSpec((seq, w), lambda b, i: (b, 0), pipeline_mode=_ONE_BUFFER)
    vspec = pl.BlockSpec((nq, GROUP_WIDTH, t), lambda b, i: (b, 0, 0), pipeline_mode=_ONE_BUFFER)
    return [qspec(w) for w in widths_q], [kspec(w) for w in widths_k], vspec, qspec(GROUP_WIDTH)


def _mla_kernel(q_ref, k_ref, vt_ref, g_ref, madd_ref, o_ref, *state, t):
    i = pl.program_id(1)
    qs = [q_ref[:, hd * LANES:(hd + 1) * LANES] for hd in range(N_HEADS)]
    keys = lambda hd, rows: k_ref[rows, hd * LANES:(hd + 1) * LANES]
    values = lambda hd, kt: vt_ref[kt, hd * HEAD_DIM:(hd + 1) * HEAD_DIM, :]
    outs = _flash_sweep(i, t, N_HEADS, qs, keys, values, state, lambda hd: madd_ref[...])
    _store_gated(o_ref, g_ref, outs)


def _mla(p, mask_add, batch, seq, t):
    qs, ks, vspec, ospec = _att_specs(t, seq, (512, 256), (512,))
    return pl.pallas_call(
        functools.partial(_mla_kernel, t=t),
        grid=(batch, seq // t),
        in_specs=[qs[0], ks[0], vspec, qs[1], pl.BlockSpec((t, t), lambda b, i: (0, 0), pipeline_mode=_ONE_BUFFER)],
        out_specs=ospec,
        out_shape=jax.ShapeDtypeStruct((batch * seq, GROUP_WIDTH), BF16),
        scratch_shapes=_flash_scratch(N_HEADS, t),
        compiler_params=_cparams(("parallel", "arbitrary")),
        name="mla_attention",
    )(p["qa"], p["ka"], p["vat"], p["ga"], mask_add)


def _diff_kernel(q_ref, k_ref, vt_ref, g_ref, bias_ref, lam_ref, subg_ref, o_ref, *state, t, lambda_init):
    i = pl.program_id(1)
    lp = lam_ref[...]
    lam = (jnp.exp(jnp.sum(lp[0:1] * lp[1:2], axis=-1, keepdims=True))
           - jnp.exp(jnp.sum(lp[2:3] * lp[3:4], axis=-1, keepdims=True)) + lambda_init)
    qs = [q_ref[:, (j // 4) * LANES:(j // 4 + 1) * LANES] * _lane_mask((j % 4) * DIFF_QK, (j % 4 + 1) * DIFF_QK, BF16)
          for j in range(N_DIFF_MAPS)]
    keys = lambda j, rows: k_ref[rows, (j // 4) * LANES:(j // 4 + 1) * LANES]
    values = lambda j, kt: vt_ref[kt, (j // 2) * HEAD_DIM:(j // 2 + 1) * HEAD_DIM, :]
    maps = _flash_sweep(i, t, N_DIFF_MAPS, qs, keys, values, state, lambda j: bias_ref[j, 1], lambda j: bias_ref[j, 0])
    outs = []
    for hd in range(N_HEADS):
        o = maps[2 * hd] - lam * maps[2 * hd + 1]
        ms = jnp.mean(o * o, axis=0, keepdims=True)
        outs.append(o * lax.rsqrt(ms + NORM_EPS) * subg_ref[...] * (1.0 - lambda_init))
    _store_gated(o_ref, g_ref, outs)


def _diff(p, bias, lam_params, subln_g, lambda_init, batch, seq, t):
    qs, ks, vspec, ospec = _att_specs(t, seq, (256, 256), (256,))
    subg = jnp.broadcast_to(subln_g.astype(F32)[:, None], (HEAD_DIM, t))
    return pl.pallas_call(
        functools.partial(_diff_kernel, t=t, lambda_init=lambda_init),
        grid=(batch, seq // t),
        in_specs=[qs[0], ks[0], vspec, qs[1],
                  pl.BlockSpec((N_DIFF_MAPS, 2, t, t), lambda b, i: (0, 0, 0, 0), pipeline_mode=_ONE_BUFFER),
                  pl.BlockSpec((4, DIFF_QK), lambda b, i: (0, 0)),
                  pl.BlockSpec((HEAD_DIM, t), lambda b, i: (0, 0))],
        out_specs=ospec,
        out_shape=jax.ShapeDtypeStruct((batch * seq, GROUP_WIDTH), BF16),
        scratch_shapes=_flash_scratch(N_DIFF_MAPS, t),
        compiler_params=_cparams(("parallel", "arbitrary")),
        name="diff_attention",
    )(p["qb"], p["kb"], p["vbt"], p["gb"], bias, lam_params.astype(F32), subg)


def _sort_key(x):
    b = lax.bitcast_convert_type(x + 0.0, jnp.int32)
    return b ^ ((b >> 31) & 0x7FFFFFFF)


def _dsa_kernel(q_ref, k_ref, vt_ref, qi_ref, ki_ref, wi_ref, g_ref, bias_ref, tri_ref, o_ref,
                score_ref, hi_ref, lo_ref, thr_ref, cnt_ref, *state, t, topk):
    i = pl.program_id(1)
    n_kt = i + 1
    idx_scale = (IDX_HEADS ** -0.5) * (IDX_DIM ** -0.5)

    wit = wi_ref[...].T * idx_scale
    qm, wrow = [], []
    for ih in range(IDX_HEADS):
        grp = slice((ih // 4) * LANES, (ih // 4 + 1) * LANES)
        lo = (ih % 4) * IDX_DIM
        qm.append(qi_ref[:, grp] * _lane_mask(lo, lo + IDX_DIM, BF16))
        wrow.append(wit[ih:ih + 1, :])

    def idx_tile(kt):
        kk = ki_ref[pl.ds(pl.multiple_of(kt * t, t), t), :]
        dot = lambda ih: lax.dot_general(kk, qm[ih], _NT, preferred_element_type=F32)
        queue = [dot(ih) for ih in range(FLASH_LOOKAHEAD)]
        tot = None
        for ih in range(IDX_HEADS):
            if ih + FLASH_LOOKAHEAD < IDX_HEADS:
                queue.append(dot(ih + FLASH_LOOKAHEAD))
            term = wrow[ih] * jnp.maximum(queue.pop(0), 0.0)
            tot = term if tot is None else tot + term
        return tot + 0.0

    def store_scores(kt, score):
        score_ref[kt] = score
        key = _sort_key(score)
        hi_ref[kt] = (key >> HALF_BITS).astype(jnp.int16)
        lo_ref[kt] = ((key & HALF_MASK) - HALF_BIAS).astype(jnp.int16)

    def key_tiles(first, n):
        for d in range(n):
            store_scores(first + d, idx_tile(first + d))

    def key_quad_body(c, carry):
        key_tiles(4 * c, 4)
        return carry

    lax.fori_loop(0, i // 4, key_quad_body, 0)
    pl.when(i % 4 >= 2)(lambda: key_tiles(i - i % 4, 2))
    pl.when(i % 2 == 1)(lambda: key_tiles(i - 1, 1))
    kchunk = lax.broadcasted_iota(jnp.int32, (t, t), 0) // CHUNK
    qchunk = lax.broadcasted_iota(jnp.int32, (t, t), 1) // CHUNK
    store_scores(i, jnp.where(kchunk <= qchunk, idx_tile(i), -jnp.inf))

    def count(pred):
        def body(kt, c):
            hit = jnp.where(pred(score_ref[kt]), 1.0, 0.0)
            return c + jnp.sum(hit.reshape(t // SUBLANES, SUBLANES, t), axis=0)
        part = lax.fori_loop(0, n_kt, body, jnp.zeros((SUBLANES, t), F32))
        return jnp.sum(part, axis=0, keepdims=True)

    rows16 = 2 * SUBLANES
    n_acc = 4

    def count16(ref, pred):
        one, zero = jnp.int16(1), jnp.int16(0)

        def body(kt, accs):
            x = ref[kt].reshape(t // rows16, rows16, t)
            accs = list(accs)
            for r in range(t // rows16):
                accs[r % n_acc] = accs[r % n_acc] + jnp.where(pred(x[r]), one, zero)
            return tuple(accs)
        accs = lax.fori_loop(0, n_kt, body, tuple(jnp.zeros((rows16, t), jnp.int16) for _ in range(n_acc)))
        tot = (accs[0] + accs[1]) + (accs[2] + accs[3])
        return jnp.sum(tot.astype(jnp.int32), axis=0, keepdims=True)

    def select16(ref, need):
        def bit_body(b, res):
            cand = res | (jnp.int32(1) << (HALF_BITS - 1 - b))
            cand16 = (cand - HALF_BIAS).astype(jnp.int16)
            cnt = count16(ref, lambda x: x >= cand16)
            return jnp.where(cnt >= need, cand, res)
        return lax.fori_loop(0, HALF_BITS, bit_body, jnp.zeros((1, t), jnp.int32)) - HALF_BIAS

    hi_thr = select16(hi_ref, topk)
    hi_thr16 = hi_thr.astype(jnp.int16)
    need_lo = topk - count16(hi_ref, lambda x: x > hi_thr16)

    def member_body(kt, c):
        lo_ref[kt] = jnp.where(hi_ref[kt] == hi_thr16, lo_ref[kt], jnp.int16(-HALF_BIAS))
        return c

    lax.fori_loop(0, n_kt, member_body, 0)
    lo_thr = select16(lo_ref, need_lo)

    def key_to_score(key):
        key = jnp.maximum(key, KEY_LOWEST)
        return lax.bitcast_convert_type(key ^ ((key >> 31) & 0x7FFFFFFF), F32)

    thr_ref[...] = key_to_score((hi_thr << HALF_BITS) | (lo_thr + HALF_BIAS))
    cnt_ref[...] = count(lambda x: x >= thr_ref[...])
    n_valid = ((i * t + lax.broadcasted_iota(jnp.int32, (1, t), 1)) // CHUNK + 1) * CHUNK
    few = n_valid <= topk

    @pl.when(jnp.max(jnp.where(few, 0.0, jnp.abs(cnt_ref[...] - topk))) > 0.0)
    def _():
        cnt_gt = count(lambda x: x > thr_ref[...])
        wrong = (cnt_ref[...] < topk) | (cnt_gt >= topk)

        @pl.when(jnp.max(jnp.where(few | jnp.logical_not(wrong), 0.0, 1.0)) > 0.0)
        def _():
            def bit_body(b, res):
                cand = res | (jnp.int32(1) << (31 - b))
                cand_score = key_to_score(cand ^ INT_MIN)
                cnt = count(lambda x: x >= cand_score)
                return jnp.where(cnt >= topk, cand, res)
            res = lax.fori_loop(0, 32, bit_body, jnp.zeros((1, t), jnp.int32))
            thr_ref[...] = key_to_score(res ^ INT_MIN)
            cnt_ref[...] = count(lambda x: x >= thr_ref[...])

    thr = thr_ref[...]
    has_ties = jnp.max(jnp.where(few, 0.0, cnt_ref[...])) > topk

    @pl.when(jnp.logical_not(has_ties))
    def _():
        def body(kt, c):
            score_ref[kt] = jnp.where(score_ref[kt] >= thr, 0.0, NEG)
            return c
        lax.fori_loop(0, n_kt, body, 0)

    @pl.when(has_ties)
    def _():
        need = topk - count(lambda x: x > thr)

        def body(kt, seen):
            x = score_ref[kt]
            eq = jnp.where(x == thr, 1.0, 0.0)
            rank = seen + jnp.dot(tri_ref[...], eq.astype(BF16), preferred_element_type=F32)
            sel = (x > thr) | ((x == thr) & (rank <= need))
            score_ref[kt] = jnp.where(sel, 0.0, NEG)
            return seen + jnp.sum(eq, axis=0, keepdims=True)
        lax.fori_loop(0, n_kt, body, jnp.zeros((1, t), F32))

    qs = [q_ref[:, (hd // 2) * LANES:(hd // 2 + 1) * LANES]
          * _lane_mask((hd % 2) * HEAD_DIM, (hd % 2 + 1) * HEAD_DIM, BF16) for hd in range(N_HEADS)]
    keys = lambda hd, rows: k_ref[rows, (hd // 2) * LANES:(hd // 2 + 1) * LANES]
    values = lambda hd, kt: vt_ref[kt, hd * HEAD_DIM:(hd + 1) * HEAD_DIM, :]
    outs = _flash_sweep(i, t, N_HEADS, qs, keys, values, state, lambda hd: bias_ref[hd, 1], lambda hd: bias_ref[hd, 0],
                        mask=lambda kt: score_ref[kt])
    _store_gated(o_ref, g_ref, outs)


def _dsa(p, bias, batch, seq, t):
    topk = min(TOPK_MAX, seq // 4)
    nq = seq // t
    qs, ks, vspec, ospec = _att_specs(t, seq, (256, 256, 128, 256), (256, 128))
    tri = jnp.asarray(np.tril(np.ones((t, t), np.float32)), BF16)
    return pl.pallas_call(
        functools.partial(_dsa_kernel, t=t, topk=topk),
        grid=(batch, nq),
        in_specs=[qs[0], ks[0], vspec, qs[1], ks[1], qs[2], qs[3],
                  pl.BlockSpec((N_HEADS, 2, t, t), lambda b, i: (0, 0, 0, 0), pipeline_mode=_ONE_BUFFER),
                  pl.BlockSpec((t, t), lambda b, i: (0, 0), pipeline_mode=_ONE_BUFFER)],
        out_specs=ospec,
        out_shape=jax.ShapeDtypeStruct((batch * seq, GROUP_WIDTH), BF16),
        scratch_shapes=[pltpu.VMEM((nq, t, t), F32), pltpu.VMEM((nq, t, t), jnp.int16),
                        pltpu.VMEM((nq, t, t), jnp.int16), pltpu.VMEM((1, t), F32), pltpu.VMEM((1, t), F32)]
        + _flash_scratch(N_HEADS, t),
        compiler_params=_cparams(("parallel", "arbitrary")),
        name="dsa_attention",
    )(p["qc"], p["kc"], p["vct"], p["qi"], p["ki"], p["wi"], p["gc"], bias, tri)


def _outproj_kernel(x_ref, ya_ref, yb_ref, yc_ref, yd_ref, ud_ref, halo_ref, cw_ref, w_ref, fg_ref, o_ref,
                    *, tiles_per_seq, final):
    i = pl.program_id(0)
    tm = x_ref.shape[0]
    u = ud_ref[...]
    halo = jnp.where(i % tiles_per_seq == 0, 0.0, halo_ref[...])
    row8 = lax.broadcasted_iota(jnp.int32, (SUBLANES, 1), 0)

    def shifted(k):
        r = pltpu.roll(u, k, 0)
        head = jnp.where(row8 < k, pltpu.roll(halo, k, 0), r[0:SUBLANES])
        return jnp.concatenate([head, r[SUBLANES:tm]], axis=0)

    cw = cw_ref[...]
    conv = cw[0:1] * shifted(CONV_WIDTH - 1)
    for j in range(1, CONV_WIDTH):
        conv = conv + cw[j:j + 1] * (shifted(CONV_WIDTH - 1 - j) if j < CONV_WIDTH - 1 else u)
    yd = (conv * yd_ref[...].astype(F32)).astype(BF16)
    acc = x_ref[...]
    for g, y in enumerate((ya_ref[...], yb_ref[...], yc_ref[...], yd)):
        acc = acc + jnp.dot(y, w_ref[g * GROUP_WIDTH:(g + 1) * GROUP_WIDTH, :], preferred_element_type=F32)
    if final:
        acc = _rms(acc, fg_ref[...])
    o_ref[...] = acc


def _outproj(x2, l, ya, yb, yc, yd, ud, conv_w, w_out, final_g, seq, final):
    n = x2.shape[0]
    tm = min(PROJ_ROWS, seq)
    tiles_per_seq = seq // tm
    row = lambda w: pl.BlockSpec((tm, w), lambda i: (i, 0))
    halo = pl.BlockSpec((SUBLANES, GROUP_WIDTH), lambda i: (jnp.maximum(i * (tm // SUBLANES) - 1, 0), 0))
    return pl.pallas_call(
        functools.partial(_outproj_kernel, tiles_per_seq=tiles_per_seq, final=final),
        grid=(n // tm,),
        in_specs=[row(D_MODEL), row(256), row(256), row(256), row(256), row(256), halo,
                  _layer_spec(conv_w, l), _layer_spec(w_out, l), pl.BlockSpec(final_g.shape, lambda i: (0, 0))],
        out_specs=row(D_MODEL),
        out_shape=jax.ShapeDtypeStruct((n, D_MODEL), F32),
        compiler_params=_cparams(("parallel",)),
        name="outproj",
    )(x2, ya, yb, yc, yd, ud, ud, conv_w, w_out, final_g)


def kernel(x, norm_g, w_in, mla_qa_g, mla_w_uq, mla_kva_g, mla_w_ukv, diff_lambda, diff_subln_g, conv_w, w_out,
           rel_bias, final_g):
    batch, seq, _ = x.shape
    depth = w_in.shape[0]
    tm = min(PROJ_ROWS, seq)
    tiles = tuple(min(t, seq) for t in (MLA_TILE, DIFF_TILE, DSA_TILE))
    assert seq % tm == 0 and tm % VT_TILE == 0 and conv_w.shape[1] == CONV_WIDTH <= SUBLANES
    assert all(seq % t == 0 and t % CHUNK == 0 and tm % t == 0 and t % VT_TILE == 0 for t in tiles)

    cos_t, sin_t = (jnp.asarray(a) for a in _rope_tables(seq))
    mask_add = jnp.asarray(_position_constants(seq, tiles[0])[2])
    bias_diff = _bias_tiles(rel_bias[:, :N_DIFF_MAPS], seq, tiles[1])
    bias_dsa = _bias_tiles(rel_bias[:, N_DIFF_MAPS:], seq, tiles[2])

    w1, wvt = _prep_w_in(w_in)
    wq1, wq2, wk, wavt = _prep_mla_up(mla_w_uq, mla_w_ukv)
    row3 = lambda a: a[:, None, :]
    w_out_b = w_out.astype(BF16)

    x2 = x.reshape(batch * seq, D_MODEL)
    for l in range(depth):
        lambda_init = 0.8 - 0.6 * math.exp(-0.3 * l)
        p = _inproj(x2, l, row3(norm_g), w1, wvt, row3(mla_qa_g), wq1, wq2, row3(mla_kva_g), wk, wavt,
                    cos_t, sin_t, seq, tiles)
        ya = _mla(p, mask_add, batch, seq, tiles[0])
        yb = _diff(p, bias_diff, diff_lambda[l], diff_subln_g[l], lambda_init, batch, seq, tiles[1])
        yc = _dsa(p, bias_dsa, batch, seq, tiles[2])
        x2 = _outproj(x2, l, ya, yb, yc, p["yd"], p["ud"], conv_w, w_out_b, final_g[None, :],
                      seq, final=(l == depth - 1))
    return x2.reshape(batch, seq, D_MODEL)
```

```python
import functools
import math

import numpy as np
import jax
import jax.numpy as jnp
from jax import lax
from jax.experimental import pallas as pl
from jax.experimental.pallas import tpu as pltpu

F32 = jnp.float32
BF16 = jnp.bfloat16

D_MODEL = 1024
CHUNK = 64
N_HEADS = 4
HEAD_DIM = 64
GROUP_WIDTH = N_HEADS * HEAD_DIM
MLA_Q_LORA = 256
MLA_KV_LORA = 128
MLA_NOPE = 64
MLA_ROPE = 32
ROPE_BASE = 10000.0
DIFF_QK = 32
IDX_HEADS = 8
IDX_DIM = 32
TOPK_MAX = 256
CONV_WIDTH = 3
NUM_BUCKETS = 32
MAX_DISTANCE = 128
N_DIFF_MAPS = 2 * N_HEADS
NORM_EPS = 1e-6
NEG = -1e30
LOG2E = math.log2(math.e)
INT_MIN = -2 ** 31
KEY_LOWEST = int(np.float32(-np.finfo(np.float32).max).view(np.int32)) ^ 0x7FFFFFFF
HALF_BITS = 16
HALF_MASK = 2 ** HALF_BITS - 1
HALF_BIAS = 2 ** (HALF_BITS - 1)

LANES = 128
SUBLANES = 8
MLA_TILE = 256
DIFF_TILE = 256
DSA_TILE = 512
VT_TILE = 256
FAR_UNIT_KEYS = 512
PROJ_ROWS = 512
FLASH_LOOKAHEAD = 4
V7X_VMEM_BYTES = 64 * 1024 * 1024
VMEM_LIMIT = V7X_VMEM_BYTES * 7 // 8

_IN_SPLITS = (
    ("a_cq", 256), ("a_ckv", 128), ("a_krope", 32), ("a_gate", 256),
    ("b_q", 256), ("b_k", 256), ("b_v", 256), ("b_gate", 256),
    ("c_q", 256), ("c_k", 256), ("c_v", 256),
    ("c_qidx", 256), ("c_kidx", 32), ("c_widx", 8), ("c_gate", 256),
    ("d_b", 256), ("d_c", 256), ("d_h", 256), ("d_gate", 256),
)

_SEGS = (
    ("cq", 256), ("ckv", 128), ("kr1", 128), ("kr2", 128), ("c_kidx4", 128), ("a_gate", 256),
    ("b_q", 256), ("b_k", 256), ("b_gate", 256),
    ("c_q", 256), ("c_k", 256), ("c_qidx", 256),
    ("c_gate", 256), ("d_b", 256), ("d_c", 256), ("d_h", 256), ("d_gate", 256), ("c_widx", 128),
)
_SEG_OFF = {}
_off = 0
for _name, _w in _SEGS:
    _SEG_OFF[_name] = (_off, _w)
    _off += _w
W1_COLS = _off

_NT = (((1,), (1,)), ((), ()))
_ONE_BUFFER = pl.Buffered(1)


def _cparams(sem):
    return pltpu.CompilerParams(dimension_semantics=sem, vmem_limit_bytes=VMEM_LIMIT)


def _split_cols(w):
    out, off = {}, 0
    for name, width in _IN_SPLITS:
        out[name] = w[..., off:off + width]
        off += width
    return out


def _rot_half_cols(w):
    half = MLA_ROPE // 2
    return jnp.concatenate([-w[..., half:], w[..., :half]], axis=-1)


def _prep_w_in(w):
    p = _split_cols(w)
    z = lambda n: jnp.zeros(w.shape[:-1] + (n,), w.dtype)
    cat = lambda parts: jnp.concatenate(parts, axis=-1)
    segs = {
        "cq": p["a_cq"], "ckv": p["a_ckv"], "a_gate": p["a_gate"],
        "kr1": cat([z(MLA_NOPE), p["a_krope"], z(LANES - MLA_NOPE - MLA_ROPE)]),
        "kr2": cat([z(MLA_NOPE), _rot_half_cols(p["a_krope"]), z(LANES - MLA_NOPE - MLA_ROPE)]),
        "b_q": p["b_q"] * (DIFF_QK ** -0.5 * LOG2E), "b_k": p["b_k"], "b_gate": p["b_gate"],
        "c_q": p["c_q"] * (HEAD_DIM ** -0.5 * LOG2E), "c_k": p["c_k"], "c_qidx": p["c_qidx"],
        "c_kidx4": cat([p["c_kidx"]] * (LANES // IDX_DIM)),
        "c_widx": cat([p["c_widx"], z(LANES - IDX_HEADS)]),
        "c_gate": p["c_gate"], "d_b": p["d_b"], "d_c": p["d_c"], "d_h": p["d_h"], "d_gate": p["d_gate"],
    }
    w1 = cat([segs[name].astype(BF16) for name, _ in _SEGS])
    wvt = jnp.swapaxes(cat([p["b_v"].astype(BF16), p["c_v"].astype(BF16)]), -1, -2)
    return w1, wvt


def _prep_mla_up(w_uq, w_ukv):
    depth = w_uq.shape[0]
    scale = (MLA_NOPE + MLA_ROPE) ** -0.5 * LOG2E
    wq = w_uq.reshape(depth, MLA_Q_LORA, N_HEADS, MLA_NOPE + MLA_ROPE) * scale
    zq = jnp.zeros((depth, MLA_Q_LORA, N_HEADS, LANES - MLA_NOPE - MLA_ROPE), w_uq.dtype)
    wq1 = jnp.concatenate([wq, zq], axis=-1).reshape(depth, MLA_Q_LORA, N_HEADS * LANES)
    wq2 = jnp.concatenate([jnp.zeros_like(wq[..., :MLA_NOPE]), _rot_half_cols(wq[..., MLA_NOPE:]), zq],
                          axis=-1).reshape(depth, MLA_Q_LORA, N_HEADS * LANES)
    wkv = w_ukv.reshape(depth, MLA_KV_LORA, N_HEADS, MLA_NOPE + HEAD_DIM)
    zk = jnp.zeros((depth, MLA_KV_LORA, N_HEADS, LANES - MLA_NOPE), w_ukv.dtype)
    wk = jnp.concatenate([wkv[..., :MLA_NOPE], zk], axis=-1).reshape(depth, MLA_KV_LORA, N_HEADS * LANES)
    wvt = jnp.swapaxes(wkv[..., MLA_NOPE:].reshape(depth, MLA_KV_LORA, N_HEADS * HEAD_DIM), -1, -2)
    return wq1.astype(BF16), wq2.astype(BF16), wk.astype(BF16), wvt.astype(BF16)


def _rope_tables(seq):
    half = MLA_ROPE // 2
    inv_freq = ROPE_BASE ** (-np.arange(half, dtype=np.float32) / half)
    ang = np.arange(seq, dtype=np.float32)[:, None] * inv_freq[None, :].astype(np.float32)
    cos, sin = np.cos(ang).astype(np.float32), np.sin(ang).astype(np.float32)
    ct = np.zeros((seq, LANES), np.float32)
    st = np.zeros((seq, LANES), np.float32)
    ct[:, :MLA_NOPE] = 1.0
    ct[:, MLA_NOPE:MLA_NOPE + half] = cos
    ct[:, MLA_NOPE + half:MLA_NOPE + MLA_ROPE] = cos
    st[:, MLA_NOPE:MLA_NOPE + half] = sin
    st[:, MLA_NOPE + half:MLA_NOPE + MLA_ROPE] = sin
    return ct, st


def _rel_bucket_np(rel):
    nb = NUM_BUCKETS // 2
    max_exact = nb // 2
    ret = np.where(rel > 0, nb, 0)
    n = np.abs(rel)
    nf = np.maximum(n, max_exact).astype(np.float32)
    large = max_exact + (np.log(nf / np.float32(max_exact)) / np.float32(math.log(MAX_DISTANCE / max_exact))
                         * np.float32(nb - max_exact)).astype(np.int32)
    large = np.minimum(large, nb - 1)
    return (ret + np.where(n < max_exact, n, large)).astype(np.int32)


def _position_constants(seq, tile):
    r = np.arange(tile)
    rel0 = r[:, None] - r[None, :]
    bidx = np.stack([_rel_bucket_np(rel0 - tile), _rel_bucket_np(rel0)]).astype(np.int32)
    far = _rel_bucket_np(np.arange(-(seq - 1), -tile))
    far_bucket = int(far[0]) if far.size else int(_rel_bucket_np(np.array([-tile - 1]))[0])
    assert far.size == 0 or np.all(far == far_bucket)
    mask_add = np.where((r[:, None] // CHUNK) <= (r[None, :] // CHUNK), 0.0, NEG).astype(np.float32)
    return bidx, far_bucket, mask_add


def _bias_kernel(tab_ref, bidx_ref, madd_ref, o_ref, *, far_bucket, buckets):
    j = pl.program_id(0)
    c = tab_ref[j, far_bucket]
    for d in range(2):
        idx = bidx_ref[d]
        acc = jnp.zeros(idx.shape, F32)
        for b in buckets[d]:
            acc = jnp.where(idx == b, (tab_ref[j, b] - c) * LOG2E, acc)
        if d == 1:
            acc = acc + madd_ref[...]
        o_ref[0, d] = acc


def _bias_tiles(table, seq, tile):
    bidx, far_bucket, mask_add = _position_constants(seq, tile)
    n_maps = table.shape[1]
    buckets = tuple(tuple(int(b) for b in np.unique(bidx[d])) for d in range(2))
    return pl.pallas_call(
        functools.partial(_bias_kernel, far_bucket=far_bucket, buckets=buckets),
        grid=(n_maps,),
        in_specs=[
            pl.BlockSpec(memory_space=pltpu.SMEM),
            pl.BlockSpec((2, tile, tile), lambda j: (0, 0, 0)),
            pl.BlockSpec((tile, tile), lambda j: (0, 0)),
        ],
        out_specs=pl.BlockSpec((1, 2, tile, tile), lambda j: (j, 0, 0, 0)),
        out_shape=jax.ShapeDtypeStruct((n_maps, 2, tile, tile), F32),
        compiler_params=_cparams(("arbitrary",)),
        name="bias_tiles",
    )(table.T.astype(F32), jnp.asarray(bidx), jnp.asarray(mask_add))


def _silu(x):
    return x * (1.0 / (1.0 + jnp.exp(-x)))


def _rms(x, g):
    return x * lax.rsqrt(jnp.mean(x * x, axis=-1, keepdims=True) + NORM_EPS) * g


def _inproj_kernel(x_ref, g_ref, w_ref, wvt_ref, qag_ref, wq1_ref, wq2_ref, kvg_ref, wk_ref, wavt_ref, cos_ref, sin_ref,
                   qa_ref, ka_ref, vat_ref, ga_ref, qb_ref, kb_ref, vbt_ref, gb_ref,
                   qc_ref, kc_ref, vct_ref, qi_ref, ki_ref, wi_ref, gc_ref, yd_ref, ud_ref):
    h = _rms(x_ref[...], g_ref[...]).astype(BF16)

    def seg(name):
        off, width = _SEG_OFF[name]
        return jnp.dot(h, w_ref[:, off:off + width], preferred_element_type=F32)

    def seg_pair(first, second):
        off, width = _SEG_OFF[first]
        assert _SEG_OFF[second] == (off + width, width) and width == LANES
        y = jnp.dot(h, w_ref[:, off:off + 2 * width], preferred_element_type=F32)
        return y[:, :width], y[:, width:]

    cos = cos_ref[...]
    sin = sin_ref[...]

    cqn = _rms(seg("cq"), qag_ref[...]).astype(BF16)
    q1 = jnp.dot(cqn, wq1_ref[...], preferred_element_type=F32)
    q2 = jnp.dot(cqn, wq2_ref[...], preferred_element_type=F32)
    ckv, kr1 = seg_pair("ckv", "kr1")
    kr2, kidx4 = seg_pair("kr2", "c_kidx4")
    kr = kr1 * cos + kr2 * sin
    ckvn = _rms(ckv, kvg_ref[...]).astype(BF16)
    kn = jnp.dot(ckvn, wk_ref[...], preferred_element_type=F32)
    for hd in range(N_HEADS):
        sl = slice(hd * LANES, (hd + 1) * LANES)
        qa_ref[:, sl] = (q1[:, sl] * cos + q2[:, sl] * sin).astype(BF16)
        ka_ref[:, sl] = (kn[:, sl] + kr).astype(BF16)
    ga_ref[...] = _silu(seg("a_gate")).astype(BF16)

    def put(ref, first_token, vt):
        tile = ref.shape[2]
        ref[first_token // tile, :, first_token % tile:first_token % tile + VT_TILE] = vt

    for c in range(x_ref.shape[0] // VT_TILE):
        rows = slice(c * VT_TILE, (c + 1) * VT_TILE)
        put(vat_ref, c * VT_TILE,
            lax.dot_general(wavt_ref[...], ckvn[rows], _NT, preferred_element_type=F32).astype(BF16))
        vt = lax.dot_general(wvt_ref[...], h[rows], _NT, preferred_element_type=F32).astype(BF16)
        put(vbt_ref, c * VT_TILE, vt[:GROUP_WIDTH])
        put(vct_ref, c * VT_TILE, vt[GROUP_WIDTH:])

    qb_ref[...] = seg("b_q").astype(BF16)
    kb_ref[...] = seg("b_k").astype(BF16)
    gb_ref[...] = _silu(seg("b_gate")).astype(BF16)

    qc_ref[...] = seg("c_q").astype(BF16)
    kc_ref[...] = seg("c_k").astype(BF16)
    qi_ref[...] = seg("c_qidx").astype(BF16)
    ki_ref[...] = kidx4.astype(BF16)
    wi_ref[...] = seg("c_widx")
    gc_ref[...] = _silu(seg("c_gate")).astype(BF16)

    yd_ref[...] = (seg("d_b") * _silu(seg("d_gate"))).astype(BF16)
    ud_ref[...] = seg("d_c") * seg("d_h")


def _layer_spec(a, l):
    return pl.BlockSpec((None,) + a.shape[1:], lambda i: (l,) + (0,) * (a.ndim - 1))


def _inproj(x2, l, g, w1, wvt, qag, wq1, wq2, kvg, wk, wavt, cos_t, sin_t, seq, tiles):
    n = x2.shape[0]
    tm = min(PROJ_ROWS, seq)
    tiles_per_seq = seq // tm
    row = lambda w: pl.BlockSpec((tm, w), lambda i: (i, 0))
    full = lambda a: _layer_spec(a, l)
    tab = pl.BlockSpec((tm, LANES), lambda i: (i % tiles_per_seq, 0))
    vt_spec = lambda t: pl.BlockSpec((tm // t, GROUP_WIDTH, t), lambda i: (i, 0, 0))
    vt_shape = lambda t: jax.ShapeDtypeStruct((n // t, GROUP_WIDTH, t), BF16)
    outs = (("qa", 512, BF16), ("ka", 512, BF16), ("vat", None, tiles[0]), ("ga", 256, BF16),
            ("qb", 256, BF16), ("kb", 256, BF16), ("vbt", None, tiles[1]), ("gb", 256, BF16),
            ("qc", 256, BF16), ("kc", 256, BF16), ("vct", None, tiles[2]), ("qi", 256, BF16),
            ("ki", 128, BF16), ("wi", 128, F32), ("gc", 256, BF16), ("yd", 256, BF16), ("ud", 256, F32))
    res = pl.pallas_call(
        _inproj_kernel,
        grid=(n // tm,),
        in_specs=[row(D_MODEL), full(g), full(w1), full(wvt), full(qag), full(wq1), full(wq2), full(kvg), full(wk),
                  full(wavt), tab, tab],
        out_specs=[vt_spec(dt) if w is None else row(w) for _, w, dt in outs],
        out_shape=[vt_shape(dt) if w is None else jax.ShapeDtypeStruct((n, w), dt) for _, w, dt in outs],
        compiler_params=_cparams(("parallel",)),
        name="inproj",
    )(x2, g, w1, wvt, qag, wq1, wq2, kvg, wk, wavt, cos_t, sin_t)
    return {name: o for (name, _, _), o in zip(outs, res)}


def _flash_scratch(n_maps, t):
    unit = _far_unit_tiles(t) * t
    return [pltpu.VMEM((n_maps, 1, t), F32), pltpu.VMEM((n_maps, 1, t), F32), pltpu.VMEM((n_maps, HEAD_DIM, t), F32),
            pltpu.VMEM((n_maps, unit, t), F32), pltpu.VMEM((n_maps, unit, t), F32)]


def _far_unit_tiles(t):
    return max(1, FAR_UNIT_KEYS // t)


def _flash_update(s, vts, old):
    t = vts[0].shape[1]
    m_tile = jnp.max(s, axis=0, keepdims=True)
    m_new = m_tile if old is None else jnp.maximum(old[0], m_tile)
    p = jnp.exp2(s - m_new)
    l_tile = jnp.sum(p, axis=0, keepdims=True)
    pb = p.astype(BF16)
    pv = jnp.dot(vts[0], pb[0:t], preferred_element_type=F32)
    for d in range(1, len(vts)):
        pv = pv + jnp.dot(vts[d], pb[d * t:(d + 1) * t], preferred_element_type=F32)
    if old is None:
        return m_new, l_tile, pv
    alpha = jnp.exp2(old[0] - m_new)
    return m_new, alpha * old[1] + l_tile, alpha * old[2] + pv


def _flash_sweep(i, t, n_maps, qs, keys, values, state, diag_add, near_add=None, mask=None):
    m_ref, l_ref, acc_ref, buf_a, buf_b = state
    n_far = i if near_add is None else jnp.maximum(i - 1, 0)
    ut = _far_unit_tiles(t)
    n_units = n_far // ut

    def load_state():
        return [(m_ref[j], l_ref[j], acc_ref[j]) for j in range(n_maps)]

    def store_state(cur):
        for j in range(n_maps):
            m_ref[j], l_ref[j], acc_ref[j] = cur[j]

    def tile_logits(j, kt, n_tiles, shared, extra=None):
        rows = pl.ds(pl.multiple_of(kt * t, t), n_tiles * t)
        s = lax.dot_general(keys(j, rows), qs[j], _NT, preferred_element_type=F32)
        add = shared
        if extra is not None:
            add = extra(j) if add is None else add + extra(j)
        return s if add is None else s + add

    def shared_mask(kt, n_tiles):
        if mask is None:
            return None
        tiles = [mask(kt + d) for d in range(n_tiles)]
        return tiles[0] if n_tiles == 1 else jnp.concatenate(tiles, axis=0)

    def run(units, first=False, prefetch=False):
        cur = [None] * n_maps if first else load_state()
        shared = [shared_mask(kt, n_tiles) for kt, n_tiles, _ in units]
        items = [(u, j) for u in range(len(units)) for j in range(n_maps)]
        if prefetch:
            items += [(None, j) for j in range(n_maps)]
            shared.append(shared_mask(0, ut))

        def logits(u, j):
            if u is None:
                return tile_logits(j, 0, ut, shared[-1])
            kt, n_tiles, extra = units[u]
            return tile_logits(j, kt, n_tiles, shared[u], extra)

        queue = [logits(*it) for it in items[:FLASH_LOOKAHEAD]]
        for n, (u, j) in enumerate(items):
            if n + FLASH_LOOKAHEAD < len(items):
                queue.append(logits(*items[n + FLASH_LOOKAHEAD]))
            s = queue.pop(0)
            if u is None:
                buf_a[j] = s
            else:
                kt, n_tiles, _ = units[u]
                cur[j] = _flash_update(s, [values(j, kt + d) for d in range(n_tiles)], cur[j])
        store_state(cur)

    def first_run(prefetch):
        if near_add is None:
            return lambda: run([(i, 1, diag_add)], first=True, prefetch=prefetch)
        return lambda: run([(i, 1, diag_add), (i - 1, 1, near_add)], first=True, prefetch=prefetch)

    if near_add is None:
        pl.when(n_units == 0)(first_run(False))
    else:
        pl.when(i == 0)(lambda: run([(i, 1, diag_add)], first=True))
        pl.when((i >= 1) & (n_units == 0))(first_run(False))
    pl.when(n_units >= 1)(first_run(True))

    def consume(buf, kt, j, old):
        return _flash_update(buf[j], [values(j, kt + d) for d in range(ut)], old)

    def overlapped(cur, buf_in, buf_out, u):
        kt_next = ut * (u + 1)
        shared = shared_mask(kt_next, ut)
        for j in range(n_maps):
            buf_out[j] = tile_logits(j, kt_next, ut, shared)
            cur[j] = consume(buf_in, ut * u, j, cur[j])

    def far_body(c, carry):
        cur = load_state()
        overlapped(cur, buf_a, buf_b, 2 * c)
        overlapped(cur, buf_b, buf_a, 2 * c + 1)
        store_state(cur)
        return carry

    lax.fori_loop(0, (n_units - 1) // 2, far_body, 0)

    @pl.when(n_units % 2 == 1)
    def _():
        cur = load_state()
        for j in range(n_maps):
            cur[j] = consume(buf_a, ut * (n_units - 1), j, cur[j])
        store_state(cur)

    @pl.when((n_units >= 2) & (n_units % 2 == 0))
    def _():
        cur = load_state()
        overlapped(cur, buf_a, buf_b, n_units - 2)
        for j in range(n_maps):
            cur[j] = consume(buf_b, ut * (n_units - 1), j, cur[j])
        store_state(cur)

    for r in range(1, ut):
        pl.when(n_far % ut >= r)(lambda r=r: run([(n_far - r, 1, None)]))
    return [acc_ref[j] / l_ref[j] for j in range(n_maps)]


def _lane_mask(lo, hi, dtype):
    lane = lax.broadcasted_iota(jnp.int32, (1, LANES), 1)
    return ((lane >= lo) & (lane < hi)).astype(dtype)


def _store_gated(o_ref, g_ref, heads_t):
    o = jnp.concatenate(heads_t, axis=0).T
    o_ref[...] = (o * g_ref[...].astype(F32)).astype(BF16)


def _att_specs(t, seq, widths_q, widths_k):
    nq = seq // t
    qspec = lambda w: pl.BlockSpec((t, w), lambda b, i: (b * nq + i, 0))
    kspec = lambda w: pl.BlockSpec((seq, w), lambda b, i: (b, 0), pipeline_mode=_ONE_BUFFER)
    vspec = pl.BlockSpec((nq, GROUP_WIDTH, t), lambda b, i: (b, 0, 0), pipeline_mode=_ONE_BUFFER)
    return [qspec(w) for w in widths_q], [kspec(w) for w in widths_k], vspec, qspec(GROUP_WIDTH)


def _mla_kernel(q_ref, k_ref, vt_ref, g_ref, madd_ref, o_ref, *state, t):
    i = pl.program_id(1)
    qs = [q_ref[:, hd * LANES:(hd + 1) * LANES] for hd in range(N_HEADS)]
    keys = lambda hd, rows: k_ref[rows, hd * LANES:(hd + 1) * LANES]
    values = lambda hd, kt: vt_ref[kt, hd * HEAD_DIM:(hd + 1) * HEAD_DIM, :]
    outs = _flash_sweep(i, t, N_HEADS, qs, keys, values, state, lambda hd: madd_ref[...])
    _store_gated(o_ref, g_ref, outs)


def _mla(p, mask_add, batch, seq, t):
    qs, ks, vspec, ospec = _att_specs(t, seq, (512, 256), (512,))
    return pl.pallas_call(
        functools.partial(_mla_kernel, t=t),
        grid=(batch, seq // t),
        in_specs=[qs[0], ks[0], vspec, qs[1], pl.BlockSpec((t, t), lambda b, i: (0, 0), pipeline_mode=_ONE_BUFFER)],
        out_specs=ospec,
        out_shape=jax.ShapeDtypeStruct((batch * seq, GROUP_WIDTH), BF16),
        scratch_shapes=_flash_scratch(N_HEADS, t),
        compiler_params=_cparams(("parallel", "arbitrary")),
        name="mla_attention",
    )(p["qa"], p["ka"], p["vat"], p["ga"], mask_add)


def _diff_kernel(q_ref, k_ref, vt_ref, g_ref, bias_ref, lam_ref, subg_ref, o_ref, *state, t, lambda_init):
    i = pl.program_id(1)
    lp = lam_ref[...]
    lam = (jnp.exp(jnp.sum(lp[0:1] * lp[1:2], axis=-1, keepdims=True))
           - jnp.exp(jnp.sum(lp[2:3] * lp[3:4], axis=-1, keepdims=True)) + lambda_init)
    qs = [q_ref[:, (j // 4) * LANES:(j // 4 + 1) * LANES] * _lane_mask((j % 4) * DIFF_QK, (j % 4 + 1) * DIFF_QK, BF16)
          for j in range(N_DIFF_MAPS)]
    keys = lambda j, rows: k_ref[rows, (j // 4) * LANES:(j // 4 + 1) * LANES]
    values = lambda j, kt: vt_ref[kt, (j // 2) * HEAD_DIM:(j // 2 + 1) * HEAD_DIM, :]
    maps = _flash_sweep(i, t, N_DIFF_MAPS, qs, keys, values, state, lambda j: bias_ref[j, 1], lambda j: bias_ref[j, 0])
    outs = []
    for hd in range(N_HEADS):
        o = maps[2 * hd] - lam * maps[2 * hd + 1]
        ms = jnp.mean(o * o, axis=0, keepdims=True)
        outs.append(o * lax.rsqrt(ms + NORM_EPS) * subg_ref[...] * (1.0 - lambda_init))
    _store_gated(o_ref, g_ref, outs)


def _diff(p, bias, lam_params, subln_g, lambda_init, batch, seq, t):
    qs, ks, vspec, ospec = _att_specs(t, seq, (256, 256), (256,))
    subg = jnp.broadcast_to(subln_g.astype(F32)[:, None], (HEAD_DIM, t))
    return pl.pallas_call(
        functools.partial(_diff_kernel, t=t, lambda_init=lambda_init),
        grid=(batch, seq // t),
        in_specs=[qs[0], ks[0], vspec, qs[1],
                  pl.BlockSpec((N_DIFF_MAPS, 2, t, t), lambda b, i: (0, 0, 0, 0), pipeline_mode=_ONE_BUFFER),
                  pl.BlockSpec((4, DIFF_QK), lambda b, i: (0, 0)),
                  pl.BlockSpec((HEAD_DIM, t), lambda b, i: (0, 0))],
        out_specs=ospec,
        out_shape=jax.ShapeDtypeStruct((batch * seq, GROUP_WIDTH), BF16),
        scratch_shapes=_flash_scratch(N_DIFF_MAPS, t),
        compiler_params=_cparams(("parallel", "arbitrary")),
        name="diff_attention",
    )(p["qb"], p["kb"], p["vbt"], p["gb"], bias, lam_params.astype(F32), subg)


def _sort_key(x):
    b = lax.bitcast_convert_type(x + 0.0, jnp.int32)
    return b ^ ((b >> 31) & 0x7FFFFFFF)


def _dsa_kernel(q_ref, k_ref, vt_ref, qi_ref, ki_ref, wi_ref, g_ref, bias_ref, tri_ref, o_ref,
                score_ref, hi_ref, lo_ref, *scratch, t, topk):
    *state, thr_ref, cnt_ref = scratch
    i = pl.program_id(1)
    n_kt = i + 1
    idx_scale = (IDX_HEADS ** -0.5) * (IDX_DIM ** -0.5)

    wit = wi_ref[...].T * idx_scale
    qm, wrow = [], []
    for ih in range(IDX_HEADS):
        grp = slice((ih // 4) * LANES, (ih // 4 + 1) * LANES)
        lo = (ih % 4) * IDX_DIM
        qm.append(qi_ref[:, grp] * _lane_mask(lo, lo + IDX_DIM, BF16))
        wrow.append(wit[ih:ih + 1, :])

    def idx_tile(kt):
        kk = ki_ref[pl.ds(pl.multiple_of(kt * t, t), t), :]
        dot = lambda ih: lax.dot_general(kk, qm[ih], _NT, preferred_element_type=F32)
        queue = [dot(ih) for ih in range(FLASH_LOOKAHEAD)]
        tot = None
        for ih in range(IDX_HEADS):
            if ih + FLASH_LOOKAHEAD < IDX_HEADS:
                queue.append(dot(ih + FLASH_LOOKAHEAD))
            term = wrow[ih] * jnp.maximum(queue.pop(0), 0.0)
            tot = term if tot is None else tot + term
        return tot + 0.0

    def store_scores(kt, score):
        score_ref[kt] = score
        key = _sort_key(score)
        hi_ref[kt] = (key >> HALF_BITS).astype(jnp.int16)
        lo_ref[kt] = ((key & HALF_MASK) - HALF_BIAS).astype(jnp.int16)

    def key_tiles(first, n):
        for d in range(n):
            store_scores(first + d, idx_tile(first + d))

    def key_quad_body(c, carry):
        key_tiles(4 * c, 4)
        return carry

    lax.fori_loop(0, i // 4, key_quad_body, 0)
    pl.when(i % 4 >= 2)(lambda: key_tiles(i - i % 4, 2))
    pl.when(i % 2 == 1)(lambda: key_tiles(i - 1, 1))
    kchunk = lax.broadcasted_iota(jnp.int32, (t, t), 0) // CHUNK
    qchunk = lax.broadcasted_iota(jnp.int32, (t, t), 1) // CHUNK
    store_scores(i, jnp.where(kchunk <= qchunk, idx_tile(i), -jnp.inf))

    def count(pred):
        def body(kt, c):
            hit = jnp.where(pred(score_ref[kt]), 1.0, 0.0)
            return c + jnp.sum(hit.reshape(t // SUBLANES, SUBLANES, t), axis=0)
        part = lax.fori_loop(0, n_kt, body, jnp.zeros((SUBLANES, t), F32))
        return jnp.sum(part, axis=0, keepdims=True)

    rows16 = 2 * SUBLANES
    n_acc = 4

    def count16(ref, pred):
        one, zero = jnp.int16(1), jnp.int16(0)

        def body(kt, accs):
            x = ref[kt].reshape(t // rows16, rows16, t)
            accs = list(accs)
            for r in range(t // rows16):
                accs[r % n_acc] = accs[r % n_acc] + jnp.where(pred(x[r]), one, zero)
            return tuple(accs)
        accs = lax.fori_loop(0, n_kt, body, tuple(jnp.zeros((rows16, t), jnp.int16) for _ in range(n_acc)))
        tot = (accs[0] + accs[1]) + (accs[2] + accs[3])
        return jnp.sum(tot.astype(jnp.int32), axis=0, keepdims=True)

    def select16(ref, need):
        def bit_body(b, res):
            cand = res | (jnp.int32(1) << (HALF_BITS - 1 - b))
            cand16 = (cand - HALF_BIAS).astype(jnp.int16)
            cnt = count16(ref, lambda x: x >= cand16)
            return jnp.where(cnt >= need, cand, res)
        return lax.fori_loop(0, HALF_BITS, bit_body, jnp.zeros((1, t), jnp.int32)) - HALF_BIAS

    hi_thr = select16(hi_ref, topk)
    hi_thr16 = hi_thr.astype(jnp.int16)
    need_lo = topk - count16(hi_ref, lambda x: x > hi_thr16)

    def member_body(kt, c):
        lo_ref[kt] = jnp.where(hi_ref[kt] == hi_thr16, lo_ref[kt], jnp.int16(-HALF_BIAS))
        return c

    lax.fori_loop(0, n_kt, member_body, 0)
    lo_thr = select16(lo_ref, need_lo)

    def key_to_score(key):
        key = jnp.maximum(key, KEY_LOWEST)
        return lax.bitcast_convert_type(key ^ ((key >> 31) & 0x7FFFFFFF), F32)

    thr_ref[...] = key_to_score((hi_thr << HALF_BITS) | (lo_thr + HALF_BIAS))
    cnt_ref[...] = count(lambda x: x >= thr_ref[...])
    n_valid = ((i * t + lax.broadcasted_iota(jnp.int32, (1, t), 1)) // CHUNK + 1) * CHUNK
    few = n_valid <= topk

    @pl.when(jnp.max(jnp.where(few, 0.0, jnp.abs(cnt_ref[...] - topk))) > 0.0)
    def _():
        cnt_gt = count(lambda x: x > thr_ref[...])
        wrong = (cnt_ref[...] < topk) | (cnt_gt >= topk)

        @pl.when(jnp.max(jnp.where(few | jnp.logical_not(wrong), 0.0, 1.0)) > 0.0)
        def _():
            def bit_body(b, res):
                cand = res | (jnp.int32(1) << (31 - b))
                cand_score = key_to_score(cand ^ INT_MIN)
                cnt = count(lambda x: x >= cand_score)
                return jnp.where(cnt >= topk, cand, res)
            res = lax.fori_loop(0, 32, bit_body, jnp.zeros((1, t), jnp.int32))
            thr_ref[...] = key_to_score(res ^ INT_MIN)
            cnt_ref[...] = count(lambda x: x >= thr_ref[...])

    thr = thr_ref[...]
    has_ties = jnp.max(jnp.where(few, 0.0, cnt_ref[...])) > topk

    @pl.when(jnp.logical_not(has_ties))
    def _():
        def body(kt, c):
            score_ref[kt] = jnp.where(score_ref[kt] >= thr, 0.0, NEG)
            return c
        lax.fori_loop(0, n_kt, body, 0)

    @pl.when(has_ties)
    def _():
        need = topk - count(lambda x: x > thr)

        def body(kt, seen):
            x = score_ref[kt]
            eq = jnp.where(x == thr, 1.0, 0.0)
            rank = seen + jnp.dot(tri_ref[...], eq.astype(BF16), preferred_element_type=F32)
            sel = (x > thr) | ((x == thr) & (rank <= need))
            score_ref[kt] = jnp.where(sel, 0.0, NEG)
            return seen + jnp.sum(eq, axis=0, keepdims=True)
        lax.fori_loop(0, n_kt, body, jnp.zeros((1, t), F32))

    qs = [q_ref[:, (hd // 2) * LANES:(hd // 2 + 1) * LANES]
          * _lane_mask((hd % 2) * HEAD_DIM, (hd % 2 + 1) * HEAD_DIM, BF16) for hd in range(N_HEADS)]
    keys = lambda hd, rows: k_ref[rows, (hd // 2) * LANES:(hd // 2 + 1) * LANES]
    values = lambda hd, kt: vt_ref[kt, hd * HEAD_DIM:(hd + 1) * HEAD_DIM, :]
    outs = _flash_sweep(i, t, N_HEADS, qs, keys, values, state, lambda hd: bias_ref[hd, 1], lambda hd: bias_ref[hd, 0],
                        mask=lambda kt: score_ref[kt])
    _store_gated(o_ref, g_ref, outs)


def _dsa(p, bias, batch, seq, t):
    topk = min(TOPK_MAX, seq // 4)
    nq = seq // t
    qs, ks, vspec, ospec = _att_specs(t, seq, (256, 256, 128, 256), (256, 128))
    tri = jnp.asarray(np.tril(np.ones((t, t), np.float32)), BF16)
    return pl.pallas_call(
        functools.partial(_dsa_kernel, t=t, topk=topk),
        grid=(batch, nq),
        in_specs=[qs[0], ks[0], vspec, qs[1], ks[1], qs[2], qs[3],
                  pl.BlockSpec((N_HEADS, 2, t, t), lambda b, i: (0, 0, 0, 0), pipeline_mode=_ONE_BUFFER),
                  pl.BlockSpec((t, t), lambda b, i: (0, 0), pipeline_mode=_ONE_BUFFER)],
        out_specs=ospec,
        out_shape=jax.ShapeDtypeStruct((batch * seq, GROUP_WIDTH), BF16),
        scratch_shapes=[pltpu.VMEM((nq, t, t), F32), pltpu.VMEM((nq, t, t), jnp.int16),
                        pltpu.VMEM((nq, t, t), jnp.int16)]
        + _flash_scratch(N_HEADS, t) + [pltpu.VMEM((1, t), F32), pltpu.VMEM((1, t), F32)],
        compiler_params=_cparams(("parallel", "arbitrary")),
        name="dsa_attention",
    )(p["qc"], p["kc"], p["vct"], p["qi"], p["ki"], p["wi"], p["gc"], bias, tri)


def _outproj_kernel(x_ref, ya_ref, yb_ref, yc_ref, yd_ref, ud_ref, halo_ref, cw_ref, w_ref, fg_ref, o_ref,
                    *, tiles_per_seq, final):
    i = pl.program_id(0)
    tm = x_ref.shape[0]
    u = ud_ref[...]
    halo = jnp.where(i % tiles_per_seq == 0, 0.0, halo_ref[...])
    row8 = lax.broadcasted_iota(jnp.int32, (SUBLANES, 1), 0)

    def shifted(k):
        r = pltpu.roll(u, k, 0)
        head = jnp.where(row8 < k, pltpu.roll(halo, k, 0), r[0:SUBLANES])
        return jnp.concatenate([head, r[SUBLANES:tm]], axis=0)

    cw = cw_ref[...]
    conv = cw[0:1] * shifted(CONV_WIDTH - 1)
    for j in range(1, CONV_WIDTH):
        conv = conv + cw[j:j + 1] * (shifted(CONV_WIDTH - 1 - j) if j < CONV_WIDTH - 1 else u)
    yd = (conv * yd_ref[...].astype(F32)).astype(BF16)
    acc = x_ref[...]
    for g, y in enumerate((ya_ref[...], yb_ref[...], yc_ref[...], yd)):
        acc = acc + jnp.dot(y, w_ref[g * GROUP_WIDTH:(g + 1) * GROUP_WIDTH, :], preferred_element_type=F32)
    if final:
        acc = _rms(acc, fg_ref[...])
    o_ref[...] = acc


def _outproj(x2, l, ya, yb, yc, yd, ud, conv_w, w_out, final_g, seq, final):
    n = x2.shape[0]
    tm = min(PROJ_ROWS, seq)
    tiles_per_seq = seq // tm
    row = lambda w: pl.BlockSpec((tm, w), lambda i: (i, 0))
    halo = pl.BlockSpec((SUBLANES, GROUP_WIDTH), lambda i: (jnp.maximum(i * (tm // SUBLANES) - 1, 0), 0))
    return pl.pallas_call(
        functools.partial(_outproj_kernel, tiles_per_seq=tiles_per_seq, final=final),
        grid=(n // tm,),
        in_specs=[row(D_MODEL), row(256), row(256), row(256), row(256), row(256), halo,
                  _layer_spec(conv_w, l), _layer_spec(w_out, l), pl.BlockSpec(final_g.shape, lambda i: (0, 0))],
        out_specs=row(D_MODEL),
        out_shape=jax.ShapeDtypeStruct((n, D_MODEL), F32),
        compiler_params=_cparams(("parallel",)),
        name="outproj",
    )(x2, ya, yb, yc, yd, ud, ud, conv_w, w_out, final_g)


def kernel(x, norm_g, w_in, mla_qa_g, mla_w_uq, mla_kva_g, mla_w_ukv, diff_lambda, diff_subln_g, conv_w, w_out,
           rel_bias, final_g):
    batch, seq, _ = x.shape
    depth = w_in.shape[0]
    tm = min(PROJ_ROWS, seq)
    tiles = tuple(min(t, seq) for t in (MLA_TILE, DIFF_TILE, DSA_TILE))
    assert seq % tm == 0 and tm % VT_TILE == 0 and conv_w.shape[1] == CONV_WIDTH <= SUBLANES
    assert all(seq % t == 0 and t % CHUNK == 0 and tm % t == 0 and t % VT_TILE == 0 for t in tiles)

    cos_t, sin_t = (jnp.asarray(a) for a in _rope_tables(seq))
    mask_add = jnp.asarray(_position_constants(seq, tiles[0])[2])
    bias_diff = _bias_tiles(rel_bias[:, :N_DIFF_MAPS], seq, tiles[1])
    bias_dsa = _bias_tiles(rel_bias[:, N_DIFF_MAPS:], seq, tiles[2])

    w1, wvt = _prep_w_in(w_in)
    wq1, wq2, wk, wavt = _prep_mla_up(mla_w_uq, mla_w_ukv)
    row3 = lambda a: a[:, None, :]
    w_out_b = w_out.astype(BF16)

    x2 = x.reshape(batch * seq, D_MODEL)
    for l in range(depth):
        lambda_init = 0.8 - 0.6 * math.exp(-0.3 * l)
        p = _inproj(x2, l, row3(norm_g), w1, wvt, row3(mla_qa_g), wq1, wq2, row3(mla_kva_g), wk, wavt,
                    cos_t, sin_t, seq, tiles)
        ya = _mla(p, mask_add, batch, seq, tiles[0])
        yb = _diff(p, bias_diff, diff_lambda[l], diff_subln_g[l], lambda_init, batch, seq, tiles[1])
        yc = _dsa(p, bias_dsa, batch, seq, tiles[2])
        x2 = _outproj(x2, l, ya, yb, yc, p["yd"], p["ud"], conv_w, w_out_b, final_g[None, :],
                      seq, final=(l == depth - 1))
    return x2.reshape(batch, seq, D_MODEL)
```

```python
import functools
import math

import numpy as np
import jax
import jax.numpy as jnp
from jax import lax
from jax.experimental import pallas as pl
from jax.experimental.pallas import tpu as pltpu

F32 = jnp.float32
BF16 = jnp.bfloat16

D_MODEL = 1024
CHUNK = 64
N_HEADS = 4
HEAD_DIM = 64
GROUP_WIDTH = N_HEADS * HEAD_DIM
MLA_Q_LORA = 256
MLA_KV_LORA = 128
MLA_NOPE = 64
MLA_ROPE = 32
ROPE_BASE = 10000.0
DIFF_QK = 32
IDX_HEADS = 8
IDX_DIM = 32
TOPK_MAX = 256
CONV_WIDTH = 3
NUM_BUCKETS = 32
MAX_DISTANCE = 128
N_DIFF_MAPS = 2 * N_HEADS
NORM_EPS = 1e-6
NEG = -1e30
LOG2E = math.log2(math.e)
INT_MIN = -2 ** 31
KEY_LOWEST = int(np.float32(-np.finfo(np.float32).max).view(np.int32)) ^ 0x7FFFFFFF
HALF_BITS = 16
HALF_MASK = 2 ** HALF_BITS - 1
HALF_BIAS = 2 ** (HALF_BITS - 1)

LANES = 128
SUBLANES = 8
MLA_TILE = 256
DIFF_TILE = 256
DSA_TILE = 512
VT_TILE = 256
FAR_UNIT_KEYS = 512
PROJ_ROWS = 512
FLASH_LOOKAHEAD = 4
V7X_VMEM_BYTES = 64 * 1024 * 1024
VMEM_LIMIT = V7X_VMEM_BYTES * 7 // 8

_IN_SPLITS = (
    ("a_cq", 256), ("a_ckv", 128), ("a_krope", 32), ("a_gate", 256),
    ("b_q", 256), ("b_k", 256), ("b_v", 256), ("b_gate", 256),
    ("c_q", 256), ("c_k", 256), ("c_v", 256),
    ("c_qidx", 256), ("c_kidx", 32), ("c_widx", 8), ("c_gate", 256),
    ("d_b", 256), ("d_c", 256), ("d_h", 256), ("d_gate", 256),
)

_SEGS = (
    ("cq", 256), ("ckv", 128), ("kr1", 128), ("kr2", 128), ("c_kidx4", 128), ("a_gate", 256),
    ("b_q", 256), ("b_k", 256), ("b_gate", 256),
    ("c_q", 256), ("c_k", 256), ("c_qidx", 256),
    ("c_gate", 256), ("d_b", 256), ("d_c", 256), ("d_h", 256), ("d_gate", 256), ("c_widx", 128),
)
_SEG_OFF = {}
_off = 0
for _name, _w in _SEGS:
    _SEG_OFF[_name] = (_off, _w)
    _off += _w
W1_COLS = _off

_NT = (((1,), (1,)), ((), ()))
_ONE_BUFFER = pl.Buffered(1)


def _cparams(sem):
    return pltpu.CompilerParams(dimension_semantics=sem, vmem_limit_bytes=VMEM_LIMIT)


def _split_cols(w):
    out, off = {}, 0
    for name, width in _IN_SPLITS:
        out[name] = w[..., off:off + width]
        off += width
    return out


def _rot_half_cols(w):
    half = MLA_ROPE // 2
    return jnp.concatenate([-w[..., half:], w[..., :half]], axis=-1)


def _prep_w_in(w):
    p = _split_cols(w)
    z = lambda n: jnp.zeros(w.shape[:-1] + (n,), w.dtype)
    cat = lambda parts: jnp.concatenate(parts, axis=-1)
    segs = {
        "cq": p["a_cq"], "ckv": p["a_ckv"], "a_gate": p["a_gate"],
        "kr1": cat([z(MLA_NOPE), p["a_krope"], z(LANES - MLA_NOPE - MLA_ROPE)]),
        "kr2": cat([z(MLA_NOPE), _rot_half_cols(p["a_krope"]), z(LANES - MLA_NOPE - MLA_ROPE)]),
        "b_q": p["b_q"] * (DIFF_QK ** -0.5 * LOG2E), "b_k": p["b_k"], "b_gate": p["b_gate"],
        "c_q": p["c_q"] * (HEAD_DIM ** -0.5 * LOG2E), "c_k": p["c_k"], "c_qidx": p["c_qidx"],
        "c_kidx4": cat([p["c_kidx"]] * (LANES // IDX_DIM)),
        "c_widx": cat([p["c_widx"], z(LANES - IDX_HEADS)]),
        "c_gate": p["c_gate"], "d_b": p["d_b"], "d_c": p["d_c"], "d_h": p["d_h"], "d_gate": p["d_gate"],
    }
    w1 = cat([segs[name].astype(BF16) for name, _ in _SEGS])
    wvt = jnp.swapaxes(cat([p["b_v"].astype(BF16), p["c_v"].astype(BF16)]), -1, -2)
    return w1, wvt


def _prep_mla_up(w_uq, w_ukv):
    depth = w_uq.shape[0]
    scale = (MLA_NOPE + MLA_ROPE) ** -0.5 * LOG2E
    wq = w_uq.reshape(depth, MLA_Q_LORA, N_HEADS, MLA_NOPE + MLA_ROPE) * scale
    zq = jnp.zeros((depth, MLA_Q_LORA, N_HEADS, LANES - MLA_NOPE - MLA_ROPE), w_uq.dtype)
    wq1 = jnp.concatenate([wq, zq], axis=-1).reshape(depth, MLA_Q_LORA, N_HEADS * LANES)
    wq2 = jnp.concatenate([jnp.zeros_like(wq[..., :MLA_NOPE]), _rot_half_cols(wq[..., MLA_NOPE:]), zq],
                          axis=-1).reshape(depth, MLA_Q_LORA, N_HEADS * LANES)
    wkv = w_ukv.reshape(depth, MLA_KV_LORA, N_HEADS, MLA_NOPE + HEAD_DIM)
    zk = jnp.zeros((depth, MLA_KV_LORA, N_HEADS, LANES - MLA_NOPE), w_ukv.dtype)
    wk = jnp.concatenate([wkv[..., :MLA_NOPE], zk], axis=-1).reshape(depth, MLA_KV_LORA, N_HEADS * LANES)
    wvt = jnp.swapaxes(wkv[..., MLA_NOPE:].reshape(depth, MLA_KV_LORA, N_HEADS * HEAD_DIM), -1, -2)
    return wq1.astype(BF16), wq2.astype(BF16), wk.astype(BF16), wvt.astype(BF16)


def _rope_tables(seq):
    half = MLA_ROPE // 2
    inv_freq = ROPE_BASE ** (-np.arange(half, dtype=np.float32) / half)
    ang = np.arange(seq, dtype=np.float32)[:, None] * inv_freq[None, :].astype(np.float32)
    cos, sin = np.cos(ang).astype(np.float32), np.sin(ang).astype(np.float32)
    ct = np.zeros((seq, LANES), np.float32)
    st = np.zeros((seq, LANES), np.float32)
    ct[:, :MLA_NOPE] = 1.0
    ct[:, MLA_NOPE:MLA_NOPE + half] = cos
    ct[:, MLA_NOPE + half:MLA_NOPE + MLA_ROPE] = cos
    st[:, MLA_NOPE:MLA_NOPE + half] = sin
    st[:, MLA_NOPE + half:MLA_NOPE + MLA_ROPE] = sin
    return ct, st


def _rel_bucket_np(rel):
    nb = NUM_BUCKETS // 2
    max_exact = nb // 2
    ret = np.where(rel > 0, nb, 0)
    n = np.abs(rel)
    nf = np.maximum(n, max_exact).astype(np.float32)
    large = max_exact + (np.log(nf / np.float32(max_exact)) / np.float32(math.log(MAX_DISTANCE / max_exact))
                         * np.float32(nb - max_exact)).astype(np.int32)
    large = np.minimum(large, nb - 1)
    return (ret + np.where(n < max_exact, n, large)).astype(np.int32)


def _position_constants(seq, tile):
    r = np.arange(tile)
    rel0 = r[:, None] - r[None, :]
    bidx = np.stack([_rel_bucket_np(rel0 - tile), _rel_bucket_np(rel0)]).astype(np.int32)
    far = _rel_bucket_np(np.arange(-(seq - 1), -tile))
    far_bucket = int(far[0]) if far.size else int(_rel_bucket_np(np.array([-tile - 1]))[0])
    assert far.size == 0 or np.all(far == far_bucket)
    mask_add = np.where((r[:, None] // CHUNK) <= (r[None, :] // CHUNK), 0.0, NEG).astype(np.float32)
    return bidx, far_bucket, mask_add


def _bias_kernel(tab_ref, bidx_ref, madd_ref, o_ref, *, far_bucket, buckets):
    j = pl.program_id(0)
    c = tab_ref[j, far_bucket]
    for d in range(2):
        idx = bidx_ref[d]
        acc = jnp.zeros(idx.shape, F32)
        for b in buckets[d]:
            acc = jnp.where(idx == b, (tab_ref[j, b] - c) * LOG2E, acc)
        if d == 1:
            acc = acc + madd_ref[...]
        o_ref[0, d] = acc


def _bias_tiles(table, seq, tile):
    bidx, far_bucket, mask_add = _position_constants(seq, tile)
    n_maps = table.shape[1]
    buckets = tuple(tuple(int(b) for b in np.unique(bidx[d])) for d in range(2))
    return pl.pallas_call(
        functools.partial(_bias_kernel, far_bucket=far_bucket, buckets=buckets),
        grid=(n_maps,),
        in_specs=[
            pl.BlockSpec(memory_space=pltpu.SMEM),
            pl.BlockSpec((2, tile, tile), lambda j: (0, 0, 0)),
            pl.BlockSpec((tile, tile), lambda j: (0, 0)),
        ],
        out_specs=pl.BlockSpec((1, 2, tile, tile), lambda j: (j, 0, 0, 0)),
        out_shape=jax.ShapeDtypeStruct((n_maps, 2, tile, tile), F32),
        compiler_params=_cparams(("arbitrary",)),
        name="bias_tiles",
    )(table.T.astype(F32), jnp.asarray(bidx), jnp.asarray(mask_add))


def _silu(x):
    return x * (1.0 / (1.0 + jnp.exp(-x)))


def _rms(x, g):
    return x * lax.rsqrt(jnp.mean(x * x, axis=-1, keepdims=True) + NORM_EPS) * g


def _inproj_kernel(x_ref, g_ref, w_ref, wvt_ref, qag_ref, wq1_ref, wq2_ref, kvg_ref, wk_ref, wavt_ref, cos_ref, sin_ref,
                   qa_ref, ka_ref, vat_ref, ga_ref, qb_ref, kb_ref, vbt_ref, gb_ref,
                   qc_ref, kc_ref, vct_ref, qi_ref, ki_ref, wi_ref, gc_ref, yd_ref, ud_ref):
    h = _rms(x_ref[...], g_ref[...]).astype(BF16)

    def seg(name):
        off, width = _SEG_OFF[name]
        return jnp.dot(h, w_ref[:, off:off + width], preferred_element_type=F32)

    def seg_pair(first, second):
        off, width = _SEG_OFF[first]
        assert _SEG_OFF[second] == (off + width, width) and width == LANES
        y = jnp.dot(h, w_ref[:, off:off + 2 * width], preferred_element_type=F32)
        return y[:, :width], y[:, width:]

    cos = cos_ref[...]
    sin = sin_ref[...]

    cqn = _rms(seg("cq"), qag_ref[...]).astype(BF16)
    q1 = jnp.dot(cqn, wq1_ref[...], preferred_element_type=F32)
    q2 = jnp.dot(cqn, wq2_ref[...], preferred_element_type=F32)
    ckv, kr1 = seg_pair("ckv", "kr1")
    kr2, kidx4 = seg_pair("kr2", "c_kidx4")
    kr = kr1 * cos + kr2 * sin
    ckvn = _rms(ckv, kvg_ref[...]).astype(BF16)
    kn = jnp.dot(ckvn, wk_ref[...], preferred_element_type=F32)
    for hd in range(N_HEADS):
        sl = slice(hd * LANES, (hd + 1) * LANES)
        qa_ref[:, sl] = (q1[:, sl] * cos + q2[:, sl] * sin).astype(BF16)
        ka_ref[:, sl] = (kn[:, sl] + kr).astype(BF16)
    ga_ref[...] = _silu(seg("a_gate")).astype(BF16)

    def put(ref, first_token, vt):
        tile = ref.shape[2]
        ref[first_token // tile, :, first_token % tile:first_token % tile + VT_TILE] = vt

    for c in range(x_ref.shape[0] // VT_TILE):
        rows = slice(c * VT_TILE, (c + 1) * VT_TILE)
        put(vat_ref, c * VT_TILE,
            lax.dot_general(wavt_ref[...], ckvn[rows], _NT, preferred_element_type=F32).astype(BF16))
        vt = lax.dot_general(wvt_ref[...], h[rows], _NT, preferred_element_type=F32).astype(BF16)
        put(vbt_ref, c * VT_TILE, vt[:GROUP_WIDTH])
        put(vct_ref, c * VT_TILE, vt[GROUP_WIDTH:])

    qb_ref[...] = seg("b_q").astype(BF16)
    kb_ref[...] = seg("b_k").astype(BF16)
    gb_ref[...] = _silu(seg("b_gate")).astype(BF16)

    qc_ref[...] = seg("c_q").astype(BF16)
    kc_ref[...] = seg("c_k").astype(BF16)
    qi_ref[...] = seg("c_qidx").astype(BF16)
    ki_ref[...] = kidx4.astype(BF16)
    wi_ref[...] = seg("c_widx")
    gc_ref[...] = _silu(seg("c_gate")).astype(BF16)

    yd_ref[...] = (seg("d_b") * _silu(seg("d_gate"))).astype(BF16)
    ud_ref[...] = seg("d_c") * seg("d_h")


def _layer_spec(a, l):
    return pl.BlockSpec((None,) + a.shape[1:], lambda i: (l,) + (0,) * (a.ndim - 1))


def _inproj(x2, l, g, w1, wvt, qag, wq1, wq2, kvg, wk, wavt, cos_t, sin_t, seq, tiles):
    n = x2.shape[0]
    tm = min(PROJ_ROWS, seq)
    tiles_per_seq = seq // tm
    row = lambda w: pl.BlockSpec((tm, w), lambda i: (i, 0))
    full = lambda a: _layer_spec(a, l)
    tab = pl.BlockSpec((tm, LANES), lambda i: (i % tiles_per_seq, 0))
    vt_spec = lambda t: pl.BlockSpec((tm // t, GROUP_WIDTH, t), lambda i: (i, 0, 0))
    vt_shape = lambda t: jax.ShapeDtypeStruct((n // t, GROUP_WIDTH, t), BF16)
    outs = (("qa", 512, BF16), ("ka", 512, BF16), ("vat", None, tiles[0]), ("ga", 256, BF16),
            ("qb", 256, BF16), ("kb", 256, BF16), ("vbt", None, tiles[1]), ("gb", 256, BF16),
            ("qc", 256, BF16), ("kc", 256, BF16), ("vct", None, tiles[2]), ("qi", 256, BF16),
            ("ki", 128, BF16), ("wi", 128, F32), ("gc", 256, BF16), ("yd", 256, BF16), ("ud", 256, F32))
    res = pl.pallas_call(
        _inproj_kernel,
        grid=(n // tm,),
        in_specs=[row(D_MODEL), full(g), full(w1), full(wvt), full(qag), full(wq1), full(wq2), full(kvg), full(wk),
                  full(wavt), tab, tab],
        out_specs=[vt_spec(dt) if w is None else row(w) for _, w, dt in outs],
        out_shape=[vt_shape(dt) if w is None else jax.ShapeDtypeStruct((n, w), dt) for _, w, dt in outs],
        compiler_params=_cparams(("parallel",)),
        name="inproj",
    )(x2, g, w1, wvt, qag, wq1, wq2, kvg, wk, wavt, cos_t, sin_t)
    return {name: o for (name, _, _), o in zip(outs, res)}


def _flash_scratch(n_maps, t):
    unit = _far_unit_tiles(t) * t
    return [pltpu.VMEM((n_maps, 1, t), F32), pltpu.VMEM((n_maps, 1, t), F32), pltpu.VMEM((n_maps, HEAD_DIM, t), F32),
            pltpu.VMEM((n_maps, unit, t), F32), pltpu.VMEM((n_maps, unit, t), F32)]


def _far_unit_tiles(t):
    return max(1, FAR_UNIT_KEYS // t)


def _flash_update(s, vts, old):
    t = vts[0].shape[1]
    m_tile = jnp.max(s, axis=0, keepdims=True)
    m_new = m_tile if old is None else jnp.maximum(old[0], m_tile)
    p = jnp.exp2(s - m_new)
    l_tile = jnp.sum(p, axis=0, keepdims=True)
    pb = p.astype(BF16)
    pv = jnp.dot(vts[0], pb[0:t], preferred_element_type=F32)
    for d in range(1, len(vts)):
        pv = pv + jnp.dot(vts[d], pb[d * t:(d + 1) * t], preferred_element_type=F32)
    if old is None:
        return m_new, l_tile, pv
    alpha = jnp.exp2(old[0] - m_new)
    return m_new, alpha * old[1] + l_tile, alpha * old[2] + pv


def _flash_sweep(i, t, n_maps, qs, keys, values, state, diag_add, near_add=None, mask=None):
    m_ref, l_ref, acc_ref, buf_a, buf_b = state
    n_far = i if near_add is None else jnp.maximum(i - 1, 0)
    ut = _far_unit_tiles(t)
    n_units = n_far // ut

    def load_state():
        return [(m_ref[j], l_ref[j], acc_ref[j]) for j in range(n_maps)]

    def store_state(cur):
        for j in range(n_maps):
            m_ref[j], l_ref[j], acc_ref[j] = cur[j]

    def tile_logits(j, kt, n_tiles, shared, extra=None):
        rows = pl.ds(pl.multiple_of(kt * t, t), n_tiles * t)
        s = lax.dot_general(keys(j, rows), qs[j], _NT, preferred_element_type=F32)
        add = shared
        if extra is not None:
            add = extra(j) if add is None else add + extra(j)
        return s if add is None else s + add

    def shared_mask(kt, n_tiles):
        if mask is None:
            return None
        tiles = [mask(kt + d) for d in range(n_tiles)]
        return tiles[0] if n_tiles == 1 else jnp.concatenate(tiles, axis=0)

    def run(units, first=False, prefetch=False):
        cur = [None] * n_maps if first else load_state()
        shared = [shared_mask(kt, n_tiles) for kt, n_tiles, _ in units]
        items = [(u, j) for u in range(len(units)) for j in range(n_maps)]
        if prefetch:
            items += [(None, j) for j in range(n_maps)]
            shared.append(shared_mask(0, ut))

        def logits(u, j):
            if u is None:
                return tile_logits(j, 0, ut, shared[-1])
            kt, n_tiles, extra = units[u]
            return tile_logits(j, kt, n_tiles, shared[u], extra)

        queue = [logits(*it) for it in items[:FLASH_LOOKAHEAD]]
        for n, (u, j) in enumerate(items):
            if n + FLASH_LOOKAHEAD < len(items):
                queue.append(logits(*items[n + FLASH_LOOKAHEAD]))
            s = queue.pop(0)
            if u is None:
                buf_a[j] = s
            else:
                kt, n_tiles, _ = units[u]
                cur[j] = _flash_update(s, [values(j, kt + d) for d in range(n_tiles)], cur[j])
        store_state(cur)

    def first_run(prefetch):
        if near_add is None:
            return lambda: run([(i, 1, diag_add)], first=True, prefetch=prefetch)
        return lambda: run([(i, 1, diag_add), (i - 1, 1, near_add)], first=True, prefetch=prefetch)

    if near_add is None:
        pl.when(n_units == 0)(first_run(False))
    else:
        pl.when(i == 0)(lambda: run([(i, 1, diag_add)], first=True))
        pl.when((i >= 1) & (n_units == 0))(first_run(False))
    pl.when(n_units >= 1)(first_run(True))

    def consume(buf, kt, j, old):
        return _flash_update(buf[j], [values(j, kt + d) for d in range(ut)], old)

    def overlapped(cur, buf_in, buf_out, u):
        kt_next = ut * (u + 1)
        shared = shared_mask(kt_next, ut)
        for j in range(n_maps):
            buf_out[j] = tile_logits(j, kt_next, ut, shared)
            cur[j] = consume(buf_in, ut * u, j, cur[j])

    def far_body(c, carry):
        cur = load_state()
        overlapped(cur, buf_a, buf_b, 2 * c)
        overlapped(cur, buf_b, buf_a, 2 * c + 1)
        store_state(cur)
        return carry

    lax.fori_loop(0, (n_units - 1) // 2, far_body, 0)

    @pl.when(n_units % 2 == 1)
    def _():
        cur = load_state()
        for j in range(n_maps):
            cur[j] = consume(buf_a, ut * (n_units - 1), j, cur[j])
        store_state(cur)

    @pl.when((n_units >= 2) & (n_units % 2 == 0))
    def _():
        cur = load_state()
        overlapped(cur, buf_a, buf_b, n_units - 2)
        for j in range(n_maps):
            cur[j] = consume(buf_b, ut * (n_units - 1), j, cur[j])
        store_state(cur)

    for r in range(1, ut):
        pl.when(n_far % ut >= r)(lambda r=r: run([(n_far - r, 1, None)]))
    return [acc_ref[j] / l_ref[j] for j in range(n_maps)]


def _lane_mask(lo, hi, dtype):
    lane = lax.broadcasted_iota(jnp.int32, (1, LANES), 1)
    return ((lane >= lo) & (lane < hi)).astype(dtype)


def _store_gated(o_ref, g_ref, heads_t):
    o = jnp.concatenate(heads_t, axis=0).T
    o_ref[...] = (o * g_ref[...].astype(F32)).astype(BF16)


def _att_specs(t, seq, widths_q, widths_k):
    nq = seq // t
    qspec = lambda w: pl.BlockSpec((t, w), lambda b, i: (b * nq + i, 0))
    kspec = lambda w: pl.BlockSpec((seq, w), lambda b, i: (b, 0), pipeline_mode=_ONE_BUFFER)
    vspec = pl.BlockSpec((nq, GROUP_WIDTH, t), lambda b, i: (b, 0, 0), pipeline_mode=_ONE_BUFFER)
    return [qspec(w) for w in widths_q], [kspec(w) for w in widths_k], vspec, qspec(GROUP_WIDTH)


def _mla_kernel(q_ref, k_ref, vt_ref, g_ref, madd_ref, o_ref, *state, t):
    i = pl.program_id(1)
    qs = [q_ref[:, hd * LANES:(hd + 1) * LANES] for hd in range(N_HEADS)]
    keys = lambda hd, rows: k_ref[rows, hd * LANES:(hd + 1) * LANES]
    values = lambda hd, kt: vt_ref[kt, hd * HEAD_DIM:(hd + 1) * HEAD_DIM, :]
    outs = _flash_sweep(i, t, N_HEADS, qs, keys, values, state, lambda hd: madd_ref[...])
    _store_gated(o_ref, g_ref, outs)


def _mla(p, mask_add, batch, seq, t):
    qs, ks, vspec, ospec = _att_specs(t, seq, (512, 256), (512,))
    return pl.pallas_call(
        functools.partial(_mla_kernel, t=t),
        grid=(batch, seq // t),
        in_specs=[qs[0], ks[0], vspec, qs[1], pl.BlockSpec((t, t), lambda b, i: (0, 0), pipeline_mode=_ONE_BUFFER)],
        out_specs=ospec,
        out_shape=jax.ShapeDtypeStruct((batch * seq, GROUP_WIDTH), BF16),
        scratch_shapes=_flash_scratch(N_HEADS, t),
        compiler_params=_cparams(("parallel", "arbitrary")),
        name="mla_attention",
    )(p["qa"], p["ka"], p["vat"], p["ga"], mask_add)


def _diff_kernel(q_ref, k_ref, vt_ref, g_ref, bias_ref, lam_ref, subg_ref, o_ref, *state, t, lambda_init):
    i = pl.program_id(1)
    lp = lam_ref[...]
    lam = (jnp.exp(jnp.sum(lp[0:1] * lp[1:2], axis=-1, keepdims=True))
           - jnp.exp(jnp.sum(lp[2:3] * lp[3:4], axis=-1, keepdims=True)) + lambda_init)
    qs = [q_ref[:, (j // 4) * LANES:(j // 4 + 1) * LANES] * _lane_mask((j % 4) * DIFF_QK, (j % 4 + 1) * DIFF_QK, BF16)
          for j in range(N_DIFF_MAPS)]
    keys = lambda j, rows: k_ref[rows, (j // 4) * LANES:(j // 4 + 1) * LANES]
    values = lambda j, kt: vt_ref[kt, (j // 2) * HEAD_DIM:(j // 2 + 1) * HEAD_DIM, :]
    maps = _flash_sweep(i, t, N_DIFF_MAPS, qs, keys, values, state, lambda j: bias_ref[j, 1], lambda j: bias_ref[j, 0])
    outs = []
    for hd in range(N_HEADS):
        o = maps[2 * hd] - lam * maps[2 * hd + 1]
        ms = jnp.mean(o * o, axis=0, keepdims=True)
        outs.append(o * lax.rsqrt(ms + NORM_EPS) * subg_ref[...] * (1.0 - lambda_init))
    _store_gated(o_ref, g_ref, outs)


def _diff(p, bias, lam_params, subln_g, lambda_init, batch, seq, t):
    qs, ks, vspec, ospec = _att_specs(t, seq, (256, 256), (256,))
    subg = jnp.broadcast_to(subln_g.astype(F32)[:, None], (HEAD_DIM, t))
    return pl.pallas_call(
        functools.partial(_diff_kernel, t=t, lambda_init=lambda_init),
        grid=(batch, seq // t),
        in_specs=[qs[0], ks[0], vspec, qs[1],
                  pl.BlockSpec((N_DIFF_MAPS, 2, t, t), lambda b, i: (0, 0, 0, 0), pipeline_mode=_ONE_BUFFER),
                  pl.BlockSpec((4, DIFF_QK), lambda b, i: (0, 0)),
                  pl.BlockSpec((HEAD_DIM, t), lambda b, i: (0, 0))],
        out_specs=ospec,
        out_shape=jax.ShapeDtypeStruct((batch * seq, GROUP_WIDTH), BF16),
        scratch_shapes=_flash_scratch(N_DIFF_MAPS, t),
        compiler_params=_cparams(("parallel", "arbitrary")),
        name="diff_attention",
    )(p["qb"], p["kb"], p["vbt"], p["gb"], bias, lam_params.astype(F32), subg)


def _sort_key(x):
    b = lax.bitcast_convert_type(x + 0.0, jnp.int32)
    return b ^ ((b >> 31) & 0x7FFFFFFF)


def _dsa_kernel(q_ref, k_ref, vt_ref, qi_ref, ki_ref, wi_ref, g_ref, bias_ref, tri_ref, o_ref,
                score_ref, hi_ref, lo_ref, thr_ref, cnt_ref, *state, t, topk):
    i = pl.program_id(1)
    n_kt = i + 1
    idx_scale = (IDX_HEADS ** -0.5) * (IDX_DIM ** -0.5)

    wit = wi_ref[...].T * idx_scale
    qm, wrow = [], []
    for ih in range(IDX_HEADS):
        grp = slice((ih // 4) * LANES, (ih // 4 + 1) * LANES)
        lo = (ih % 4) * IDX_DIM
        qm.append(qi_ref[:, grp] * _lane_mask(lo, lo + IDX_DIM, BF16))
        wrow.append(wit[ih:ih + 1, :])

    def idx_tile(kt):
        kk = ki_ref[pl.ds(pl.multiple_of(kt * t, t), t), :]
        dot = lambda ih: lax.dot_general(kk, qm[ih], _NT, preferred_element_type=F32)
        queue = [dot(ih) for ih in range(FLASH_LOOKAHEAD)]
        tot = None
        for ih in range(IDX_HEADS):
            if ih + FLASH_LOOKAHEAD < IDX_HEADS:
                queue.append(dot(ih + FLASH_LOOKAHEAD))
            term = wrow[ih] * jnp.maximum(queue.pop(0), 0.0)
            tot = term if tot is None else tot + term
        return tot + 0.0

    def store_scores(kt, score):
        score_ref[kt] = score
        key = _sort_key(score)
        hi_ref[kt] = (key >> HALF_BITS).astype(jnp.int16)
        lo_ref[kt] = ((key & HALF_MASK) - HALF_BIAS).astype(jnp.int16)

    def key_tiles(first, n):
        for d in range(n):
            store_scores(first + d, idx_tile(first + d))

    def key_quad_body(c, carry):
        key_tiles(4 * c, 4)
        return carry

    lax.fori_loop(0, i // 4, key_quad_body, 0)
    pl.when(i % 4 >= 2)(lambda: key_tiles(i - i % 4, 2))
    pl.when(i % 2 == 1)(lambda: key_tiles(i - 1, 1))
    kchunk = lax.broadcasted_iota(jnp.int32, (t, t), 0) // CHUNK
    qchunk = lax.broadcasted_iota(jnp.int32, (t, t), 1) // CHUNK
    store_scores(i, jnp.where(kchunk <= qchunk, idx_tile(i), -jnp.inf))

    def count(pred):
        def body(kt, c):
            hit = jnp.where(pred(score_ref[kt]), 1.0, 0.0)
            return c + jnp.sum(hit.reshape(t // SUBLANES, SUBLANES, t), axis=0)
        part = lax.fori_loop(0, n_kt, body, jnp.zeros((SUBLANES, t), F32))
        return jnp.sum(part, axis=0, keepdims=True)

    rows16 = 2 * SUBLANES
    n_acc = 4

    def key_to_score(key):
        key = jnp.maximum(key, KEY_LOWEST)
        return lax.bitcast_convert_type(key ^ ((key >> 31) & 0x7FFFFFFF), F32)

    def packed_search(n):
        def count16(ref, pred):
            one, zero = jnp.int16(1), jnp.int16(0)
            accs = [jnp.zeros((rows16, t), jnp.int16) for _ in range(n_acc)]
            for kt in range(n):
                x = ref[kt].reshape(t // rows16, rows16, t)
                for r in range(t // rows16):
                    accs[r % n_acc] = accs[r % n_acc] + jnp.where(pred(x[r]), one, zero)
            tot = (accs[0] + accs[1]) + (accs[2] + accs[3])
            return jnp.sum(tot.astype(jnp.int32), axis=0, keepdims=True)

        def select16(ref, need):
            def bit_body(b, res):
                cand = res | (jnp.int32(1) << (HALF_BITS - 1 - b))
                cand16 = (cand - HALF_BIAS).astype(jnp.int16)
                cnt = count16(ref, lambda x: x >= cand16)
                return jnp.where(cnt >= need, cand, res)
            return lax.fori_loop(0, HALF_BITS, bit_body, jnp.zeros((1, t), jnp.int32)) - HALF_BIAS

        hi_thr = select16(hi_ref, topk)
        hi_thr16 = hi_thr.astype(jnp.int16)
        need_lo = topk - count16(hi_ref, lambda x: x > hi_thr16)
        for kt in range(n):
            lo_ref[kt] = jnp.where(hi_ref[kt] == hi_thr16, lo_ref[kt], jnp.int16(-HALF_BIAS))
        lo_thr = select16(lo_ref, need_lo)
        thr_ref[...] = key_to_score((hi_thr << HALF_BITS) | (lo_thr + HALF_BIAS))

    for n in range(1, score_ref.shape[0] + 1):
        pl.when(n_kt == n)(functools.partial(packed_search, n))

    cnt_ref[...] = count(lambda x: x >= thr_ref[...])
    n_valid = ((i * t + lax.broadcasted_iota(jnp.int32, (1, t), 1)) // CHUNK + 1) * CHUNK
    few = n_valid <= topk

    @pl.when(jnp.max(jnp.where(few, 0.0, jnp.abs(cnt_ref[...] - topk))) > 0.0)
    def _():
        cnt_gt = count(lambda x: x > thr_ref[...])
        wrong = (cnt_ref[...] < topk) | (cnt_gt >= topk)

        @pl.when(jnp.max(jnp.where(few | jnp.logical_not(wrong), 0.0, 1.0)) > 0.0)
        def _():
            def bit_body(b, res):
                cand = res | (jnp.int32(1) << (31 - b))
                cand_score = key_to_score(cand ^ INT_MIN)
                cnt = count(lambda x: x >= cand_score)
                return jnp.where(cnt >= topk, cand, res)
            res = lax.fori_loop(0, 32, bit_body, jnp.zeros((1, t), jnp.int32))
            thr_ref[...] = key_to_score(res ^ INT_MIN)
            cnt_ref[...] = count(lambda x: x >= thr_ref[...])

    thr = thr_ref[...]
    has_ties = jnp.max(jnp.where(few, 0.0, cnt_ref[...])) > topk

    @pl.when(jnp.logical_not(has_ties))
    def _():
        def body(kt, c):
            score_ref[kt] = jnp.where(score_ref[kt] >= thr, 0.0, NEG)
            return c
        lax.fori_loop(0, n_kt, body, 0)

    @pl.when(has_ties)
    def _():
        need = topk - count(lambda x: x > thr)

        def body(kt, seen):
            x = score_ref[kt]
            eq = jnp.where(x == thr, 1.0, 0.0)
            rank = seen + jnp.dot(tri_ref[...], eq.astype(BF16), preferred_element_type=F32)
            sel = (x > thr) | ((x == thr) & (rank <= need))
            score_ref[kt] = jnp.where(sel, 0.0, NEG)
            return seen + jnp.sum(eq, axis=0, keepdims=True)
        lax.fori_loop(0, n_kt, body, jnp.zeros((1, t), F32))

    qs = [q_ref[:, (hd // 2) * LANES:(hd // 2 + 1) * LANES]
          * _lane_mask((hd % 2) * HEAD_DIM, (hd % 2 + 1) * HEAD_DIM, BF16) for hd in range(N_HEADS)]
    keys = lambda hd, rows: k_ref[rows, (hd // 2) * LANES:(hd // 2 + 1) * LANES]
    values = lambda hd, kt: vt_ref[kt, hd * HEAD_DIM:(hd + 1) * HEAD_DIM, :]
    outs = _flash_sweep(i, t, N_HEADS, qs, keys, values, state, lambda hd: bias_ref[hd, 1], lambda hd: bias_ref[hd, 0],
                        mask=lambda kt: score_ref[kt])
    _store_gated(o_ref, g_ref, outs)


def _dsa(p, bias, batch, seq, t):
    topk = min(TOPK_MAX, seq // 4)
    nq = seq // t
    qs, ks, vspec, ospec = _att_specs(t, seq, (256, 256, 128, 256), (256, 128))
    tri = jnp.asarray(np.tril(np.ones((t, t), np.float32)), BF16)
    return pl.pallas_call(
        functools.partial(_dsa_kernel, t=t, topk=topk),
        grid=(batch, nq),
        in_specs=[qs[0], ks[0], vspec, qs[1], ks[1], qs[2], qs[3],
                  pl.BlockSpec((N_HEADS, 2, t, t), lambda b, i: (0, 0, 0, 0), pipeline_mode=_ONE_BUFFER),
                  pl.BlockSpec((t, t), lambda b, i: (0, 0), pipeline_mode=_ONE_BUFFER)],
        out_specs=ospec,
        out_shape=jax.ShapeDtypeStruct((batch * seq, GROUP_WIDTH), BF16),
        scratch_shapes=[pltpu.VMEM((nq, t, t), F32), pltpu.VMEM((nq, t, t), jnp.int16),
                        pltpu.VMEM((nq, t, t), jnp.int16), pltpu.VMEM((1, t), F32), pltpu.VMEM((1, t), F32)]
        + _flash_scratch(N_HEADS, t),
        compiler_params=_cparams(("parallel", "arbitrary")),
        name="dsa_attention",
    )(p["qc"], p["kc"], p["vct"], p["qi"], p["ki"], p["wi"], p["gc"], bias, tri)


def _outproj_kernel(x_ref, ya_ref, yb_ref, yc_ref, yd_ref, ud_ref, halo_ref, cw_ref, w_ref, fg_ref, o_ref,
                    *, tiles_per_seq, final):
    i = pl.program_id(0)
    tm = x_ref.shape[0]
    u = ud_ref[...]
    halo = jnp.where(i % tiles_per_seq == 0, 0.0, halo_ref[...])
    row8 = lax.broadcasted_iota(jnp.int32, (SUBLANES, 1), 0)

    def shifted(k):
        r = pltpu.roll(u, k, 0)
        head = jnp.where(row8 < k, pltpu.roll(halo, k, 0), r[0:SUBLANES])
        return jnp.concatenate([head, r[SUBLANES:tm]], axis=0)

    cw = cw_ref[...]
    conv = cw[0:1] * shifted(CONV_WIDTH - 1)
    for j in range(1, CONV_WIDTH):
        conv = conv + cw[j:j + 1] * (shifted(CONV_WIDTH - 1 - j) if j < CONV_WIDTH - 1 else u)
    yd = (conv * yd_ref[...].astype(F32)).astype(BF16)
    acc = x_ref[...]
    for g, y in enumerate((ya_ref[...], yb_ref[...], yc_ref[...], yd)):
        acc = acc + jnp.dot(y, w_ref[g * GROUP_WIDTH:(g + 1) * GROUP_WIDTH, :], preferred_element_type=F32)
    if final:
        acc = _rms(acc, fg_ref[...])
    o_ref[...] = acc


def _outproj(x2, l, ya, yb, yc, yd, ud, conv_w, w_out, final_g, seq, final):
    n = x2.shape[0]
    tm = min(PROJ_ROWS, seq)
    tiles_per_seq = seq // tm
    row = lambda w: pl.BlockSpec((tm, w), lambda i: (i, 0))
    halo = pl.BlockSpec((SUBLANES, GROUP_WIDTH), lambda i: (jnp.maximum(i * (tm // SUBLANES) - 1, 0), 0))
    return pl.pallas_call(
        functools.partial(_outproj_kernel, tiles_per_seq=tiles_per_seq, final=final),
        grid=(n // tm,),
        in_specs=[row(D_MODEL), row(256), row(256), row(256), row(256), row(256), halo,
                  _layer_spec(conv_w, l), _layer_spec(w_out, l), pl.BlockSpec(final_g.shape, lambda i: (0, 0))],
        out_specs=row(D_MODEL),
        out_shape=jax.ShapeDtypeStruct((n, D_MODEL), F32),
        compiler_params=_cparams(("parallel",)),
        name="outproj",
    )(x2, ya, yb, yc, yd, ud, ud, conv_w, w_out, final_g)


def kernel(x, norm_g, w_in, mla_qa_g, mla_w_uq, mla_kva_g, mla_w_ukv, diff_lambda, diff_subln_g, conv_w, w_out,
           rel_bias, final_g):
    batch, seq, _ = x.shape
    depth = w_in.shape[0]
    tm = min(PROJ_ROWS, seq)
    tiles = tuple(min(t, seq) for t in (MLA_TILE, DIFF_TILE, DSA_TILE))
    assert seq % tm == 0 and tm % VT_TILE == 0 and conv_w.shape[1] == CONV_WIDTH <= SUBLANES
    assert all(seq % t == 0 and t % CHUNK == 0 and tm % t == 0 and t % VT_TILE == 0 for t in tiles)

    cos_t, sin_t = (jnp.asarray(a) for a in _rope_tables(seq))
    mask_add = jnp.asarray(_position_constants(seq, tiles[0])[2])
    bias_diff = _bias_tiles(rel_bias[:, :N_DIFF_MAPS], seq, tiles[1])
    bias_dsa = _bias_tiles(rel_bias[:, N_DIFF_MAPS:], seq, tiles[2])

    w1, wvt = _prep_w_in(w_in)
    wq1, wq2, wk, wavt = _prep_mla_up(mla_w_uq, mla_w_ukv)
    row3 = lambda a: a[:, None, :]
    w_out_b = w_out.astype(BF16)

    x2 = x.reshape(batch * seq, D_MODEL)
    for l in range(depth):
        lambda_init = 0.8 - 0.6 * math.exp(-0.3 * l)
        p = _inproj(x2, l, row3(norm_g), w1, wvt, row3(mla_qa_g), wq1, wq2, row3(mla_kva_g), wk, wavt,
                    cos_t, sin_t, seq, tiles)
        ya = _mla(p, mask_add, batch, seq, tiles[0])
        yb = _diff(p, bias_diff, diff_lambda[l], diff_subln_g[l], lambda_init, batch, seq, tiles[1])
        yc = _dsa(p, bias_dsa, batch, seq, tiles[2])
        x2 = _outproj(x2, l, ya, yb, yc, p["yd"], p["ud"], conv_w, w_out_b, final_g[None, :],
                      seq, final=(l == depth - 1))
    return x2.reshape(batch, seq, D_MODEL)
```

```python
import functools
import math

import numpy as np
import jax
import jax.numpy as jnp
from jax import lax
from jax.experimental import pallas as pl
from jax.experimental.pallas import tpu as pltpu

F32 = jnp.float32
BF16 = jnp.bfloat16

D_MODEL = 1024
CHUNK = 64
N_HEADS = 4
HEAD_DIM = 64
GROUP_WIDTH = N_HEADS * HEAD_DIM
MLA_Q_LORA = 256
MLA_KV_LORA = 128
MLA_NOPE = 64
MLA_ROPE = 32
ROPE_BASE = 10000.0
DIFF_QK = 32
IDX_HEADS = 8
IDX_DIM = 32
TOPK_MAX = 256
CONV_WIDTH = 3
NUM_BUCKETS = 32
MAX_DISTANCE = 128
N_DIFF_MAPS = 2 * N_HEADS
NORM_EPS = 1e-6
NEG = -1e30
LOG2E = math.log2(math.e)
INT_MIN = -2 ** 31
KEY_LOWEST = int(np.float32(-np.finfo(np.float32).max).view(np.int32)) ^ 0x7FFFFFFF
HALF_BITS = 16
HALF_MASK = 2 ** HALF_BITS - 1
HALF_BIAS = 2 ** (HALF_BITS - 1)

LANES = 128
SUBLANES = 8
MLA_TILE = 256
DIFF_TILE = 256
DSA_TILE = 512
VT_TILE = 256
FAR_UNIT_KEYS = 512
PROJ_ROWS = 512
FLASH_LOOKAHEAD = 4
V7X_VMEM_BYTES = 64 * 1024 * 1024
VMEM_LIMIT = V7X_VMEM_BYTES * 7 // 8

_IN_SPLITS = (
    ("a_cq", 256), ("a_ckv", 128), ("a_krope", 32), ("a_gate", 256),
    ("b_q", 256), ("b_k", 256), ("b_v", 256), ("b_gate", 256),
    ("c_q", 256), ("c_k", 256), ("c_v", 256),
    ("c_qidx", 256), ("c_kidx", 32), ("c_widx", 8), ("c_gate", 256),
    ("d_b", 256), ("d_c", 256), ("d_h", 256), ("d_gate", 256),
)

_SEGS = (
    ("cq", 256), ("ckv", 128), ("kr1", 128), ("kr2", 128), ("c_kidx4", 128), ("a_gate", 256),
    ("b_q", 256), ("b_k", 256), ("b_gate", 256),
    ("c_q", 256), ("c_k", 256), ("c_qidx", 256),
    ("c_gate", 256), ("d_b", 256), ("d_c", 256), ("d_h", 256), ("d_gate", 256), ("c_widx", 128),
)
_SEG_OFF = {}
_off = 0
for _name, _w in _SEGS:
    _SEG_OFF[_name] = (_off, _w)
    _off += _w
W1_COLS = _off

_NT = (((1,), (1,)), ((), ()))
_ONE_BUFFER = pl.Buffered(1)


def _cparams(sem):
    return pltpu.CompilerParams(dimension_semantics=sem, vmem_limit_bytes=VMEM_LIMIT)


def _split_cols(w):
    out, off = {}, 0
    for name, width in _IN_SPLITS:
        out[name] = w[..., off:off + width]
        off += width
    return out


def _rot_half_cols(w):
    half = MLA_ROPE // 2
    return jnp.concatenate([-w[..., half:], w[..., :half]], axis=-1)


def _prep_w_in(w):
    p = _split_cols(w)
    z = lambda n: jnp.zeros(w.shape[:-1] + (n,), w.dtype)
    cat = lambda parts: jnp.concatenate(parts, axis=-1)
    segs = {
        "cq": p["a_cq"], "ckv": p["a_ckv"], "a_gate": p["a_gate"],
        "kr1": cat([z(MLA_NOPE), p["a_krope"], z(LANES - MLA_NOPE - MLA_ROPE)]),
        "kr2": cat([z(MLA_NOPE), _rot_half_cols(p["a_krope"]), z(LANES - MLA_NOPE - MLA_ROPE)]),
        "b_q": p["b_q"] * (DIFF_QK ** -0.5 * LOG2E), "b_k": p["b_k"], "b_gate": p["b_gate"],
        "c_q": p["c_q"] * (HEAD_DIM ** -0.5 * LOG2E), "c_k": p["c_k"], "c_qidx": p["c_qidx"],
        "c_kidx4": cat([p["c_kidx"]] * (LANES // IDX_DIM)),
        "c_widx": cat([p["c_widx"], z(LANES - IDX_HEADS)]),
        "c_gate": p["c_gate"], "d_b": p["d_b"], "d_c": p["d_c"], "d_h": p["d_h"], "d_gate": p["d_gate"],
    }
    w1 = cat([segs[name].astype(BF16) for name, _ in _SEGS])
    wvt = jnp.swapaxes(cat([p["b_v"].astype(BF16), p["c_v"].astype(BF16)]), -1, -2)
    return w1, wvt


def _prep_mla_up(w_uq, w_ukv):
    depth = w_uq.shape[0]
    scale = (MLA_NOPE + MLA_ROPE) ** -0.5 * LOG2E
    wq = w_uq.reshape(depth, MLA_Q_LORA, N_HEADS, MLA_NOPE + MLA_ROPE) * scale
    zq = jnp.zeros((depth, MLA_Q_LORA, N_HEADS, LANES - MLA_NOPE - MLA_ROPE), w_uq.dtype)
    wq1 = jnp.concatenate([wq, zq], axis=-1).reshape(depth, MLA_Q_LORA, N_HEADS * LANES)
    wq2 = jnp.concatenate([jnp.zeros_like(wq[..., :MLA_NOPE]), _rot_half_cols(wq[..., MLA_NOPE:]), zq],
                          axis=-1).reshape(depth, MLA_Q_LORA, N_HEADS * LANES)
    wkv = w_ukv.reshape(depth, MLA_KV_LORA, N_HEADS, MLA_NOPE + HEAD_DIM)
    zk = jnp.zeros((depth, MLA_KV_LORA, N_HEADS, LANES - MLA_NOPE), w_ukv.dtype)
    wk = jnp.concatenate([wkv[..., :MLA_NOPE], zk], axis=-1).reshape(depth, MLA_KV_LORA, N_HEADS * LANES)
    wvt = jnp.swapaxes(wkv[..., MLA_NOPE:].reshape(depth, MLA_KV_LORA, N_HEADS * HEAD_DIM), -1, -2)
    return wq1.astype(BF16), wq2.astype(BF16), wk.astype(BF16), wvt.astype(BF16)


def _rope_tables(seq):
    half = MLA_ROPE // 2
    inv_freq = ROPE_BASE ** (-np.arange(half, dtype=np.float32) / half)
    ang = np.arange(seq, dtype=np.float32)[:, None] * inv_freq[None, :].astype(np.float32)
    cos, sin = np.cos(ang).astype(np.float32), np.sin(ang).astype(np.float32)
    ct = np.zeros((seq, LANES), np.float32)
    st = np.zeros((seq, LANES), np.float32)
    ct[:, :MLA_NOPE] = 1.0
    ct[:, MLA_NOPE:MLA_NOPE + half] = cos
    ct[:, MLA_NOPE + half:MLA_NOPE + MLA_ROPE] = cos
    st[:, MLA_NOPE:MLA_NOPE + half] = sin
    st[:, MLA_NOPE + half:MLA_NOPE + MLA_ROPE] = sin
    return ct, st


def _rel_bucket_np(rel):
    nb = NUM_BUCKETS // 2
    max_exact = nb // 2
    ret = np.where(rel > 0, nb, 0)
    n = np.abs(rel)
    nf = np.maximum(n, max_exact).astype(np.float32)
    large = max_exact + (np.log(nf / np.float32(max_exact)) / np.float32(math.log(MAX_DISTANCE / max_exact))
                         * np.float32(nb - max_exact)).astype(np.int32)
    large = np.minimum(large, nb - 1)
    return (ret + np.where(n < max_exact, n, large)).astype(np.int32)


def _position_constants(seq, tile):
    r = np.arange(tile)
    rel0 = r[:, None] - r[None, :]
    bidx = np.stack([_rel_bucket_np(rel0 - tile), _rel_bucket_np(rel0)]).astype(np.int32)
    far = _rel_bucket_np(np.arange(-(seq - 1), -tile))
    far_bucket = int(far[0]) if far.size else int(_rel_bucket_np(np.array([-tile - 1]))[0])
    assert far.size == 0 or np.all(far == far_bucket)
    mask_add = np.where((r[:, None] // CHUNK) <= (r[None, :] // CHUNK), 0.0, NEG).astype(np.float32)
    return bidx, far_bucket, mask_add


def _bias_kernel(tab_ref, bidx_ref, madd_ref, o_ref, *, far_bucket, buckets):
    j = pl.program_id(0)
    c = tab_ref[j, far_bucket]
    for d in range(2):
        idx = bidx_ref[d]
        acc = jnp.zeros(idx.shape, F32)
        for b in buckets[d]:
            acc = jnp.where(idx == b, (tab_ref[j, b] - c) * LOG2E, acc)
        if d == 1:
            acc = acc + madd_ref[...]
        o_ref[0, d] = acc


def _bias_tiles(table, seq, tile):
    bidx, far_bucket, mask_add = _position_constants(seq, tile)
    n_maps = table.shape[1]
    buckets = tuple(tuple(int(b) for b in np.unique(bidx[d])) for d in range(2))
    return pl.pallas_call(
        functools.partial(_bias_kernel, far_bucket=far_bucket, buckets=buckets),
        grid=(n_maps,),
        in_specs=[
            pl.BlockSpec(memory_space=pltpu.SMEM),
            pl.BlockSpec((2, tile, tile), lambda j: (0, 0, 0)),
            pl.BlockSpec((tile, tile), lambda j: (0, 0)),
        ],
        out_specs=pl.BlockSpec((1, 2, tile, tile), lambda j: (j, 0, 0, 0)),
        out_shape=jax.ShapeDtypeStruct((n_maps, 2, tile, tile), F32),
        compiler_params=_cparams(("arbitrary",)),
        name="bias_tiles",
    )(table.T.astype(F32), jnp.asarray(bidx), jnp.asarray(mask_add))


def _silu(x):
    return x * (1.0 / (1.0 + jnp.exp(-x)))


def _rms(x, g):
    return x * lax.rsqrt(jnp.mean(x * x, axis=-1, keepdims=True) + NORM_EPS) * g


def _inproj_kernel(x_ref, g_ref, w_ref, wvt_ref, qag_ref, wq1_ref, wq2_ref, kvg_ref, wk_ref, wavt_ref, cos_ref, sin_ref,
                   qa_ref, ka_ref, vat_ref, ga_ref, qb_ref, kb_ref, vbt_ref, gb_ref,
                   qc_ref, kc_ref, vct_ref, qi_ref, ki_ref, wi_ref, gc_ref, yd_ref, ud_ref):
    h = _rms(x_ref[...], g_ref[...]).astype(BF16)

    def seg(name):
        off, width = _SEG_OFF[name]
        return jnp.dot(h, w_ref[:, off:off + width], preferred_element_type=F32)

    def seg_pair(first, second):
        off, width = _SEG_OFF[first]
        assert _SEG_OFF[second] == (off + width, width) and width == LANES
        y = jnp.dot(h, w_ref[:, off:off + 2 * width], preferred_element_type=F32)
        return y[:, :width], y[:, width:]

    cos = cos_ref[...]
    sin = sin_ref[...]

    cqn = _rms(seg("cq"), qag_ref[...]).astype(BF16)
    q1 = jnp.dot(cqn, wq1_ref[...], preferred_element_type=F32)
    q2 = jnp.dot(cqn, wq2_ref[...], preferred_element_type=F32)
    ckv, kr1 = seg_pair("ckv", "kr1")
    kr2, kidx4 = seg_pair("kr2", "c_kidx4")
    kr = kr1 * cos + kr2 * sin
    ckvn = _rms(ckv, kvg_ref[...]).astype(BF16)
    kn = jnp.dot(ckvn, wk_ref[...], preferred_element_type=F32)
    for hd in range(N_HEADS):
        sl = slice(hd * LANES, (hd + 1) * LANES)
        qa_ref[:, sl] = (q1[:, sl] * cos + q2[:, sl] * sin).astype(BF16)
        ka_ref[:, sl] = (kn[:, sl] + kr).astype(BF16)
    ga_ref[...] = _silu(seg("a_gate")).astype(BF16)

    def put(ref, first_token, vt):
        tile = ref.shape[2]
        ref[first_token // tile, :, first_token % tile:first_token % tile + VT_TILE] = vt

    for c in range(x_ref.shape[0] // VT_TILE):
        rows = slice(c * VT_TILE, (c + 1) * VT_TILE)
        put(vat_ref, c * VT_TILE,
            lax.dot_general(wavt_ref[...], ckvn[rows], _NT, preferred_element_type=F32).astype(BF16))
        vt = lax.dot_general(wvt_ref[...], h[rows], _NT, preferred_element_type=F32).astype(BF16)
        put(vbt_ref, c * VT_TILE, vt[:GROUP_WIDTH])
        put(vct_ref, c * VT_TILE, vt[GROUP_WIDTH:])

    qb_ref[...] = seg("b_q").astype(BF16)
    kb_ref[...] = seg("b_k").astype(BF16)
    gb_ref[...] = _silu(seg("b_gate")).astype(BF16)

    qc_ref[...] = seg("c_q").astype(BF16)
    kc_ref[...] = seg("c_k").astype(BF16)
    qi_ref[...] = seg("c_qidx").astype(BF16)
    ki_ref[...] = kidx4.astype(BF16)
    wi_ref[...] = seg("c_widx")
    gc_ref[...] = _silu(seg("c_gate")).astype(BF16)

    yd_ref[...] = (seg("d_b") * _silu(seg("d_gate"))).astype(BF16)
    ud_ref[...] = seg("d_c") * seg("d_h")


def _layer_spec(a, l):
    return pl.BlockSpec((None,) + a.shape[1:], lambda i: (l,) + (0,) * (a.ndim - 1))


def _inproj(x2, l, g, w1, wvt, qag, wq1, wq2, kvg, wk, wavt, cos_t, sin_t, seq, tiles):
    n = x2.shape[0]
    tm = min(PROJ_ROWS, seq)
    tiles_per_seq = seq // tm
    row = lambda w: pl.BlockSpec((tm, w), lambda i: (i, 0))
    full = lambda a: _layer_spec(a, l)
    tab = pl.BlockSpec((tm, LANES), lambda i: (i % tiles_per_seq, 0))
    vt_spec = lambda t: pl.BlockSpec((tm // t, GROUP_WIDTH, t), lambda i: (i, 0, 0))
    vt_shape = lambda t: jax.ShapeDtypeStruct((n // t, GROUP_WIDTH, t), BF16)
    outs = (("qa", 512, BF16), ("ka", 512, BF16), ("vat", None, tiles[0]), ("ga", 256, BF16),
            ("qb", 256, BF16), ("kb", 256, BF16), ("vbt", None, tiles[1]), ("gb", 256, BF16),
            ("qc", 256, BF16), ("kc", 256, BF16), ("vct", None, tiles[2]), ("qi", 256, BF16),
            ("ki", 128, BF16), ("wi", 128, F32), ("gc", 256, BF16), ("yd", 256, BF16), ("ud", 256, F32))
    res = pl.pallas_call(
        _inproj_kernel,
        grid=(n // tm,),
        in_specs=[row(D_MODEL), full(g), full(w1), full(wvt), full(qag), full(wq1), full(wq2), full(kvg), full(wk),
                  full(wavt), tab, tab],
        out_specs=[vt_spec(dt) if w is None else row(w) for _, w, dt in outs],
        out_shape=[vt_shape(dt) if w is None else jax.ShapeDtypeStruct((n, w), dt) for _, w, dt in outs],
        compiler_params=_cparams(("parallel",)),
        name="inproj",
    )(x2, g, w1, wvt, qag, wq1, wq2, kvg, wk, wavt, cos_t, sin_t)
    return {name: o for (name, _, _), o in zip(outs, res)}


def _flash_scratch(n_maps, t):
    unit = _far_unit_tiles(t) * t
    return [pltpu.VMEM((n_maps, 1, t), F32), pltpu.VMEM((n_maps, 1, t), F32), pltpu.VMEM((n_maps, HEAD_DIM, t), F32),
            pltpu.VMEM((n_maps, unit, t), F32), pltpu.VMEM((n_maps, unit, t), F32)]


def _far_unit_tiles(t):
    return max(1, FAR_UNIT_KEYS // t)


def _flash_update(s, vts, old):
    t = vts[0].shape[1]
    m_tile = jnp.max(s, axis=0, keepdims=True)
    m_new = m_tile if old is None else jnp.maximum(old[0], m_tile)
    p = jnp.exp2(s - m_new)
    l_tile = jnp.sum(p, axis=0, keepdims=True)
    pb = p.astype(BF16)
    pv = jnp.dot(vts[0], pb[0:t], preferred_element_type=F32)
    for d in range(1, len(vts)):
        pv = pv + jnp.dot(vts[d], pb[d * t:(d + 1) * t], preferred_element_type=F32)
    if old is None:
        return m_new, l_tile, pv
    alpha = jnp.exp2(old[0] - m_new)
    return m_new, alpha * old[1] + l_tile, alpha * old[2] + pv


def _flash_sweep(i, t, n_maps, qs, keys, values, state, diag_add, near_add=None, mask=None):
    m_ref, l_ref, acc_ref, buf_a, buf_b = state
    n_far = i if near_add is None else jnp.maximum(i - 1, 0)
    ut = _far_unit_tiles(t)
    n_units = n_far // ut

    def load_state():
        return [(m_ref[j], l_ref[j], acc_ref[j]) for j in range(n_maps)]

    def store_state(cur):
        for j in range(n_maps):
            m_ref[j], l_ref[j], acc_ref[j] = cur[j]

    def tile_logits(j, kt, n_tiles, shared, extra=None):
        rows = pl.ds(pl.multiple_of(kt * t, t), n_tiles * t)
        s = lax.dot_general(keys(j, rows), qs[j], _NT, preferred_element_type=F32)
        add = shared
        if extra is not None:
            add = extra(j) if add is None else add + extra(j)
        return s if add is None else s + add

    def shared_mask(kt, n_tiles):
        if mask is None:
            return None
        tiles = [mask(kt + d) for d in range(n_tiles)]
        return tiles[0] if n_tiles == 1 else jnp.concatenate(tiles, axis=0)

    def run(units, first=False, prefetch=False):
        cur = [None] * n_maps if first else load_state()
        shared = [shared_mask(kt, n_tiles) for kt, n_tiles, _ in units]
        items = [(u, j) for u in range(len(units)) for j in range(n_maps)]
        if prefetch:
            items += [(None, j) for j in range(n_maps)]
            shared.append(shared_mask(0, ut))

        def logits(u, j):
            if u is None:
                return tile_logits(j, 0, ut, shared[-1])
            kt, n_tiles, extra = units[u]
            return tile_logits(j, kt, n_tiles, shared[u], extra)

        queue = [logits(*it) for it in items[:FLASH_LOOKAHEAD]]
        for n, (u, j) in enumerate(items):
            if n + FLASH_LOOKAHEAD < len(items):
                queue.append(logits(*items[n + FLASH_LOOKAHEAD]))
            s = queue.pop(0)
            if u is None:
                buf_a[j] = s
            else:
                kt, n_tiles, _ = units[u]
                cur[j] = _flash_update(s, [values(j, kt + d) for d in range(n_tiles)], cur[j])
        store_state(cur)

    def first_run(prefetch):
        if near_add is None:
            return lambda: run([(i, 1, diag_add)], first=True, prefetch=prefetch)
        return lambda: run([(i, 1, diag_add), (i - 1, 1, near_add)], first=True, prefetch=prefetch)

    if near_add is None:
        pl.when(n_units == 0)(first_run(False))
    else:
        pl.when(i == 0)(lambda: run([(i, 1, diag_add)], first=True))
        pl.when((i >= 1) & (n_units == 0))(first_run(False))
    pl.when(n_units >= 1)(first_run(True))

    def consume(buf, kt, j, old):
        return _flash_update(buf[j], [values(j, kt + d) for d in range(ut)], old)

    def overlapped(cur, buf_in, buf_out, u):
        kt_next = ut * (u + 1)
        shared = shared_mask(kt_next, ut)
        for j in range(n_maps):
            buf_out[j] = tile_logits(j, kt_next, ut, shared)
            cur[j] = consume(buf_in, ut * u, j, cur[j])

    def far_body(c, carry):
        cur = load_state()
        overlapped(cur, buf_a, buf_b, 2 * c)
        overlapped(cur, buf_b, buf_a, 2 * c + 1)
        store_state(cur)
        return carry

    lax.fori_loop(0, (n_units - 1) // 2, far_body, 0)

    @pl.when(n_units % 2 == 1)
    def _():
        cur = load_state()
        for j in range(n_maps):
            cur[j] = consume(buf_a, ut * (n_units - 1), j, cur[j])
        store_state(cur)

    @pl.when((n_units >= 2) & (n_units % 2 == 0))
    def _():
        cur = load_state()
        overlapped(cur, buf_a, buf_b, n_units - 2)
        for j in range(n_maps):
            cur[j] = consume(buf_b, ut * (n_units - 1), j, cur[j])
        store_state(cur)

    for r in range(1, ut):
        pl.when(n_far % ut >= r)(lambda r=r: run([(n_far - r, 1, None)]))
    return [acc_ref[j] / l_ref[j] for j in range(n_maps)]


def _lane_mask(lo, hi, dtype):
    lane = lax.broadcasted_iota(jnp.int32, (1, LANES), 1)
    return ((lane >= lo) & (lane < hi)).astype(dtype)


def _store_gated(o_ref, g_ref, heads_t):
    o = jnp.concatenate(heads_t, axis=0).T
    o_ref[...] = (o * g_ref[...].astype(F32)).astype(BF16)


def _att_specs(t, seq, widths_q, widths_k):
    nq = seq // t
    qspec = lambda w: pl.BlockSpec((t, w), lambda b, i: (b * nq + i, 0))
    kspec = lambda w: pl.BlockSpec((seq, w), lambda b, i: (b, 0), pipeline_mode=_ONE_BUFFER)
    vspec = pl.BlockSpec((nq, GROUP_WIDTH, t), lambda b, i: (b, 0, 0), pipeline_mode=_ONE_BUFFER)
    return [qspec(w) for w in widths_q], [kspec(w) for w in widths_k], vspec, qspec(GROUP_WIDTH)


def _mla_kernel(q_ref, k_ref, vt_ref, g_ref, madd_ref, o_ref, *state, t):
    i = pl.program_id(1)
    qs = [q_ref[:, hd * LANES:(hd + 1) * LANES] for hd in range(N_HEADS)]
    keys = lambda hd, rows: k_ref[rows, hd * LANES:(hd + 1) * LANES]
    values = lambda hd, kt: vt_ref[kt, hd * HEAD_DIM:(hd + 1) * HEAD_DIM, :]
    outs = _flash_sweep(i, t, N_HEADS, qs, keys, values, state, lambda hd: madd_ref[...])
    _store_gated(o_ref, g_ref, outs)


def _mla(p, mask_add, batch, seq, t):
    qs, ks, vspec, ospec = _att_specs(t, seq, (512, 256), (512,))
    return pl.pallas_call(
        functools.partial(_mla_kernel, t=t),
        grid=(batch, seq // t),
        in_specs=[qs[0], ks[0], vspec, qs[1], pl.BlockSpec((t, t), lambda b, i: (0, 0), pipeline_mode=_ONE_BUFFER)],
        out_specs=ospec,
        out_shape=jax.ShapeDtypeStruct((batch * seq, GROUP_WIDTH), BF16),
        scratch_shapes=_flash_scratch(N_HEADS, t),
        compiler_params=_cparams(("parallel", "arbitrary")),
        name="mla_attention",
    )(p["qa"], p["ka"], p["vat"], p["ga"], mask_add)


def _diff_kernel(q_ref, k_ref, vt_ref, g_ref, bias_ref, lam_ref, subg_ref, o_ref, *state, t, lambda_init):
    i = pl.program_id(1)
    lp = lam_ref[...]
    lam = (jnp.exp(jnp.sum(lp[0:1] * lp[1:2], axis=-1, keepdims=True))
           - jnp.exp(jnp.sum(lp[2:3] * lp[3:4], axis=-1, keepdims=True)) + lambda_init)
    qs = [q_ref[:, (j // 4) * LANES:(j // 4 + 1) * LANES] * _lane_mask((j % 4) * DIFF_QK, (j % 4 + 1) * DIFF_QK, BF16)
          for j in range(N_DIFF_MAPS)]
    keys = lambda j, rows: k_ref[rows, (j // 4) * LANES:(j // 4 + 1) * LANES]
    values = lambda j, kt: vt_ref[kt, (j // 2) * HEAD_DIM:(j // 2 + 1) * HEAD_DIM, :]
    maps = _flash_sweep(i, t, N_DIFF_MAPS, qs, keys, values, state, lambda j: bias_ref[j, 1], lambda j: bias_ref[j, 0])
    outs = []
    for hd in range(N_HEADS):
        o = maps[2 * hd] - lam * maps[2 * hd + 1]
        ms = jnp.mean(o * o, axis=0, keepdims=True)
        outs.append(o * lax.rsqrt(ms + NORM_EPS) * subg_ref[...] * (1.0 - lambda_init))
    _store_gated(o_ref, g_ref, outs)


def _diff(p, bias, lam_params, subln_g, lambda_init, batch, seq, t):
    qs, ks, vspec, ospec = _att_specs(t, seq, (256, 256), (256,))
    subg = jnp.broadcast_to(subln_g.astype(F32)[:, None], (HEAD_DIM, t))
    return pl.pallas_call(
        functools.partial(_diff_kernel, t=t, lambda_init=lambda_init),
        grid=(batch, seq // t),
        in_specs=[qs[0], ks[0], vspec, qs[1],
                  pl.BlockSpec((N_DIFF_MAPS, 2, t, t), lambda b, i: (0, 0, 0, 0), pipeline_mode=_ONE_BUFFER),
                  pl.BlockSpec((4, DIFF_QK), lambda b, i: (0, 0)),
                  pl.BlockSpec((HEAD_DIM, t), lambda b, i: (0, 0))],
        out_specs=ospec,
        out_shape=jax.ShapeDtypeStruct((batch * seq, GROUP_WIDTH), BF16),
        scratch_shapes=_flash_scratch(N_DIFF_MAPS, t),
        compiler_params=_cparams(("parallel", "arbitrary")),
        name="diff_attention",
    )(p["qb"], p["kb"], p["vbt"], p["gb"], bias, lam_params.astype(F32), subg)


def _sort_key(x):
    b = lax.bitcast_convert_type(x + 0.0, jnp.int32)
    return b ^ ((b >> 31) & 0x7FFFFFFF)


def _dsa_kernel(q_ref, k_ref, vt_ref, qi_ref, ki_ref, wi_ref, g_ref, bias_ref, tri_ref, o_ref,
                score_ref, hi_ref, lo_ref, thr_ref, cnt_ref, *state, t, topk):
    i = pl.program_id(1)
    n_kt = i + 1
    idx_scale = (IDX_HEADS ** -0.5) * (IDX_DIM ** -0.5)

    wit = wi_ref[...].T * idx_scale
    qm, wrow = [], []
    for ih in range(IDX_HEADS):
        grp = slice((ih // 4) * LANES, (ih // 4 + 1) * LANES)
        lo = (ih % 4) * IDX_DIM
        qm.append(qi_ref[:, grp] * _lane_mask(lo, lo + IDX_DIM, BF16))
        wrow.append(wit[ih:ih + 1, :])

    def idx_tile(kt):
        kk = ki_ref[pl.ds(pl.multiple_of(kt * t, t), t), :]
        dot = lambda ih: lax.dot_general(kk, qm[ih], _NT, preferred_element_type=F32)
        queue = [dot(ih) for ih in range(FLASH_LOOKAHEAD)]
        tot = None
        for ih in range(IDX_HEADS):
            if ih + FLASH_LOOKAHEAD < IDX_HEADS:
                queue.append(dot(ih + FLASH_LOOKAHEAD))
            term = wrow[ih] * jnp.maximum(queue.pop(0), 0.0)
            tot = term if tot is None else tot + term
        return tot + 0.0

    def store_scores(kt, score):
        score_ref[kt] = score
        key = _sort_key(score)
        hi_ref[kt] = (key >> HALF_BITS).astype(jnp.int16)
        lo_ref[kt] = ((key & HALF_MASK) - HALF_BIAS).astype(jnp.int16)

    def key_tiles(first, n):
        for d in range(n):
            store_scores(first + d, idx_tile(first + d))

    def key_quad_body(c, carry):
        key_tiles(4 * c, 4)
        return carry

    lax.fori_loop(0, i // 4, key_quad_body, 0)
    pl.when(i % 4 >= 2)(lambda: key_tiles(i - i % 4, 2))
    pl.when(i % 2 == 1)(lambda: key_tiles(i - 1, 1))
    kchunk = lax.broadcasted_iota(jnp.int32, (t, t), 0) // CHUNK
    qchunk = lax.broadcasted_iota(jnp.int32, (t, t), 1) // CHUNK
    store_scores(i, jnp.where(kchunk <= qchunk, idx_tile(i), -jnp.inf))

    def count(pred):
        def body(kt, c):
            hit = jnp.where(pred(score_ref[kt]), 1.0, 0.0)
            return c + jnp.sum(hit.reshape(t // SUBLANES, SUBLANES, t), axis=0)
        part = lax.fori_loop(0, n_kt, body, jnp.zeros((SUBLANES, t), F32))
        return jnp.sum(part, axis=0, keepdims=True)

    rows16 = 2 * SUBLANES
    n_acc = 4

    def key_to_score(key):
        key = jnp.maximum(key, KEY_LOWEST)
        return lax.bitcast_convert_type(key ^ ((key >> 31) & 0x7FFFFFFF), F32)

    def packed_search(n):
        def count16(ref, pred):
            one, zero = jnp.int16(1), jnp.int16(0)
            groups = t // LANES
            accs = [[jnp.zeros((rows16, LANES), jnp.int16) for _ in range(groups)] for _ in range(n_acc)]
            for kt in range(n):
                x = ref[kt].reshape(t // rows16, rows16, t)
                for r in range(t // rows16):
                    first = (r * rows16) // LANES if kt == n - 1 else 0
                    for g in range(first, groups):
                        lanes = slice(g * LANES, (g + 1) * LANES)
                        a = accs[r % n_acc]
                        a[g] = a[g] + jnp.where(pred(x[r][:, lanes], lanes), one, zero)
            tot = [(accs[0][g] + accs[1][g]) + (accs[2][g] + accs[3][g]) for g in range(groups)]
            tot = jnp.concatenate(tot, axis=-1)
            return jnp.sum(tot.astype(jnp.int32), axis=0, keepdims=True)

        def select16(ref, need):
            def bit_body(b, res):
                cand = res | (jnp.int32(1) << (HALF_BITS - 1 - b))
                cand16 = (cand - HALF_BIAS).astype(jnp.int16)
                cnt = count16(ref, lambda x, lanes: x >= cand16[:, lanes])
                return jnp.where(cnt >= need, cand, res)
            return lax.fori_loop(0, HALF_BITS, bit_body, jnp.zeros((1, t), jnp.int32)) - HALF_BIAS

        hi_thr = select16(hi_ref, topk)
        hi_thr16 = hi_thr.astype(jnp.int16)
        need_lo = topk - count16(hi_ref, lambda x, lanes: x > hi_thr16[:, lanes])
        for kt in range(n):
            lo_ref[kt] = jnp.where(hi_ref[kt] == hi_thr16, lo_ref[kt], jnp.int16(-HALF_BIAS))
        lo_thr = select16(lo_ref, need_lo)
        thr_ref[...] = key_to_score((hi_thr << HALF_BITS) | (lo_thr + HALF_BIAS))

    for n in range(1, score_ref.shape[0] + 1):
        pl.when(n_kt == n)(functools.partial(packed_search, n))

    cnt_ref[...] = count(lambda x: x >= thr_ref[...])
    n_valid = ((i * t + lax.broadcasted_iota(jnp.int32, (1, t), 1)) // CHUNK + 1) * CHUNK
    few = n_valid <= topk

    @pl.when(jnp.max(jnp.where(few, 0.0, jnp.abs(cnt_ref[...] - topk))) > 0.0)
    def _():
        cnt_gt = count(lambda x: x > thr_ref[...])
        wrong = (cnt_ref[...] < topk) | (cnt_gt >= topk)

        @pl.when(jnp.max(jnp.where(few | jnp.logical_not(wrong), 0.0, 1.0)) > 0.0)
        def _():
            def bit_body(b, res):
                cand = res | (jnp.int32(1) << (31 - b))
                cand_score = key_to_score(cand ^ INT_MIN)
                cnt = count(lambda x: x >= cand_score)
                return jnp.where(cnt >= topk, cand, res)
            res = lax.fori_loop(0, 32, bit_body, jnp.zeros((1, t), jnp.int32))
            thr_ref[...] = key_to_score(res ^ INT_MIN)
            cnt_ref[...] = count(lambda x: x >= thr_ref[...])

    thr = thr_ref[...]
    has_ties = jnp.max(jnp.where(few, 0.0, cnt_ref[...])) > topk

    @pl.when(jnp.logical_not(has_ties))
    def _():
        def body(kt, c):
            score_ref[kt] = jnp.where(score_ref[kt] >= thr, 0.0, NEG)
            return c
        lax.fori_loop(0, n_kt, body, 0)

    @pl.when(has_ties)
    def _():
        need = topk - count(lambda x: x > thr)

        def body(kt, seen):
            x = score_ref[kt]
            eq = jnp.where(x == thr, 1.0, 0.0)
            rank = seen + jnp.dot(tri_ref[...], eq.astype(BF16), preferred_element_type=F32)
            sel = (x > thr) | ((x == thr) & (rank <= need))
            score_ref[kt] = jnp.where(sel, 0.0, NEG)
            return seen + jnp.sum(eq, axis=0, keepdims=True)
        lax.fori_loop(0, n_kt, body, jnp.zeros((1, t), F32))

    qs = [q_ref[:, (hd // 2) * LANES:(hd // 2 + 1) * LANES]
          * _lane_mask((hd % 2) * HEAD_DIM, (hd % 2 + 1) * HEAD_DIM, BF16) for hd in range(N_HEADS)]
    keys = lambda hd, rows: k_ref[rows, (hd // 2) * LANES:(hd // 2 + 1) * LANES]
    values = lambda hd, kt: vt_ref[kt, hd * HEAD_DIM:(hd + 1) * HEAD_DIM, :]
    outs = _flash_sweep(i, t, N_HEADS, qs, keys, values, state, lambda hd: bias_ref[hd, 1], lambda hd: bias_ref[hd, 0],
                        mask=lambda kt: score_ref[kt])
    _store_gated(o_ref, g_ref, outs)


def _dsa(p, bias, batch, seq, t):
    topk = min(TOPK_MAX, seq // 4)
    nq = seq // t
    qs, ks, vspec, ospec = _att_specs(t, seq, (256, 256, 128, 256), (256, 128))
    tri = jnp.asarray(np.tril(np.ones((t, t), np.float32)), BF16)
    return pl.pallas_call(
        functools.partial(_dsa_kernel, t=t, topk=topk),
        grid=(batch, nq),
        in_specs=[qs[0], ks[0], vspec, qs[1], ks[1], qs[2], qs[3],
                  pl.BlockSpec((N_HEADS, 2, t, t), lambda b, i: (0, 0, 0, 0), pipeline_mode=_ONE_BUFFER),
                  pl.BlockSpec((t, t), lambda b, i: (0, 0), pipeline_mode=_ONE_BUFFER)],
        out_specs=ospec,
        out_shape=jax.ShapeDtypeStruct((batch * seq, GROUP_WIDTH), BF16),
        scratch_shapes=[pltpu.VMEM((nq, t, t), F32), pltpu.VMEM((nq, t, t), jnp.int16),
                        pltpu.VMEM((nq, t, t), jnp.int16), pltpu.VMEM((1, t), F32), pltpu.VMEM((1, t), F32)]
        + _flash_scratch(N_HEADS, t),
        compiler_params=_cparams(("parallel", "arbitrary")),
        name="dsa_attention",
    )(p["qc"], p["kc"], p["vct"], p["qi"], p["ki"], p["wi"], p["gc"], bias, tri)


def _outproj_kernel(x_ref, ya_ref, yb_ref, yc_ref, yd_ref, ud_ref, halo_ref, cw_ref, w_ref, fg_ref, o_ref,
                    *, tiles_per_seq, final):
    i = pl.program_id(0)
    tm = x_ref.shape[0]
    u = ud_ref[...]
    halo = jnp.where(i % tiles_per_seq == 0, 0.0, halo_ref[...])
    row8 = lax.broadcasted_iota(jnp.int32, (SUBLANES, 1), 0)

    def shifted(k):
        r = pltpu.roll(u, k, 0)
        head = jnp.where(row8 < k, pltpu.roll(halo, k, 0), r[0:SUBLANES])
        return jnp.concatenate([head, r[SUBLANES:tm]], axis=0)

    cw = cw_ref[...]
    conv = cw[0:1] * shifted(CONV_WIDTH - 1)
    for j in range(1, CONV_WIDTH):
        conv = conv + cw[j:j + 1] * (shifted(CONV_WIDTH - 1 - j) if j < CONV_WIDTH - 1 else u)
    yd = (conv * yd_ref[...].astype(F32)).astype(BF16)
    acc = x_ref[...]
    for g, y in enumerate((ya_ref[...], yb_ref[...], yc_ref[...], yd)):
        acc = acc + jnp.dot(y, w_ref[g * GROUP_WIDTH:(g + 1) * GROUP_WIDTH, :], preferred_element_type=F32)
    if final:
        acc = _rms(acc, fg_ref[...])
    o_ref[...] = acc


def _outproj(x2, l, ya, yb, yc, yd, ud, conv_w, w_out, final_g, seq, final):
    n = x2.shape[0]
    tm = min(PROJ_ROWS, seq)
    tiles_per_seq = seq // tm
    row = lambda w: pl.BlockSpec((tm, w), lambda i: (i, 0))
    halo = pl.BlockSpec((SUBLANES, GROUP_WIDTH), lambda i: (jnp.maximum(i * (tm // SUBLANES) - 1, 0), 0))
    return pl.pallas_call(
        functools.partial(_outproj_kernel, tiles_per_seq=tiles_per_seq, final=final),
        grid=(n // tm,),
        in_specs=[row(D_MODEL), row(256), row(256), row(256), row(256), row(256), halo,
                  _layer_spec(conv_w, l), _layer_spec(w_out, l), pl.BlockSpec(final_g.shape, lambda i: (0, 0))],
        out_specs=row(D_MODEL),
        out_shape=jax.ShapeDtypeStruct((n, D_MODEL), F32),
        compiler_params=_cparams(("parallel",)),
        name="outproj",
    )(x2, ya, yb, yc, yd, ud, ud, conv_w, w_out, final_g)


def kernel(x, norm_g, w_in, mla_qa_g, mla_w_uq, mla_kva_g, mla_w_ukv, diff_lambda, diff_subln_g, conv_w, w_out,
           rel_bias, final_g):
    batch, seq, _ = x.shape
    depth = w_in.shape[0]
    tm = min(PROJ_ROWS, seq)
    tiles = tuple(min(t, seq) for t in (MLA_TILE, DIFF_TILE, DSA_TILE))
    assert seq % tm == 0 and tm % VT_TILE == 0 and conv_w.shape[1] == CONV_WIDTH <= SUBLANES
    assert all(seq % t == 0 and t % CHUNK == 0 and tm % t == 0 and t % VT_TILE == 0 for t in tiles)

    cos_t, sin_t = (jnp.asarray(a) for a in _rope_tables(seq))
    mask_add = jnp.asarray(_position_constants(seq, tiles[0])[2])
    bias_diff = _bias_tiles(rel_bias[:, :N_DIFF_MAPS], seq, tiles[1])
    bias_dsa = _bias_tiles(rel_bias[:, N_DIFF_MAPS:], seq, tiles[2])

    w1, wvt = _prep_w_in(w_in)
    wq1, wq2, wk, wavt = _prep_mla_up(mla_w_uq, mla_w_ukv)
    row3 = lambda a: a[:, None, :]
    w_out_b = w_out.astype(BF16)

    x2 = x.reshape(batch * seq, D_MODEL)
    for l in range(depth):
        lambda_init = 0.8 - 0.6 * math.exp(-0.3 * l)
        p = _inproj(x2, l, row3(norm_g), w1, wvt, row3(mla_qa_g), wq1, wq2, row3(mla_kva_g), wk, wavt,
                    cos_t, sin_t, seq, tiles)
        ya = _mla(p, mask_add, batch, seq, tiles[0])
        yb = _diff(p, bias_diff, diff_lambda[l], diff_subln_g[l], lambda_init, batch, seq, tiles[1])
        yc = _dsa(p, bias_dsa, batch, seq, tiles[2])
        x2 = _outproj(x2, l, ya, yb, yc, p["yd"], p["ud"], conv_w, w_out_b, final_g[None, :],
                      seq, final=(l == depth - 1))
    return x2.reshape(batch, seq, D_MODEL)
```

```python
import functools
import math

import numpy as np
import jax
import jax.numpy as jnp
from jax import lax
from jax.experimental import pallas as pl
from jax.experimental.pallas import tpu as pltpu

F32 = jnp.float32
BF16 = jnp.bfloat16

D_MODEL = 1024
CHUNK = 64
N_HEADS = 4
HEAD_DIM = 64
GROUP_WIDTH = N_HEADS * HEAD_DIM
MLA_Q_LORA = 256
MLA_KV_LORA = 128
MLA_NOPE = 64
MLA_ROPE = 32
ROPE_BASE = 10000.0
DIFF_QK = 32
IDX_HEADS = 8
IDX_DIM = 32
TOPK_MAX = 256
CONV_WIDTH = 3
NUM_BUCKETS = 32
MAX_DISTANCE = 128
N_DIFF_MAPS = 2 * N_HEADS
NORM_EPS = 1e-6
NEG = -1e30
LOG2E = math.log2(math.e)
INT_MIN = -2 ** 31
KEY_LOWEST = int(np.float32(-np.finfo(np.float32).max).view(np.int32)) ^ 0x7FFFFFFF
HALF_BITS = 16
HALF_MASK = 2 ** HALF_BITS - 1
HALF_BIAS = 2 ** (HALF_BITS - 1)

LANES = 128
SUBLANES = 8
MLA_TILE = 256
DIFF_TILE = 256
DSA_TILE = 512
VT_TILE = 256
FAR_UNIT_KEYS = 512
PROJ_ROWS = 512
FLASH_LOOKAHEAD = 6
V7X_VMEM_BYTES = 64 * 1024 * 1024
VMEM_LIMIT = V7X_VMEM_BYTES * 7 // 8

_IN_SPLITS = (
    ("a_cq", 256), ("a_ckv", 128), ("a_krope", 32), ("a_gate", 256),
    ("b_q", 256), ("b_k", 256), ("b_v", 256), ("b_gate", 256),
    ("c_q", 256), ("c_k", 256), ("c_v", 256),
    ("c_qidx", 256), ("c_kidx", 32), ("c_widx", 8), ("c_gate", 256),
    ("d_b", 256), ("d_c", 256), ("d_h", 256), ("d_gate", 256),
)

_SEGS = (
    ("cq", 256), ("ckv", 128), ("kr1", 128), ("kr2", 128), ("c_kidx4", 128), ("a_gate", 256),
    ("b_q", 256), ("b_k", 256), ("b_gate", 256),
    ("c_q", 256), ("c_k", 256), ("c_qidx", 256),
    ("c_gate", 256), ("d_b", 256), ("d_c", 256), ("d_h", 256), ("d_gate", 256), ("c_widx", 128),
)
_SEG_OFF = {}
_off = 0
for _name, _w in _SEGS:
    _SEG_OFF[_name] = (_off, _w)
    _off += _w
W1_COLS = _off

_NT = (((1,), (1,)), ((), ()))
_ONE_BUFFER = pl.Buffered(1)


def _cparams(sem):
    return pltpu.CompilerParams(dimension_semantics=sem, vmem_limit_bytes=VMEM_LIMIT)


def _split_cols(w):
    out, off = {}, 0
    for name, width in _IN_SPLITS:
        out[name] = w[..., off:off + width]
        off += width
    return out


def _rot_half_cols(w):
    half = MLA_ROPE // 2
    return jnp.concatenate([-w[..., half:], w[..., :half]], axis=-1)


def _prep_w_in(w):
    p = _split_cols(w)
    z = lambda n: jnp.zeros(w.shape[:-1] + (n,), w.dtype)
    cat = lambda parts: jnp.concatenate(parts, axis=-1)
    segs = {
        "cq": p["a_cq"], "ckv": p["a_ckv"], "a_gate": p["a_gate"],
        "kr1": cat([z(MLA_NOPE), p["a_krope"], z(LANES - MLA_NOPE - MLA_ROPE)]),
        "kr2": cat([z(MLA_NOPE), _rot_half_cols(p["a_krope"]), z(LANES - MLA_NOPE - MLA_ROPE)]),
        "b_q": p["b_q"] * (DIFF_QK ** -0.5 * LOG2E), "b_k": p["b_k"], "b_gate": p["b_gate"],
        "c_q": p["c_q"] * (HEAD_DIM ** -0.5 * LOG2E), "c_k": p["c_k"], "c_qidx": p["c_qidx"],
        "c_kidx4": cat([p["c_kidx"]] * (LANES // IDX_DIM)),
        "c_widx": cat([p["c_widx"], z(LANES - IDX_HEADS)]),
        "c_gate": p["c_gate"], "d_b": p["d_b"], "d_c": p["d_c"], "d_h": p["d_h"], "d_gate": p["d_gate"],
    }
    w1 = cat([segs[name].astype(BF16) for name, _ in _SEGS])
    wvt = jnp.swapaxes(cat([p["b_v"].astype(BF16), p["c_v"].astype(BF16)]), -1, -2)
    return w1, wvt


def _prep_mla_up(w_uq, w_ukv):
    depth = w_uq.shape[0]
    scale = (MLA_NOPE + MLA_ROPE) ** -0.5 * LOG2E
    wq = w_uq.reshape(depth, MLA_Q_LORA, N_HEADS, MLA_NOPE + MLA_ROPE) * scale
    zq = jnp.zeros((depth, MLA_Q_LORA, N_HEADS, LANES - MLA_NOPE - MLA_ROPE), w_uq.dtype)
    wq1 = jnp.concatenate([wq, zq], axis=-1).reshape(depth, MLA_Q_LORA, N_HEADS * LANES)
    wq2 = jnp.concatenate([jnp.zeros_like(wq[..., :MLA_NOPE]), _rot_half_cols(wq[..., MLA_NOPE:]), zq],
                          axis=-1).reshape(depth, MLA_Q_LORA, N_HEADS * LANES)
    wkv = w_ukv.reshape(depth, MLA_KV_LORA, N_HEADS, MLA_NOPE + HEAD_DIM)
    zk = jnp.zeros((depth, MLA_KV_LORA, N_HEADS, LANES - MLA_NOPE), w_ukv.dtype)
    wk = jnp.concatenate([wkv[..., :MLA_NOPE], zk], axis=-1).reshape(depth, MLA_KV_LORA, N_HEADS * LANES)
    wvt = jnp.swapaxes(wkv[..., MLA_NOPE:].reshape(depth, MLA_KV_LORA, N_HEADS * HEAD_DIM), -1, -2)
    return wq1.astype(BF16), wq2.astype(BF16), wk.astype(BF16), wvt.astype(BF16)


def _rope_tables(seq):
    half = MLA_ROPE // 2
    inv_freq = ROPE_BASE ** (-np.arange(half, dtype=np.float32) / half)
    ang = np.arange(seq, dtype=np.float32)[:, None] * inv_freq[None, :].astype(np.float32)
    cos, sin = np.cos(ang).astype(np.float32), np.sin(ang).astype(np.float32)
    ct = np.zeros((seq, LANES), np.float32)
    st = np.zeros((seq, LANES), np.float32)
    ct[:, :MLA_NOPE] = 1.0
    ct[:, MLA_NOPE:MLA_NOPE + half] = cos
    ct[:, MLA_NOPE + half:MLA_NOPE + MLA_ROPE] = cos
    st[:, MLA_NOPE:MLA_NOPE + half] = sin
    st[:, MLA_NOPE + half:MLA_NOPE + MLA_ROPE] = sin
    return ct, st


def _rel_bucket_np(rel):
    nb = NUM_BUCKETS // 2
    max_exact = nb // 2
    ret = np.where(rel > 0, nb, 0)
    n = np.abs(rel)
    nf = np.maximum(n, max_exact).astype(np.float32)
    large = max_exact + (np.log(nf / np.float32(max_exact)) / np.float32(math.log(MAX_DISTANCE / max_exact))
                         * np.float32(nb - max_exact)).astype(np.int32)
    large = np.minimum(large, nb - 1)
    return (ret + np.where(n < max_exact, n, large)).astype(np.int32)


def _position_constants(seq, tile):
    r = np.arange(tile)
    rel0 = r[:, None] - r[None, :]
    bidx = np.stack([_rel_bucket_np(rel0 - tile), _rel_bucket_np(rel0)]).astype(np.int32)
    far = _rel_bucket_np(np.arange(-(seq - 1), -tile))
    far_bucket = int(far[0]) if far.size else int(_rel_bucket_np(np.array([-tile - 1]))[0])
    assert far.size == 0 or np.all(far == far_bucket)
    mask_add = np.where((r[:, None] // CHUNK) <= (r[None, :] // CHUNK), 0.0, NEG).astype(np.float32)
    return bidx, far_bucket, mask_add


def _bias_kernel(tab_ref, bidx_ref, madd_ref, o_ref, *, far_bucket, buckets):
    j = pl.program_id(0)
    c = tab_ref[j, far_bucket]
    for d in range(2):
        idx = bidx_ref[d]
        acc = jnp.zeros(idx.shape, F32)
        for b in buckets[d]:
            acc = jnp.where(idx == b, (tab_ref[j, b] - c) * LOG2E, acc)
        if d == 1:
            acc = acc + madd_ref[...]
        o_ref[0, d] = acc


def _bias_tiles(table, seq, tile):
    bidx, far_bucket, mask_add = _position_constants(seq, tile)
    n_maps = table.shape[1]
    buckets = tuple(tuple(int(b) for b in np.unique(bidx[d])) for d in range(2))
    return pl.pallas_call(
        functools.partial(_bias_kernel, far_bucket=far_bucket, buckets=buckets),
        grid=(n_maps,),
        in_specs=[
            pl.BlockSpec(memory_space=pltpu.SMEM),
            pl.BlockSpec((2, tile, tile), lambda j: (0, 0, 0)),
            pl.BlockSpec((tile, tile), lambda j: (0, 0)),
        ],
        out_specs=pl.BlockSpec((1, 2, tile, tile), lambda j: (j, 0, 0, 0)),
        out_shape=jax.ShapeDtypeStruct((n_maps, 2, tile, tile), F32),
        compiler_params=_cparams(("arbitrary",)),
        name="bias_tiles",
    )(table.T.astype(F32), jnp.asarray(bidx), jnp.asarray(mask_add))


def _silu(x):
    return x * (1.0 / (1.0 + jnp.exp(-x)))


def _rms(x, g):
    return x * lax.rsqrt(jnp.mean(x * x, axis=-1, keepdims=True) + NORM_EPS) * g


def _inproj_kernel(x_ref, g_ref, w_ref, wvt_ref, qag_ref, wq1_ref, wq2_ref, kvg_ref, wk_ref, wavt_ref, cos_ref, sin_ref,
                   qa_ref, ka_ref, vat_ref, ga_ref, qb_ref, kb_ref, vbt_ref, gb_ref,
                   qc_ref, kc_ref, vct_ref, qi_ref, ki_ref, wi_ref, gc_ref, yd_ref, ud_ref):
    h = _rms(x_ref[...], g_ref[...]).astype(BF16)

    def seg(name):
        off, width = _SEG_OFF[name]
        return jnp.dot(h, w_ref[:, off:off + width], preferred_element_type=F32)

    def seg_pair(first, second):
        off, width = _SEG_OFF[first]
        assert _SEG_OFF[second] == (off + width, width) and width == LANES
        y = jnp.dot(h, w_ref[:, off:off + 2 * width], preferred_element_type=F32)
        return y[:, :width], y[:, width:]

    cos = cos_ref[...]
    sin = sin_ref[...]

    cqn = _rms(seg("cq"), qag_ref[...]).astype(BF16)
    q1 = jnp.dot(cqn, wq1_ref[...], preferred_element_type=F32)
    q2 = jnp.dot(cqn, wq2_ref[...], preferred_element_type=F32)
    ckv, kr1 = seg_pair("ckv", "kr1")
    kr2, kidx4 = seg_pair("kr2", "c_kidx4")
    kr = kr1 * cos + kr2 * sin
    ckvn = _rms(ckv, kvg_ref[...]).astype(BF16)
    kn = jnp.dot(ckvn, wk_ref[...], preferred_element_type=F32)
    for hd in range(N_HEADS):
        sl = slice(hd * LANES, (hd + 1) * LANES)
        qa_ref[:, sl] = (q1[:, sl] * cos + q2[:, sl] * sin).astype(BF16)
        ka_ref[:, sl] = (kn[:, sl] + kr).astype(BF16)
    ga_ref[...] = _silu(seg("a_gate")).astype(BF16)

    def put(ref, first_token, vt):
        tile = ref.shape[2]
        ref[first_token // tile, :, first_token % tile:first_token % tile + VT_TILE] = vt

    for c in range(x_ref.shape[0] // VT_TILE):
        rows = slice(c * VT_TILE, (c + 1) * VT_TILE)
        put(vat_ref, c * VT_TILE,
            lax.dot_general(wavt_ref[...], ckvn[rows], _NT, preferred_element_type=F32).astype(BF16))
        vt = lax.dot_general(wvt_ref[...], h[rows], _NT, preferred_element_type=F32).astype(BF16)
        put(vbt_ref, c * VT_TILE, vt[:GROUP_WIDTH])
        put(vct_ref, c * VT_TILE, vt[GROUP_WIDTH:])

    qb_ref[...] = seg("b_q").astype(BF16)
    kb_ref[...] = seg("b_k").astype(BF16)
    gb_ref[...] = _silu(seg("b_gate")).astype(BF16)

    qc_ref[...] = seg("c_q").astype(BF16)
    kc_ref[...] = seg("c_k").astype(BF16)
    qi_ref[...] = seg("c_qidx").astype(BF16)
    ki_ref[...] = kidx4.astype(BF16)
    wi_ref[...] = seg("c_widx")
    gc_ref[...] = _silu(seg("c_gate")).astype(BF16)

    yd_ref[...] = (seg("d_b") * _silu(seg("d_gate"))).astype(BF16)
    ud_ref[...] = seg("d_c") * seg("d_h")


def _layer_spec(a, l):
    return pl.BlockSpec((None,) + a.shape[1:], lambda i: (l,) + (0,) * (a.ndim - 1))


def _inproj(x2, l, g, w1, wvt, qag, wq1, wq2, kvg, wk, wavt, cos_t, sin_t, seq, tiles):
    n = x2.shape[0]
    tm = min(PROJ_ROWS, seq)
    tiles_per_seq = seq // tm
    row = lambda w: pl.BlockSpec((tm, w), lambda i: (i, 0))
    full = lambda a: _layer_spec(a, l)
    tab = pl.BlockSpec((tm, LANES), lambda i: (i % tiles_per_seq, 0))
    vt_spec = lambda t: pl.BlockSpec((tm // t, GROUP_WIDTH, t), lambda i: (i, 0, 0))
    vt_shape = lambda t: jax.ShapeDtypeStruct((n // t, GROUP_WIDTH, t), BF16)
    outs = (("qa", 512, BF16), ("ka", 512, BF16), ("vat", None, tiles[0]), ("ga", 256, BF16),
            ("qb", 256, BF16), ("kb", 256, BF16), ("vbt", None, tiles[1]), ("gb", 256, BF16),
            ("qc", 256, BF16), ("kc", 256, BF16), ("vct", None, tiles[2]), ("qi", 256, BF16),
            ("ki", 128, BF16), ("wi", 128, F32), ("gc", 256, BF16), ("yd", 256, BF16), ("ud", 256, F32))
    res = pl.pallas_call(
        _inproj_kernel,
        grid=(n // tm,),
        in_specs=[row(D_MODEL), full(g), full(w1), full(wvt), full(qag), full(wq1), full(wq2), full(kvg), full(wk),
                  full(wavt), tab, tab],
        out_specs=[vt_spec(dt) if w is None else row(w) for _, w, dt in outs],
        out_shape=[vt_shape(dt) if w is None else jax.ShapeDtypeStruct((n, w), dt) for _, w, dt in outs],
        compiler_params=_cparams(("parallel",)),
        name="inproj",
    )(x2, g, w1, wvt, qag, wq1, wq2, kvg, wk, wavt, cos_t, sin_t)
    return {name: o for (name, _, _), o in zip(outs, res)}


def _flash_scratch(n_maps, t):
    unit = _far_unit_tiles(t) * t
    return [pltpu.VMEM((n_maps, 1, t), F32), pltpu.VMEM((n_maps, 1, t), F32), pltpu.VMEM((n_maps, HEAD_DIM, t), F32),
            pltpu.VMEM((n_maps, unit, t), F32), pltpu.VMEM((n_maps, unit, t), F32)]


def _far_unit_tiles(t):
    return max(1, FAR_UNIT_KEYS // t)


def _flash_update(s, vts, old):
    t = vts[0].shape[1]
    m_tile = jnp.max(s, axis=0, keepdims=True)
    m_new = m_tile if old is None else jnp.maximum(old[0], m_tile)
    p = jnp.exp2(s - m_new)
    l_tile = jnp.sum(p, axis=0, keepdims=True)
    pb = p.astype(BF16)
    pv = jnp.dot(vts[0], pb[0:t], preferred_element_type=F32)
    for d in range(1, len(vts)):
        pv = pv + jnp.dot(vts[d], pb[d * t:(d + 1) * t], preferred_element_type=F32)
    if old is None:
        return m_new, l_tile, pv
    alpha = jnp.exp2(old[0] - m_new)
    return m_new, alpha * old[1] + l_tile, alpha * old[2] + pv


def _flash_sweep(i, t, n_maps, qs, keys, values, state, diag_add, near_add=None, mask=None):
    m_ref, l_ref, acc_ref, buf_a, buf_b = state
    n_far = i if near_add is None else jnp.maximum(i - 1, 0)
    ut = _far_unit_tiles(t)
    n_units = n_far // ut

    def load_state():
        return [(m_ref[j], l_ref[j], acc_ref[j]) for j in range(n_maps)]

    def store_state(cur):
        for j in range(n_maps):
            m_ref[j], l_ref[j], acc_ref[j] = cur[j]

    def tile_logits(j, kt, n_tiles, shared, extra=None):
        rows = pl.ds(pl.multiple_of(kt * t, t), n_tiles * t)
        s = lax.dot_general(keys(j, rows), qs[j], _NT, preferred_element_type=F32)
        add = shared
        if extra is not None:
            add = extra(j) if add is None else add + extra(j)
        return s if add is None else s + add

    def shared_mask(kt, n_tiles):
        if mask is None:
            return None
        tiles = [mask(kt + d) for d in range(n_tiles)]
        return tiles[0] if n_tiles == 1 else jnp.concatenate(tiles, axis=0)

    def run(units, first=False, prefetch=False):
        cur = [None] * n_maps if first else load_state()
        shared = [shared_mask(kt, n_tiles) for kt, n_tiles, _ in units]
        items = [(u, j) for u in range(len(units)) for j in range(n_maps)]
        if prefetch:
            items += [(None, j) for j in range(n_maps)]
            shared.append(shared_mask(0, ut))

        def logits(u, j):
            if u is None:
                return tile_logits(j, 0, ut, shared[-1])
            kt, n_tiles, extra = units[u]
            return tile_logits(j, kt, n_tiles, shared[u], extra)

        queue = [logits(*it) for it in items[:FLASH_LOOKAHEAD]]
        for n, (u, j) in enumerate(items):
            if n + FLASH_LOOKAHEAD < len(items):
                queue.append(logits(*items[n + FLASH_LOOKAHEAD]))
            s = queue.pop(0)
            if u is None:
                buf_a[j] = s
            else:
                kt, n_tiles, _ = units[u]
                cur[j] = _flash_update(s, [values(j, kt + d) for d in range(n_tiles)], cur[j])
        store_state(cur)

    def first_run(prefetch):
        if near_add is None:
            return lambda: run([(i, 1, diag_add)], first=True, prefetch=prefetch)
        return lambda: run([(i, 1, diag_add), (i - 1, 1, near_add)], first=True, prefetch=prefetch)

    if near_add is None:
        pl.when(n_units == 0)(first_run(False))
    else:
        pl.when(i == 0)(lambda: run([(i, 1, diag_add)], first=True))
        pl.when((i >= 1) & (n_units == 0))(first_run(False))
    pl.when(n_units >= 1)(first_run(True))

    def consume(buf, kt, j, old):
        return _flash_update(buf[j], [values(j, kt + d) for d in range(ut)], old)

    def overlapped(cur, buf_in, buf_out, u):
        kt_next = ut * (u + 1)
        shared = shared_mask(kt_next, ut)
        for j in range(n_maps):
            buf_out[j] = tile_logits(j, kt_next, ut, shared)
            cur[j] = consume(buf_in, ut * u, j, cur[j])

    def far_body(c, carry):
        cur = load_state()
        overlapped(cur, buf_a, buf_b, 2 * c)
        overlapped(cur, buf_b, buf_a, 2 * c + 1)
        store_state(cur)
        return carry

    lax.fori_loop(0, (n_units - 1) // 2, far_body, 0)

    @pl.when(n_units % 2 == 1)
    def _():
        cur = load_state()
        for j in range(n_maps):
            cur[j] = consume(buf_a, ut * (n_units - 1), j, cur[j])
        store_state(cur)

    @pl.when((n_units >= 2) & (n_units % 2 == 0))
    def _():
        cur = load_state()
        overlapped(cur, buf_a, buf_b, n_units - 2)
        for j in range(n_maps):
            cur[j] = consume(buf_b, ut * (n_units - 1), j, cur[j])
        store_state(cur)

    for r in range(1, ut):
        pl.when(n_far % ut >= r)(lambda r=r: run([(n_far - r, 1, None)]))
    return [acc_ref[j] / l_ref[j] for j in range(n_maps)]


def _lane_mask(lo, hi, dtype):
    lane = lax.broadcasted_iota(jnp.int32, (1, LANES), 1)
    return ((lane >= lo) & (lane < hi)).astype(dtype)


def _store_gated(o_ref, g_ref, heads_t):
    o = jnp.concatenate(heads_t, axis=0).T
    o_ref[...] = (o * g_ref[...].astype(F32)).astype(BF16)


def _att_specs(t, seq, widths_q, widths_k):
    nq = seq // t
    qspec = lambda w: pl.BlockSpec((t, w), lambda b, i: (b * nq + i, 0))
    kspec = lambda w: pl.BlockSpec((seq, w), lambda b, i: (b, 0), pipeline_mode=_ONE_BUFFER)
    vspec = pl.BlockSpec((nq, GROUP_WIDTH, t), lambda b, i: (b, 0, 0), pipeline_mode=_ONE_BUFFER)
    return [qspec(w) for w in widths_q], [kspec(w) for w in widths_k], vspec, qspec(GROUP_WIDTH)


def _mla_kernel(q_ref, k_ref, vt_ref, g_ref, madd_ref, o_ref, *state, t):
    i = pl.program_id(1)
    qs = [q_ref[:, hd * LANES:(hd + 1) * LANES] for hd in range(N_HEADS)]
    keys = lambda hd, rows: k_ref[rows, hd * LANES:(hd + 1) * LANES]
    values = lambda hd, kt: vt_ref[kt, hd * HEAD_DIM:(hd + 1) * HEAD_DIM, :]
    outs = _flash_sweep(i, t, N_HEADS, qs, keys, values, state, lambda hd: madd_ref[...])
    _store_gated(o_ref, g_ref, outs)


def _mla(p, mask_add, batch, seq, t):
    qs, ks, vspec, ospec = _att_specs(t, seq, (512, 256), (512,))
    return pl.pallas_call(
        functools.partial(_mla_kernel, t=t),
        grid=(batch, seq // t),
        in_specs=[qs[0], ks[0], vspec, qs[1], pl.BlockSpec((t, t), lambda b, i: (0, 0), pipeline_mode=_ONE_BUFFER)],
        out_specs=ospec,
        out_shape=jax.ShapeDtypeStruct((batch * seq, GROUP_WIDTH), BF16),
        scratch_shapes=_flash_scratch(N_HEADS, t),
        compiler_params=_cparams(("parallel", "arbitrary")),
        name="mla_attention",
    )(p["qa"], p["ka"], p["vat"], p["ga"], mask_add)


def _diff_kernel(q_ref, k_ref, vt_ref, g_ref, bias_ref, lam_ref, subg_ref, o_ref, *state, t, lambda_init):
    i = pl.program_id(1)
    lp = lam_ref[...]
    lam = (jnp.exp(jnp.sum(lp[0:1] * lp[1:2], axis=-1, keepdims=True))
           - jnp.exp(jnp.sum(lp[2:3] * lp[3:4], axis=-1, keepdims=True)) + lambda_init)
    qs = [q_ref[:, (j // 4) * LANES:(j // 4 + 1) * LANES] * _lane_mask((j % 4) * DIFF_QK, (j % 4 + 1) * DIFF_QK, BF16)
          for j in range(N_DIFF_MAPS)]
    keys = lambda j, rows: k_ref[rows, (j // 4) * LANES:(j // 4 + 1) * LANES]
    values = lambda j, kt: vt_ref[kt, (j // 2) * HEAD_DIM:(j // 2 + 1) * HEAD_DIM, :]
    maps = _flash_sweep(i, t, N_DIFF_MAPS, qs, keys, values, state, lambda j: bias_ref[j, 1], lambda j: bias_ref[j, 0])
    outs = []
    for hd in range(N_HEADS):
        o = maps[2 * hd] - lam * maps[2 * hd + 1]
        ms = jnp.mean(o * o, axis=0, keepdims=True)
        outs.append(o * lax.rsqrt(ms + NORM_EPS) * subg_ref[...] * (1.0 - lambda_init))
    _store_gated(o_ref, g_ref, outs)


def _diff(p, bias, lam_params, subln_g, lambda_init, batch, seq, t):
    qs, ks, vspec, ospec = _att_specs(t, seq, (256, 256), (256,))
    subg = jnp.broadcast_to(subln_g.astype(F32)[:, None], (HEAD_DIM, t))
    return pl.pallas_call(
        functools.partial(_diff_kernel, t=t, lambda_init=lambda_init),
        grid=(batch, seq // t),
        in_specs=[qs[0], ks[0], vspec, qs[1],
                  pl.BlockSpec((N_DIFF_MAPS, 2, t, t), lambda b, i: (0, 0, 0, 0), pipeline_mode=_ONE_BUFFER),
                  pl.BlockSpec((4, DIFF_QK), lambda b, i: (0, 0)),
                  pl.BlockSpec((HEAD_DIM, t), lambda b, i: (0, 0))],
        out_specs=ospec,
        out_shape=jax.ShapeDtypeStruct((batch * seq, GROUP_WIDTH), BF16),
        scratch_shapes=_flash_scratch(N_DIFF_MAPS, t),
        compiler_params=_cparams(("parallel", "arbitrary")),
        name="diff_attention",
    )(p["qb"], p["kb"], p["vbt"], p["gb"], bias, lam_params.astype(F32), subg)


def _sort_key(x):
    b = lax.bitcast_convert_type(x + 0.0, jnp.int32)
    return b ^ ((b >> 31) & 0x7FFFFFFF)


def _dsa_kernel(q_ref, k_ref, vt_ref, qi_ref, ki_ref, wi_ref, g_ref, bias_ref, tri_ref, o_ref,
                score_ref, hi_ref, lo_ref, thr_ref, cnt_ref, *state, t, topk):
    i = pl.program_id(1)
    n_kt = i + 1
    idx_scale = (IDX_HEADS ** -0.5) * (IDX_DIM ** -0.5)

    wit = wi_ref[...].T * idx_scale
    qm, wrow = [], []
    for ih in range(IDX_HEADS):
        grp = slice((ih // 4) * LANES, (ih // 4 + 1) * LANES)
        lo = (ih % 4) * IDX_DIM
        qm.append(qi_ref[:, grp] * _lane_mask(lo, lo + IDX_DIM, BF16))
        wrow.append(wit[ih:ih + 1, :])

    def idx_tile(kt):
        kk = ki_ref[pl.ds(pl.multiple_of(kt * t, t), t), :]
        dot = lambda ih: lax.dot_general(kk, qm[ih], _NT, preferred_element_type=F32)
        queue = [dot(ih) for ih in range(FLASH_LOOKAHEAD)]
        tot = None
        for ih in range(IDX_HEADS):
            if ih + FLASH_LOOKAHEAD < IDX_HEADS:
                queue.append(dot(ih + FLASH_LOOKAHEAD))
            term = wrow[ih] * jnp.maximum(queue.pop(0), 0.0)
            tot = term if tot is None else tot + term
        return tot + 0.0

    def store_scores(kt, score):
        score_ref[kt] = score
        key = _sort_key(score)
        hi_ref[kt] = (key >> HALF_BITS).astype(jnp.int16)
        lo_ref[kt] = ((key & HALF_MASK) - HALF_BIAS).astype(jnp.int16)

    def key_tiles(first, n):
        for d in range(n):
            store_scores(first + d, idx_tile(first + d))

    def key_quad_body(c, carry):
        key_tiles(4 * c, 4)
        return carry

    lax.fori_loop(0, i // 4, key_quad_body, 0)
    pl.when(i % 4 >= 2)(lambda: key_tiles(i - i % 4, 2))
    pl.when(i % 2 == 1)(lambda: key_tiles(i - 1, 1))
    kchunk = lax.broadcasted_iota(jnp.int32, (t, t), 0) // CHUNK
    qchunk = lax.broadcasted_iota(jnp.int32, (t, t), 1) // CHUNK
    store_scores(i, jnp.where(kchunk <= qchunk, idx_tile(i), -jnp.inf))

    def count(pred):
        def body(kt, c):
            hit = jnp.where(pred(score_ref[kt]), 1.0, 0.0)
            return c + jnp.sum(hit.reshape(t // SUBLANES, SUBLANES, t), axis=0)
        part = lax.fori_loop(0, n_kt, body, jnp.zeros((SUBLANES, t), F32))
        return jnp.sum(part, axis=0, keepdims=True)

    rows16 = 2 * SUBLANES
    n_acc = 4

    def key_to_score(key):
        key = jnp.maximum(key, KEY_LOWEST)
        return lax.bitcast_convert_type(key ^ ((key >> 31) & 0x7FFFFFFF), F32)

    def packed_search(n):
        def count16(ref, pred):
            one, zero = jnp.int16(1), jnp.int16(0)
            accs = [jnp.zeros((rows16, t), jnp.int16) for _ in range(n_acc)]
            for kt in range(n):
                x = ref[kt].reshape(t // rows16, rows16, t)
                for r in range(t // rows16):
                    accs[r % n_acc] = accs[r % n_acc] + jnp.where(pred(x[r]), one, zero)
            tot = (accs[0] + accs[1]) + (accs[2] + accs[3])
            return jnp.sum(tot.astype(jnp.int32), axis=0, keepdims=True)

        def select16(ref, need):
            def bit_body(b, res):
                cand = res | (jnp.int32(1) << (HALF_BITS - 1 - b))
                cand16 = (cand - HALF_BIAS).astype(jnp.int16)
                cnt = count16(ref, lambda x: x >= cand16)
                return jnp.where(cnt >= need, cand, res)
            return lax.fori_loop(0, HALF_BITS, bit_body, jnp.zeros((1, t), jnp.int32)) - HALF_BIAS

        hi_thr = select16(hi_ref, topk)
        hi_thr16 = hi_thr.astype(jnp.int16)
        need_lo = topk - count16(hi_ref, lambda x: x > hi_thr16)
        for kt in range(n):
            lo_ref[kt] = jnp.where(hi_ref[kt] == hi_thr16, lo_ref[kt], jnp.int16(-HALF_BIAS))
        lo_thr = select16(lo_ref, need_lo)
        thr_ref[...] = key_to_score((hi_thr << HALF_BITS) | (lo_thr + HALF_BIAS))

    for n in range(1, score_ref.shape[0] + 1):
        pl.when(n_kt == n)(functools.partial(packed_search, n))

    cnt_ref[...] = count(lambda x: x >= thr_ref[...])
    n_valid = ((i * t + lax.broadcasted_iota(jnp.int32, (1, t), 1)) // CHUNK + 1) * CHUNK
    few = n_valid <= topk

    @pl.when(jnp.max(jnp.where(few, 0.0, jnp.abs(cnt_ref[...] - topk))) > 0.0)
    def _():
        cnt_gt = count(lambda x: x > thr_ref[...])
        wrong = (cnt_ref[...] < topk) | (cnt_gt >= topk)

        @pl.when(jnp.max(jnp.where(few | jnp.logical_not(wrong), 0.0, 1.0)) > 0.0)
        def _():
            def bit_body(b, res):
                cand = res | (jnp.int32(1) << (31 - b))
                cand_score = key_to_score(cand ^ INT_MIN)
                cnt = count(lambda x: x >= cand_score)
                return jnp.where(cnt >= topk, cand, res)
            res = lax.fori_loop(0, 32, bit_body, jnp.zeros((1, t), jnp.int32))
            thr_ref[...] = key_to_score(res ^ INT_MIN)
            cnt_ref[...] = count(lambda x: x >= thr_ref[...])

    thr = thr_ref[...]
    has_ties = jnp.max(jnp.where(few, 0.0, cnt_ref[...])) > topk

    @pl.when(jnp.logical_not(has_ties))
    def _():
        def body(kt, c):
            score_ref[kt] = jnp.where(score_ref[kt] >= thr, 0.0, NEG)
            return c
        lax.fori_loop(0, n_kt, body, 0)

    @pl.when(has_ties)
    def _():
        need = topk - count(lambda x: x > thr)

        def body(kt, seen):
            x = score_ref[kt]
            eq = jnp.where(x == thr, 1.0, 0.0)
            rank = seen + jnp.dot(tri_ref[...], eq.astype(BF16), preferred_element_type=F32)
            sel = (x > thr) | ((x == thr) & (rank <= need))
            score_ref[kt] = jnp.where(sel, 0.0, NEG)
            return seen + jnp.sum(eq, axis=0, keepdims=True)
        lax.fori_loop(0, n_kt, body, jnp.zeros((1, t), F32))

    qs = [q_ref[:, (hd // 2) * LANES:(hd // 2 + 1) * LANES]
          * _lane_mask((hd % 2) * HEAD_DIM, (hd % 2 + 1) * HEAD_DIM, BF16) for hd in range(N_HEADS)]
    keys = lambda hd, rows: k_ref[rows, (hd // 2) * LANES:(hd // 2 + 1) * LANES]
    values = lambda hd, kt: vt_ref[kt, hd * HEAD_DIM:(hd + 1) * HEAD_DIM, :]
    outs = _flash_sweep(i, t, N_HEADS, qs, keys, values, state, lambda hd: bias_ref[hd, 1], lambda hd: bias_ref[hd, 0],
                        mask=lambda kt: score_ref[kt])
    _store_gated(o_ref, g_ref, outs)


def _dsa(p, bias, batch, seq, t):
    topk = min(TOPK_MAX, seq // 4)
    nq = seq // t
    qs, ks, vspec, ospec = _att_specs(t, seq, (256, 256, 128, 256), (256, 128))
    tri = jnp.asarray(np.tril(np.ones((t, t), np.float32)), BF16)
    return pl.pallas_call(
        functools.partial(_dsa_kernel, t=t, topk=topk),
        grid=(batch, nq),
        in_specs=[qs[0], ks[0], vspec, qs[1], ks[1], qs[2], qs[3],
                  pl.BlockSpec((N_HEADS, 2, t, t), lambda b, i: (0, 0, 0, 0), pipeline_mode=_ONE_BUFFER),
                  pl.BlockSpec((t, t), lambda b, i: (0, 0), pipeline_mode=_ONE_BUFFER)],
        out_specs=ospec,
        out_shape=jax.ShapeDtypeStruct((batch * seq, GROUP_WIDTH), BF16),
        scratch_shapes=[pltpu.VMEM((nq, t, t), F32), pltpu.VMEM((nq, t, t), jnp.int16),
                        pltpu.VMEM((nq, t, t), jnp.int16), pltpu.VMEM((1, t), F32), pltpu.VMEM((1, t), F32)]
        + _flash_scratch(N_HEADS, t),
        compiler_params=_cparams(("parallel", "arbitrary")),
        name="dsa_attention",
    )(p["qc"], p["kc"], p["vct"], p["qi"], p["ki"], p["wi"], p["gc"], bias, tri)


def _outproj_kernel(x_ref, ya_ref, yb_ref, yc_ref, yd_ref, ud_ref, halo_ref, cw_ref, w_ref, fg_ref, o_ref,
                    *, tiles_per_seq, final):
    i = pl.program_id(0)
    tm = x_ref.shape[0]
    u = ud_ref[...]
    halo = jnp.where(i % tiles_per_seq == 0, 0.0, halo_ref[...])
    row8 = lax.broadcasted_iota(jnp.int32, (SUBLANES, 1), 0)

    def shifted(k):
        r = pltpu.roll(u, k, 0)
        head = jnp.where(row8 < k, pltpu.roll(halo, k, 0), r[0:SUBLANES])
        return jnp.concatenate([head, r[SUBLANES:tm]], axis=0)

    cw = cw_ref[...]
    conv = cw[0:1] * shifted(CONV_WIDTH - 1)
    for j in range(1, CONV_WIDTH):
        conv = conv + cw[j:j + 1] * (shifted(CONV_WIDTH - 1 - j) if j < CONV_WIDTH - 1 else u)
    yd = (conv * yd_ref[...].astype(F32)).astype(BF16)
    acc = x_ref[...]
    for g, y in enumerate((ya_ref[...], yb_ref[...], yc_ref[...], yd)):
        acc = acc + jnp.dot(y, w_ref[g * GROUP_WIDTH:(g + 1) * GROUP_WIDTH, :], preferred_element_type=F32)
    if final:
        acc = _rms(acc, fg_ref[...])
    o_ref[...] = acc


def _outproj(x2, l, ya, yb, yc, yd, ud, conv_w, w_out, final_g, seq, final):
    n = x2.shape[0]
    tm = min(PROJ_ROWS, seq)
    tiles_per_seq = seq // tm
    row = lambda w: pl.BlockSpec((tm, w), lambda i: (i, 0))
    halo = pl.BlockSpec((SUBLANES, GROUP_WIDTH), lambda i: (jnp.maximum(i * (tm // SUBLANES) - 1, 0), 0))
    return pl.pallas_call(
        functools.partial(_outproj_kernel, tiles_per_seq=tiles_per_seq, final=final),
        grid=(n // tm,),
        in_specs=[row(D_MODEL), row(256), row(256), row(256), row(256), row(256), halo,
                  _layer_spec(conv_w, l), _layer_spec(w_out, l), pl.BlockSpec(final_g.shape, lambda i: (0, 0))],
        out_specs=row(D_MODEL),
        out_shape=jax.ShapeDtypeStruct((n, D_MODEL), F32),
        compiler_params=_cparams(("parallel",)),
        name="outproj",
    )(x2, ya, yb, yc, yd, ud, ud, conv_w, w_out, final_g)


def kernel(x, norm_g, w_in, mla_qa_g, mla_w_uq, mla_kva_g, mla_w_ukv, diff_lambda, diff_subln_g, conv_w, w_out,
           rel_bias, final_g):
    batch, seq, _ = x.shape
    depth = w_in.shape[0]
    tm = min(PROJ_ROWS, seq)
    tiles = tuple(min(t, seq) for t in (MLA_TILE, DIFF_TILE, DSA_TILE))
    assert seq % tm == 0 and tm % VT_TILE == 0 and conv_w.shape[1] == CONV_WIDTH <= SUBLANES
    assert all(seq % t == 0 and t % CHUNK == 0 and tm % t == 0 and t % VT_TILE == 0 for t in tiles)

    cos_t, sin_t = (jnp.asarray(a) for a in _rope_tables(seq))
    mask_add = jnp.asarray(_position_constants(seq, tiles[0])[2])
    bias_diff = _bias_tiles(rel_bias[:, :N_DIFF_MAPS], seq, tiles[1])
    bias_dsa = _bias_tiles(rel_bias[:, N_DIFF_MAPS:], seq, tiles[2])

    w1, wvt = _prep_w_in(w_in)
    wq1, wq2, wk, wavt = _prep_mla_up(mla_w_uq, mla_w_ukv)
    row3 = lambda a: a[:, None, :]
    w_out_b = w_out.astype(BF16)

    x2 = x.reshape(batch * seq, D_MODEL)
    for l in range(depth):
        lambda_init = 0.8 - 0.6 * math.exp(-0.3 * l)
        p = _inproj(x2, l, row3(norm_g), w1, wvt, row3(mla_qa_g), wq1, wq2, row3(mla_kva_g), wk, wavt,
                    cos_t, sin_t, seq, tiles)
        ya = _mla(p, mask_add, batch, seq, tiles[0])
        yb = _diff(p, bias_diff, diff_lambda[l], diff_subln_g[l], lambda_init, batch, seq, tiles[1])
        yc = _dsa(p, bias_dsa, batch, seq, tiles[2])
        x2 = _outproj(x2, l, ya, yb, yc, p["yd"], p["ud"], conv_w, w_out_b, final_g[None, :],
                      seq, final=(l == depth - 1))
    return x2.reshape(batch, seq, D_MODEL)
```

```python
import functools
import math

import numpy as np
import jax
import jax.numpy as jnp
from jax import lax
from jax.experimental import pallas as pl
from jax.experimental.pallas import tpu as pltpu

F32 = jnp.float32
BF16 = jnp.bfloat16

D_MODEL = 1024
CHUNK = 64
N_HEADS = 4
HEAD_DIM = 64
GROUP_WIDTH = N_HEADS * HEAD_DIM
MLA_Q_LORA = 256
MLA_KV_LORA = 128
MLA_NOPE = 64
MLA_ROPE = 32
ROPE_BASE = 10000.0
DIFF_QK = 32
IDX_HEADS = 8
IDX_DIM = 32
TOPK_MAX = 256
CONV_WIDTH = 3
NUM_BUCKETS = 32
MAX_DISTANCE = 128
N_DIFF_MAPS = 2 * N_HEADS
NORM_EPS = 1e-6
NEG = -1e30
LOG2E = math.log2(math.e)
INT_MIN = -2 ** 31
KEY_LOWEST = int(np.float32(-np.finfo(np.float32).max).view(np.int32)) ^ 0x7FFFFFFF
HALF_BITS = 16
HALF_MASK = 2 ** HALF_BITS - 1
HALF_BIAS = 2 ** (HALF_BITS - 1)

LANES = 128
SUBLANES = 8
MLA_TILE = 256
DIFF_TILE = 256
DSA_TILE = 512
VT_TILE = 256
FAR_UNIT_KEYS = 512
PROJ_ROWS = 512
FLASH_LOOKAHEAD = 6
V7X_VMEM_BYTES = 64 * 1024 * 1024
VMEM_LIMIT = V7X_VMEM_BYTES * 7 // 8

_IN_SPLITS = (
    ("a_cq", 256), ("a_ckv", 128), ("a_krope", 32), ("a_gate", 256),
    ("b_q", 256), ("b_k", 256), ("b_v", 256), ("b_gate", 256),
    ("c_q", 256), ("c_k", 256), ("c_v", 256),
    ("c_qidx", 256), ("c_kidx", 32), ("c_widx", 8), ("c_gate", 256),
    ("d_b", 256), ("d_c", 256), ("d_h", 256), ("d_gate", 256),
)

_SEGS = (
    ("cq", 256), ("ckv", 128), ("kr1", 128), ("kr2", 128), ("c_kidx4", 128), ("a_gate", 256),
    ("b_q", 256), ("b_k", 256), ("b_gate", 256),
    ("c_q", 256), ("c_k", 256), ("c_qidx", 256),
    ("c_gate", 256), ("d_b", 256), ("d_c", 256), ("d_h", 256), ("d_gate", 256), ("c_widx", 128),
)
_SEG_OFF = {}
_off = 0
for _name, _w in _SEGS:
    _SEG_OFF[_name] = (_off, _w)
    _off += _w
W1_COLS = _off

_NT = (((1,), (1,)), ((), ()))
_ONE_BUFFER = pl.Buffered(1)


def _cparams(sem):
    return pltpu.CompilerParams(dimension_semantics=sem, vmem_limit_bytes=VMEM_LIMIT)


def _split_cols(w):
    out, off = {}, 0
    for name, width in _IN_SPLITS:
        out[name] = w[..., off:off + width]
        off += width
    return out


def _rot_half_cols(w):
    half = MLA_ROPE // 2
    return jnp.concatenate([-w[..., half:], w[..., :half]], axis=-1)


def _prep_w_in(w):
    p = _split_cols(w)
    z = lambda n: jnp.zeros(w.shape[:-1] + (n,), w.dtype)
    cat = lambda parts: jnp.concatenate(parts, axis=-1)
    segs = {
        "cq": p["a_cq"], "ckv": p["a_ckv"], "a_gate": p["a_gate"],
        "kr1": cat([z(MLA_NOPE), p["a_krope"], z(LANES - MLA_NOPE - MLA_ROPE)]),
        "kr2": cat([z(MLA_NOPE), _rot_half_cols(p["a_krope"]), z(LANES - MLA_NOPE - MLA_ROPE)]),
        "b_q": p["b_q"] * (DIFF_QK ** -0.5 * LOG2E), "b_k": p["b_k"], "b_gate": p["b_gate"],
        "c_q": p["c_q"] * (HEAD_DIM ** -0.5 * LOG2E), "c_k": p["c_k"], "c_qidx": p["c_qidx"],
        "c_kidx4": cat([p["c_kidx"]] * (LANES // IDX_DIM)),
        "c_widx": cat([p["c_widx"], z(LANES - IDX_HEADS)]),
        "c_gate": p["c_gate"], "d_b": p["d_b"], "d_c": p["d_c"], "d_h": p["d_h"], "d_gate": p["d_gate"],
    }
    w1 = cat([segs[name].astype(BF16) for name, _ in _SEGS])
    wvt = jnp.swapaxes(cat([p["b_v"].astype(BF16), p["c_v"].astype(BF16)]), -1, -2)
    return w1, wvt


def _prep_mla_up(w_uq, w_ukv):
    depth = w_uq.shape[0]
    scale = (MLA_NOPE + MLA_ROPE) ** -0.5 * LOG2E
    wq = w_uq.reshape(depth, MLA_Q_LORA, N_HEADS, MLA_NOPE + MLA_ROPE) * scale
    zq = jnp.zeros((depth, MLA_Q_LORA, N_HEADS, LANES - MLA_NOPE - MLA_ROPE), w_uq.dtype)
    wq1 = jnp.concatenate([wq, zq], axis=-1).reshape(depth, MLA_Q_LORA, N_HEADS * LANES)
    wq2 = jnp.concatenate([jnp.zeros_like(wq[..., :MLA_NOPE]), _rot_half_cols(wq[..., MLA_NOPE:]), zq],
                          axis=-1).reshape(depth, MLA_Q_LORA, N_HEADS * LANES)
    wkv = w_ukv.reshape(depth, MLA_KV_LORA, N_HEADS, MLA_NOPE + HEAD_DIM)
    zk = jnp.zeros((depth, MLA_KV_LORA, N_HEADS, LANES - MLA_NOPE), w_ukv.dtype)
    wk = jnp.concatenate([wkv[..., :MLA_NOPE], zk], axis=-1).reshape(depth, MLA_KV_LORA, N_HEADS * LANES)
    wvt = jnp.swapaxes(wkv[..., MLA_NOPE:].reshape(depth, MLA_KV_LORA, N_HEADS * HEAD_DIM), -1, -2)
    return wq1.astype(BF16), wq2.astype(BF16), wk.astype(BF16), wvt.astype(BF16)


def _rope_tables(seq):
    half = MLA_ROPE // 2
    inv_freq = ROPE_BASE ** (-np.arange(half, dtype=np.float32) / half)
    ang = np.arange(seq, dtype=np.float32)[:, None] * inv_freq[None, :].astype(np.float32)
    cos, sin = np.cos(ang).astype(np.float32), np.sin(ang).astype(np.float32)
    ct = np.zeros((seq, LANES), np.float32)
    st = np.zeros((seq, LANES), np.float32)
    ct[:, :MLA_NOPE] = 1.0
    ct[:, MLA_NOPE:MLA_NOPE + half] = cos
    ct[:, MLA_NOPE + half:MLA_NOPE + MLA_ROPE] = cos
    st[:, MLA_NOPE:MLA_NOPE + half] = sin
    st[:, MLA_NOPE + half:MLA_NOPE + MLA_ROPE] = sin
    return ct, st


def _rel_bucket_np(rel):
    nb = NUM_BUCKETS // 2
    max_exact = nb // 2
    ret = np.where(rel > 0, nb, 0)
    n = np.abs(rel)
    nf = np.maximum(n, max_exact).astype(np.float32)
    large = max_exact + (np.log(nf / np.float32(max_exact)) / np.float32(math.log(MAX_DISTANCE / max_exact))
                         * np.float32(nb - max_exact)).astype(np.int32)
    large = np.minimum(large, nb - 1)
    return (ret + np.where(n < max_exact, n, large)).astype(np.int32)


def _position_constants(seq, tile):
    r = np.arange(tile)
    rel0 = r[:, None] - r[None, :]
    bidx = np.stack([_rel_bucket_np(rel0 - tile), _rel_bucket_np(rel0)]).astype(np.int32)
    far = _rel_bucket_np(np.arange(-(seq - 1), -tile))
    far_bucket = int(far[0]) if far.size else int(_rel_bucket_np(np.array([-tile - 1]))[0])
    assert far.size == 0 or np.all(far == far_bucket)
    mask_add = np.where((r[:, None] // CHUNK) <= (r[None, :] // CHUNK), 0.0, NEG).astype(np.float32)
    return bidx, far_bucket, mask_add


def _bias_kernel(tab_ref, bidx_ref, madd_ref, o_ref, *, far_bucket, buckets):
    j = pl.program_id(0)
    c = tab_ref[j, far_bucket]
    for d in range(2):
        idx = bidx_ref[d]
        acc = jnp.zeros(idx.shape, F32)
        for b in buckets[d]:
            acc = jnp.where(idx == b, (tab_ref[j, b] - c) * LOG2E, acc)
        if d == 1:
            acc = acc + madd_ref[...]
        o_ref[0, d] = acc


def _bias_tiles(table, seq, tile):
    bidx, far_bucket, mask_add = _position_constants(seq, tile)
    n_maps = table.shape[1]
    buckets = tuple(tuple(int(b) for b in np.unique(bidx[d])) for d in range(2))
    return pl.pallas_call(
        functools.partial(_bias_kernel, far_bucket=far_bucket, buckets=buckets),
        grid=(n_maps,),
        in_specs=[
            pl.BlockSpec(memory_space=pltpu.SMEM),
            pl.BlockSpec((2, tile, tile), lambda j: (0, 0, 0)),
            pl.BlockSpec((tile, tile), lambda j: (0, 0)),
        ],
        out_specs=pl.BlockSpec((1, 2, tile, tile), lambda j: (j, 0, 0, 0)),
        out_shape=jax.ShapeDtypeStruct((n_maps, 2, tile, tile), F32),
        compiler_params=_cparams(("arbitrary",)),
        name="bias_tiles",
    )(table.T.astype(F32), jnp.asarray(bidx), jnp.asarray(mask_add))


def _silu(x):
    return x * (1.0 / (1.0 + jnp.exp(-x)))


def _rms(x, g):
    return x * lax.rsqrt(jnp.mean(x * x, axis=-1, keepdims=True) + NORM_EPS) * g


def _inproj_kernel(x_ref, g_ref, w_ref, wvt_ref, qag_ref, wq1_ref, wq2_ref, kvg_ref, wk_ref, wavt_ref, cos_ref, sin_ref,
                   qa_ref, ka_ref, vat_ref, ga_ref, qb_ref, kb_ref, vbt_ref, gb_ref,
                   qc_ref, kc_ref, vct_ref, qi_ref, ki_ref, wi_ref, gc_ref, yd_ref, ud_ref):
    h = _rms(x_ref[...], g_ref[...]).astype(BF16)

    def seg(name):
        off, width = _SEG_OFF[name]
        return jnp.dot(h, w_ref[:, off:off + width], preferred_element_type=F32)

    def seg_pair(first, second):
        off, width = _SEG_OFF[first]
        assert _SEG_OFF[second] == (off + width, width) and width == LANES
        y = jnp.dot(h, w_ref[:, off:off + 2 * width], preferred_element_type=F32)
        return y[:, :width], y[:, width:]

    cos = cos_ref[...]
    sin = sin_ref[...]

    cqn = _rms(seg("cq"), qag_ref[...]).astype(BF16)
    q1 = jnp.dot(cqn, wq1_ref[...], preferred_element_type=F32)
    q2 = jnp.dot(cqn, wq2_ref[...], preferred_element_type=F32)
    ckv, kr1 = seg_pair("ckv", "kr1")
    kr2, kidx4 = seg_pair("kr2", "c_kidx4")
    kr = kr1 * cos + kr2 * sin
    ckvn = _rms(ckv, kvg_ref[...]).astype(BF16)
    kn = jnp.dot(ckvn, wk_ref[...], preferred_element_type=F32)
    for hd in range(N_HEADS):
        sl = slice(hd * LANES, (hd + 1) * LANES)
        qa_ref[:, sl] = (q1[:, sl] * cos + q2[:, sl] * sin).astype(BF16)
        ka_ref[:, sl] = (kn[:, sl] + kr).astype(BF16)
    ga_ref[...] = _silu(seg("a_gate")).astype(BF16)

    def put(ref, first_token, vt):
        tile = ref.shape[2]
        ref[first_token // tile, :, first_token % tile:first_token % tile + VT_TILE] = vt

    for c in range(x_ref.shape[0] // VT_TILE):
        rows = slice(c * VT_TILE, (c + 1) * VT_TILE)
        put(vat_ref, c * VT_TILE,
            lax.dot_general(wavt_ref[...], ckvn[rows], _NT, preferred_element_type=F32).astype(BF16))
        vt = lax.dot_general(wvt_ref[...], h[rows], _NT, preferred_element_type=F32).astype(BF16)
        put(vbt_ref, c * VT_TILE, vt[:GROUP_WIDTH])
        put(vct_ref, c * VT_TILE, vt[GROUP_WIDTH:])

    qb_ref[...] = seg("b_q").astype(BF16)
    kb_ref[...] = seg("b_k").astype(BF16)
    gb_ref[...] = _silu(seg("b_gate")).astype(BF16)

    qc_ref[...] = seg("c_q").astype(BF16)
    kc_ref[...] = seg("c_k").astype(BF16)
    qi_ref[...] = seg("c_qidx").astype(BF16)
    ki_ref[...] = kidx4.astype(BF16)
    wi_ref[...] = seg("c_widx")
    gc_ref[...] = _silu(seg("c_gate")).astype(BF16)

    yd_ref[...] = (seg("d_b") * _silu(seg("d_gate"))).astype(BF16)
    ud_ref[...] = seg("d_c") * seg("d_h")


def _layer_spec(a, l):
    return pl.BlockSpec((None,) + a.shape[1:], lambda i: (l,) + (0,) * (a.ndim - 1))


def _inproj(x2, l, g, w1, wvt, qag, wq1, wq2, kvg, wk, wavt, cos_t, sin_t, seq, tiles):
    n = x2.shape[0]
    tm = min(PROJ_ROWS, seq)
    tiles_per_seq = seq // tm
    row = lambda w: pl.BlockSpec((tm, w), lambda i: (i, 0))
    full = lambda a: _layer_spec(a, l)
    tab = pl.BlockSpec((tm, LANES), lambda i: (i % tiles_per_seq, 0))
    vt_spec = lambda t: pl.BlockSpec((tm // t, GROUP_WIDTH, t), lambda i: (i, 0, 0))
    vt_shape = lambda t: jax.ShapeDtypeStruct((n // t, GROUP_WIDTH, t), BF16)
    outs = (("qa", 512, BF16), ("ka", 512, BF16), ("vat", None, tiles[0]), ("ga", 256, BF16),
            ("qb", 256, BF16), ("kb", 256, BF16), ("vbt", None, tiles[1]), ("gb", 256, BF16),
            ("qc", 256, BF16), ("kc", 256, BF16), ("vct", None, tiles[2]), ("qi", 256, BF16),
            ("ki", 128, BF16), ("wi", 128, F32), ("gc", 256, BF16), ("yd", 256, BF16), ("ud", 256, F32))
    res = pl.pallas_call(
        _inproj_kernel,
        grid=(n // tm,),
        in_specs=[row(D_MODEL), full(g), full(w1), full(wvt), full(qag), full(wq1), full(wq2), full(kvg), full(wk),
                  full(wavt), tab, tab],
        out_specs=[vt_spec(dt) if w is None else row(w) for _, w, dt in outs],
        out_shape=[vt_shape(dt) if w is None else jax.ShapeDtypeStruct((n, w), dt) for _, w, dt in outs],
        compiler_params=_cparams(("parallel",)),
        name="inproj",
    )(x2, g, w1, wvt, qag, wq1, wq2, kvg, wk, wavt, cos_t, sin_t)
    return {name: o for (name, _, _), o in zip(outs, res)}


def _flash_scratch(n_maps, t):
    unit = _far_unit_tiles(t) * t
    return [pltpu.VMEM((n_maps, 1, t), F32), pltpu.VMEM((n_maps, 1, t), F32), pltpu.VMEM((n_maps, HEAD_DIM, t), F32),
            pltpu.VMEM((n_maps, unit, t), F32), pltpu.VMEM((n_maps, unit, t), F32)]


def _far_unit_tiles(t):
    return max(1, FAR_UNIT_KEYS // t)


def _flash_update(s, vts, old):
    m_tile = jnp.max(s, axis=0, keepdims=True)
    m_new = m_tile if old is None else jnp.maximum(old[0], m_tile)
    p = jnp.exp2(s - m_new)
    l_tile = jnp.sum(p, axis=0, keepdims=True)
    pb = p.astype(BF16)
    vt = vts[0] if len(vts) == 1 else jnp.concatenate(vts, axis=1)
    pv = jnp.dot(vt, pb, preferred_element_type=F32)
    if old is None:
        return m_new, l_tile, pv
    alpha = jnp.exp2(old[0] - m_new)
    return m_new, alpha * old[1] + l_tile, alpha * old[2] + pv


def _flash_sweep(i, t, n_maps, qs, keys, values, state, diag_add, near_add=None, mask=None):
    m_ref, l_ref, acc_ref, buf_a, buf_b = state
    n_far = i if near_add is None else jnp.maximum(i - 1, 0)
    ut = _far_unit_tiles(t)
    n_units = n_far // ut

    def load_state():
        return [(m_ref[j], l_ref[j], acc_ref[j]) for j in range(n_maps)]

    def store_state(cur):
        for j in range(n_maps):
            m_ref[j], l_ref[j], acc_ref[j] = cur[j]

    def tile_logits(j, kt, n_tiles, shared, extra=None):
        rows = pl.ds(pl.multiple_of(kt * t, t), n_tiles * t)
        s = lax.dot_general(keys(j, rows), qs[j], _NT, preferred_element_type=F32)
        add = shared
        if extra is not None:
            add = extra(j) if add is None else add + extra(j)
        return s if add is None else s + add

    def shared_mask(kt, n_tiles):
        if mask is None:
            return None
        tiles = [mask(kt + d) for d in range(n_tiles)]
        return tiles[0] if n_tiles == 1 else jnp.concatenate(tiles, axis=0)

    def run(units, first=False, prefetch=False):
        cur = [None] * n_maps if first else load_state()
        shared = [shared_mask(kt, n_tiles) for kt, n_tiles, _ in units]
        items = [(u, j) for u in range(len(units)) for j in range(n_maps)]
        if prefetch:
            items += [(None, j) for j in range(n_maps)]
            shared.append(shared_mask(0, ut))

        def logits(u, j):
            if u is None:
                return tile_logits(j, 0, ut, shared[-1])
            kt, n_tiles, extra = units[u]
            return tile_logits(j, kt, n_tiles, shared[u], extra)

        queue = [logits(*it) for it in items[:FLASH_LOOKAHEAD]]
        for n, (u, j) in enumerate(items):
            if n + FLASH_LOOKAHEAD < len(items):
                queue.append(logits(*items[n + FLASH_LOOKAHEAD]))
            s = queue.pop(0)
            if u is None:
                buf_a[j] = s
            else:
                kt, n_tiles, _ = units[u]
                cur[j] = _flash_update(s, [values(j, kt + d) for d in range(n_tiles)], cur[j])
        store_state(cur)

    def first_run(prefetch):
        if near_add is None:
            return lambda: run([(i, 1, diag_add)], first=True, prefetch=prefetch)
        return lambda: run([(i, 1, diag_add), (i - 1, 1, near_add)], first=True, prefetch=prefetch)

    if near_add is None:
        pl.when(n_units == 0)(first_run(False))
    else:
        pl.when(i == 0)(lambda: run([(i, 1, diag_add)], first=True))
        pl.when((i >= 1) & (n_units == 0))(first_run(False))
    pl.when(n_units >= 1)(first_run(True))

    def consume(buf, kt, j, old):
        return _flash_update(buf[j], [values(j, kt + d) for d in range(ut)], old)

    def overlapped(cur, buf_in, buf_out, u):
        kt_next = ut * (u + 1)
        shared = shared_mask(kt_next, ut)
        for j in range(n_maps):
            buf_out[j] = tile_logits(j, kt_next, ut, shared)
            cur[j] = consume(buf_in, ut * u, j, cur[j])

    def far_body(c, carry):
        cur = load_state()
        overlapped(cur, buf_a, buf_b, 2 * c)
        overlapped(cur, buf_b, buf_a, 2 * c + 1)
        store_state(cur)
        return carry

    lax.fori_loop(0, (n_units - 1) // 2, far_body, 0)

    @pl.when(n_units % 2 == 1)
    def _():
        cur = load_state()
        for j in range(n_maps):
            cur[j] = consume(buf_a, ut * (n_units - 1), j, cur[j])
        store_state(cur)

    @pl.when((n_units >= 2) & (n_units % 2 == 0))
    def _():
        cur = load_state()
        overlapped(cur, buf_a, buf_b, n_units - 2)
        for j in range(n_maps):
            cur[j] = consume(buf_b, ut * (n_units - 1), j, cur[j])
        store_state(cur)

    for r in range(1, ut):
        pl.when(n_far % ut >= r)(lambda r=r: run([(n_far - r, 1, None)]))
    return [acc_ref[j] / l_ref[j] for j in range(n_maps)]


def _lane_mask(lo, hi, dtype):
    lane = lax.broadcasted_iota(jnp.int32, (1, LANES), 1)
    return ((lane >= lo) & (lane < hi)).astype(dtype)


def _store_gated(o_ref, g_ref, heads_t):
    o = jnp.concatenate(heads_t, axis=0).T
    o_ref[...] = (o * g_ref[...].astype(F32)).astype(BF16)


def _att_specs(t, seq, widths_q, widths_k):
    nq = seq // t
    qspec = lambda w: pl.BlockSpec((t, w), lambda b, i: (b * nq + i, 0))
    kspec = lambda w: pl.BlockSpec((seq, w), lambda b, i: (b, 0), pipeline_mode=_ONE_BUFFER)
    vspec = pl.BlockSpec((nq, GROUP_WIDTH, t), lambda b, i: (b, 0, 0), pipeline_mode=_ONE_BUFFER)
    return [qspec(w) for w in widths_q], [kspec(w) for w in widths_k], vspec, qspec(GROUP_WIDTH)


def _mla_kernel(q_ref, k_ref, vt_ref, g_ref, madd_ref, o_ref, *state, t):
    i = pl.program_id(1)
    qs = [q_ref[:, hd * LANES:(hd + 1) * LANES] for hd in range(N_HEADS)]
    keys = lambda hd, rows: k_ref[rows, hd * LANES:(hd + 1) * LANES]
    values = lambda hd, kt: vt_ref[kt, hd * HEAD_DIM:(hd + 1) * HEAD_DIM, :]
    outs = _flash_sweep(i, t, N_HEADS, qs, keys, values, state, lambda hd: madd_ref[...])
    _store_gated(o_ref, g_ref, outs)


def _mla(p, mask_add, batch, seq, t):
    qs, ks, vspec, ospec = _att_specs(t, seq, (512, 256), (512,))
    return pl.pallas_call(
        functools.partial(_mla_kernel, t=t),
        grid=(batch, seq // t),
        in_specs=[qs[0], ks[0], vspec, qs[1], pl.BlockSpec((t, t), lambda b, i: (0, 0), pipeline_mode=_ONE_BUFFER)],
        out_specs=ospec,
        out_shape=jax.ShapeDtypeStruct((batch * seq, GROUP_WIDTH), BF16),
        scratch_shapes=_flash_scratch(N_HEADS, t),
        compiler_params=_cparams(("parallel", "arbitrary")),
        name="mla_attention",
    )(p["qa"], p["ka"], p["vat"], p["ga"], mask_add)


def _diff_kernel(q_ref, k_ref, vt_ref, g_ref, bias_ref, lam_ref, subg_ref, o_ref, *state, t, lambda_init):
    i = pl.program_id(1)
    lp = lam_ref[...]
    lam = (jnp.exp(jnp.sum(lp[0:1] * lp[1:2], axis=-1, keepdims=True))
           - jnp.exp(jnp.sum(lp[2:3] * lp[3:4], axis=-1, keepdims=True)) + lambda_init)
    qs = [q_ref[:, (j // 4) * LANES:(j // 4 + 1) * LANES] * _lane_mask((j % 4) * DIFF_QK, (j % 4 + 1) * DIFF_QK, BF16)
          for j in range(N_DIFF_MAPS)]
    keys = lambda j, rows: k_ref[rows, (j // 4) * LANES:(j // 4 + 1) * LANES]
    values = lambda j, kt: vt_ref[kt, (j // 2) * HEAD_DIM:(j // 2 + 1) * HEAD_DIM, :]
    maps = _flash_sweep(i, t, N_DIFF_MAPS, qs, keys, values, state, lambda j: bias_ref[j, 1], lambda j: bias_ref[j, 0])
    outs = []
    for hd in range(N_HEADS):
        o = maps[2 * hd] - lam * maps[2 * hd + 1]
        ms = jnp.mean(o * o, axis=0, keepdims=True)
        outs.append(o * lax.rsqrt(ms + NORM_EPS) * subg_ref[...] * (1.0 - lambda_init))
    _store_gated(o_ref, g_ref, outs)


def _diff(p, bias, lam_params, subln_g, lambda_init, batch, seq, t):
    qs, ks, vspec, ospec = _att_specs(t, seq, (256, 256), (256,))
    subg = jnp.broadcast_to(subln_g.astype(F32)[:, None], (HEAD_DIM, t))
    return pl.pallas_call(
        functools.partial(_diff_kernel, t=t, lambda_init=lambda_init),
        grid=(batch, seq // t),
        in_specs=[qs[0], ks[0], vspec, qs[1],
                  pl.BlockSpec((N_DIFF_MAPS, 2, t, t), lambda b, i: (0, 0, 0, 0), pipeline_mode=_ONE_BUFFER),
                  pl.BlockSpec((4, DIFF_QK), lambda b, i: (0, 0)),
                  pl.BlockSpec((HEAD_DIM, t), lambda b, i: (0, 0))],
        out_specs=ospec,
        out_shape=jax.ShapeDtypeStruct((batch * seq, GROUP_WIDTH), BF16),
        scratch_shapes=_flash_scratch(N_DIFF_MAPS, t),
        compiler_params=_cparams(("parallel", "arbitrary")),
        name="diff_attention",
    )(p["qb"], p["kb"], p["vbt"], p["gb"], bias, lam_params.astype(F32), subg)


def _sort_key(x):
    b = lax.bitcast_convert_type(x + 0.0, jnp.int32)
    return b ^ ((b >> 31) & 0x7FFFFFFF)


def _dsa_kernel(q_ref, k_ref, vt_ref, qi_ref, ki_ref, wi_ref, g_ref, bias_ref, tri_ref, o_ref,
                score_ref, hi_ref, lo_ref, thr_ref, cnt_ref, *state, t, topk):
    i = pl.program_id(1)
    n_kt = i + 1
    idx_scale = (IDX_HEADS ** -0.5) * (IDX_DIM ** -0.5)

    wit = wi_ref[...].T * idx_scale
    qm, wrow = [], []
    for ih in range(IDX_HEADS):
        grp = slice((ih // 4) * LANES, (ih // 4 + 1) * LANES)
        lo = (ih % 4) * IDX_DIM
        qm.append(qi_ref[:, grp] * _lane_mask(lo, lo + IDX_DIM, BF16))
        wrow.append(wit[ih:ih + 1, :])

    def idx_tile(kt):
        kk = ki_ref[pl.ds(pl.multiple_of(kt * t, t), t), :]
        dot = lambda ih: lax.dot_general(kk, qm[ih], _NT, preferred_element_type=F32)
        queue = [dot(ih) for ih in range(FLASH_LOOKAHEAD)]
        tot = None
        for ih in range(IDX_HEADS):
            if ih + FLASH_LOOKAHEAD < IDX_HEADS:
                queue.append(dot(ih + FLASH_LOOKAHEAD))
            term = wrow[ih] * jnp.maximum(queue.pop(0), 0.0)
            tot = term if tot is None else tot + term
        return tot + 0.0

    def store_scores(kt, score):
        score_ref[kt] = score
        key = _sort_key(score)
        hi_ref[kt] = (key >> HALF_BITS).astype(jnp.int16)
        lo_ref[kt] = ((key & HALF_MASK) - HALF_BIAS).astype(jnp.int16)

    def key_tiles(first, n):
        for d in range(n):
            store_scores(first + d, idx_tile(first + d))

    def key_quad_body(c, carry):
        key_tiles(4 * c, 4)
        return carry

    lax.fori_loop(0, i // 4, key_quad_body, 0)
    pl.when(i % 4 >= 2)(lambda: key_tiles(i - i % 4, 2))
    pl.when(i % 2 == 1)(lambda: key_tiles(i - 1, 1))
    kchunk = lax.broadcasted_iota(jnp.int32, (t, t), 0) // CHUNK
    qchunk = lax.broadcasted_iota(jnp.int32, (t, t), 1) // CHUNK
    store_scores(i, jnp.where(kchunk <= qchunk, idx_tile(i), -jnp.inf))

    def count(pred):
        def body(kt, c):
            hit = jnp.where(pred(score_ref[kt]), 1.0, 0.0)
            return c + jnp.sum(hit.reshape(t // SUBLANES, SUBLANES, t), axis=0)
        part = lax.fori_loop(0, n_kt, body, jnp.zeros((SUBLANES, t), F32))
        return jnp.sum(part, axis=0, keepdims=True)

    rows16 = 2 * SUBLANES
    n_acc = 4

    def key_to_score(key):
        key = jnp.maximum(key, KEY_LOWEST)
        return lax.bitcast_convert_type(key ^ ((key >> 31) & 0x7FFFFFFF), F32)

    def packed_search(n):
        def count16(ref, pred):
            one, zero = jnp.int16(1), jnp.int16(0)
            accs = [jnp.zeros((rows16, t), jnp.int16) for _ in range(n_acc)]
            for kt in range(n):
                x = ref[kt].reshape(t // rows16, rows16, t)
                for r in range(t // rows16):
                    accs[r % n_acc] = accs[r % n_acc] + jnp.where(pred(x[r]), one, zero)
            tot = (accs[0] + accs[1]) + (accs[2] + accs[3])
            return jnp.sum(tot.astype(jnp.int32), axis=0, keepdims=True)

        def select16(ref, need):
            def bit_body(b, res):
                cand = res | (jnp.int32(1) << (HALF_BITS - 1 - b))
                cand16 = (cand - HALF_BIAS).astype(jnp.int16)
                cnt = count16(ref, lambda x: x >= cand16)
                return jnp.where(cnt >= need, cand, res)
            return lax.fori_loop(0, HALF_BITS, bit_body, jnp.zeros((1, t), jnp.int32)) - HALF_BIAS

        hi_thr = select16(hi_ref, topk)
        hi_thr16 = hi_thr.astype(jnp.int16)
        need_lo = topk - count16(hi_ref, lambda x: x > hi_thr16)
        for kt in range(n):
            lo_ref[kt] = jnp.where(hi_ref[kt] == hi_thr16, lo_ref[kt], jnp.int16(-HALF_BIAS))
        lo_thr = select16(lo_ref, need_lo)
        thr_ref[...] = key_to_score((hi_thr << HALF_BITS) | (lo_thr + HALF_BIAS))

    for n in range(1, score_ref.shape[0] + 1):
        pl.when(n_kt == n)(functools.partial(packed_search, n))

    cnt_ref[...] = count(lambda x: x >= thr_ref[...])
    n_valid = ((i * t + lax.broadcasted_iota(jnp.int32, (1, t), 1)) // CHUNK + 1) * CHUNK
    few = n_valid <= topk

    @pl.when(jnp.max(jnp.where(few, 0.0, jnp.abs(cnt_ref[...] - topk))) > 0.0)
    def _():
        cnt_gt = count(lambda x: x > thr_ref[...])
        wrong = (cnt_ref[...] < topk) | (cnt_gt >= topk)

        @pl.when(jnp.max(jnp.where(few | jnp.logical_not(wrong), 0.0, 1.0)) > 0.0)
        def _():
            def bit_body(b, res):
                cand = res | (jnp.int32(1) << (31 - b))
                cand_score = key_to_score(cand ^ INT_MIN)
                cnt = count(lambda x: x >= cand_score)
                return jnp.where(cnt >= topk, cand, res)
            res = lax.fori_loop(0, 32, bit_body, jnp.zeros((1, t), jnp.int32))
            thr_ref[...] = key_to_score(res ^ INT_MIN)
            cnt_ref[...] = count(lambda x: x >= thr_ref[...])

    thr = thr_ref[...]
    has_ties = jnp.max(jnp.where(few, 0.0, cnt_ref[...])) > topk

    @pl.when(jnp.logical_not(has_ties))
    def _():
        def body(kt, c):
            score_ref[kt] = jnp.where(score_ref[kt] >= thr, 0.0, NEG)
            return c
        lax.fori_loop(0, n_kt, body, 0)

    @pl.when(has_ties)
    def _():
        need = topk - count(lambda x: x > thr)

        def body(kt, seen):
            x = score_ref[kt]
            eq = jnp.where(x == thr, 1.0, 0.0)
            rank = seen + jnp.dot(tri_ref[...], eq.astype(BF16), preferred_element_type=F32)
            sel = (x > thr) | ((x == thr) & (rank <= need))
            score_ref[kt] = jnp.where(sel, 0.0, NEG)
            return seen + jnp.sum(eq, axis=0, keepdims=True)
        lax.fori_loop(0, n_kt, body, jnp.zeros((1, t), F32))

    qs = [q_ref[:, (hd // 2) * LANES:(hd // 2 + 1) * LANES]
          * _lane_mask((hd % 2) * HEAD_DIM, (hd % 2 + 1) * HEAD_DIM, BF16) for hd in range(N_HEADS)]
    keys = lambda hd, rows: k_ref[rows, (hd // 2) * LANES:(hd // 2 + 1) * LANES]
    values = lambda hd, kt: vt_ref[kt, hd * HEAD_DIM:(hd + 1) * HEAD_DIM, :]
    outs = _flash_sweep(i, t, N_HEADS, qs, keys, values, state, lambda hd: bias_ref[hd, 1], lambda hd: bias_ref[hd, 0],
                        mask=lambda kt: score_ref[kt])
    _store_gated(o_ref, g_ref, outs)


def _dsa(p, bias, batch, seq, t):
    topk = min(TOPK_MAX, seq // 4)
    nq = seq // t
    qs, ks, vspec, ospec = _att_specs(t, seq, (256, 256, 128, 256), (256, 128))
    tri = jnp.asarray(np.tril(np.ones((t, t), np.float32)), BF16)
    return pl.pallas_call(
        functools.partial(_dsa_kernel, t=t, topk=topk),
        grid=(batch, nq),
        in_specs=[qs[0], ks[0], vspec, qs[1], ks[1], qs[2], qs[3],
                  pl.BlockSpec((N_HEADS, 2, t, t), lambda b, i: (0, 0, 0, 0), pipeline_mode=_ONE_BUFFER),
                  pl.BlockSpec((t, t), lambda b, i: (0, 0), pipeline_mode=_ONE_BUFFER)],
        out_specs=ospec,
        out_shape=jax.ShapeDtypeStruct((batch * seq, GROUP_WIDTH), BF16),
        scratch_shapes=[pltpu.VMEM((nq, t, t), F32), pltpu.VMEM((nq, t, t), jnp.int16),
                        pltpu.VMEM((nq, t, t), jnp.int16), pltpu.VMEM((1, t), F32), pltpu.VMEM((1, t), F32)]
        + _flash_scratch(N_HEADS, t),
        compiler_params=_cparams(("parallel", "arbitrary")),
        name="dsa_attention",
    )(p["qc"], p["kc"], p["vct"], p["qi"], p["ki"], p["wi"], p["gc"], bias, tri)


def _outproj_kernel(x_ref, ya_ref, yb_ref, yc_ref, yd_ref, ud_ref, halo_ref, cw_ref, w_ref, fg_ref, o_ref,
                    *, tiles_per_seq, final):
    i = pl.program_id(0)
    tm = x_ref.shape[0]
    u = ud_ref[...]
    halo = jnp.where(i % tiles_per_seq == 0, 0.0, halo_ref[...])
    row8 = lax.broadcasted_iota(jnp.int32, (SUBLANES, 1), 0)

    def shifted(k):
        r = pltpu.roll(u, k, 0)
        head = jnp.where(row8 < k, pltpu.roll(halo, k, 0), r[0:SUBLANES])
        return jnp.concatenate([head, r[SUBLANES:tm]], axis=0)

    cw = cw_ref[...]
    conv = cw[0:1] * shifted(CONV_WIDTH - 1)
    for j in range(1, CONV_WIDTH):
        conv = conv + cw[j:j + 1] * (shifted(CONV_WIDTH - 1 - j) if j < CONV_WIDTH - 1 else u)
    yd = (conv * yd_ref[...].astype(F32)).astype(BF16)
    acc = x_ref[...]
    for g, y in enumerate((ya_ref[...], yb_ref[...], yc_ref[...], yd)):
        acc = acc + jnp.dot(y, w_ref[g * GROUP_WIDTH:(g + 1) * GROUP_WIDTH, :], preferred_element_type=F32)
    if final:
        acc = _rms(acc, fg_ref[...])
    o_ref[...] = acc


def _outproj(x2, l, ya, yb, yc, yd, ud, conv_w, w_out, final_g, seq, final):
    n = x2.shape[0]
    tm = min(PROJ_ROWS, seq)
    tiles_per_seq = seq // tm
    row = lambda w: pl.BlockSpec((tm, w), lambda i: (i, 0))
    halo = pl.BlockSpec((SUBLANES, GROUP_WIDTH), lambda i: (jnp.maximum(i * (tm // SUBLANES) - 1, 0), 0))
    return pl.pallas_call(
        functools.partial(_outproj_kernel, tiles_per_seq=tiles_per_seq, final=final),
        grid=(n // tm,),
        in_specs=[row(D_MODEL), row(256), row(256), row(256), row(256), row(256), halo,
                  _layer_spec(conv_w, l), _layer_spec(w_out, l), pl.BlockSpec(final_g.shape, lambda i: (0, 0))],
        out_specs=row(D_MODEL),
        out_shape=jax.ShapeDtypeStruct((n, D_MODEL), F32),
        compiler_params=_cparams(("parallel",)),
        name="outproj",
    )(x2, ya, yb, yc, yd, ud, ud, conv_w, w_out, final_g)


def kernel(x, norm_g, w_in, mla_qa_g, mla_w_uq, mla_kva_g, mla_w_ukv, diff_lambda, diff_subln_g, conv_w, w_out,
           rel_bias, final_g):
    batch, seq, _ = x.shape
    depth = w_in.shape[0]
    tm = min(PROJ_ROWS, seq)
    tiles = tuple(min(t, seq) for t in (MLA_TILE, DIFF_TILE, DSA_TILE))
    assert seq % tm == 0 and tm % VT_TILE == 0 and conv_w.shape[1] == CONV_WIDTH <= SUBLANES
    assert all(seq % t == 0 and t % CHUNK == 0 and tm % t == 0 and t % VT_TILE == 0 for t in tiles)

    cos_t, sin_t = (jnp.asarray(a) for a in _rope_tables(seq))
    mask_add = jnp.asarray(_position_constants(seq, tiles[0])[2])
    bias_diff = _bias_tiles(rel_bias[:, :N_DIFF_MAPS], seq, tiles[1])
    bias_dsa = _bias_tiles(rel_bias[:, N_DIFF_MAPS:], seq, tiles[2])

    w1, wvt = _prep_w_in(w_in)
    wq1, wq2, wk, wavt = _prep_mla_up(mla_w_uq, mla_w_ukv)
    row3 = lambda a: a[:, None, :]
    w_out_b = w_out.astype(BF16)

    x2 = x.reshape(batch * seq, D_MODEL)
    for l in range(depth):
        lambda_init = 0.8 - 0.6 * math.exp(-0.3 * l)
        p = _inproj(x2, l, row3(norm_g), w1, wvt, row3(mla_qa_g), wq1, wq2, row3(mla_kva_g), wk, wavt,
                    cos_t, sin_t, seq, tiles)
        ya = _mla(p, mask_add, batch, seq, tiles[0])
        yb = _diff(p, bias_diff, diff_lambda[l], diff_subln_g[l], lambda_init, batch, seq, tiles[1])
        yc = _dsa(p, bias_dsa, batch, seq, tiles[2])
        x2 = _outproj(x2, l, ya, yb, yc, p["yd"], p["ud"], conv_w, w_out_b, final_g[None, :],
                      seq, final=(l == depth - 1))
    return x2.reshape(batch, seq, D_MODEL)
```
